```python
import math
import jax, jax.numpy as jnp
from jax import lax
import numpy as np

D_MODEL = 2048
BATCH = 2
SEQ = 16384
DEPTH = 1
DEC_BATCH = 2
DEC_SEQ = 8192
PAST_LEN = 128

D_HYENA = 1024
HYENA_ORDER = 2
SHORT_CONV = 3
FILTER_EMB = 33
FILTER_HIDDEN = 64
FAST_DECAY_PCT = 0.3
SLOW_DECAY_PCT = 1.5
DECAY_TARGET = 1e-2
FILTER_GAIN = 0.02
HEAD_DIM = 128
HEADS_PER_GROUP = 4
ATT_GROUPS = ((128, 1), (512, 4), (2048, 16))
N_GROUPS = len(ATT_GROUPS)
D_ATT = N_GROUPS * HEADS_PER_GROUP * HEAD_DIM
D_ATT_OUT = HEADS_PER_GROUP * HEAD_DIM
ROPE_DIM = HEAD_DIM // 4
ROPE_THETA = 500000.0
N_BRANCH = 2
D_IN_PROJ = 3 * D_HYENA + 3 * D_ATT + N_BRANCH * D_MODEL
D_FF = 5632
EPS = 1e-6
NEG_INF = -1e30

kernel_name = 'hybrid_hyena_dilated_attn_encoder'


def rmsnorm(x, g):
    xf = x.astype(jnp.float32)
    y = xf * lax.rsqrt(jnp.mean(xf * xf, axis=-1, keepdims=True) + EPS)
    return (y * g.astype(jnp.float32)).astype(x.dtype)


def swiglu(x, w_gate, w_up, w_down):
    return (jax.nn.silu(x @ w_gate) * (x @ w_up)) @ w_down


def short_conv(x, w, b):
    L = x.shape[1]
    p = SHORT_CONV // 2
    xp = jnp.pad(x, ((0, 0), (p, p), (0, 0)))
    y = xp[:, 0:L] * w[0]
    for i in range(1, SHORT_CONV):
        y = y + xp[:, i:i + L] * w[i]
    return y + b


def hyena_filters(L, fw1, fb1, fw2, fb2, fw3, fb3, fw4, ffreq):
    f32 = jnp.float32
    t = jnp.linspace(0.0, 1.0, L, dtype=f32)[:, None]
    bands = (FILTER_EMB - 1) // 2
    w = (2.0 * math.pi / L) * jnp.arange(L, dtype=f32)[:, None]
    f = jnp.linspace(1e-4, bands - 1, bands, dtype=f32)[None, :]
    z = jnp.concatenate([t, jnp.cos(f * w), -jnp.sin(f * w)], axis=-1)
    freq = ffreq.astype(f32)
    h = jnp.sin(freq * (z @ fw1.astype(f32) + fb1.astype(f32)))
    h = jnp.sin(freq * (h @ fw2.astype(f32) + fb2.astype(f32)))
    h = jnp.sin(freq * (h @ fw3.astype(f32) + fb3.astype(f32)))
    h = h @ fw4.astype(f32)
    max_decay = math.log(DECAY_TARGET) / FAST_DECAY_PCT
    min_decay = math.log(DECAY_TARGET) / SLOW_DECAY_PCT
    deltas = jnp.abs(jnp.linspace(min_decay, max_decay, D_HYENA, dtype=f32))
    decay = jnp.exp(-t * deltas)
    return h.reshape(L, HYENA_ORDER, 2, D_HYENA) * decay[:, None, None, :]


def two_sided_longconv(z, h_fwd, h_bwd, bias):
    L, C = h_fwd.shape
    k = jnp.concatenate([h_fwd, jnp.zeros((1, C), jnp.float32), h_bwd[:0:-1]], axis=0)
    kf = jnp.fft.rfft(k, axis=0)
    zf = jnp.fft.rfft(z, n=2 * L, axis=1)
    y = jnp.fft.irfft(zf * kf[None], n=2 * L, axis=1)[:, :L]
    return y + z * bias


def hyena_mixer(hy_in, conv_w, conv_b, fw1, fb1, fw2, fb2, fw3, fb3, fw4, ffreq, hy_bias):
    L = hy_in.shape[1]
    zs = short_conv(hy_in, conv_w, conv_b).astype(jnp.float32)
    v, x1, x2 = jnp.split(zs, 3, axis=-1)
    h = hyena_filters(L, fw1, fb1, fw2, fb2, fw3, fb3, fw4, ffreq)
    y = v
    for n, gate in enumerate((x1, x2)):
        y = gate * two_sided_longconv(y, h[:, n, 0], h[:, n, 1], hy_bias[n].astype(jnp.float32))
    return y


def rope_partial(x, pos):
    half = ROPE_DIM // 2
    inv_freq = jnp.power(ROPE_THETA, -jnp.arange(half, dtype=jnp.float32) / half)
    ang = pos[:, None] * inv_freq[None, :]
    shape = (1, x.shape[1]) + (1,) * (x.ndim - 3) + (half,)
    cos = jnp.cos(ang).reshape(shape)
    sin = jnp.sin(ang).reshape(shape)
    xr = x[..., :ROPE_DIM].astype(jnp.float32)
    xa, xb = xr[..., :half], xr[..., half:]
    rot = jnp.concatenate([xa * cos - xb * sin, xb * cos + xa * sin], axis=-1)
    return jnp.concatenate([rot.astype(x.dtype), x[..., ROPE_DIM:]], axis=-1)


def dilated_window_attention(q, k, v, window, dilation):
    Bsz, L, H, Dh = q.shape
    radius = window // (2 * dilation)
    blk = radius
    span = dilation * blk
    Lp = -(-L // span) * span
    Ls = Lp // dilation
    nb = Ls // blk

    def to_blocks(t):
        t = jnp.pad(t, ((0, 0), (0, Lp - L), (0, 0), (0, 0)))
        t = t.reshape(Bsz, Ls, dilation, H, Dh).transpose(0, 2, 1, 3, 4)
        return t.reshape(Bsz, dilation, nb, blk, H, Dh)

    def neighbours(t):
        tp = jnp.pad(t, ((0, 0), (0, 0), (1, 1), (0, 0), (0, 0), (0, 0)))
        return jnp.concatenate([tp[:, :, :-2], tp[:, :, 1:-1], tp[:, :, 2:]], axis=3)

    qb = to_blocks(q)
    kn = neighbours(to_blocks(k))
    vn = neighbours(to_blocks(v))
    scores = jnp.einsum('brnqhd,brnkhd->brnhqk', qb, kn, preferred_element_type=jnp.float32)
    qi = jnp.arange(blk)
    ki = jnp.arange(3 * blk) - blk
    band = jnp.abs(ki[None, :] - qi[:, None]) <= radius
    strided_key = jnp.arange(nb)[:, None] * blk + ki[None, :]
    key_pos = strided_key[None] * dilation + jnp.arange(dilation)[:, None, None]
    valid = (strided_key >= 0)[None] & (key_pos < L)
    mask = band[None, None] & valid[:, :, None, :]
    scores = jnp.where(mask[None, :, :, None], scores, NEG_INF)
    m = jnp.max(scores, axis=-1, keepdims=True)
    p = jnp.exp(scores - m)
    s = jnp.sum(p, axis=-1, keepdims=True)
    out = jnp.einsum('brnhqk,brnkhd->brnqhd', (p / s).astype(v.dtype), vn,
                     preferred_element_type=jnp.float32)
    lse = (m + jnp.log(s))[..., 0]
    out = out.reshape(Bsz, dilation, Ls, H, Dh).transpose(0, 2, 1, 3, 4).reshape(Bsz, Lp, H, Dh)[:, :L]
    lse = lse.transpose(0, 1, 2, 4, 3).reshape(Bsz, dilation, Ls, H).transpose(0, 2, 1, 3).reshape(Bsz, Lp, H)[:, :L]
    return out, lse


def token_mixer(u, w_in, hy_conv_w, hy_conv_b, fw1, fb1, fw2, fb2, fw3, fb3, fw4, ffreq, hy_bias,
                w_hy_proj, w_att_proj, w_out):
    Bsz, L, _ = u.shape
    proj = u @ w_in
    c0 = 3 * D_HYENA
    c1 = c0 + 3 * D_ATT
    hy_in = proj[..., :c0]
    qkv = proj[..., c0:c1].reshape(Bsz, L, 3, N_GROUPS, HEADS_PER_GROUP, HEAD_DIM)
    gates = jax.nn.sigmoid(proj[..., c1:].astype(jnp.float32)).reshape(Bsz, L, N_BRANCH, D_MODEL)
    a = hyena_mixer(hy_in, hy_conv_w, hy_conv_b, fw1, fb1, fw2, fb2, fw3, fb3, fw4, ffreq, hy_bias)
    a = a.astype(u.dtype) @ w_hy_proj
    pos = jnp.arange(L, dtype=jnp.float32)
    q = rope_partial(qkv[:, :, 0], pos) * (HEAD_DIM ** -0.5)
    k = rope_partial(qkv[:, :, 1], pos)
    v = qkv[:, :, 2]
    outs = []
    lses = []
    for g, (window, dilation) in enumerate(ATT_GROUPS):
        o, s = dilated_window_attention(q[:, :, g], k[:, :, g], v[:, :, g], window, dilation)
        outs.append(o)
        lses.append(s)
    wts = jax.nn.softmax(jnp.stack(lses, axis=0), axis=0)
    att = jnp.sum(wts[..., None] * jnp.stack(outs, axis=0), axis=0)
    b = att.reshape(Bsz, L, D_ATT_OUT).astype(u.dtype) @ w_att_proj
    merged = (gates[:, :, 0] * a + gates[:, :, 1] * b).astype(u.dtype)
    return merged @ w_out


def encoder_layer(x, ffn1_pre_g, ffn1_post_g, ffn1_w_gate, ffn1_w_up, ffn1_w_down,
                  mix_pre_g, mix_post_g, w_in, hy_conv_w, hy_conv_b,
                  filt_w1, filt_b1, filt_w2, filt_b2, filt_w3, filt_b3, filt_w4, filt_freq, hy_bias,
                  w_hy_proj, w_att_proj, w_out,
                  ffn2_pre_g, ffn2_post_g, ffn2_w_gate, ffn2_w_up, ffn2_w_down):
    x = x + 0.5 * rmsnorm(swiglu(rmsnorm(x, ffn1_pre_g), ffn1_w_gate, ffn1_w_up, ffn1_w_down), ffn1_post_g)
    mix = token_mixer(rmsnorm(x, mix_pre_g), w_in, hy_conv_w, hy_conv_b,
                      filt_w1, filt_b1, filt_w2, filt_b2, filt_w3, filt_b3, filt_w4, filt_freq, hy_bias,
                      w_hy_proj, w_att_proj, w_out)
    x = x + rmsnorm(mix, mix_post_g)
    x = x + 0.5 * rmsnorm(swiglu(rmsnorm(x, ffn2_pre_g), ffn2_w_gate, ffn2_w_up, ffn2_w_down), ffn2_post_g)
    return x


def setup_inputs(seed: int = 0) -> dict:
    key = jax.random.key(seed)
    ks = jax.random.split(key, 29)
    f32 = jnp.float32

    def nrm(k, shape, scale):
        return jax.random.normal(k, shape, f32) * scale

    def gain(k, d):
        return 1.0 + nrm(k, (DEPTH, d), 0.02)

    return {
        'x_prompt': nrm(ks[0], (BATCH, SEQ, D_MODEL), 1.0),
        'x_sample': nrm(ks[1], (DEC_BATCH, DEC_SEQ, D_MODEL), 1.0),
        'ffn1_pre_g': gain(ks[2], D_MODEL),
        'ffn1_post_g': gain(ks[3], D_MODEL),
        'ffn1_w_gate': nrm(ks[4], (DEPTH, D_MODEL, D_FF), D_MODEL ** -0.5),
        'ffn1_w_up': nrm(ks[5], (DEPTH, D_MODEL, D_FF), D_MODEL ** -0.5),
        'ffn1_w_down': nrm(ks[6], (DEPTH, D_FF, D_MODEL), D_FF ** -0.5),
        'mix_pre_g': gain(ks[7], D_MODEL),
        'mix_post_g': gain(ks[8], D_MODEL),
        'w_in': nrm(ks[9], (DEPTH, D_MODEL, D_IN_PROJ), D_MODEL ** -0.5),
        'hy_conv_w': nrm(ks[10], (DEPTH, SHORT_CONV, 3 * D_HYENA), SHORT_CONV ** -0.5),
        'hy_conv_b': nrm(ks[11], (DEPTH, 3 * D_HYENA), 0.02),
        'filt_w1': nrm(ks[12], (DEPTH, FILTER_EMB, FILTER_HIDDEN), FILTER_EMB ** -0.5),
        'filt_b1': nrm(ks[13], (DEPTH, FILTER_HIDDEN), 0.5),
        'filt_w2': nrm(ks[14], (DEPTH, FILTER_HIDDEN, FILTER_HIDDEN), FILTER_HIDDEN ** -0.5),
        'filt_b2': nrm(ks[15], (DEPTH, FILTER_HIDDEN), 0.5),
        'filt_w3': nrm(ks[16], (DEPTH, FILTER_HIDDEN, FILTER_HIDDEN), FILTER_HIDDEN ** -0.5),
        'filt_b3': nrm(ks[17], (DEPTH, FILTER_HIDDEN), 0.5),
        'filt_w4': nrm(ks[18], (DEPTH, FILTER_HIDDEN, HYENA_ORDER * 2 * D_HYENA), FILTER_GAIN * FILTER_HIDDEN ** -0.5),
        'filt_freq': gain(ks[19], FILTER_HIDDEN),
        'hy_bias': nrm(ks[20], (DEPTH, HYENA_ORDER, D_HYENA), 0.5),
        'w_hy_proj': nrm(ks[21], (DEPTH, D_HYENA, D_MODEL), D_HYENA ** -0.5),
        'w_att_proj': nrm(ks[22], (DEPTH, D_ATT_OUT, D_MODEL), D_ATT_OUT ** -0.5),
        'w_out': nrm(ks[23], (DEPTH, D_MODEL, D_MODEL), D_MODEL ** -0.5),
        'ffn2_pre_g': gain(ks[24], D_MODEL),
        'ffn2_post_g': gain(ks[25], D_MODEL),
        'ffn2_w_gate': nrm(ks[26], (DEPTH, D_MODEL, D_FF), D_MODEL ** -0.5),
        'ffn2_w_up': nrm(ks[27], (DEPTH, D_MODEL, D_FF), D_MODEL ** -0.5),
        'ffn2_w_down': nrm(ks[28], (DEPTH, D_FF, D_MODEL), D_FF ** -0.5),
    }


def reference(x_prompt, x_sample, ffn1_pre_g, ffn1_post_g, ffn1_w_gate, ffn1_w_up, ffn1_w_down,
              mix_pre_g, mix_post_g, w_in, hy_conv_w, hy_conv_b,
              filt_w1, filt_b1, filt_w2, filt_b2, filt_w3, filt_b3, filt_w4, filt_freq, hy_bias,
              w_hy_proj, w_att_proj, w_out,
              ffn2_pre_g, ffn2_post_g, ffn2_w_gate, ffn2_w_up, ffn2_w_down):
    params = (ffn1_pre_g, ffn1_post_g, ffn1_w_gate, ffn1_w_up, ffn1_w_down,
              mix_pre_g, mix_post_g, w_in, hy_conv_w, hy_conv_b,
              filt_w1, filt_b1, filt_w2, filt_b2, filt_w3, filt_b3, filt_w4, filt_freq, hy_bias,
              w_hy_proj, w_att_proj, w_out,
              ffn2_pre_g, ffn2_post_g, ffn2_w_gate, ffn2_w_up, ffn2_w_down)

    def run(x):
        for layer in range(DEPTH):
            x = encoder_layer(x, *[p[layer] for p in params])
        return x

    y_prompt = run(x_prompt)
    y_sample = run(x_sample)
    return (y_prompt, y_sample)
```

```python
import math
from functools import partial

import jax
import jax.numpy as jnp
from jax import lax
from jax.experimental import pallas as pl
from jax.experimental.pallas import tpu as pltpu

D_MODEL = 2048
D_HYENA = 1024
HYENA_ORDER = 2
SHORT_CONV = 3
FILTER_EMB = 33
FAST_DECAY_PCT = 0.3
SLOW_DECAY_PCT = 1.5
DECAY_TARGET = 1e-2
HEAD_DIM = 128
HEADS_PER_GROUP = 4
ATT_GROUPS = ((128, 1), (512, 4), (2048, 16))
N_GROUPS = len(ATT_GROUPS)
D_ATT = N_GROUPS * HEADS_PER_GROUP * HEAD_DIM
D_ATT_OUT = HEADS_PER_GROUP * HEAD_DIM
ROPE_DIM = HEAD_DIM // 4
ROPE_THETA = 500000.0
N_BRANCH = 2
D_IN_PROJ = 3 * D_HYENA + 3 * D_ATT + N_BRANCH * D_MODEL
D_FF = 5632
EPS = 1e-6
NEG_INF = -1e30

VMEM_LIMIT_BYTES = 56 * 1024 * 1024
BF16 = jnp.bfloat16
F32 = jnp.float32


def _rms(x, g):
    return x * lax.rsqrt(jnp.mean(x * x, axis=-1, keepdims=True) + EPS) * g


def _ffn_kernel(x_ref, pre_g_ref, post_g_ref, wg_ref, wu_ref, wd_ref, o_ref, xn_ref, acc_ref):
    j = pl.program_id(1)

    @pl.when(j == 0)
    def _():
        xn_ref[...] = _rms(x_ref[...], pre_g_ref[...]).astype(BF16)
        acc_ref[...] = jnp.zeros_like(acc_ref)

    xn = xn_ref[...]
    gate = jnp.dot(xn, wg_ref[...], preferred_element_type=F32)
    up = jnp.dot(xn, wu_ref[...], preferred_element_type=F32)
    h = (gate * jax.nn.sigmoid(gate) * up).astype(BF16)
    acc_ref[...] += jnp.dot(h, wd_ref[...], preferred_element_type=F32)

    @pl.when(j == pl.num_programs(1) - 1)
    def _():
        o_ref[...] = x_ref[...] + 0.5 * _rms(acc_ref[...], post_g_ref[...])


def ffn_block(x, pre_g, post_g, wg, wu, wd, *, tm=512, tf=512):
    M, D = x.shape
    FF = wg.shape[1]
    return pl.pallas_call(
        _ffn_kernel,
        grid=(M // tm, FF // tf),
        in_specs=[
            pl.BlockSpec((tm, D), lambda i, j: (i, 0)),
            pl.BlockSpec((1, D), lambda i, j: (0, 0)),
            pl.BlockSpec((1, D), lambda i, j: (0, 0)),
            pl.BlockSpec((D, tf), lambda i, j: (0, j)),
            pl.BlockSpec((D, tf), lambda i, j: (0, j)),
            pl.BlockSpec((tf, D), lambda i, j: (j, 0)),
        ],
        out_specs=pl.BlockSpec((tm, D), lambda i, j: (i, 0)),
        out_shape=jax.ShapeDtypeStruct((M, D), F32),
        scratch_shapes=[pltpu.VMEM((tm, D), BF16), pltpu.VMEM((tm, D), F32)],
        compiler_params=pltpu.CompilerParams(
            dimension_semantics=("parallel", "arbitrary"), vmem_limit_bytes=VMEM_LIMIT_BYTES),
        name="ffn_block",
    )(x, pre_g, post_g, wg, wu, wd)


def _inproj_kernel(nb_hy, nb_qkv, x_ref, g_ref, w_ref, hy_ref, qkv_ref, gate_ref, xn_ref):
    j = pl.program_id(1)

    @pl.when(j == 0)
    def _():
        xn_ref[...] = _rms(x_ref[...], g_ref[...]).astype(BF16)

    res = jnp.dot(xn_ref[...], w_ref[...], preferred_element_type=F32)

    @pl.when(j < nb_hy)
    def _():
        hy_ref[...] = res.astype(hy_ref.dtype)

    @pl.when((j >= nb_hy) & (j < nb_hy + nb_qkv))
    def _():
        qkv_ref[...] = res.astype(qkv_ref.dtype)

    @pl.when(j >= nb_hy + nb_qkv)
    def _():
        gate_ref[...] = res.astype(gate_ref.dtype)


def norm_inproj(x, g, w, *, tm=1024, tn=512):
    M, D = x.shape
    n_hy, n_qkv, n_gate = 3 * D_HYENA, 3 * D_ATT, N_BRANCH * D_MODEL
    nb_hy, nb_qkv, nb_gate = n_hy // tn, n_qkv // tn, n_gate // tn
    assert w.shape[1] == (nb_hy + nb_qkv + nb_gate) * tn
    return pl.pallas_call(
        partial(_inproj_kernel, nb_hy, nb_qkv),
        grid=(M // tm, nb_hy + nb_qkv + nb_gate),
        in_specs=[
            pl.BlockSpec((tm, D), lambda i, j: (i, 0)),
            pl.BlockSpec((1, D), lambda i, j: (0, 0)),
            pl.BlockSpec((D, tn), lambda i, j: (0, j)),
        ],
        out_specs=[
            pl.BlockSpec((tm, tn), lambda i, j: (i, jnp.minimum(j, nb_hy - 1))),
            pl.BlockSpec((tm, tn), lambda i, j: (i, jnp.clip(j - nb_hy, 0, nb_qkv - 1))),
            pl.BlockSpec((tm, tn), lambda i, j: (i, jnp.maximum(j - nb_hy - nb_qkv, 0))),
        ],
        out_shape=[
            jax.ShapeDtypeStruct((M, n_hy), F32),
            jax.ShapeDtypeStruct((M, n_qkv), F32),
            jax.ShapeDtypeStruct((M, n_gate), F32),
        ],
        scratch_shapes=[pltpu.VMEM((tm, D), BF16)],
        compiler_params=pltpu.CompilerParams(
            dimension_semantics=("parallel", "arbitrary"), vmem_limit_bytes=VMEM_LIMIT_BYTES),
        name="norm_inproj",
    )(x, g, w)


def _merge_kernel(x_ref, a_ref, b_ref, ga_ref, gb_ref, post_g_ref, whp_ref, wap_ref, wo_ref, o_ref):
    a = jnp.dot(a_ref[...], whp_ref[...], preferred_element_type=F32)
    b = jnp.dot(b_ref[...], wap_ref[...], preferred_element_type=F32)
    merged = (jax.nn.sigmoid(ga_ref[...]) * a + jax.nn.sigmoid(gb_ref[...]) * b).astype(BF16)
    mix = jnp.dot(merged, wo_ref[...], preferred_element_type=F32)
    o_ref[...] = x_ref[...] + _rms(mix, post_g_ref[...])


def merge_out(x, a_in, b_in, gates, post_g, whp, wap, wo, *, tm=256):
    M, D = x.shape
    const = lambda i: (0, 0)
    return pl.pallas_call(
        _merge_kernel,
        grid=(M // tm,),
        in_specs=[
            pl.BlockSpec((tm, D), lambda i: (i, 0)),
            pl.BlockSpec((tm, a_in.shape[1]), lambda i: (i, 0)),
            pl.BlockSpec((tm, b_in.shape[1]), lambda i: (i, 0)),
            pl.BlockSpec((tm, D), lambda i: (i, 0)),
            pl.BlockSpec((tm, D), lambda i: (i, 1)),
            pl.BlockSpec((1, D), const),
            pl.BlockSpec(whp.shape, const),
            pl.BlockSpec(wap.shape, const),
            pl.BlockSpec(wo.shape, const),
        ],
        out_specs=pl.BlockSpec((tm, D), lambda i: (i, 0)),
        out_shape=jax.ShapeDtypeStruct((M, D), F32),
        compiler_params=pltpu.CompilerParams(
            dimension_semantics=("parallel",), vmem_limit_bytes=VMEM_LIMIT_BYTES),
        name="merge_out",
    )(x, a_in, b_in, gates, gates, post_g, whp, wap, wo)


def short_conv(x, w, b):
    L = x.shape[1]
    p = SHORT_CONV // 2
    xp = jnp.pad(x, ((0, 0), (p, p), (0, 0)))
    y = xp[:, 0:L] * w[0]
    for i in range(1, SHORT_CONV):
        y = y + xp[:, i:i + L] * w[i]
    return y + b


def hyena_filters(L, fw1, fb1, fw2, fb2, fw3, fb3, fw4, ffreq):
    t = jnp.linspace(0.0, 1.0, L, dtype=F32)[:, None]
    bands = (FILTER_EMB - 1) // 2
    w = (2.0 * math.pi / L) * jnp.arange(L, dtype=F32)[:, None]
    f = jnp.linspace(1e-4, bands - 1, bands, dtype=F32)[None, :]
    z = jnp.concatenate([t, jnp.cos(f * w), -jnp.sin(f * w)], axis=-1)
    h = jnp.sin(ffreq * (z @ fw1 + fb1))
    h = jnp.sin(ffreq * (h @ fw2 + fb2))
    h = jnp.sin(ffreq * (h @ fw3 + fb3))
    h = h @ fw4
    max_decay = math.log(DECAY_TARGET) / FAST_DECAY_PCT
    min_decay = math.log(DECAY_TARGET) / SLOW_DECAY_PCT
    deltas = jnp.abs(jnp.linspace(min_decay, max_decay, D_HYENA, dtype=F32))
    decay = jnp.exp(-t * deltas)
    return h.reshape(L, HYENA_ORDER, 2, D_HYENA) * decay[:, None, None, :]


def two_sided_longconv(z, h_fwd, h_bwd, bias):
    L, C = h_fwd.shape
    k = jnp.concatenate([h_fwd, jnp.zeros((1, C), F32), h_bwd[:0:-1]], axis=0)
    kf = jnp.fft.rfft(k, axis=0)
    zf = jnp.fft.rfft(z, n=2 * L, axis=1)
    y = jnp.fft.irfft(zf * kf[None], n=2 * L, axis=1)[:, :L]
    return y + z * bias


def hyena_mixer(hy_in, conv_w, conv_b, filt, hy_bias):
    L = hy_in.shape[1]
    zs = short_conv(hy_in, conv_w, conv_b)
    v, x1, x2 = jnp.split(zs, 3, axis=-1)
    h = hyena_filters(L, *filt)
    y = v
    for n, gate in enumerate((x1, x2)):
        y = gate * two_sided_longconv(y, h[:, n, 0], h[:, n, 1], hy_bias[n])
    return y


def rope_partial(x, pos):
    half = ROPE_DIM // 2
    inv_freq = jnp.power(ROPE_THETA, -jnp.arange(half, dtype=F32) / half)
    ang = pos[:, None] * inv_freq[None, :]
    shape = (1, x.shape[1]) + (1,) * (x.ndim - 3) + (half,)
    cos = jnp.cos(ang).reshape(shape)
    sin = jnp.sin(ang).reshape(shape)
    xr = x[..., :ROPE_DIM]
    xa, xb = xr[..., :half], xr[..., half:]
    rot = jnp.concatenate([xa * cos - xb * sin, xb * cos + xa * sin], axis=-1)
    return jnp.concatenate([rot, x[..., ROPE_DIM:]], axis=-1)


def dilated_window_attention(q, k, v, window, dilation):
    Bsz, L, H, Dh = q.shape
    radius = window // (2 * dilation)
    blk = radius
    span = dilation * blk
    Lp = -(-L // span) * span
    Ls = Lp // dilation
    nb = Ls // blk

    def to_blocks(t):
        t = jnp.pad(t, ((0, 0), (0, Lp - L), (0, 0), (0, 0)))
        t = t.reshape(Bsz, Ls, dilation, H, Dh).transpose(0, 2, 1, 3, 4)
        return t.reshape(Bsz, dilation, nb, blk, H, Dh)

    def neighbours(t):
        tp = jnp.pad(t, ((0, 0), (0, 0), (1, 1), (0, 0), (0, 0), (0, 0)))
        return jnp.concatenate([tp[:, :, :-2], tp[:, :, 1:-1], tp[:, :, 2:]], axis=3)

    qb = to_blocks(q)
    kn = neighbours(to_blocks(k))
    vn = neighbours(to_blocks(v))
    scores = jnp.einsum('brnqhd,brnkhd->brnhqk', qb, kn, preferred_element_type=F32)
    qi = jnp.arange(blk)
    ki = jnp.arange(3 * blk) - blk
    band = jnp.abs(ki[None, :] - qi[:, None]) <= radius
    strided_key = jnp.arange(nb)[:, None] * blk + ki[None, :]
    key_pos = strided_key[None] * dilation + jnp.arange(dilation)[:, None, None]
    valid = (strided_key >= 0)[None] & (key_pos < L)
    mask = band[None, None] & valid[:, :, None, :]
    scores = jnp.where(mask[None, :, :, None], scores, NEG_INF)
    m = jnp.max(scores, axis=-1, keepdims=True)
    p = jnp.exp(scores - m)
    s = jnp.sum(p, axis=-1, keepdims=True)
    out = jnp.einsum('brnhqk,brnkhd->brnqhd', (p / s), vn, preferred_element_type=F32)
    lse = (m + jnp.log(s))[..., 0]
    out = out.reshape(Bsz, dilation, Ls, H, Dh).transpose(0, 2, 1, 3, 4).reshape(Bsz, Lp, H, Dh)[:, :L]
    lse = lse.transpose(0, 1, 2, 4, 3).reshape(Bsz, dilation, Ls, H).transpose(0, 2, 1, 3).reshape(Bsz, Lp, H)[:, :L]
    return out, lse


def attention_branch(qkv, L):
    pos = jnp.arange(L, dtype=F32)
    q = rope_partial(qkv[:, :, 0], pos) * (HEAD_DIM ** -0.5)
    k = rope_partial(qkv[:, :, 1], pos)
    v = qkv[:, :, 2]
    outs, lses = [], []
    for g, (window, dilation) in enumerate(ATT_GROUPS):
        o, s = dilated_window_attention(q[:, :, g], k[:, :, g], v[:, :, g], window, dilation)
        outs.append(o)
        lses.append(s)
    wts = jax.nn.softmax(jnp.stack(lses, axis=0), axis=0)
    return jnp.sum(wts[..., None] * jnp.stack(outs, axis=0), axis=0)


def _layer(x, p):
    Bsz, L, D = x.shape
    M = Bsz * L
    x0 = x.reshape(M, D)
    x1 = ffn_block(x0, p['ffn1_pre_g'], p['ffn1_post_g'], p['ffn1_w_gate'], p['ffn1_w_up'], p['ffn1_w_down'])
    hy_in, qkv, gates = norm_inproj(x1, p['mix_pre_g'], p['w_in'])
    hy_in = hy_in.reshape(Bsz, L, 3 * D_HYENA)
    qkv = qkv.reshape(Bsz, L, 3, N_GROUPS, HEADS_PER_GROUP, HEAD_DIM)
    a = hyena_mixer(hy_in, p['hy_conv_w'], p['hy_conv_b'], p['filt'], p['hy_bias'])
    att = attention_branch(qkv, L)
    a2 = a.reshape(M, D_HYENA).astype(BF16)
    b2 = att.reshape(M, D_ATT_OUT).astype(BF16)
    x2 = merge_out(x1, a2, b2, gates, p['mix_post_g'], p['w_hy_proj'], p['w_att_proj'], p['w_out'])
    x3 = ffn_block(x2, p['ffn2_pre_g'], p['ffn2_post_g'], p['ffn2_w_gate'], p['ffn2_w_up'], p['ffn2_w_down'])
    return x3.reshape(Bsz, L, D)


def kernel(x_prompt, x_sample, ffn1_pre_g, ffn1_post_g, ffn1_w_gate, ffn1_w_up, ffn1_w_down, mix_pre_g, mix_post_g, w_in, hy_conv_w, hy_conv_b, filt_w1, filt_b1, filt_w2, filt_b2, filt_w3, filt_b3, filt_w4, filt_freq, hy_bias, w_hy_proj, w_att_proj, w_out, ffn2_pre_g, ffn2_post_g, ffn2_w_gate, ffn2_w_up, ffn2_w_down):
    assert ffn1_w_gate.shape[0] == 1
    p = {
        'ffn1_pre_g': ffn1_pre_g, 'ffn1_post_g': ffn1_post_g,
        'ffn1_w_gate': ffn1_w_gate[0].astype(BF16), 'ffn1_w_up': ffn1_w_up[0].astype(BF16),
        'ffn1_w_down': ffn1_w_down[0].astype(BF16),
        'mix_pre_g': mix_pre_g, 'mix_post_g': mix_post_g,
        'w_in': w_in[0].astype(BF16),
        'hy_conv_w': hy_conv_w[0], 'hy_conv_b': hy_conv_b[0],
        'filt': (filt_w1[0], filt_b1[0], filt_w2[0], filt_b2[0], filt_w3[0], filt_b3[0], filt_w4[0], filt_freq[0]),
        'hy_bias': hy_bias[0],
        'w_hy_proj': w_hy_proj[0].astype(BF16), 'w_att_proj': w_att_proj[0].astype(BF16),
        'w_out': w_out[0].astype(BF16),
        'ffn2_pre_g': ffn2_pre_g, 'ffn2_post_g': ffn2_post_g,
        'ffn2_w_gate': ffn2_w_gate[0].astype(BF16), 'ffn2_w_up': ffn2_w_up[0].astype(BF16),
        'ffn2_w_down': ffn2_w_down[0].astype(BF16),
    }
    return (_layer(x_prompt, p), _layer(x_sample, p))
```

```python
import math
from functools import partial

import numpy as np
import jax
import jax.numpy as jnp
from jax import lax
from jax.experimental import pallas as pl
from jax.experimental.pallas import tpu as pltpu

D_MODEL = 2048
D_HYENA = 1024
HYENA_ORDER = 2
SHORT_CONV = 3
FILTER_EMB = 33
FILTER_BANDS = (FILTER_EMB - 1) // 2
FAST_DECAY_PCT = 0.3
SLOW_DECAY_PCT = 1.5
DECAY_TARGET = 1e-2
HEAD_DIM = 128
HEADS_PER_GROUP = 4
ATT_GROUPS = ((128, 1), (512, 4), (2048, 16))
N_GROUPS = len(ATT_GROUPS)
D_ATT = N_GROUPS * HEADS_PER_GROUP * HEAD_DIM
D_ATT_OUT = HEADS_PER_GROUP * HEAD_DIM
ROPE_DIM = HEAD_DIM // 4
ROPE_THETA = 500000.0
N_BRANCH = 2
D_IN_PROJ = 3 * D_HYENA + 3 * D_ATT + N_BRANCH * D_MODEL
D_FF = 5632
EPS = 1e-6
NEG_INF = -1e30

LANES = 128
FFT_INNER = 256
VMEM_LIMIT_BYTES = 56 * 1024 * 1024
BF16 = jnp.bfloat16
F32 = jnp.float32
HIGHEST = lax.Precision.HIGHEST


def _rms(x, g):
    return x * lax.rsqrt(jnp.mean(x * x, axis=-1, keepdims=True) + EPS) * g


def _dot(a, b):
    return jnp.dot(a, b, preferred_element_type=F32)


def _params(*sem):
    return pltpu.CompilerParams(dimension_semantics=sem, vmem_limit_bytes=VMEM_LIMIT_BYTES)


def _ffn_kernel(x_ref, pre_g_ref, post_g_ref, wg_ref, wu_ref, wd_ref, o_ref, xn_ref, acc_ref):
    j = pl.program_id(1)

    @pl.when(j == 0)
    def _():
        xn_ref[...] = _rms(x_ref[...], pre_g_ref[...]).astype(BF16)
        acc_ref[...] = jnp.zeros_like(acc_ref)

    xn = xn_ref[...]
    gate = _dot(xn, wg_ref[...])
    up = _dot(xn, wu_ref[...])
    h = (gate * jax.nn.sigmoid(gate) * up).astype(BF16)
    acc_ref[...] += _dot(h, wd_ref[...])

    @pl.when(j == pl.num_programs(1) - 1)
    def _():
        o_ref[...] = x_ref[...] + 0.5 * _rms(acc_ref[...], post_g_ref[...])


def ffn_block(x, pre_g, post_g, wg, wu, wd, *, tm=512, tf=512):
    M, D = x.shape
    FF = wg.shape[1]
    return pl.pallas_call(
        _ffn_kernel,
        grid=(M // tm, FF // tf),
        in_specs=[
            pl.BlockSpec((tm, D), lambda i, j: (i, 0)),
            pl.BlockSpec((1, D), lambda i, j: (0, 0)),
            pl.BlockSpec((1, D), lambda i, j: (0, 0)),
            pl.BlockSpec((D, tf), lambda i, j: (0, j)),
            pl.BlockSpec((D, tf), lambda i, j: (0, j)),
            pl.BlockSpec((tf, D), lambda i, j: (j, 0)),
        ],
        out_specs=pl.BlockSpec((tm, D), lambda i, j: (i, 0)),
        out_shape=jax.ShapeDtypeStruct((M, D), F32),
        scratch_shapes=[pltpu.VMEM((tm, D), BF16), pltpu.VMEM((tm, D), F32)],
        compiler_params=_params("parallel", "arbitrary"),
        name="ffn_block",
    )(x, pre_g, post_g, wg, wu, wd)


def rope_tables(L):
    half = ROPE_DIM // 2
    inv_freq = jnp.power(ROPE_THETA, -jnp.arange(half, dtype=F32) / half)
    ang = jnp.arange(L, dtype=F32)[:, None] * inv_freq[None, :]
    cos, sin = jnp.cos(ang), jnp.sin(ang)
    rest = HEAD_DIM - ROPE_DIM
    c = jnp.concatenate([cos, cos, jnp.ones((L, rest), F32)], axis=1)
    s_lo = jnp.concatenate([-sin, jnp.zeros((L, half + rest), F32)], axis=1)
    s_hi = jnp.concatenate([jnp.zeros((L, half), F32), sin, jnp.zeros((L, rest), F32)], axis=1)
    return c, s_lo, s_hi


def _inproj_kernel(nb_hy, nb_att, x_ref, g_ref, w_ref, c_ref, slo_ref, shi_ref,
                   hy_ref, qkv_ref, gate_ref, xn_ref):
    j = pl.program_id(1)
    tn = w_ref.shape[1]
    heads = tn // HEAD_DIM
    half = ROPE_DIM // 2

    @pl.when(j == 0)
    def _():
        xn_ref[...] = _rms(x_ref[...], g_ref[...]).astype(BF16)

    res = _dot(xn_ref[...], w_ref[...])

    @pl.when(j < nb_hy)
    def _():
        hy_ref[...] = res.astype(hy_ref.dtype)

    @pl.when((j >= nb_hy) & (j < nb_hy + 2 * nb_att))
    def _():
        wide = lambda t_ref: jnp.concatenate([t_ref[...]] * heads, axis=1)
        rot = (res * wide(c_ref) + pltpu.roll(res, tn - half, 1) * wide(slo_ref)
               + pltpu.roll(res, half, 1) * wide(shi_ref))
        scale = jnp.where(j < nb_hy + nb_att, HEAD_DIM ** -0.5, 1.0)
        qkv_ref[...] = (rot * scale).astype(qkv_ref.dtype)

    @pl.when((j >= nb_hy + 2 * nb_att) & (j < nb_hy + 3 * nb_att))
    def _():
        qkv_ref[...] = res.astype(qkv_ref.dtype)

    @pl.when(j >= nb_hy + 3 * nb_att)
    def _():
        gate_ref[...] = res.astype(gate_ref.dtype)


def norm_inproj(x, g, w, L, *, tm=1024, tn=512):
    M, D = x.shape
    n_hy, n_qkv, n_gate = 3 * D_HYENA, 3 * D_ATT, N_BRANCH * D_MODEL
    nb_hy, nb_att, nb_gate = n_hy // tn, D_ATT // tn, n_gate // tn
    nb_qkv = 3 * nb_att
    assert w.shape[1] == (nb_hy + nb_qkv + nb_gate) * tn and tn % HEAD_DIM == 0 and L % tm == 0
    rope_spec = pl.BlockSpec((tm, HEAD_DIM), lambda i, j: (i % (L // tm), 0))
    return pl.pallas_call(
        partial(_inproj_kernel, nb_hy, nb_att),
        grid=(M // tm, nb_hy + nb_qkv + nb_gate),
        in_specs=[
            pl.BlockSpec((tm, D), lambda i, j: (i, 0)),
            pl.BlockSpec((1, D), lambda i, j: (0, 0)),
            pl.BlockSpec((D, tn), lambda i, j: (0, j)),
            rope_spec, rope_spec, rope_spec,
        ],
        out_specs=[
            pl.BlockSpec((tm, tn), lambda i, j: (i, jnp.minimum(j, nb_hy - 1))),
            pl.BlockSpec((tm, tn), lambda i, j: (i, jnp.clip(j - nb_hy, 0, nb_qkv - 1))),
            pl.BlockSpec((tm, tn), lambda i, j: (i, jnp.maximum(j - nb_hy - nb_qkv, 0))),
        ],
        out_shape=[
            jax.ShapeDtypeStruct((M, n_hy), F32),
            jax.ShapeDtypeStruct((M, n_qkv), BF16),
            jax.ShapeDtypeStruct((M, n_gate), F32),
        ],
        scratch_shapes=[pltpu.VMEM((tm, D), BF16)],
        compiler_params=_params("parallel", "arbitrary"),
        name="norm_inproj",
    )(x, g, w, *rope_tables(L))


ATT_RADIUS = 64
assert all(w // (2 * d) == ATT_RADIUS for w, d in ATT_GROUPS)


def _attn_kernel(T, Ls, q_ref, kp_ref, kc_ref, kn_ref, vp_ref, vc_ref, vn_ref, o_ref, lse_ref, kbuf, vbuf):
    t = pl.program_id(2)
    R = ATT_RADIUS
    SB = 2 * R
    for buf, prv, cur, nxt in ((kbuf, kp_ref, kc_ref, kn_ref), (vbuf, vp_ref, vc_ref, vn_ref)):
        buf[0:R] = prv[...]
        buf[R:R + T] = cur[...]
        buf[R + T:R + T + R] = nxt[...]
    qi = lax.broadcasted_iota(jnp.int32, (SB, 2 * SB), 0)
    kk = lax.broadcasted_iota(jnp.int32, (SB, 2 * SB), 1)
    band = (kk >= qi) & (kk <= qi + 2 * R)
    lane = lax.broadcasted_iota(jnp.int32, (SB, LANES), 1)
    for sb in range(T // SB):
        kpos = t * T + (sb * SB - R) + kk
        mask = band & (kpos >= 0) & (kpos < Ls)
        lse = jnp.zeros((SB, LANES), F32)
        for h in range(HEADS_PER_GROUP):
            cols = slice(h * HEAD_DIM, (h + 1) * HEAD_DIM)
            q = q_ref[sb * SB:(sb + 1) * SB, cols]
            k = kbuf[sb * SB:(sb + 2) * SB, cols]
            v = vbuf[sb * SB:(sb + 2) * SB, cols]
            s = lax.dot_general(q, k, (((1,), (1,)), ((), ())), preferred_element_type=F32)
            s = jnp.where(mask, s, NEG_INF)
            m = jnp.max(s, axis=1, keepdims=True)
            p = jnp.exp(s - m)
            l = jnp.sum(p, axis=1, keepdims=True)
            o = _dot(p.astype(BF16), v) / l
            o_ref[sb * SB:(sb + 1) * SB, cols] = o.astype(o_ref.dtype)
            lse = jnp.where(lane == h, m + jnp.log(l), lse)
        lse_ref[sb * SB:(sb + 1) * SB, :] = lse


def dilated_attention_group(qkv, Bsz, L, g, *, T=256):
    d = ATT_GROUPS[g][1]
    M = Bsz * L
    Ls = L // d
    R = ATT_RADIUS
    GW = HEADS_PER_GROUP * HEAD_DIM
    assert L % d == 0 and Ls % T == 0 and T % (2 * R) == 0
    nrb, nhb, hpt = Ls // T, Ls // R, T // R
    ncol = 3 * N_GROUPS
    view = qkv.reshape(M // d, d * 3 * D_ATT)

    def main(which):
        return pl.BlockSpec((T, GW), lambda b, r, t: (b * nrb + t, r * ncol + which * N_GROUPS + g))

    def prev(which):
        return pl.BlockSpec((R, GW), lambda b, r, t: (jnp.maximum(b * nhb + t * hpt - 1, b * nhb),
                                                      r * ncol + which * N_GROUPS + g))

    def nxt(which):
        return pl.BlockSpec((R, GW), lambda b, r, t: (jnp.minimum(b * nhb + (t + 1) * hpt, (b + 1) * nhb - 1),
                                                      r * ncol + which * N_GROUPS + g))

    o, lse = pl.pallas_call(
        partial(_attn_kernel, T, Ls),
        grid=(Bsz, d, nrb),
        in_specs=[main(0), prev(1), main(1), nxt(1), prev(2), main(2), nxt(2)],
        out_specs=[pl.BlockSpec((T, GW), lambda b, r, t: (b * nrb + t, r)),
                   pl.BlockSpec((T, LANES), lambda b, r, t: (b * nrb + t, r))],
        out_shape=[jax.ShapeDtypeStruct((M // d, d * GW), BF16),
                   jax.ShapeDtypeStruct((M // d, d * LANES), F32)],
        scratch_shapes=[pltpu.VMEM((T + 2 * R, GW), BF16), pltpu.VMEM((T + 2 * R, GW), BF16)],
        compiler_params=_params("parallel", "parallel", "arbitrary"),
        name=f"dilated_attention_g{g}",
    )(view, view, view, view, view, view, view)
    return o.reshape(M, GW), lse.reshape(M, LANES)


def _merge_kernel(x_ref, a_ref, o0_ref, o1_ref, o2_ref, l0_ref, l1_ref, l2_ref, ga_ref, gb_ref,
                  post_g_ref, whp_ref, wap_ref, wo_ref, o_ref):
    lses = [l0_ref[...], l1_ref[...], l2_ref[...]]
    mx = jnp.maximum(jnp.maximum(lses[0], lses[1]), lses[2])
    es = [jnp.exp(l - mx) for l in lses]
    den = es[0] + es[1] + es[2]
    wts = [e / den for e in es]
    outs = [o0_ref, o1_ref, o2_ref]
    heads = []
    for h in range(HEADS_PER_GROUP):
        cols = slice(h * HEAD_DIM, (h + 1) * HEAD_DIM)
        heads.append(sum(wts[g][:, h:h + 1] * outs[g][:, cols].astype(F32) for g in range(N_GROUPS)))
    att = jnp.concatenate(heads, axis=1).astype(BF16)
    a = _dot(a_ref[...], whp_ref[...])
    b = _dot(att, wap_ref[...])
    merged = (jax.nn.sigmoid(ga_ref[...]) * a + jax.nn.sigmoid(gb_ref[...]) * b).astype(BF16)
    mix = _dot(merged, wo_ref[...])
    o_ref[...] = x_ref[...] + _rms(mix, post_g_ref[...])


def merge_out(x, a_in, att_outs, att_lses, gates, post_g, whp, wap, wo, *, tm=256):
    M, D = x.shape
    const = lambda i: (0, 0)
    rows = lambda a: pl.BlockSpec((tm, a.shape[1]), lambda i: (i, 0))
    return pl.pallas_call(
        _merge_kernel,
        grid=(M // tm,),
        in_specs=[rows(x), rows(a_in)] + [rows(o) for o in att_outs] + [rows(l) for l in att_lses] + [
            pl.BlockSpec((tm, D), lambda i: (i, 0)),
            pl.BlockSpec((tm, D), lambda i: (i, 1)),
            pl.BlockSpec((1, D), const),
            pl.BlockSpec(whp.shape, const),
            pl.BlockSpec(wap.shape, const),
            pl.BlockSpec(wo.shape, const),
        ],
        out_specs=pl.BlockSpec((tm, D), lambda i: (i, 0)),
        out_shape=jax.ShapeDtypeStruct((M, D), F32),
        compiler_params=_params("parallel"),
        name="merge_out",
    )(x, a_in, *att_outs, *att_lses, gates, gates, post_g, whp, wap, wo)


def _dft_constants(L):
    N = 2 * L
    N2 = FFT_INNER
    N1 = N // N2
    h = N1 // 2
    idx1 = np.arange(N1)
    ang1 = -2.0 * np.pi * ((idx1[:, None] * idx1[None, :]) % N1) / N1
    f1r, f1i = np.cos(ang1), np.sin(ang1)
    s_data = np.block([[f1r[:, :h], -f1i[:, :h]], [f1i[:, :h], f1r[:, :h]]])
    s_filt = np.concatenate([f1r, f1i], axis=0)
    ar, ai = f1r[:h, :], -f1i[:h, :]
    t_fin = np.block([[ar, -ai], [ai, ar]])
    idx2 = np.arange(N2)
    ang2 = -2.0 * np.pi * ((idx2[:, None] * idx2[None, :]) % N2) / N2
    angt = -2.0 * np.pi * (idx1[:, None] * idx2[None, :]) / N
    return dict(
        N1=N1,
        s_data=jnp.asarray(s_data, BF16), s_filt=jnp.asarray(s_filt, BF16), t_fin=jnp.asarray(t_fin, BF16),
        f2r=jnp.asarray(np.cos(ang2), F32), f2i=jnp.asarray(np.sin(ang2), F32),
        twr=jnp.asarray(np.cos(angt).reshape(N1, 1, N2), F32),
        twi=jnp.asarray(np.sin(angt).reshape(N1, 1, N2), F32),
    )


def _shortconv_kernel(tl, L, x_ref, prev_ref, next_ref, w_ref, b_ref, o_ref):
    i = pl.program_id(1)
    x = x_ref[...]
    row = lax.broadcasted_iota(jnp.int32, x.shape, 0)
    pos0 = (i * tl) % L
    prev_row = jnp.where(pos0 == 0, 0.0, prev_ref[7:8, :])
    next_row = jnp.where(pos0 + tl == L, 0.0, next_ref[0:1, :])
    xm = jnp.where(row == 0, prev_row, pltpu.roll(x, 1, 0))
    xp = jnp.where(row == tl - 1, next_row, pltpu.roll(x, tl - 1, 0))
    y = xm * w_ref[0:1, :] + x * w_ref[1:2, :] + xp * w_ref[2:3, :] + b_ref[...]
    o_ref[0] = y.astype(o_ref.dtype)


def hyena_shortconv(hy, w, b, L, *, tl=512):
    M = hy.shape[0]
    C = hy.shape[1] // 3
    nblk8 = M // 8
    return pl.pallas_call(
        partial(_shortconv_kernel, tl, L),
        grid=(3, M // tl),
        in_specs=[
            pl.BlockSpec((tl, C), lambda j, i: (i, j)),
            pl.BlockSpec((8, C), lambda j, i: (jnp.maximum(i * (tl // 8) - 1, 0), j)),
            pl.BlockSpec((8, C), lambda j, i: (jnp.minimum((i + 1) * (tl // 8), nblk8 - 1), j)),
            pl.BlockSpec((SHORT_CONV, C), lambda j, i: (0, j)),
            pl.BlockSpec((1, C), lambda j, i: (0, j)),
        ],
        out_specs=pl.BlockSpec((1, tl, C), lambda j, i: (j, i, 0)),
        out_shape=jax.ShapeDtypeStruct((3, M, C), BF16),
        compiler_params=_params("parallel", "parallel"),
        name="hyena_shortconv",
    )(hy, hy, hy, w, b)


def _filter_kernel(L, tl, w1_ref, b1_ref, w2_ref, b2_ref, w3_ref, b3_ref, w4f_ref, w4b_ref,
                   freq_ref, fvec_ref, delta_ref, kf_ref):
    i = pl.program_id(0)
    C = delta_ref.shape[1] // 2
    row = i * tl + lax.broadcasted_iota(jnp.int32, (tl, LANES), 0)
    lane = lax.broadcasted_iota(jnp.int32, (tl, LANES), 1)
    row_wide = i * tl + lax.broadcasted_iota(jnp.int32, (tl, 2 * C), 0)
    freq = freq_ref[...]

    def mlp(pos, pos_wide, w4_ref):
        posf = pos.astype(F32)
        t = posf / (L - 1)
        a = fvec_ref[...] * ((2.0 * math.pi / L) * posf)
        feats = jnp.where(lane < FILTER_BANDS, jnp.cos(a),
                          jnp.where(lane < 2 * FILTER_BANDS, -jnp.sin(a),
                                    jnp.where(lane == 2 * FILTER_BANDS, t, 0.0)))
        h = jnp.sin(freq * (jnp.dot(feats, w1_ref[...], precision=HIGHEST, preferred_element_type=F32) + b1_ref[...]))
        h = jnp.sin(freq * (jnp.dot(h, w2_ref[...], precision=HIGHEST, preferred_element_type=F32) + b2_ref[...]))
        h = jnp.sin(freq * (jnp.dot(h, w3_ref[...], precision=HIGHEST, preferred_element_type=F32) + b3_ref[...]))
        h = jnp.dot(h, w4_ref[...], precision=HIGHEST, preferred_element_type=F32)
        t_wide = pos_wide.astype(F32) / (L - 1)
        return h * jnp.exp(-t_wide * delta_ref[...])

    hf = mlp(row, row_wide, w4f_ref)
    hb = mlp(L - row, L - row_wide, w4b_ref)
    hb = jnp.where(row_wide == 0, 0.0, hb)
    for n in range(HYENA_ORDER):
        kf_ref[n, 0] = hf[:, n * C:(n + 1) * C].astype(kf_ref.dtype)
        kf_ref[n, 1] = hb[:, n * C:(n + 1) * C].astype(kf_ref.dtype)


def hyena_filter_taps(L, fw1, fb1, fw2, fb2, fw3, fb3, fw4, ffreq, *, tl=256):
    C = D_HYENA
    H = fw2.shape[0]
    assert H <= LANES and 2 * FILTER_BANDS + 1 <= LANES

    def pad2(a, r, c):
        return jnp.zeros((r, c), F32).at[:a.shape[0], :a.shape[1]].set(a)

    w1 = jnp.concatenate([fw1[1:], fw1[:1]], axis=0)
    w1 = pad2(w1, LANES, LANES)
    w2 = pad2(fw2, LANES, LANES)
    w3 = pad2(fw3, LANES, LANES)
    w4 = fw4.reshape(H, HYENA_ORDER, 2, C)
    w4f = pad2(w4[:, :, 0].reshape(H, HYENA_ORDER * C), LANES, HYENA_ORDER * C)
    w4b = pad2(w4[:, :, 1].reshape(H, HYENA_ORDER * C), LANES, HYENA_ORDER * C)
    b1, b2, b3 = (pad2(b[None, :], 1, LANES) for b in (fb1, fb2, fb3))
    freq = pad2(ffreq[None, :], 1, LANES)
    bands = np.linspace(1e-4, FILTER_BANDS - 1, FILTER_BANDS, dtype=np.float32)
    fvec = np.zeros((1, LANES), np.float32)
    fvec[0, :FILTER_BANDS] = bands
    fvec[0, FILTER_BANDS:2 * FILTER_BANDS] = bands
    max_decay = math.log(DECAY_TARGET) / FAST_DECAY_PCT
    min_decay = math.log(DECAY_TARGET) / SLOW_DECAY_PCT
    deltas = np.abs(np.linspace(min_decay, max_decay, C, dtype=np.float32))
    delta2 = np.tile(deltas[None, :], (1, HYENA_ORDER))
    const = lambda i: (0, 0)
    args = (w1, b1, w2, b2, w3, b3, w4f, w4b, freq, jnp.asarray(fvec), jnp.asarray(delta2))
    return pl.pallas_call(
        partial(_filter_kernel, L, tl),
        grid=(L // tl,),
        in_specs=[pl.BlockSpec(a.shape, const) for a in args],
        out_specs=pl.BlockSpec((HYENA_ORDER, 2, tl, C), lambda i: (0, 0, i, 0)),
        out_shape=jax.ShapeDtypeStruct((HYENA_ORDER, 2, L, C), BF16),
        compiler_params=_params("parallel"),
        name="hyena_filter_taps",
    )(*args)


def _stage1_kernel(s_ref, x_ref, y_ref):
    n1 = s_ref.shape[1]
    x = x_ref[...].reshape(n1, x_ref.shape[-1])
    y = _dot(s_ref[...], x)
    y_ref[...] = y.reshape(y_ref.shape).astype(y_ref.dtype)


def fft_stage1(x, s, *, tc=4096):
    _, h, W = x.shape
    N1 = 2 * h
    return pl.pallas_call(
        _stage1_kernel,
        grid=(W // tc,),
        in_specs=[pl.BlockSpec((2 * N1, N1), lambda j: (0, 0)),
                  pl.BlockSpec((2, h, tc), lambda j: (0, 0, j))],
        out_specs=pl.BlockSpec((2, N1, tc), lambda j: (0, 0, j)),
        out_shape=jax.ShapeDtypeStruct((2, N1, W), BF16),
        compiler_params=_params("parallel"),
        name="fft_stage1",
    )(s, x)


def _inner_dft(f2r_ref, f2i_ref, twr_ref, twi_ref):
    twr, twi = twr_ref[0], twi_ref[0]
    f2r, f2i = f2r_ref[...], f2i_ref[...]
    return f2r * twr - f2i * twi, f2r * twi + f2i * twr


def _mid_filter_kernel(f2r_ref, f2i_ref, twr_ref, twi_ref, y_ref, h_ref):
    gr, gi = _inner_dft(f2r_ref, f2i_ref, twr_ref, twi_ref)
    grb, gib = gr.astype(BF16), gi.astype(BF16)
    yr, yi = y_ref[0, 0], y_ref[1, 0]
    h_ref[0, 0] = _dot(grb, yr) - _dot(gib, yi)
    h_ref[1, 0] = _dot(gib, yr) + _dot(grb, yi)


def _mid_kernel(inv_n, f2r_ref, f2i_ref, twr_ref, twi_ref, y_ref, h_ref, u_ref):
    gr, gi = _inner_dft(f2r_ref, f2i_ref, twr_ref, twi_ref)
    grb, gib = gr.astype(BF16), gi.astype(BF16)
    yr, yi = y_ref[0, 0], y_ref[1, 0]
    zr = _dot(grb, yr) - _dot(gib, yi)
    zi = _dot(gib, yr) + _dot(grb, yi)
    hr, hi = h_ref[0, 0], h_ref[1, 0]
    pr = (zr * hr - zi * hi).astype(BF16)
    pi = (zr * hi + zi * hr).astype(BF16)
    irb = (gr.T * inv_n).astype(BF16)
    iib = (gi.T * (-inv_n)).astype(BF16)
    u_ref[0, 0] = (_dot(irb, pr) - _dot(iib, pi)).astype(u_ref.dtype)
    u_ref[1, 0] = (_dot(iib, pr) + _dot(irb, pi)).astype(u_ref.dtype)


def _mid_specs(N1, C):
    N2 = FFT_INNER
    blk = pl.BlockSpec((2, 1, N2, C), lambda k: (0, k, 0, 0))
    consts = [pl.BlockSpec((N2, N2), lambda k: (0, 0)), pl.BlockSpec((N2, N2), lambda k: (0, 0)),
              pl.BlockSpec((1, 1, N2), lambda k: (k, 0, 0)), pl.BlockSpec((1, 1, N2), lambda k: (k, 0, 0))]
    return blk, consts


def fft_mid_filter(y, dc):
    _, N1, W = y.shape
    C = W // FFT_INNER
    blk, consts = _mid_specs(N1, C)
    return pl.pallas_call(
        _mid_filter_kernel,
        grid=(N1,),
        in_specs=consts + [blk],
        out_specs=blk,
        out_shape=jax.ShapeDtypeStruct((2, N1, FFT_INNER, C), F32),
        compiler_params=_params("parallel"),
        name="fft_mid_filter",
    )(dc['f2r'], dc['f2i'], dc['twr'], dc['twi'], y.reshape(2, N1, FFT_INNER, C))


def fft_mid(y, hspec, dc):
    _, N1, W = y.shape
    C = W // FFT_INNER
    blk, consts = _mid_specs(N1, C)
    u = pl.pallas_call(
        partial(_mid_kernel, 1.0 / (N1 * FFT_INNER)),
        grid=(N1,),
        in_specs=consts + [blk, blk],
        out_specs=blk,
        out_shape=jax.ShapeDtypeStruct((2, N1, FFT_INNER, C), BF16),
        compiler_params=_params("parallel"),
        name="fft_mid",
    )(dc['f2r'], dc['f2i'], dc['twr'], dc['twi'], y.reshape(2, N1, FFT_INNER, C), hspec)
    return u.reshape(2, N1, W)


def _final_kernel(t_ref, u_ref, z_ref, gate_ref, bias_ref, o_ref):
    n1 = t_ref.shape[0]
    u = u_ref[...].reshape(2 * n1, u_ref.shape[-1])
    conv = _dot(t_ref[...], u).reshape(o_ref.shape)
    z = z_ref[...].astype(F32)
    o_ref[...] = (gate_ref[...].astype(F32) * (conv + bias_ref[...] * z)).astype(o_ref.dtype)


def fft_final(u, t_fin, z, gate, bias_row, *, tc=4096):
    _, N1, W = u.shape
    h = N1 // 2
    half = pl.BlockSpec((2, h, tc), lambda j: (0, 0, j))
    return pl.pallas_call(
        _final_kernel,
        grid=(W // tc,),
        in_specs=[pl.BlockSpec((N1, 2 * N1), lambda j: (0, 0)),
                  pl.BlockSpec((2, N1, tc), lambda j: (0, 0, j)),
                  half, half,
                  pl.BlockSpec((1, 1, tc), lambda j: (0, 0, 0))],
        out_specs=half,
        out_shape=jax.ShapeDtypeStruct((2, h, W), BF16),
        compiler_params=_params("parallel"),
        name="fft_final",
    )(t_fin, u, z, gate, bias_row)


def hyena_branch(hy, L, p, *, tc=4096):
    M = hy.shape[0]
    assert M == 2 * L, "the batch pair rides as real/imaginary parts"
    C = hy.shape[1] // 3
    dc = _dft_constants(L)
    N1 = dc['N1']
    W = FFT_INNER * C
    taps = hyena_filter_taps(L, *p['filt'])
    zs = hyena_shortconv(hy, p['hy_conv_w'], p['hy_conv_b'], L).reshape(3, 2, N1 // 2, W)
    z = zs[0]
    for n in range(HYENA_ORDER):
        hspec = fft_mid_filter(fft_stage1(taps[n].reshape(2, N1 // 2, W), dc['s_filt'], tc=tc), dc)
        u = fft_mid(fft_stage1(z, dc['s_data'], tc=tc), hspec, dc)
        bias_row = jnp.tile(p['hy_bias'][n], tc // C).reshape(1, 1, tc)
        z = fft_final(u, dc['t_fin'], z, zs[n + 1], bias_row, tc=tc)
    return z.reshape(M, C)


def _layer(x, p):
    Bsz, L, D = x.shape
    M = Bsz * L
    x0 = x.reshape(M, D)
    x1 = ffn_block(x0, p['ffn1_pre_g'], p['ffn1_post_g'], p['ffn1_w_gate'], p['ffn1_w_up'], p['ffn1_w_down'])
    hy_in, qkv, gates = norm_inproj(x1, p['mix_pre_g'], p['w_in'], L)
    a2 = hyena_branch(hy_in, L, p)
    att = [dilated_attention_group(qkv, Bsz, L, g) for g in range(N_GROUPS)]
    x2 = merge_out(x1, a2, [o for o, _ in att], [l for _, l in att], gates, p['mix_post_g'],
                   p['w_hy_proj'], p['w_att_proj'], p['w_out'])
    x3 = ffn_block(x2, p['ffn2_pre_g'], p['ffn2_post_g'], p['ffn2_w_gate'], p['ffn2_w_up'], p['ffn2_w_down'])
    return x3.reshape(Bsz, L, D)


def kernel(x_prompt, x_sample, ffn1_pre_g, ffn1_post_g, ffn1_w_gate, ffn1_w_up, ffn1_w_down, mix_pre_g, mix_post_g, w_in, hy_conv_w, hy_conv_b, filt_w1, filt_b1, filt_w2, filt_b2, filt_w3, filt_b3, filt_w4, filt_freq, hy_bias, w_hy_proj, w_att_proj, w_out, ffn2_pre_g, ffn2_post_g, ffn2_w_gate, ffn2_w_up, ffn2_w_down):
    assert ffn1_w_gate.shape[0] == 1
    p = {
        'ffn1_pre_g': ffn1_pre_g, 'ffn1_post_g': ffn1_post_g,
        'ffn1_w_gate': ffn1_w_gate[0].astype(BF16), 'ffn1_w_up': ffn1_w_up[0].astype(BF16),
        'ffn1_w_down': ffn1_w_down[0].astype(BF16),
        'mix_pre_g': mix_pre_g, 'mix_post_g': mix_post_g,
        'w_in': w_in[0].astype(BF16),
        'hy_conv_w': hy_conv_w[0], 'hy_conv_b': hy_conv_b,
        'filt': (filt_w1[0], filt_b1[0], filt_w2[0], filt_b2[0], filt_w3[0], filt_b3[0], filt_w4[0], filt_freq[0]),
        'hy_bias': hy_bias[0],
        'w_hy_proj': w_hy_proj[0].astype(BF16), 'w_att_proj': w_att_proj[0].astype(BF16),
        'w_out': w_out[0].astype(BF16),
        'ffn2_pre_g': ffn2_pre_g, 'ffn2_post_g': ffn2_post_g,
        'ffn2_w_gate': ffn2_w_gate[0].astype(BF16), 'ffn2_w_up': ffn2_w_up[0].astype(BF16),
        'ffn2_w_down': ffn2_w_down[0].astype(BF16),
    }
    return (_layer(x_prompt, p), _layer(x_sample, p))
```

```python
import math
from functools import partial

import numpy as np
import jax
import jax.numpy as jnp
from jax import lax
from jax.experimental import pallas as pl
from jax.experimental.pallas import tpu as pltpu

D_MODEL = 2048
D_HYENA = 1024
HYENA_ORDER = 2
SHORT_CONV = 3
FILTER_EMB = 33
FILTER_BANDS = (FILTER_EMB - 1) // 2
FAST_DECAY_PCT = 0.3
SLOW_DECAY_PCT = 1.5
DECAY_TARGET = 1e-2
HEAD_DIM = 128
HEADS_PER_GROUP = 4
ATT_GROUPS = ((128, 1), (512, 4), (2048, 16))
N_GROUPS = len(ATT_GROUPS)
D_ATT = N_GROUPS * HEADS_PER_GROUP * HEAD_DIM
D_ATT_OUT = HEADS_PER_GROUP * HEAD_DIM
ROPE_DIM = HEAD_DIM // 4
ROPE_THETA = 500000.0
N_BRANCH = 2
D_IN_PROJ = 3 * D_HYENA + 3 * D_ATT + N_BRANCH * D_MODEL
D_FF = 5632
EPS = 1e-6
NEG_INF = -1e30

LANES = 128
FFT_INNER = 256
VMEM_LIMIT_BYTES = 56 * 1024 * 1024
BF16 = jnp.bfloat16
F32 = jnp.float32
HIGHEST = lax.Precision.HIGHEST


def _rms(x, g):
    return x * lax.rsqrt(jnp.mean(x * x, axis=-1, keepdims=True) + EPS) * g


def _dot(a, b):
    return jnp.dot(a, b, preferred_element_type=F32)


def _params(*sem):
    return pltpu.CompilerParams(dimension_semantics=sem, vmem_limit_bytes=VMEM_LIMIT_BYTES)


def _ffn_kernel(x_ref, pre_g_ref, post_g_ref, wg_ref, wu_ref, wd_ref, o_ref, xn_ref, acc_ref):
    j = pl.program_id(1)

    @pl.when(j == 0)
    def _():
        xn_ref[...] = _rms(x_ref[...], pre_g_ref[...]).astype(BF16)
        acc_ref[...] = jnp.zeros_like(acc_ref)

    xn = xn_ref[...]
    gate = _dot(xn, wg_ref[...])
    up = _dot(xn, wu_ref[...])
    h = (gate * jax.nn.sigmoid(gate) * up).astype(BF16)
    acc_ref[...] += _dot(h, wd_ref[...])

    @pl.when(j == pl.num_programs(1) - 1)
    def _():
        o_ref[...] = x_ref[...] + 0.5 * _rms(acc_ref[...], post_g_ref[...])


def ffn_block(x, pre_g, post_g, wg, wu, wd, *, tm=512, tf=512):
    M, D = x.shape
    FF = wg.shape[1]
    return pl.pallas_call(
        _ffn_kernel,
        grid=(M // tm, FF // tf),
        in_specs=[
            pl.BlockSpec((tm, D), lambda i, j: (i, 0)),
            pl.BlockSpec((1, D), lambda i, j: (0, 0)),
            pl.BlockSpec((1, D), lambda i, j: (0, 0)),
            pl.BlockSpec((D, tf), lambda i, j: (0, j)),
            pl.BlockSpec((D, tf), lambda i, j: (0, j)),
            pl.BlockSpec((tf, D), lambda i, j: (j, 0)),
        ],
        out_specs=pl.BlockSpec((tm, D), lambda i, j: (i, 0)),
        out_shape=jax.ShapeDtypeStruct((M, D), F32),
        scratch_shapes=[pltpu.VMEM((tm, D), BF16), pltpu.VMEM((tm, D), F32)],
        compiler_params=_params("parallel", "arbitrary"),
        name="ffn_block",
    )(x, pre_g, post_g, wg, wu, wd)


def rope_tables(L):
    half = ROPE_DIM // 2
    inv_freq = jnp.power(ROPE_THETA, -jnp.arange(half, dtype=F32) / half)
    ang = jnp.arange(L, dtype=F32)[:, None] * inv_freq[None, :]
    cos, sin = jnp.cos(ang), jnp.sin(ang)
    rest = HEAD_DIM - ROPE_DIM
    c = jnp.concatenate([cos, cos, jnp.ones((L, rest), F32)], axis=1)
    s_lo = jnp.concatenate([-sin, jnp.zeros((L, half + rest), F32)], axis=1)
    s_hi = jnp.concatenate([jnp.zeros((L, half), F32), sin, jnp.zeros((L, rest), F32)], axis=1)
    return c, s_lo, s_hi


def _inproj_kernel(nb_hy, nb_att, x_ref, g_ref, w_ref, c_ref, slo_ref, shi_ref,
                   hy_ref, qkv_ref, gate_ref, xn_ref):
    j = pl.program_id(1)
    tn = w_ref.shape[1]
    heads = tn // HEAD_DIM
    half = ROPE_DIM // 2

    @pl.when(j == 0)
    def _():
        xn_ref[...] = _rms(x_ref[...], g_ref[...]).astype(BF16)

    res = _dot(xn_ref[...], w_ref[...])

    @pl.when(j < nb_hy)
    def _():
        hy_ref[...] = res.astype(hy_ref.dtype)

    @pl.when((j >= nb_hy) & (j < nb_hy + 2 * nb_att))
    def _():
        wide = lambda t_ref: jnp.concatenate([t_ref[...]] * heads, axis=1)
        rot = (res * wide(c_ref) + pltpu.roll(res, tn - half, 1) * wide(slo_ref)
               + pltpu.roll(res, half, 1) * wide(shi_ref))
        scale = jnp.where(j < nb_hy + nb_att, HEAD_DIM ** -0.5, 1.0)
        qkv_ref[...] = (rot * scale).astype(qkv_ref.dtype)

    @pl.when((j >= nb_hy + 2 * nb_att) & (j < nb_hy + 3 * nb_att))
    def _():
        qkv_ref[...] = res.astype(qkv_ref.dtype)

    @pl.when(j >= nb_hy + 3 * nb_att)
    def _():
        gate_ref[...] = res.astype(gate_ref.dtype)


def norm_inproj(x, g, w, L, *, tm=1024, tn=512):
    M, D = x.shape
    n_hy, n_qkv, n_gate = 3 * D_HYENA, 3 * D_ATT, N_BRANCH * D_MODEL
    nb_hy, nb_att, nb_gate = n_hy // tn, D_ATT // tn, n_gate // tn
    nb_qkv = 3 * nb_att
    assert w.shape[1] == (nb_hy + nb_qkv + nb_gate) * tn and tn % HEAD_DIM == 0 and L % tm == 0
    rope_spec = pl.BlockSpec((tm, HEAD_DIM), lambda i, j: (i % (L // tm), 0))
    return pl.pallas_call(
        partial(_inproj_kernel, nb_hy, nb_att),
        grid=(M // tm, nb_hy + nb_qkv + nb_gate),
        in_specs=[
            pl.BlockSpec((tm, D), lambda i, j: (i, 0)),
            pl.BlockSpec((1, D), lambda i, j: (0, 0)),
            pl.BlockSpec((D, tn), lambda i, j: (0, j)),
            rope_spec, rope_spec, rope_spec,
        ],
        out_specs=[
            pl.BlockSpec((tm, tn), lambda i, j: (i, jnp.minimum(j, nb_hy - 1))),
            pl.BlockSpec((tm, tn), lambda i, j: (i, jnp.clip(j - nb_hy, 0, nb_qkv - 1))),
            pl.BlockSpec((tm, tn), lambda i, j: (i, jnp.maximum(j - nb_hy - nb_qkv, 0))),
        ],
        out_shape=[
            jax.ShapeDtypeStruct((M, n_hy), F32),
            jax.ShapeDtypeStruct((M, n_qkv), BF16),
            jax.ShapeDtypeStruct((M, n_gate), F32),
        ],
        scratch_shapes=[pltpu.VMEM((tm, D), BF16)],
        compiler_params=_params("parallel", "arbitrary"),
        name="norm_inproj",
    )(x, g, w, *rope_tables(L))


ATT_RADIUS = 64
assert all(w // (2 * d) == ATT_RADIUS for w, d in ATT_GROUPS)


def _attn_kernel(T, Ls, q_ref, kp_ref, kc_ref, kn_ref, vp_ref, vc_ref, vn_ref, o_ref, lse_ref, kbuf, vbuf):
    t = pl.program_id(2)
    R = ATT_RADIUS
    SB = 2 * R
    for buf, prv, cur, nxt in ((kbuf, kp_ref, kc_ref, kn_ref), (vbuf, vp_ref, vc_ref, vn_ref)):
        buf[0:R] = prv[...]
        buf[R:R + T] = cur[...]
        buf[R + T:R + T + R] = nxt[...]
    qi = lax.broadcasted_iota(jnp.int32, (SB, 2 * SB), 0)
    kk = lax.broadcasted_iota(jnp.int32, (SB, 2 * SB), 1)
    band = (kk >= qi) & (kk <= qi + 2 * R)
    lane = lax.broadcasted_iota(jnp.int32, (SB, LANES), 1)
    for sb in range(T // SB):
        kpos = t * T + (sb * SB - R) + kk
        mask = band & (kpos >= 0) & (kpos < Ls)
        lse = jnp.zeros((SB, LANES), F32)
        for h in range(HEADS_PER_GROUP):
            cols = slice(h * HEAD_DIM, (h + 1) * HEAD_DIM)
            q = q_ref[sb * SB:(sb + 1) * SB, cols]
            k = kbuf[sb * SB:(sb + 2) * SB, cols]
            v = vbuf[sb * SB:(sb + 2) * SB, cols]
            s = lax.dot_general(q, k, (((1,), (1,)), ((), ())), preferred_element_type=F32)
            s = jnp.where(mask, s, NEG_INF)
            m = jnp.max(s, axis=1, keepdims=True)
            p = jnp.exp(s - m)
            l = jnp.sum(p, axis=1, keepdims=True)
            o = _dot(p.astype(BF16), v) / l
            o_ref[sb * SB:(sb + 1) * SB, cols] = o.astype(o_ref.dtype)
            lse = jnp.where(lane == h, m + jnp.log(l), lse)
        lse_ref[sb * SB:(sb + 1) * SB, :] = lse


SLAB = 16


def _softmax_heads(q, k, v, mask):
    lane = lax.broadcasted_iota(jnp.int32, (q.shape[0], LANES), 1)
    lse = jnp.zeros((q.shape[0], LANES), F32)
    outs = []
    for h in range(HEADS_PER_GROUP):
        cols = slice(h * HEAD_DIM, (h + 1) * HEAD_DIM)
        s = lax.dot_general(q[:, cols], k[:, cols], (((1,), (1,)), ((), ())), preferred_element_type=F32)
        s = jnp.where(mask, s, NEG_INF)
        m = jnp.max(s, axis=1, keepdims=True)
        p = jnp.exp(s - m)
        l = jnp.sum(p, axis=1, keepdims=True)
        outs.append(_dot(p.astype(BF16), v[:, cols]) / l)
        lse = jnp.where(lane == h, m + jnp.log(l), lse)
    return jnp.concatenate(outs, axis=1), lse


def _attn_slab_kernel(d, NS, Ls, q_ref, kp_ref, kc_ref, kn_ref, vp_ref, vc_ref, vn_ref, o_ref, lse_ref,
                      qs, ks, vs, os_, ls):
    t = pl.program_id(1)
    R = ATT_RADIUS
    SB = 2 * R
    J = SLAB // d
    NH = kp_ref.shape[0]
    QA = SB // J
    KA = QA + 2 * NH
    n_sb = NS // QA
    to_class_major = lambda ref: pltpu.einshape("abc->bac", ref[...])
    qs[...] = to_class_major(q_ref)
    for buf, prv, cur, nxt in ((ks, kp_ref, kc_ref, kn_ref), (vs, vp_ref, vc_ref, vn_ref)):
        buf[:, 0:NH] = to_class_major(prv)
        buf[:, NH:NH + NS] = to_class_major(cur)
        buf[:, NH + NS:NH + NS + NH] = to_class_major(nxt)
    qi = lax.broadcasted_iota(jnp.int32, (SB, 2 * SB), 0)
    kk = lax.broadcasted_iota(jnp.int32, (SB, 2 * SB), 1)
    sq_rel = J * (qi % QA) + qi // QA
    sk_rel = J * (kk % KA - NH) + kk // KA
    band = jnp.abs(sk_rel - sq_rel) <= R

    def body(it, carry):
        r = it // n_sb
        a0 = pl.multiple_of((it % n_sb) * QA, QA)
        sk = J * (t * NS + a0) + sk_rel
        mask = band & (sk >= 0) & (sk < Ls)
        gather = lambda buf, n: jnp.concatenate([buf[j * d + r, pl.ds(a0, n), :] for j in range(J)], axis=0)
        o, lse = _softmax_heads(gather(qs, QA), gather(ks, KA), gather(vs, KA), mask)
        o = o.astype(os_.dtype)
        for j in range(J):
            os_[j * d + r, pl.ds(a0, QA), :] = o[j * QA:(j + 1) * QA]
            ls[j * d + r, pl.ds(a0, QA), :] = lse[j * QA:(j + 1) * QA]
        return carry

    lax.fori_loop(0, d * n_sb, body, 0)
    o_ref[...] = pltpu.einshape("bac->abc", os_[...])
    lse_ref[...] = pltpu.einshape("bac->abc", ls[...])


def dilated_attention_slabs(qkv, Bsz, L, g, *, NS=128):
    d = ATT_GROUPS[g][1]
    M = Bsz * L
    R = ATT_RADIUS
    GW = HEADS_PER_GROUP * HEAD_DIM
    NH = R * d // SLAB
    TB = NS * SLAB
    assert SLAB % d == 0 and L % TB == 0 and NS % NH == 0 and (2 * R) % (SLAB // d) == 0
    nmb, nhb, hpm = L // TB, L // (NH * SLAB), NS // NH
    view = qkv.reshape(M // SLAB, SLAB, 3 * D_ATT)
    col = lambda which: which * N_GROUPS + g

    def main(which):
        return pl.BlockSpec((NS, SLAB, GW), lambda b, t: (b * nmb + t, 0, col(which)))

    def prev(which):
        return pl.BlockSpec((NH, SLAB, GW), lambda b, t: (jnp.maximum(b * nhb + t * hpm - 1, b * nhb), 0, col(which)))

    def nxt(which):
        return pl.BlockSpec((NH, SLAB, GW),
                            lambda b, t: (jnp.minimum(b * nhb + (t + 1) * hpm, (b + 1) * nhb - 1), 0, col(which)))

    o, lse = pl.pallas_call(
        partial(_attn_slab_kernel, d, NS, L // d),
        grid=(Bsz, nmb),
        in_specs=[main(0), prev(1), main(1), nxt(1), prev(2), main(2), nxt(2)],
        out_specs=[pl.BlockSpec((NS, SLAB, GW), lambda b, t: (b * nmb + t, 0, 0)),
                   pl.BlockSpec((NS, SLAB, LANES), lambda b, t: (b * nmb + t, 0, 0))],
        out_shape=[jax.ShapeDtypeStruct((M // SLAB, SLAB, GW), BF16),
                   jax.ShapeDtypeStruct((M // SLAB, SLAB, LANES), F32)],
        scratch_shapes=[pltpu.VMEM((SLAB, NS, GW), BF16),
                        pltpu.VMEM((SLAB, NS + 2 * NH, GW), BF16), pltpu.VMEM((SLAB, NS + 2 * NH, GW), BF16),
                        pltpu.VMEM((SLAB, NS, GW), BF16), pltpu.VMEM((SLAB, NS, LANES), F32)],
        compiler_params=_params("parallel", "arbitrary"),
        name=f"dilated_attention_g{g}",
    )(view, view, view, view, view, view, view)
    return o.reshape(M, GW), lse.reshape(M, LANES)


def dilated_attention_group(qkv, Bsz, L, g, *, T=256):
    d = ATT_GROUPS[g][1]
    M = Bsz * L
    Ls = L // d
    R = ATT_RADIUS
    GW = HEADS_PER_GROUP * HEAD_DIM
    assert L % d == 0 and Ls % T == 0 and T % (2 * R) == 0
    nrb, nhb, hpt = Ls // T, Ls // R, T // R
    ncol = 3 * N_GROUPS
    view = qkv.reshape(M // d, d * 3 * D_ATT)

    def main(which):
        return pl.BlockSpec((T, GW), lambda b, r, t: (b * nrb + t, r * ncol + which * N_GROUPS + g))

    def prev(which):
        return pl.BlockSpec((R, GW), lambda b, r, t: (jnp.maximum(b * nhb + t * hpt - 1, b * nhb),
                                                      r * ncol + which * N_GROUPS + g))

    def nxt(which):
        return pl.BlockSpec((R, GW), lambda b, r, t: (jnp.minimum(b * nhb + (t + 1) * hpt, (b + 1) * nhb - 1),
                                                      r * ncol + which * N_GROUPS + g))

    o, lse = pl.pallas_call(
        partial(_attn_kernel, T, Ls),
        grid=(Bsz, d, nrb),
        in_specs=[main(0), prev(1), main(1), nxt(1), prev(2), main(2), nxt(2)],
        out_specs=[pl.BlockSpec((T, GW), lambda b, r, t: (b * nrb + t, r)),
                   pl.BlockSpec((T, LANES), lambda b, r, t: (b * nrb + t, r))],
        out_shape=[jax.ShapeDtypeStruct((M // d, d * GW), BF16),
                   jax.ShapeDtypeStruct((M // d, d * LANES), F32)],
        scratch_shapes=[pltpu.VMEM((T + 2 * R, GW), BF16), pltpu.VMEM((T + 2 * R, GW), BF16)],
        compiler_params=_params("parallel", "parallel", "arbitrary"),
        name=f"dilated_attention_g{g}",
    )(view, view, view, view, view, view, view)
    return o.reshape(M, GW), lse.reshape(M, LANES)


def _merge_kernel(x_ref, a_ref, o0_ref, o1_ref, o2_ref, l0_ref, l1_ref, l2_ref, ga_ref, gb_ref,
                  post_g_ref, whp_ref, wap_ref, wo_ref, o_ref):
    lses = [l0_ref[...], l1_ref[...], l2_ref[...]]
    mx = jnp.maximum(jnp.maximum(lses[0], lses[1]), lses[2])
    es = [jnp.exp(l - mx) for l in lses]
    den = es[0] + es[1] + es[2]
    wts = [e / den for e in es]
    outs = [o0_ref, o1_ref, o2_ref]
    heads = []
    for h in range(HEADS_PER_GROUP):
        cols = slice(h * HEAD_DIM, (h + 1) * HEAD_DIM)
        heads.append(sum(wts[g][:, h:h + 1] * outs[g][:, cols].astype(F32) for g in range(N_GROUPS)))
    att = jnp.concatenate(heads, axis=1).astype(BF16)
    a = _dot(a_ref[...], whp_ref[...])
    b = _dot(att, wap_ref[...])
    merged = (jax.nn.sigmoid(ga_ref[...]) * a + jax.nn.sigmoid(gb_ref[...]) * b).astype(BF16)
    mix = _dot(merged, wo_ref[...])
    o_ref[...] = x_ref[...] + _rms(mix, post_g_ref[...])


def merge_out(x, a_in, att_outs, att_lses, gates, post_g, whp, wap, wo, *, tm=256):
    M, D = x.shape
    const = lambda i: (0, 0)
    rows = lambda a: pl.BlockSpec((tm, a.shape[1]), lambda i: (i, 0))
    return pl.pallas_call(
        _merge_kernel,
        grid=(M // tm,),
        in_specs=[rows(x), rows(a_in)] + [rows(o) for o in att_outs] + [rows(l) for l in att_lses] + [
            pl.BlockSpec((tm, D), lambda i: (i, 0)),
            pl.BlockSpec((tm, D), lambda i: (i, 1)),
            pl.BlockSpec((1, D), const),
            pl.BlockSpec(whp.shape, const),
            pl.BlockSpec(wap.shape, const),
            pl.BlockSpec(wo.shape, const),
        ],
        out_specs=pl.BlockSpec((tm, D), lambda i: (i, 0)),
        out_shape=jax.ShapeDtypeStruct((M, D), F32),
        compiler_params=_params("parallel"),
        name="merge_out",
    )(x, a_in, *att_outs, *att_lses, gates, gates, post_g, whp, wap, wo)


def _dft_constants(L):
    N = 2 * L
    N2 = FFT_INNER
    N1 = N // N2
    h = N1 // 2
    idx1 = np.arange(N1)
    ang1 = -2.0 * np.pi * ((idx1[:, None] * idx1[None, :]) % N1) / N1
    f1r, f1i = np.cos(ang1), np.sin(ang1)
    s_data = np.block([[f1r[:, :h], -f1i[:, :h]], [f1i[:, :h], f1r[:, :h]]])
    s_filt = np.concatenate([f1r, f1i], axis=0)
    ar, ai = f1r[:h, :], -f1i[:h, :]
    t_fin = np.block([[ar, -ai], [ai, ar]])
    idx2 = np.arange(N2)
    ang2 = -2.0 * np.pi * ((idx2[:, None] * idx2[None, :]) % N2) / N2
    angt = -2.0 * np.pi * (idx1[:, None] * idx2[None, :]) / N
    return dict(
        N1=N1,
        s_data=jnp.asarray(s_data, BF16), s_filt=jnp.asarray(s_filt, BF16), t_fin=jnp.asarray(t_fin, BF16),
        f2r=jnp.asarray(np.cos(ang2), F32), f2i=jnp.asarray(np.sin(ang2), F32),
        twr=jnp.asarray(np.cos(angt).reshape(N1, 1, N2), F32),
        twi=jnp.asarray(np.sin(angt).reshape(N1, 1, N2), F32),
    )


def _shortconv_kernel(tl, L, x_ref, prev_ref, next_ref, w_ref, b_ref, o_ref):
    i = pl.program_id(1)
    x = x_ref[...]
    row = lax.broadcasted_iota(jnp.int32, x.shape, 0)
    pos0 = (i * tl) % L
    prev_row = jnp.where(pos0 == 0, 0.0, prev_ref[7:8, :])
    next_row = jnp.where(pos0 + tl == L, 0.0, next_ref[0:1, :])
    xm = jnp.where(row == 0, prev_row, pltpu.roll(x, 1, 0))
    xp = jnp.where(row == tl - 1, next_row, pltpu.roll(x, tl - 1, 0))
    y = xm * w_ref[0:1, :] + x * w_ref[1:2, :] + xp * w_ref[2:3, :] + b_ref[...]
    o_ref[0] = y.astype(o_ref.dtype)


def hyena_shortconv(hy, w, b, L, *, tl=512):
    M = hy.shape[0]
    C = hy.shape[1] // 3
    nblk8 = M // 8
    return pl.pallas_call(
        partial(_shortconv_kernel, tl, L),
        grid=(3, M // tl),
        in_specs=[
            pl.BlockSpec((tl, C), lambda j, i: (i, j)),
            pl.BlockSpec((8, C), lambda j, i: (jnp.maximum(i * (tl // 8) - 1, 0), j)),
            pl.BlockSpec((8, C), lambda j, i: (jnp.minimum((i + 1) * (tl // 8), nblk8 - 1), j)),
            pl.BlockSpec((SHORT_CONV, C), lambda j, i: (0, j)),
            pl.BlockSpec((1, C), lambda j, i: (0, j)),
        ],
        out_specs=pl.BlockSpec((1, tl, C), lambda j, i: (j, i, 0)),
        out_shape=jax.ShapeDtypeStruct((3, M, C), BF16),
        compiler_params=_params("parallel", "parallel"),
        name="hyena_shortconv",
    )(hy, hy, hy, w, b)


def _filter_kernel(L, tl, w1_ref, b1_ref, w2_ref, b2_ref, w3_ref, b3_ref, w4f_ref, w4b_ref,
                   freq_ref, fvec_ref, delta_ref, kf_ref):
    i = pl.program_id(0)
    C = delta_ref.shape[1] // 2
    row = i * tl + lax.broadcasted_iota(jnp.int32, (tl, LANES), 0)
    lane = lax.broadcasted_iota(jnp.int32, (tl, LANES), 1)
    row_wide = i * tl + lax.broadcasted_iota(jnp.int32, (tl, 2 * C), 0)
    freq = freq_ref[...]

    def mlp(pos, pos_wide, w4_ref):
        posf = pos.astype(F32)
        t = posf / (L - 1)
        a = fvec_ref[...] * ((2.0 * math.pi / L) * posf)
        feats = jnp.where(lane < FILTER_BANDS, jnp.cos(a),
                          jnp.where(lane < 2 * FILTER_BANDS, -jnp.sin(a),
                                    jnp.where(lane == 2 * FILTER_BANDS, t, 0.0)))
        h = jnp.sin(freq * (jnp.dot(feats, w1_ref[...], precision=HIGHEST, preferred_element_type=F32) + b1_ref[...]))
        h = jnp.sin(freq * (jnp.dot(h, w2_ref[...], precision=HIGHEST, preferred_element_type=F32) + b2_ref[...]))
        h = jnp.sin(freq * (jnp.dot(h, w3_ref[...], precision=HIGHEST, preferred_element_type=F32) + b3_ref[...]))
        h = jnp.dot(h, w4_ref[...], precision=HIGHEST, preferred_element_type=F32)
        t_wide = pos_wide.astype(F32) / (L - 1)
        return h * jnp.exp(-t_wide * delta_ref[...])

    hf = mlp(row, row_wide, w4f_ref)
    hb = mlp(L - row, L - row_wide, w4b_ref)
    hb = jnp.where(row_wide == 0, 0.0, hb)
    for n in range(HYENA_ORDER):
        kf_ref[n, 0] = hf[:, n * C:(n + 1) * C].astype(kf_ref.dtype)
        kf_ref[n, 1] = hb[:, n * C:(n + 1) * C].astype(kf_ref.dtype)


def hyena_filter_taps(L, fw1, fb1, fw2, fb2, fw3, fb3, fw4, ffreq, *, tl=256):
    C = D_HYENA
    H = fw2.shape[0]
    assert H <= LANES and 2 * FILTER_BANDS + 1 <= LANES

    def pad2(a, r, c):
        return jnp.zeros((r, c), F32).at[:a.shape[0], :a.shape[1]].set(a)

    w1 = jnp.concatenate([fw1[1:], fw1[:1]], axis=0)
    w1 = pad2(w1, LANES, LANES)
    w2 = pad2(fw2, LANES, LANES)
    w3 = pad2(fw3, LANES, LANES)
    w4 = fw4.reshape(H, HYENA_ORDER, 2, C)
    w4f = pad2(w4[:, :, 0].reshape(H, HYENA_ORDER * C), LANES, HYENA_ORDER * C)
    w4b = pad2(w4[:, :, 1].reshape(H, HYENA_ORDER * C), LANES, HYENA_ORDER * C)
    b1, b2, b3 = (pad2(b[None, :], 1, LANES) for b in (fb1, fb2, fb3))
    freq = pad2(ffreq[None, :], 1, LANES)
    bands = np.linspace(1e-4, FILTER_BANDS - 1, FILTER_BANDS, dtype=np.float32)
    fvec = np.zeros((1, LANES), np.float32)
    fvec[0, :FILTER_BANDS] = bands
    fvec[0, FILTER_BANDS:2 * FILTER_BANDS] = bands
    max_decay = math.log(DECAY_TARGET) / FAST_DECAY_PCT
    min_decay = math.log(DECAY_TARGET) / SLOW_DECAY_PCT
    deltas = np.abs(np.linspace(min_decay, max_decay, C, dtype=np.float32))
    delta2 = np.tile(deltas[None, :], (1, HYENA_ORDER))
    const = lambda i: (0, 0)
    args = (w1, b1, w2, b2, w3, b3, w4f, w4b, freq, jnp.asarray(fvec), jnp.asarray(delta2))
    return pl.pallas_call(
        partial(_filter_kernel, L, tl),
        grid=(L // tl,),
        in_specs=[pl.BlockSpec(a.shape, const) for a in args],
        out_specs=pl.BlockSpec((HYENA_ORDER, 2, tl, C), lambda i: (0, 0, i, 0)),
        out_shape=jax.ShapeDtypeStruct((HYENA_ORDER, 2, L, C), BF16),
        compiler_params=_params("parallel"),
        name="hyena_filter_taps",
    )(*args)


FFT_ROWS = 16
FFT_COLS = 512


def _stage1_kernel(s_ref, x_ref, y_ref):
    _, h, rows, cols = x_ref.shape
    s = s_ref[...]
    xt = pltpu.einshape("abc->bac", x_ref[...].reshape(2 * h, rows, cols))
    yt = jnp.stack([_dot(s, xt[b]).astype(y_ref.dtype) for b in range(rows)], axis=0)
    y_ref[...] = pltpu.einshape("bac->abc", yt).reshape(y_ref.shape)


def fft_stage1(x, which, s):
    _, _, h, N2, C = x.shape
    N1 = 2 * h
    return pl.pallas_call(
        _stage1_kernel,
        grid=(N2 // FFT_ROWS, C // FFT_COLS),
        in_specs=[pl.BlockSpec((2 * N1, N1), lambda j, c: (0, 0)),
                  pl.BlockSpec((None, 2, h, FFT_ROWS, FFT_COLS), lambda j, c: (which, 0, 0, j, c))],
        out_specs=pl.BlockSpec((2, N1, FFT_ROWS, FFT_COLS), lambda j, c: (0, 0, j, c)),
        out_shape=jax.ShapeDtypeStruct((2, N1, N2, C), BF16),
        compiler_params=_params("parallel", "parallel"),
        name="fft_stage1",
    )(s, x)


def _inner_dft(f2r_ref, f2i_ref, twr_ref, twi_ref):
    twr, twi = twr_ref[0], twi_ref[0]
    f2r, f2i = f2r_ref[...], f2i_ref[...]
    return f2r * twr - f2i * twi, f2r * twi + f2i * twr


def _mid_filter_kernel(f2r_ref, f2i_ref, twr_ref, twi_ref, y_ref, h_ref):
    gr, gi = _inner_dft(f2r_ref, f2i_ref, twr_ref, twi_ref)
    grb, gib = gr.astype(BF16), gi.astype(BF16)
    yr, yi = y_ref[0, 0], y_ref[1, 0]
    h_ref[0, 0] = _dot(grb, yr) - _dot(gib, yi)
    h_ref[1, 0] = _dot(gib, yr) + _dot(grb, yi)


def _mid_kernel(inv_n, f2r_ref, f2i_ref, twr_ref, twi_ref, y_ref, h_ref, u_ref):
    gr, gi = _inner_dft(f2r_ref, f2i_ref, twr_ref, twi_ref)
    grb, gib = gr.astype(BF16), gi.astype(BF16)
    yr, yi = y_ref[0, 0], y_ref[1, 0]
    zr = _dot(grb, yr) - _dot(gib, yi)
    zi = _dot(gib, yr) + _dot(grb, yi)
    hr, hi = h_ref[0, 0], h_ref[1, 0]
    pr = (zr * hr - zi * hi).astype(BF16)
    pi = (zr * hi + zi * hr).astype(BF16)
    irb = (gr.T * inv_n).astype(BF16)
    iib = (gi.T * (-inv_n)).astype(BF16)
    u_ref[0, 0] = (_dot(irb, pr) - _dot(iib, pi)).astype(u_ref.dtype)
    u_ref[1, 0] = (_dot(iib, pr) + _dot(irb, pi)).astype(u_ref.dtype)


def _mid_specs(N1, C):
    N2 = FFT_INNER
    blk = pl.BlockSpec((2, 1, N2, C), lambda k: (0, k, 0, 0))
    consts = [pl.BlockSpec((N2, N2), lambda k: (0, 0)), pl.BlockSpec((N2, N2), lambda k: (0, 0)),
              pl.BlockSpec((1, 1, N2), lambda k: (k, 0, 0)), pl.BlockSpec((1, 1, N2), lambda k: (k, 0, 0))]
    return blk, consts


def fft_mid_filter(y, dc):
    _, N1, _, C = y.shape
    blk, consts = _mid_specs(N1, C)
    return pl.pallas_call(
        _mid_filter_kernel,
        grid=(N1,),
        in_specs=consts + [blk],
        out_specs=blk,
        out_shape=jax.ShapeDtypeStruct((2, N1, FFT_INNER, C), F32),
        compiler_params=_params("parallel"),
        name="fft_mid_filter",
    )(dc['f2r'], dc['f2i'], dc['twr'], dc['twi'], y)


def fft_mid(y, hspec, dc):
    _, N1, _, C = y.shape
    blk, consts = _mid_specs(N1, C)
    return pl.pallas_call(
        partial(_mid_kernel, 1.0 / (N1 * FFT_INNER)),
        grid=(N1,),
        in_specs=consts + [blk, blk],
        out_specs=blk,
        out_shape=jax.ShapeDtypeStruct((2, N1, FFT_INNER, C), BF16),
        compiler_params=_params("parallel"),
        name="fft_mid",
    )(dc['f2r'], dc['f2i'], dc['twr'], dc['twi'], y, hspec)


def _final_kernel(t_ref, u_ref, z_ref, gate_ref, bias_ref, o_ref):
    _, n1, rows, cols = u_ref.shape
    t = t_ref[...]
    ut = pltpu.einshape("abc->bac", u_ref[...].reshape(2 * n1, rows, cols))
    convt = jnp.stack([_dot(t, ut[b]) for b in range(rows)], axis=0)
    conv = pltpu.einshape("bac->abc", convt).reshape(o_ref.shape)
    z = z_ref[...].astype(F32)
    o_ref[...] = (gate_ref[...].astype(F32) * (conv + bias_ref[...] * z)).astype(o_ref.dtype)


def fft_final(u, t_fin, z, z_which, gate, gate_which, bias, bias_which):
    _, N1, N2, C = u.shape
    h = N1 // 2

    def half(which):
        return pl.BlockSpec((None, 2, h, FFT_ROWS, FFT_COLS), lambda j, c: (which, 0, 0, j, c))

    return pl.pallas_call(
        _final_kernel,
        grid=(N2 // FFT_ROWS, C // FFT_COLS),
        in_specs=[pl.BlockSpec((N1, 2 * N1), lambda j, c: (0, 0)),
                  pl.BlockSpec((2, N1, FFT_ROWS, FFT_COLS), lambda j, c: (0, 0, j, c)),
                  half(z_which), half(gate_which),
                  pl.BlockSpec((None, 1, FFT_COLS), lambda j, c: (bias_which, 0, c))],
        out_specs=half(0),
        out_shape=jax.ShapeDtypeStruct((1, 2, h, N2, C), BF16),
        compiler_params=_params("parallel", "parallel"),
        name="fft_final",
    )(t_fin, u, z, gate, bias.reshape(bias.shape[0], 1, C))


def hyena_branch(hy, L, p):
    M = hy.shape[0]
    assert M == 2 * L, "the batch pair rides as real/imaginary parts"
    C = hy.shape[1] // 3
    dc = _dft_constants(L)
    split = (2, dc['N1'] // 2, FFT_INNER, C)
    taps = hyena_filter_taps(L, *p['filt']).reshape(HYENA_ORDER, *split)
    zs = hyena_shortconv(hy, p['hy_conv_w'], p['hy_conv_b'], L).reshape(3, *split)
    z, z_which = zs, 0
    for n in range(HYENA_ORDER):
        hspec = fft_mid_filter(fft_stage1(taps, n, dc['s_filt']), dc)
        u = fft_mid(fft_stage1(z, z_which, dc['s_data']), hspec, dc)
        z, z_which = fft_final(u, dc['t_fin'], z, z_which, zs, n + 1, p['hy_bias'], n), 0
    return z.reshape(M, C)


def _layer(x, p):
    Bsz, L, D = x.shape
    M = Bsz * L
    x0 = x.reshape(M, D)
    x1 = ffn_block(x0, p['ffn1_pre_g'], p['ffn1_post_g'], p['ffn1_w_gate'], p['ffn1_w_up'], p['ffn1_w_down'])
    hy_in, qkv, gates = norm_inproj(x1, p['mix_pre_g'], p['w_in'], L)
    a2 = hyena_branch(hy_in, L, p)
    att = [dilated_attention_group(qkv, Bsz, L, g) if ATT_GROUPS[g][1] == 1 else
           dilated_attention_slabs(qkv, Bsz, L, g) for g in range(N_GROUPS)]
    x2 = merge_out(x1, a2, [o for o, _ in att], [l for _, l in att], gates, p['mix_post_g'],
                   p['w_hy_proj'], p['w_att_proj'], p['w_out'])
    x3 = ffn_block(x2, p['ffn2_pre_g'], p['ffn2_post_g'], p['ffn2_w_gate'], p['ffn2_w_up'], p['ffn2_w_down'])
    return x3.reshape(Bsz, L, D)


def kernel(x_prompt, x_sample, ffn1_pre_g, ffn1_post_g, ffn1_w_gate, ffn1_w_up, ffn1_w_down, mix_pre_g, mix_post_g, w_in, hy_conv_w, hy_conv_b, filt_w1, filt_b1, filt_w2, filt_b2, filt_w3, filt_b3, filt_w4, filt_freq, hy_bias, w_hy_proj, w_att_proj, w_out, ffn2_pre_g, ffn2_post_g, ffn2_w_gate, ffn2_w_up, ffn2_w_down):
    assert ffn1_w_gate.shape[0] == 1
    p = {
        'ffn1_pre_g': ffn1_pre_g, 'ffn1_post_g': ffn1_post_g,
        'ffn1_w_gate': ffn1_w_gate[0].astype(BF16), 'ffn1_w_up': ffn1_w_up[0].astype(BF16),
        'ffn1_w_down': ffn1_w_down[0].astype(BF16),
        'mix_pre_g': mix_pre_g, 'mix_post_g': mix_post_g,
        'w_in': w_in[0].astype(BF16),
        'hy_conv_w': hy_conv_w[0], 'hy_conv_b': hy_conv_b,
        'filt': (filt_w1[0], filt_b1[0], filt_w2[0], filt_b2[0], filt_w3[0], filt_b3[0], filt_w4[0], filt_freq[0]),
        'hy_bias': hy_bias[0],
        'w_hy_proj': w_hy_proj[0].astype(BF16), 'w_att_proj': w_att_proj[0].astype(BF16),
        'w_out': w_out[0].astype(BF16),
        'ffn2_pre_g': ffn2_pre_g, 'ffn2_post_g': ffn2_post_g,
        'ffn2_w_gate': ffn2_w_gate[0].astype(BF16), 'ffn2_w_up': ffn2_w_up[0].astype(BF16),
        'ffn2_w_down': ffn2_w_down[0].astype(BF16),
    }
    return (_layer(x_prompt, p), _layer(x_sample, p))
```

```python
import math
from functools import partial

import numpy as np
import jax
import jax.numpy as jnp
from jax import lax
from jax.experimental import pallas as pl
from jax.experimental.pallas import tpu as pltpu

D_MODEL = 2048
D_HYENA = 1024
HYENA_ORDER = 2
SHORT_CONV = 3
FILTER_EMB = 33
FILTER_BANDS = (FILTER_EMB - 1) // 2
FAST_DECAY_PCT = 0.3
SLOW_DECAY_PCT = 1.5
DECAY_TARGET = 1e-2
HEAD_DIM = 128
HEADS_PER_GROUP = 4
ATT_GROUPS = ((128, 1), (512, 4), (2048, 16))
N_GROUPS = len(ATT_GROUPS)
D_ATT = N_GROUPS * HEADS_PER_GROUP * HEAD_DIM
D_ATT_OUT = HEADS_PER_GROUP * HEAD_DIM
ROPE_DIM = HEAD_DIM // 4
ROPE_THETA = 500000.0
N_BRANCH = 2
D_IN_PROJ = 3 * D_HYENA + 3 * D_ATT + N_BRANCH * D_MODEL
D_FF = 5632
EPS = 1e-6
NEG_INF = -1e30

LANES = 128
FFT_INNER = 256
VMEM_BYTES_V7X = 64 * 1024 * 1024
VMEM_LIMIT_BYTES = VMEM_BYTES_V7X - 4 * 1024 * 1024
BF16 = jnp.bfloat16
F32 = jnp.float32
HIGHEST = lax.Precision.HIGHEST


def _rms(x, g):
    return x * lax.rsqrt(jnp.mean(x * x, axis=-1, keepdims=True) + EPS) * g


def _dot(a, b):
    return jnp.dot(a, b, preferred_element_type=F32)


def _params(*sem):
    return pltpu.CompilerParams(dimension_semantics=sem, vmem_limit_bytes=VMEM_LIMIT_BYTES)


def _ffn_kernel(x_ref, pre_g_ref, post_g_ref, wg_ref, wu_ref, wd_ref, o_ref, xn_ref):
    j = pl.program_id(1)

    @pl.when(j == 0)
    def _():
        xn_ref[...] = _rms(x_ref[...], pre_g_ref[...]).astype(BF16)
        o_ref[...] = jnp.zeros_like(o_ref)

    xn = xn_ref[...]
    gate = _dot(xn, wg_ref[...])
    up = _dot(xn, wu_ref[...])
    h = (gate * jax.nn.sigmoid(gate) * up).astype(BF16)
    o_ref[...] += _dot(h, wd_ref[...])

    @pl.when(j == pl.num_programs(1) - 1)
    def _():
        o_ref[...] = x_ref[...] + 0.5 * _rms(o_ref[...], post_g_ref[...])


def ffn_block(x, pre_g, post_g, wg, wu, wd, *, tm=512, tf=512):
    M, D = x.shape
    FF = wg.shape[1]
    return pl.pallas_call(
        _ffn_kernel,
        grid=(M // tm, FF // tf),
        in_specs=[
            pl.BlockSpec((tm, D), lambda i, j: (i, 0)),
            pl.BlockSpec((1, D), lambda i, j: (0, 0)),
            pl.BlockSpec((1, D), lambda i, j: (0, 0)),
            pl.BlockSpec((D, tf), lambda i, j: (0, j)),
            pl.BlockSpec((D, tf), lambda i, j: (0, j)),
            pl.BlockSpec((tf, D), lambda i, j: (j, 0)),
        ],
        out_specs=pl.BlockSpec((tm, D), lambda i, j: (i, 0)),
        out_shape=jax.ShapeDtypeStruct((M, D), F32),
        scratch_shapes=[pltpu.VMEM((tm, D), BF16)],
        compiler_params=_params("parallel", "arbitrary"),
        name="ffn_block",
    )(x, pre_g, post_g, wg, wu, wd)


def rope_tables(L):
    half = ROPE_DIM // 2
    inv_freq = jnp.power(ROPE_THETA, -jnp.arange(half, dtype=F32) / half)
    ang = jnp.arange(L, dtype=F32)[:, None] * inv_freq[None, :]
    cos, sin = jnp.cos(ang), jnp.sin(ang)
    rest = HEAD_DIM - ROPE_DIM
    c = jnp.concatenate([cos, cos, jnp.ones((L, rest), F32)], axis=1)
    s_lo = jnp.concatenate([-sin, jnp.zeros((L, half + rest), F32)], axis=1)
    s_hi = jnp.concatenate([jnp.zeros((L, half), F32), sin, jnp.zeros((L, rest), F32)], axis=1)
    return c, s_lo, s_hi


def _inproj_kernel(nb_hy, nb_att, x_ref, g_ref, w_ref, c_ref, slo_ref, shi_ref,
                   hy_ref, qkv_ref, gate_ref, xn_ref):
    j = pl.program_id(1)
    tn = w_ref.shape[1]
    heads = tn // HEAD_DIM
    half = ROPE_DIM // 2

    @pl.when(j == 0)
    def _():
        xn_ref[...] = _rms(x_ref[...], g_ref[...]).astype(BF16)

    res = lambda: _dot(xn_ref[...], w_ref[...])

    @pl.when(j < nb_hy)
    def _():
        hy_ref[...] = res().astype(hy_ref.dtype)

    @pl.when((j >= nb_hy) & (j < nb_hy + 2 * nb_att))
    def _():
        wide = lambda t_ref: jnp.concatenate([t_ref[...]] * heads, axis=1)
        r = res()
        rot = (r * wide(c_ref) + pltpu.roll(r, tn - half, 1) * wide(slo_ref)
               + pltpu.roll(r, half, 1) * wide(shi_ref))
        scale = jnp.where(j < nb_hy + nb_att, HEAD_DIM ** -0.5, 1.0)
        qkv_ref[...] = (rot * scale).astype(qkv_ref.dtype)

    @pl.when((j >= nb_hy + 2 * nb_att) & (j < nb_hy + 3 * nb_att))
    def _():
        qkv_ref[...] = res().astype(qkv_ref.dtype)

    @pl.when(j >= nb_hy + 3 * nb_att)
    def _():
        gate_ref[...] = res().astype(gate_ref.dtype)


def norm_inproj(x, g, w, L, *, tm=1024, tn=512):
    M, D = x.shape
    n_hy, n_qkv, n_gate = 3 * D_HYENA, 3 * D_ATT, N_BRANCH * D_MODEL
    nb_hy, nb_att, nb_gate = n_hy // tn, D_ATT // tn, n_gate // tn
    nb_qkv = 3 * nb_att
    assert w.shape[1] == (nb_hy + nb_qkv + nb_gate) * tn and tn % HEAD_DIM == 0 and L % tm == 0
    rope_spec = pl.BlockSpec((tm, HEAD_DIM), lambda i, j: (i % (L // tm), 0))
    return pl.pallas_call(
        partial(_inproj_kernel, nb_hy, nb_att),
        grid=(M // tm, nb_hy + nb_qkv + nb_gate),
        in_specs=[
            pl.BlockSpec((tm, D), lambda i, j: (i, 0)),
            pl.BlockSpec((1, D), lambda i, j: (0, 0)),
            pl.BlockSpec((D, tn), lambda i, j: (0, j)),
            rope_spec, rope_spec, rope_spec,
        ],
        out_specs=[
            pl.BlockSpec((tm, tn), lambda i, j: (i, jnp.minimum(j, nb_hy - 1))),
            pl.BlockSpec((tm, tn), lambda i, j: (i, jnp.clip(j - nb_hy, 0, nb_qkv - 1))),
            pl.BlockSpec((tm, tn), lambda i, j: (i, jnp.maximum(j - nb_hy - nb_qkv, 0))),
        ],
        out_shape=[
            jax.ShapeDtypeStruct((M, n_hy), BF16),
            jax.ShapeDtypeStruct((M, n_qkv), BF16),
            jax.ShapeDtypeStruct((M, n_gate), F32),
        ],
        scratch_shapes=[pltpu.VMEM((tm, D), BF16)],
        compiler_params=_params("parallel", "arbitrary"),
        name="norm_inproj",
    )(x, g, w, *rope_tables(L))


ATT_RADIUS = 64
assert all(w // (2 * d) == ATT_RADIUS for w, d in ATT_GROUPS)


def _attn_kernel(T, Ls, q_ref, kp_ref, kc_ref, kn_ref, vp_ref, vc_ref, vn_ref, o_ref, lse_ref, kbuf, vbuf):
    t = pl.program_id(2)
    R = ATT_RADIUS
    SB = 2 * R
    for buf, prv, cur, nxt in ((kbuf, kp_ref, kc_ref, kn_ref), (vbuf, vp_ref, vc_ref, vn_ref)):
        buf[0:R] = prv[...]
        buf[R:R + T] = cur[...]
        buf[R + T:R + T + R] = nxt[...]
    qi = lax.broadcasted_iota(jnp.int32, (SB, 2 * SB), 0)
    kk = lax.broadcasted_iota(jnp.int32, (SB, 2 * SB), 1)
    band = (kk >= qi) & (kk <= qi + 2 * R)
    lane = lax.broadcasted_iota(jnp.int32, (SB, LANES), 1)
    for sb in range(T // SB):
        kpos = t * T + (sb * SB - R) + kk
        mask = band & (kpos >= 0) & (kpos < Ls)
        lse = jnp.zeros((SB, LANES), F32)
        for h in range(HEADS_PER_GROUP):
            cols = slice(h * HEAD_DIM, (h + 1) * HEAD_DIM)
            q = q_ref[sb * SB:(sb + 1) * SB, cols]
            k = kbuf[sb * SB:(sb + 2) * SB, cols]
            v = vbuf[sb * SB:(sb + 2) * SB, cols]
            s = lax.dot_general(q, k, (((1,), (1,)), ((), ())), preferred_element_type=F32)
            s = jnp.where(mask, s, NEG_INF)
            m = jnp.max(s, axis=1, keepdims=True)
            p = jnp.exp(s - m)
            l = jnp.sum(p, axis=1, keepdims=True)
            o = _dot(p.astype(BF16), v) / l
            o_ref[sb * SB:(sb + 1) * SB, cols] = o.astype(o_ref.dtype)
            lse = jnp.where(lane == h, m + jnp.log(l), lse)
        lse_ref[sb * SB:(sb + 1) * SB, :] = lse


SLAB = 16


def _softmax_heads(q, k, v, mask):
    lane = lax.broadcasted_iota(jnp.int32, (q.shape[0], LANES), 1)
    lse = jnp.zeros((q.shape[0], LANES), F32)
    outs = []
    for h in range(HEADS_PER_GROUP):
        cols = slice(h * HEAD_DIM, (h + 1) * HEAD_DIM)
        s = lax.dot_general(q[:, cols], k[:, cols], (((1,), (1,)), ((), ())), preferred_element_type=F32)
        s = jnp.where(mask, s, NEG_INF)
        m = jnp.max(s, axis=1, keepdims=True)
        p = jnp.exp(s - m)
        l = jnp.sum(p, axis=1, keepdims=True)
        outs.append(_dot(p.astype(BF16), v[:, cols]) / l)
        lse = jnp.where(lane == h, m + jnp.log(l), lse)
    return jnp.concatenate(outs, axis=1), lse


def _attn_slab_kernel(d, NS, Ls, q_ref, kp_ref, kc_ref, kn_ref, vp_ref, vc_ref, vn_ref, o_ref, lse_ref,
                      qs, ks, vs, os_, ls):
    t = pl.program_id(1)
    R = ATT_RADIUS
    SB = 2 * R
    J = SLAB // d
    NH = kp_ref.shape[0]
    QA = SB // J
    KA = QA + 2 * NH
    n_sb = NS // QA
    to_class_major = lambda ref: pltpu.einshape("abc->bac", ref[...])
    qs[...] = to_class_major(q_ref)
    for buf, prv, cur, nxt in ((ks, kp_ref, kc_ref, kn_ref), (vs, vp_ref, vc_ref, vn_ref)):
        buf[:, 0:NH] = to_class_major(prv)
        buf[:, NH:NH + NS] = to_class_major(cur)
        buf[:, NH + NS:NH + NS + NH] = to_class_major(nxt)
    qi = lax.broadcasted_iota(jnp.int32, (SB, 2 * SB), 0)
    kk = lax.broadcasted_iota(jnp.int32, (SB, 2 * SB), 1)
    sq_rel = J * (qi % QA) + qi // QA
    sk_rel = J * (kk % KA - NH) + kk // KA
    band = jnp.abs(sk_rel - sq_rel) <= R

    def body(it, carry):
        r = it // n_sb
        a0 = pl.multiple_of((it % n_sb) * QA, QA)
        sk = J * (t * NS + a0) + sk_rel
        mask = band & (sk >= 0) & (sk < Ls)
        gather = lambda buf, n: jnp.concatenate([buf[j * d + r, pl.ds(a0, n), :] for j in range(J)], axis=0)
        o, lse = _softmax_heads(gather(qs, QA), gather(ks, KA), gather(vs, KA), mask)
        o = o.astype(os_.dtype)
        for j in range(J):
            os_[j * d + r, pl.ds(a0, QA), :] = o[j * QA:(j + 1) * QA]
            ls[j * d + r, pl.ds(a0, QA), :] = lse[j * QA:(j + 1) * QA]
        return carry

    lax.fori_loop(0, d * n_sb, body, 0)
    o_ref[...] = pltpu.einshape("bac->abc", os_[...])
    lse_ref[...] = pltpu.einshape("bac->abc", ls[...])


def dilated_attention_slabs(qkv, Bsz, L, g, *, NS=128):
    d = ATT_GROUPS[g][1]
    M = Bsz * L
    R = ATT_RADIUS
    GW = HEADS_PER_GROUP * HEAD_DIM
    NH = R * d // SLAB
    TB = NS * SLAB
    assert SLAB % d == 0 and L % TB == 0 and NS % NH == 0 and (2 * R) % (SLAB // d) == 0
    nmb, nhb, hpm = L // TB, L // (NH * SLAB), NS // NH
    view = qkv.reshape(M // SLAB, SLAB, 3 * D_ATT)
    col = lambda which: which * N_GROUPS + g

    def main(which):
        return pl.BlockSpec((NS, SLAB, GW), lambda b, t: (b * nmb + t, 0, col(which)))

    def prev(which):
        return pl.BlockSpec((NH, SLAB, GW), lambda b, t: (jnp.maximum(b * nhb + t * hpm - 1, b * nhb), 0, col(which)))

    def nxt(which):
        return pl.BlockSpec((NH, SLAB, GW),
                            lambda b, t: (jnp.minimum(b * nhb + (t + 1) * hpm, (b + 1) * nhb - 1), 0, col(which)))

    o, lse = pl.pallas_call(
        partial(_attn_slab_kernel, d, NS, L // d),
        grid=(Bsz, nmb),
        in_specs=[main(0), prev(1), main(1), nxt(1), prev(2), main(2), nxt(2)],
        out_specs=[pl.BlockSpec((NS, SLAB, GW), lambda b, t: (b * nmb + t, 0, 0)),
                   pl.BlockSpec((NS, SLAB, LANES), lambda b, t: (b * nmb + t, 0, 0))],
        out_shape=[jax.ShapeDtypeStruct((M // SLAB, SLAB, GW), BF16),
                   jax.ShapeDtypeStruct((M // SLAB, SLAB, LANES), F32)],
        scratch_shapes=[pltpu.VMEM((SLAB, NS, GW), BF16),
                        pltpu.VMEM((SLAB, NS + 2 * NH, GW), BF16), pltpu.VMEM((SLAB, NS + 2 * NH, GW), BF16),
                        pltpu.VMEM((SLAB, NS, GW), BF16), pltpu.VMEM((SLAB, NS, LANES), F32)],
        compiler_params=_params("parallel", "arbitrary"),
        name=f"dilated_attention_g{g}",
    )(view, view, view, view, view, view, view)
    return o.reshape(M, GW), lse.reshape(M, LANES)


def dilated_attention_group(qkv, Bsz, L, g, *, T=256):
    d = ATT_GROUPS[g][1]
    M = Bsz * L
    Ls = L // d
    R = ATT_RADIUS
    GW = HEADS_PER_GROUP * HEAD_DIM
    assert L % d == 0 and Ls % T == 0 and T % (2 * R) == 0
    nrb, nhb, hpt = Ls // T, Ls // R, T // R
    ncol = 3 * N_GROUPS
    view = qkv.reshape(M // d, d * 3 * D_ATT)

    def main(which):
        return pl.BlockSpec((T, GW), lambda b, r, t: (b * nrb + t, r * ncol + which * N_GROUPS + g))

    def prev(which):
        return pl.BlockSpec((R, GW), lambda b, r, t: (jnp.maximum(b * nhb + t * hpt - 1, b * nhb),
                                                      r * ncol + which * N_GROUPS + g))

    def nxt(which):
        return pl.BlockSpec((R, GW), lambda b, r, t: (jnp.minimum(b * nhb + (t + 1) * hpt, (b + 1) * nhb - 1),
                                                      r * ncol + which * N_GROUPS + g))

    o, lse = pl.pallas_call(
        partial(_attn_kernel, T, Ls),
        grid=(Bsz, d, nrb),
        in_specs=[main(0), prev(1), main(1), nxt(1), prev(2), main(2), nxt(2)],
        out_specs=[pl.BlockSpec((T, GW), lambda b, r, t: (b * nrb + t, r)),
                   pl.BlockSpec((T, LANES), lambda b, r, t: (b * nrb + t, r))],
        out_shape=[jax.ShapeDtypeStruct((M // d, d * GW), BF16),
                   jax.ShapeDtypeStruct((M // d, d * LANES), F32)],
        scratch_shapes=[pltpu.VMEM((T + 2 * R, GW), BF16), pltpu.VMEM((T + 2 * R, GW), BF16)],
        compiler_params=_params("parallel", "parallel", "arbitrary"),
        name=f"dilated_attention_g{g}",
    )(view, view, view, view, view, view, view)
    return o.reshape(M, GW), lse.reshape(M, LANES)


def _merge_kernel(x_ref, a_ref, o0_ref, o1_ref, o2_ref, l0_ref, l1_ref, l2_ref, ga_ref, gb_ref,
                  post_g_ref, whp_ref, wap_ref, wo_ref, o_ref):
    lses = [l0_ref[...], l1_ref[...], l2_ref[...]]
    mx = jnp.maximum(jnp.maximum(lses[0], lses[1]), lses[2])
    es = [jnp.exp(l - mx) for l in lses]
    den = es[0] + es[1] + es[2]
    wts = [e / den for e in es]
    outs = [o0_ref, o1_ref, o2_ref]
    heads = []
    for h in range(HEADS_PER_GROUP):
        cols = slice(h * HEAD_DIM, (h + 1) * HEAD_DIM)
        heads.append(sum(wts[g][:, h:h + 1] * outs[g][:, cols].astype(F32) for g in range(N_GROUPS)))
    att = jnp.concatenate(heads, axis=1).astype(BF16)
    a = _dot(a_ref[...], whp_ref[...])
    b = _dot(att, wap_ref[...])
    merged = (jax.nn.sigmoid(ga_ref[...]) * a + jax.nn.sigmoid(gb_ref[...]) * b).astype(BF16)
    mix = _dot(merged, wo_ref[...])
    o_ref[...] = x_ref[...] + _rms(mix, post_g_ref[...])


def merge_out(x, a_in, att_outs, att_lses, gates, post_g, whp, wap, wo, *, tm=256):
    M, D = x.shape
    const = lambda i: (0, 0)
    rows = lambda a: pl.BlockSpec((tm, a.shape[1]), lambda i: (i, 0))
    return pl.pallas_call(
        _merge_kernel,
        grid=(M // tm,),
        in_specs=[rows(x), rows(a_in)] + [rows(o) for o in att_outs] + [rows(l) for l in att_lses] + [
            pl.BlockSpec((tm, D), lambda i: (i, 0)),
            pl.BlockSpec((tm, D), lambda i: (i, 1)),
            pl.BlockSpec((1, D), const),
            pl.BlockSpec(whp.shape, const),
            pl.BlockSpec(wap.shape, const),
            pl.BlockSpec(wo.shape, const),
        ],
        out_specs=pl.BlockSpec((tm, D), lambda i: (i, 0)),
        out_shape=jax.ShapeDtypeStruct((M, D), F32),
        compiler_params=_params("parallel"),
        name="merge_out",
    )(x, a_in, *att_outs, *att_lses, gates, gates, post_g, whp, wap, wo)


def _dft_constants(L):
    N = 2 * L
    N2 = FFT_INNER
    N1 = N // N2
    h = N1 // 2
    idx1 = np.arange(N1)
    ang1 = -2.0 * np.pi * ((idx1[:, None] * idx1[None, :]) % N1) / N1
    f1r, f1i = np.cos(ang1), np.sin(ang1)
    s_data = np.block([[f1r[:, :h], -f1i[:, :h]], [f1i[:, :h], f1r[:, :h]]])
    s_filt = np.concatenate([f1r, f1i], axis=0)
    ar, ai = f1r[:h, :], -f1i[:h, :]
    t_fin = np.block([[ar, -ai], [ai, ar]])
    idx2 = np.arange(N2)
    ang2 = -2.0 * np.pi * ((idx2[:, None] * idx2[None, :]) % N2) / N2
    angt = -2.0 * np.pi * (idx1[:, None] * idx2[None, :]) / N
    return dict(
        N1=N1,
        s_data=jnp.asarray(s_data, BF16), s_filt=jnp.asarray(s_filt, BF16), t_fin=jnp.asarray(t_fin, BF16),
        f2r=jnp.asarray(np.cos(ang2), F32), f2i=jnp.asarray(np.sin(ang2), F32),
        twr=jnp.asarray(np.cos(angt).reshape(N1, 1, N2), F32),
        twi=jnp.asarray(np.sin(angt).reshape(N1, 1, N2), F32),
    )


def _shortconv_kernel(tl, L, x_ref, prev_ref, next_ref, w_ref, b_ref, o_ref):
    i = pl.program_id(1)
    x = x_ref[...].astype(F32)
    row = lax.broadcasted_iota(jnp.int32, x.shape, 0)
    pos0 = (i * tl) % L
    halo = prev_ref.shape[0]
    prev_row = jnp.where(pos0 == 0, 0.0, prev_ref[halo - 1:halo, :].astype(F32))
    next_row = jnp.where(pos0 + tl == L, 0.0, next_ref[0:1, :].astype(F32))
    xm = jnp.where(row == 0, prev_row, pltpu.roll(x, 1, 0))
    xp = jnp.where(row == tl - 1, next_row, pltpu.roll(x, tl - 1, 0))
    y = xm * w_ref[0:1, :] + x * w_ref[1:2, :] + xp * w_ref[2:3, :] + b_ref[...]
    o_ref[0] = y.astype(o_ref.dtype)


def hyena_shortconv(hy, w, b, L, *, tl=512):
    M = hy.shape[0]
    C = hy.shape[1] // 3
    halo = SLAB
    nhalo = M // halo
    return pl.pallas_call(
        partial(_shortconv_kernel, tl, L),
        grid=(3, M // tl),
        in_specs=[
            pl.BlockSpec((tl, C), lambda j, i: (i, j)),
            pl.BlockSpec((halo, C), lambda j, i: (jnp.maximum(i * (tl // halo) - 1, 0), j)),
            pl.BlockSpec((halo, C), lambda j, i: (jnp.minimum((i + 1) * (tl // halo), nhalo - 1), j)),
            pl.BlockSpec((SHORT_CONV, C), lambda j, i: (0, j)),
            pl.BlockSpec((1, C), lambda j, i: (0, j)),
        ],
        out_specs=pl.BlockSpec((1, tl, C), lambda j, i: (j, i, 0)),
        out_shape=jax.ShapeDtypeStruct((3, M, C), BF16),
        compiler_params=_params("parallel", "parallel"),
        name="hyena_shortconv",
    )(hy, hy, hy, w, b)


def _filter_kernel(L, tl, w1_ref, b1_ref, w2_ref, b2_ref, w3_ref, b3_ref, w4hi_ref, w4lo_ref,
                   freq_ref, fvec_ref, delta_ref, kf_ref):
    i = pl.program_id(0)
    HALF = LANES // 2
    wide = delta_ref.shape[1]
    C = wide // (2 * HYENA_ORDER)
    row = i * tl + lax.broadcasted_iota(jnp.int32, (tl, LANES), 0)
    lane = lax.broadcasted_iota(jnp.int32, (tl, LANES), 1)
    posf = jnp.where(lane < HALF, row, L - row).astype(F32)
    t = posf / (L - 1)
    a = fvec_ref[...] * ((2.0 * math.pi / L) * posf)
    lh = lane % HALF
    feats = jnp.where(lh < FILTER_BANDS, jnp.cos(a),
                      jnp.where(lh < 2 * FILTER_BANDS, -jnp.sin(a),
                                jnp.where(lh == 2 * FILTER_BANDS, t, 0.0)))
    freq = freq_ref[...]
    dense = lambda v, w_ref, b_ref: jnp.sin(freq * (
        jnp.dot(v, w_ref[...], precision=HIGHEST, preferred_element_type=F32) + b_ref[...]))
    h = dense(dense(dense(feats, w1_ref, b1_ref), w2_ref, b2_ref), w3_ref, b3_ref)
    h_hi = h.astype(BF16)
    h_lo = (h - h_hi.astype(F32)).astype(BF16)
    taps = _dot(h_hi, w4hi_ref[...]) + _dot(h_lo, w4hi_ref[...]) + _dot(h_hi, w4lo_ref[...])
    row_w = i * tl + lax.broadcasted_iota(jnp.int32, (tl, wide), 0)
    col_w = lax.broadcasted_iota(jnp.int32, (tl, wide), 1)
    backward = col_w >= wide // 2
    t_w = jnp.where(backward, L - row_w, row_w).astype(F32) / (L - 1)
    taps = taps * jnp.exp(-t_w * delta_ref[...])
    taps = jnp.where(backward & (row_w == 0), 0.0, taps)
    for n in range(HYENA_ORDER):
        for direction in range(2):
            c0 = (direction * HYENA_ORDER + n) * C
            kf_ref[n, direction] = taps[:, c0:c0 + C].astype(kf_ref.dtype)


def hyena_filter_taps(L, fw1, fb1, fw2, fb2, fw3, fb3, fw4, ffreq, *, tl=256):
    C = D_HYENA
    H = fw2.shape[0]
    HALF = LANES // 2
    assert H <= HALF and 2 * FILTER_BANDS + 1 <= HALF

    def both(a, rows):
        blk = jnp.zeros((HALF if rows else 1, HALF), F32).at[:a.shape[0], :a.shape[1]].set(a)
        if not rows:
            return jnp.concatenate([blk, blk], axis=1)
        zero = jnp.zeros_like(blk)
        return jnp.concatenate([jnp.concatenate([blk, zero], axis=1), jnp.concatenate([zero, blk], axis=1)], axis=0)

    w1 = both(jnp.concatenate([fw1[1:], fw1[:1]], axis=0), True)
    w2, w3 = both(fw2, True), both(fw3, True)
    b1, b2, b3, freq = (both(v[None, :], False) for v in (fb1, fb2, fb3, ffreq))
    w4 = fw4.reshape(H, HYENA_ORDER, 2, C)
    wide = 2 * HYENA_ORDER * C
    w4p = jnp.zeros((LANES, wide), F32)
    w4p = w4p.at[:H, :wide // 2].set(w4[:, :, 0].reshape(H, HYENA_ORDER * C))
    w4p = w4p.at[HALF:HALF + H, wide // 2:].set(w4[:, :, 1].reshape(H, HYENA_ORDER * C))
    w4hi = w4p.astype(BF16)
    w4lo = (w4p - w4hi.astype(F32)).astype(BF16)
    bands = np.linspace(1e-4, FILTER_BANDS - 1, FILTER_BANDS, dtype=np.float32)
    fvec = np.zeros((1, LANES), np.float32)
    for base in (0, HALF):
        fvec[0, base:base + FILTER_BANDS] = bands
        fvec[0, base + FILTER_BANDS:base + 2 * FILTER_BANDS] = bands
    max_decay = math.log(DECAY_TARGET) / FAST_DECAY_PCT
    min_decay = math.log(DECAY_TARGET) / SLOW_DECAY_PCT
    deltas = np.abs(np.linspace(min_decay, max_decay, C, dtype=np.float32))
    delta_w = np.tile(deltas[None, :], (1, 2 * HYENA_ORDER))
    const = lambda i: (0, 0)
    args = (w1, b1, w2, b2, w3, b3, w4hi, w4lo, freq, jnp.asarray(fvec), jnp.asarray(delta_w))
    return pl.pallas_call(
        partial(_filter_kernel, L, tl),
        grid=(L // tl,),
        in_specs=[pl.BlockSpec(a.shape, const) for a in args],
        out_specs=pl.BlockSpec((HYENA_ORDER, 2, tl, C), lambda i: (0, 0, i, 0)),
        out_shape=jax.ShapeDtypeStruct((HYENA_ORDER, 2, L, C), BF16),
        compiler_params=_params("parallel"),
        name="hyena_filter_taps",
    )(*args)


FFT_ROWS = 16
FFT_COLS = 512


def _stage1_kernel(s_ref, x_ref, y_ref):
    _, h, rows, cols = x_ref.shape
    s = s_ref[...]
    xt = pltpu.einshape("abc->bac", x_ref[...].reshape(2 * h, rows, cols))
    yt = jnp.stack([_dot(s, xt[b]).astype(y_ref.dtype) for b in range(rows)], axis=0)
    y_ref[...] = pltpu.einshape("bac->abc", yt).reshape(y_ref.shape)


def fft_stage1(x, which, s):
    _, _, h, N2, C = x.shape
    N1 = 2 * h
    return pl.pallas_call(
        _stage1_kernel,
        grid=(N2 // FFT_ROWS, C // FFT_COLS),
        in_specs=[pl.BlockSpec((2 * N1, N1), lambda j, c: (0, 0)),
                  pl.BlockSpec((None, 2, h, FFT_ROWS, FFT_COLS), lambda j, c: (which, 0, 0, j, c))],
        out_specs=pl.BlockSpec((2, N1, FFT_ROWS, FFT_COLS), lambda j, c: (0, 0, j, c)),
        out_shape=jax.ShapeDtypeStruct((2, N1, N2, C), BF16),
        compiler_params=_params("parallel", "parallel"),
        name="fft_stage1",
    )(s, x)


def _inner_dft(f2r_ref, f2i_ref, twr_ref, twi_ref):
    twr, twi = twr_ref[0], twi_ref[0]
    f2r, f2i = f2r_ref[...], f2i_ref[...]
    return f2r * twr - f2i * twi, f2r * twi + f2i * twr


def _mid_filter_kernel(f2r_ref, f2i_ref, twr_ref, twi_ref, y_ref, h_ref):
    gr, gi = _inner_dft(f2r_ref, f2i_ref, twr_ref, twi_ref)
    grb, gib = gr.astype(BF16), gi.astype(BF16)
    yr, yi = y_ref[0, 0], y_ref[1, 0]
    h_ref[0, 0] = (_dot(grb, yr) - _dot(gib, yi)).astype(h_ref.dtype)
    h_ref[1, 0] = (_dot(gib, yr) + _dot(grb, yi)).astype(h_ref.dtype)


def _mid_kernel(inv_n, f2r_ref, f2i_ref, twr_ref, twi_ref, y_ref, h_ref, u_ref):
    gr, gi = _inner_dft(f2r_ref, f2i_ref, twr_ref, twi_ref)
    grb, gib = gr.astype(BF16), gi.astype(BF16)
    yr, yi = y_ref[0, 0], y_ref[1, 0]
    zr = _dot(grb, yr) - _dot(gib, yi)
    zi = _dot(gib, yr) + _dot(grb, yi)
    hr, hi = h_ref[0, 0].astype(F32), h_ref[1, 0].astype(F32)
    pr = (zr * hr - zi * hi).astype(BF16)
    pi = (zr * hi + zi * hr).astype(BF16)
    irb = (gr.T * inv_n).astype(BF16)
    iib = (gi.T * (-inv_n)).astype(BF16)
    u_ref[0, 0] = (_dot(irb, pr) - _dot(iib, pi)).astype(u_ref.dtype)
    u_ref[1, 0] = (_dot(iib, pr) + _dot(irb, pi)).astype(u_ref.dtype)


def _mid_specs(N1, C):
    N2 = FFT_INNER
    blk = pl.BlockSpec((2, 1, N2, C), lambda k: (0, k, 0, 0))
    consts = [pl.BlockSpec((N2, N2), lambda k: (0, 0)), pl.BlockSpec((N2, N2), lambda k: (0, 0)),
              pl.BlockSpec((1, 1, N2), lambda k: (k, 0, 0)), pl.BlockSpec((1, 1, N2), lambda k: (k, 0, 0))]
    return blk, consts


def fft_mid_filter(y, dc):
    _, N1, _, C = y.shape
    blk, consts = _mid_specs(N1, C)
    return pl.pallas_call(
        _mid_filter_kernel,
        grid=(N1,),
        in_specs=consts + [blk],
        out_specs=blk,
        out_shape=jax.ShapeDtypeStruct((2, N1, FFT_INNER, C), BF16),
        compiler_params=_params("parallel"),
        name="fft_mid_filter",
    )(dc['f2r'], dc['f2i'], dc['twr'], dc['twi'], y)


def fft_mid(y, hspec, dc):
    _, N1, _, C = y.shape
    blk, consts = _mid_specs(N1, C)
    return pl.pallas_call(
        partial(_mid_kernel, 1.0 / (N1 * FFT_INNER)),
        grid=(N1,),
        in_specs=consts + [blk, blk],
        out_specs=blk,
        out_shape=jax.ShapeDtypeStruct((2, N1, FFT_INNER, C), BF16),
        compiler_params=_params("parallel"),
        name="fft_mid",
    )(dc['f2r'], dc['f2i'], dc['twr'], dc['twi'], y, hspec)


def _final_kernel(t_ref, u_ref, z_ref, gate_ref, bias_ref, o_ref):
    _, n1, rows, cols = u_ref.shape
    t = t_ref[...]
    ut = pltpu.einshape("abc->bac", u_ref[...].reshape(2 * n1, rows, cols))
    convt = jnp.stack([_dot(t, ut[b]) for b in range(rows)], axis=0)
    conv = pltpu.einshape("bac->abc", convt).reshape(o_ref.shape)
    z = z_ref[...].astype(F32)
    o_ref[...] = (gate_ref[...].astype(F32) * (conv + bias_ref[...] * z)).astype(o_ref.dtype)


def fft_final(u, t_fin, z, z_which, gate, gate_which, bias, bias_which):
    _, N1, N2, C = u.shape
    h = N1 // 2

    def half(which):
        return pl.BlockSpec((None, 2, h, FFT_ROWS, FFT_COLS), lambda j, c: (which, 0, 0, j, c))

    return pl.pallas_call(
        _final_kernel,
        grid=(N2 // FFT_ROWS, C // FFT_COLS),
        in_specs=[pl.BlockSpec((N1, 2 * N1), lambda j, c: (0, 0)),
                  pl.BlockSpec((2, N1, FFT_ROWS, FFT_COLS), lambda j, c: (0, 0, j, c)),
                  half(z_which), half(gate_which),
                  pl.BlockSpec((None, 1, FFT_COLS), lambda j, c: (bias_which, 0, c))],
        out_specs=half(0),
        out_shape=jax.ShapeDtypeStruct((1, 2, h, N2, C), BF16),
        compiler_params=_params("parallel", "parallel"),
        name="fft_final",
    )(t_fin, u, z, gate, bias.reshape(bias.shape[0], 1, C))


def hyena_branch(hy, L, p):
    M = hy.shape[0]
    assert M == 2 * L, "the batch pair rides as real/imaginary parts"
    C = hy.shape[1] // 3
    dc = _dft_constants(L)
    split = (2, dc['N1'] // 2, FFT_INNER, C)
    taps = hyena_filter_taps(L, *p['filt']).reshape(HYENA_ORDER, *split)
    zs = hyena_shortconv(hy, p['hy_conv_w'], p['hy_conv_b'], L).reshape(3, *split)
    z, z_which = zs, 0
    for n in range(HYENA_ORDER):
        hspec = fft_mid_filter(fft_stage1(taps, n, dc['s_filt']), dc)
        u = fft_mid(fft_stage1(z, z_which, dc['s_data']), hspec, dc)
        z, z_which = fft_final(u, dc['t_fin'], z, z_which, zs, n + 1, p['hy_bias'], n), 0
    return z.reshape(M, C)


def _layer(x, p):
    Bsz, L, D = x.shape
    M = Bsz * L
    x0 = x.reshape(M, D)
    x1 = ffn_block(x0, p['ffn1_pre_g'], p['ffn1_post_g'], p['ffn1_w_gate'], p['ffn1_w_up'], p['ffn1_w_down'])
    hy_in, qkv, gates = norm_inproj(x1, p['mix_pre_g'], p['w_in'], L)
    a2 = hyena_branch(hy_in, L, p)
    att = [dilated_attention_group(qkv, Bsz, L, g) if ATT_GROUPS[g][1] == 1 else
           dilated_attention_slabs(qkv, Bsz, L, g) for g in range(N_GROUPS)]
    x2 = merge_out(x1, a2, [o for o, _ in att], [l for _, l in att], gates, p['mix_post_g'],
                   p['w_hy_proj'], p['w_att_proj'], p['w_out'])
    x3 = ffn_block(x2, p['ffn2_pre_g'], p['ffn2_post_g'], p['ffn2_w_gate'], p['ffn2_w_up'], p['ffn2_w_down'])
    return x3.reshape(Bsz, L, D)


def kernel(x_prompt, x_sample, ffn1_pre_g, ffn1_post_g, ffn1_w_gate, ffn1_w_up, ffn1_w_down, mix_pre_g, mix_post_g, w_in, hy_conv_w, hy_conv_b, filt_w1, filt_b1, filt_w2, filt_b2, filt_w3, filt_b3, filt_w4, filt_freq, hy_bias, w_hy_proj, w_att_proj, w_out, ffn2_pre_g, ffn2_post_g, ffn2_w_gate, ffn2_w_up, ffn2_w_down):
    assert ffn1_w_gate.shape[0] == 1
    p = {
        'ffn1_pre_g': ffn1_pre_g, 'ffn1_post_g': ffn1_post_g,
        'ffn1_w_gate': ffn1_w_gate[0].astype(BF16), 'ffn1_w_up': ffn1_w_up[0].astype(BF16),
        'ffn1_w_down': ffn1_w_down[0].astype(BF16),
        'mix_pre_g': mix_pre_g, 'mix_post_g': mix_post_g,
        'w_in': w_in[0].astype(BF16),
        'hy_conv_w': hy_conv_w[0], 'hy_conv_b': hy_conv_b,
        'filt': (filt_w1[0], filt_b1[0], filt_w2[0], filt_b2[0], filt_w3[0], filt_b3[0], filt_w4[0], filt_freq[0]),
        'hy_bias': hy_bias[0],
        'w_hy_proj': w_hy_proj[0].astype(BF16), 'w_att_proj': w_att_proj[0].astype(BF16),
        'w_out': w_out[0].astype(BF16),
        'ffn2_pre_g': ffn2_pre_g, 'ffn2_post_g': ffn2_post_g,
        'ffn2_w_gate': ffn2_w_gate[0].astype(BF16), 'ffn2_w_up': ffn2_w_up[0].astype(BF16),
        'ffn2_w_down': ffn2_w_down[0].astype(BF16),
    }
    return (_layer(x_prompt, p), _layer(x_sample, p))
```

```python
import math
from functools import partial

import numpy as np
import jax
import jax.numpy as jnp
from jax import lax
from jax.experimental import pallas as pl
from jax.experimental.pallas import tpu as pltpu

D_MODEL = 2048
D_HYENA = 1024
HYENA_ORDER = 2
SHORT_CONV = 3
FILTER_EMB = 33
FILTER_BANDS = (FILTER_EMB - 1) // 2
FAST_DECAY_PCT = 0.3
SLOW_DECAY_PCT = 1.5
DECAY_TARGET = 1e-2
HEAD_DIM = 128
HEADS_PER_GROUP = 4
ATT_GROUPS = ((128, 1), (512, 4), (2048, 16))
N_GROUPS = len(ATT_GROUPS)
D_ATT = N_GROUPS * HEADS_PER_GROUP * HEAD_DIM
D_ATT_OUT = HEADS_PER_GROUP * HEAD_DIM
ROPE_DIM = HEAD_DIM // 4
ROPE_THETA = 500000.0
N_BRANCH = 2
D_IN_PROJ = 3 * D_HYENA + 3 * D_ATT + N_BRANCH * D_MODEL
D_FF = 5632
EPS = 1e-6
NEG_INF = -1e30

LANES = 128
FFT_INNER = 256
VMEM_BYTES_V7X = 64 * 1024 * 1024
VMEM_LIMIT_BYTES = VMEM_BYTES_V7X - 4 * 1024 * 1024
BF16 = jnp.bfloat16
F32 = jnp.float32
HIGHEST = lax.Precision.HIGHEST


def _rms(x, g):
    return x * lax.rsqrt(jnp.mean(x * x, axis=-1, keepdims=True) + EPS) * g


def _dot(a, b):
    return jnp.dot(a, b, preferred_element_type=F32)


def _params(*sem):
    return pltpu.CompilerParams(dimension_semantics=sem, vmem_limit_bytes=VMEM_LIMIT_BYTES)


def _ffn_kernel(x_ref, pre_g_ref, post_g_ref, wg_ref, wu_ref, wd_ref, o_ref, xn_ref):
    j = pl.program_id(1)

    @pl.when(j == 0)
    def _():
        xn_ref[...] = _rms(x_ref[...], pre_g_ref[...]).astype(BF16)
        o_ref[...] = jnp.zeros_like(o_ref)

    xn = xn_ref[...]
    gate = _dot(xn, wg_ref[...])
    up = _dot(xn, wu_ref[...])
    h = (gate * jax.nn.sigmoid(gate) * up).astype(BF16)
    o_ref[...] += _dot(h, wd_ref[...])

    @pl.when(j == pl.num_programs(1) - 1)
    def _():
        o_ref[...] = x_ref[...] + 0.5 * _rms(o_ref[...], post_g_ref[...])


def ffn_block(x, pre_g, post_g, wg, wu, wd, *, tm=1024, tf=512):
    M, D = x.shape
    FF = wg.shape[1]
    return pl.pallas_call(
        _ffn_kernel,
        grid=(M // tm, FF // tf),
        in_specs=[
            pl.BlockSpec((tm, D), lambda i, j: (i, 0), pipeline_mode=pl.Buffered(1)),
            pl.BlockSpec((1, D), lambda i, j: (0, 0)),
            pl.BlockSpec((1, D), lambda i, j: (0, 0)),
            pl.BlockSpec((D, tf), lambda i, j: (0, j)),
            pl.BlockSpec((D, tf), lambda i, j: (0, j)),
            pl.BlockSpec((tf, D), lambda i, j: (j, 0)),
        ],
        out_specs=pl.BlockSpec((tm, D), lambda i, j: (i, 0)),
        out_shape=jax.ShapeDtypeStruct((M, D), F32),
        scratch_shapes=[pltpu.VMEM((tm, D), BF16)],
        compiler_params=_params("parallel", "arbitrary"),
        name="ffn_block",
    )(x, pre_g, post_g, wg, wu, wd)


def rope_tables(L):
    half = ROPE_DIM // 2
    inv_freq = jnp.power(ROPE_THETA, -jnp.arange(half, dtype=F32) / half)
    ang = jnp.arange(L, dtype=F32)[:, None] * inv_freq[None, :]
    cos, sin = jnp.cos(ang), jnp.sin(ang)
    rest = HEAD_DIM - ROPE_DIM
    c = jnp.concatenate([cos, cos, jnp.ones((L, rest), F32)], axis=1)
    s_lo = jnp.concatenate([-sin, jnp.zeros((L, half + rest), F32)], axis=1)
    s_hi = jnp.concatenate([jnp.zeros((L, half), F32), sin, jnp.zeros((L, rest), F32)], axis=1)
    scale = HEAD_DIM ** -0.5
    zero = jnp.zeros_like(c)
    return (jnp.stack([c * scale, c, jnp.ones_like(c)]), jnp.stack([s_lo * scale, s_lo, zero]),
            jnp.stack([s_hi * scale, s_hi, zero]))


def _norm_matmul_kernel(rotate, x_ref, g_ref, w_ref, *rest):
    if rotate:
        c_ref, slo_ref, shi_ref, o_ref, xn_ref = rest
    else:
        o_ref, xn_ref = rest

    @pl.when(pl.program_id(1) == 0)
    def _():
        xn_ref[...] = _rms(x_ref[...], g_ref[...]).astype(BF16)

    r = _dot(xn_ref[...], w_ref[...])
    if rotate:
        tn = r.shape[1]
        half = ROPE_DIM // 2
        wide = lambda t_ref: jnp.concatenate([t_ref[...]] * (tn // HEAD_DIM), axis=1)
        r = r * wide(c_ref) + pltpu.roll(r, tn - half, 1) * wide(slo_ref) + pltpu.roll(r, half, 1) * wide(shi_ref)
    o_ref[...] = r.astype(o_ref.dtype)


def norm_matmul(x, g, w, out_dtype, *, tm, tn, rope_len=None, name):
    M, D = x.shape
    N = w.shape[1]
    in_specs = [
        pl.BlockSpec((tm, D), lambda i, j: (i, 0)),
        pl.BlockSpec((1, D), lambda i, j: (0, 0)),
        pl.BlockSpec((D, tn), lambda i, j: (0, j)),
    ]
    args = [x, g, w]
    if rope_len is not None:
        assert tn == D_ATT and N == 3 * D_ATT and rope_len % tm == 0
        spec = pl.BlockSpec((None, tm, HEAD_DIM), lambda i, j: (j, i % (rope_len // tm), 0))
        in_specs += [spec, spec, spec]
        args += list(rope_tables(rope_len))
    return pl.pallas_call(
        partial(_norm_matmul_kernel, rope_len is not None),
        grid=(M // tm, N // tn),
        in_specs=in_specs,
        out_specs=pl.BlockSpec((tm, tn), lambda i, j: (i, j)),
        out_shape=jax.ShapeDtypeStruct((M, N), out_dtype),
        scratch_shapes=[pltpu.VMEM((tm, D), BF16)],
        compiler_params=_params("parallel", "arbitrary"),
        name=name,
    )(*args)


ATT_RADIUS = 64
assert all(w // (2 * d) == ATT_RADIUS for w, d in ATT_GROUPS)


def _attn_kernel(T, Ls, q_ref, kp_ref, kc_ref, kn_ref, vp_ref, vc_ref, vn_ref, o_ref, lse_ref, kbuf, vbuf):
    t = pl.program_id(2)
    R = ATT_RADIUS
    SB = 2 * R
    for buf, prv, cur, nxt in ((kbuf, kp_ref, kc_ref, kn_ref), (vbuf, vp_ref, vc_ref, vn_ref)):
        buf[0:R] = prv[...]
        buf[R:R + T] = cur[...]
        buf[R + T:R + T + R] = nxt[...]
    qi = lax.broadcasted_iota(jnp.int32, (SB, 2 * SB), 0)
    kk = lax.broadcasted_iota(jnp.int32, (SB, 2 * SB), 1)
    band = (kk >= qi) & (kk <= qi + 2 * R)
    lane = lax.broadcasted_iota(jnp.int32, (SB, LANES), 1)
    for sb in range(T // SB):
        kpos = t * T + (sb * SB - R) + kk
        mask = band & (kpos >= 0) & (kpos < Ls)
        lse = jnp.zeros((SB, LANES), F32)
        for h in range(HEADS_PER_GROUP):
            cols = slice(h * HEAD_DIM, (h + 1) * HEAD_DIM)
            q = q_ref[sb * SB:(sb + 1) * SB, cols]
            k = kbuf[sb * SB:(sb + 2) * SB, cols]
            v = vbuf[sb * SB:(sb + 2) * SB, cols]
            s = lax.dot_general(q, k, (((1,), (1,)), ((), ())), preferred_element_type=F32)
            s = jnp.where(mask, s, NEG_INF)
            m = jnp.max(s, axis=1, keepdims=True)
            p = jnp.exp(s - m)
            l = jnp.sum(p, axis=1, keepdims=True)
            o = _dot(p.astype(BF16), v) / l
            o_ref[sb * SB:(sb + 1) * SB, cols] = o.astype(o_ref.dtype)
            lse = jnp.where(lane == h, m + jnp.log(l), lse)
        lse_ref[sb * SB:(sb + 1) * SB, :] = lse


SLAB = 16


def _softmax_heads(q, k, v, mask):
    lane = lax.broadcasted_iota(jnp.int32, (q.shape[0], LANES), 1)
    lse = jnp.zeros((q.shape[0], LANES), F32)
    outs = []
    for h in range(HEADS_PER_GROUP):
        cols = slice(h * HEAD_DIM, (h + 1) * HEAD_DIM)
        s = lax.dot_general(q[:, cols], k[:, cols], (((1,), (1,)), ((), ())), preferred_element_type=F32)
        s = jnp.where(mask, s, NEG_INF)
        m = jnp.max(s, axis=1, keepdims=True)
        p = jnp.exp(s - m)
        l = jnp.sum(p, axis=1, keepdims=True)
        outs.append(_dot(p.astype(BF16), v[:, cols]) / l)
        lse = jnp.where(lane == h, m + jnp.log(l), lse)
    return jnp.concatenate(outs, axis=1), lse


def _attn_slab_kernel(d, NS, Ls, q_ref, kp_ref, kc_ref, kn_ref, vp_ref, vc_ref, vn_ref, o_ref, lse_ref,
                      qs, ks, vs, os_, ls):
    t = pl.program_id(1)
    R = ATT_RADIUS
    SB = 2 * R
    J = SLAB // d
    NH = kp_ref.shape[0]
    QA = SB // J
    KA = QA + 2 * NH
    n_sb = NS // QA
    to_class_major = lambda ref: pltpu.einshape("abc->bac", ref[...])
    qs[...] = to_class_major(q_ref)
    for buf, prv, cur, nxt in ((ks, kp_ref, kc_ref, kn_ref), (vs, vp_ref, vc_ref, vn_ref)):
        buf[:, 0:NH] = to_class_major(prv)
        buf[:, NH:NH + NS] = to_class_major(cur)
        buf[:, NH + NS:NH + NS + NH] = to_class_major(nxt)
    qi = lax.broadcasted_iota(jnp.int32, (SB, 2 * SB), 0)
    kk = lax.broadcasted_iota(jnp.int32, (SB, 2 * SB), 1)
    sq_rel = J * (qi % QA) + qi // QA
    sk_rel = J * (kk % KA - NH) + kk // KA
    band = jnp.abs(sk_rel - sq_rel) <= R

    def body(it, carry):
        r = it // n_sb
        a0 = pl.multiple_of((it % n_sb) * QA, QA)
        sk = J * (t * NS + a0) + sk_rel
        mask = band & (sk >= 0) & (sk < Ls)
        gather = lambda buf, n: jnp.concatenate([buf[j * d + r, pl.ds(a0, n), :] for j in range(J)], axis=0)
        o, lse = _softmax_heads(gather(qs, QA), gather(ks, KA), gather(vs, KA), mask)
        o = o.astype(os_.dtype)
        for j in range(J):
            os_[j * d + r, pl.ds(a0, QA), :] = o[j * QA:(j + 1) * QA]
            ls[j * d + r, pl.ds(a0, QA), :] = lse[j * QA:(j + 1) * QA]
        return carry

    lax.fori_loop(0, d * n_sb, body, 0)
    o_ref[...] = pltpu.einshape("bac->abc", os_[...])
    lse_ref[...] = pltpu.einshape("bac->abc", ls[...])


def dilated_attention_slabs(qkv, Bsz, L, g, *, NS=128):
    d = ATT_GROUPS[g][1]
    M = Bsz * L
    R = ATT_RADIUS
    GW = HEADS_PER_GROUP * HEAD_DIM
    NH = R * d // SLAB
    TB = NS * SLAB
    assert SLAB % d == 0 and L % TB == 0 and NS % NH == 0 and (2 * R) % (SLAB // d) == 0
    nmb, nhb, hpm = L // TB, L // (NH * SLAB), NS // NH
    view = qkv.reshape(M // SLAB, SLAB, 3 * D_ATT)
    col = lambda which: which * N_GROUPS + g

    def main(which):
        return pl.BlockSpec((NS, SLAB, GW), lambda b, t: (b * nmb + t, 0, col(which)))

    def prev(which):
        return pl.BlockSpec((NH, SLAB, GW), lambda b, t: (jnp.maximum(b * nhb + t * hpm - 1, b * nhb), 0, col(which)))

    def nxt(which):
        return pl.BlockSpec((NH, SLAB, GW),
                            lambda b, t: (jnp.minimum(b * nhb + (t + 1) * hpm, (b + 1) * nhb - 1), 0, col(which)))

    o, lse = pl.pallas_call(
        partial(_attn_slab_kernel, d, NS, L // d),
        grid=(Bsz, nmb),
        in_specs=[main(0), prev(1), main(1), nxt(1), prev(2), main(2), nxt(2)],
        out_specs=[pl.BlockSpec((NS, SLAB, GW), lambda b, t: (b * nmb + t, 0, 0)),
                   pl.BlockSpec((NS, SLAB, LANES), lambda b, t: (b * nmb + t, 0, 0))],
        out_shape=[jax.ShapeDtypeStruct((M // SLAB, SLAB, GW), BF16),
                   jax.ShapeDtypeStruct((M // SLAB, SLAB, LANES), F32)],
        scratch_shapes=[pltpu.VMEM((SLAB, NS, GW), BF16),
                        pltpu.VMEM((SLAB, NS + 2 * NH, GW), BF16), pltpu.VMEM((SLAB, NS + 2 * NH, GW), BF16),
                        pltpu.VMEM((SLAB, NS, GW), BF16), pltpu.VMEM((SLAB, NS, LANES), F32)],
        compiler_params=_params("parallel", "arbitrary"),
        name=f"dilated_attention_g{g}",
    )(view, view, view, view, view, view, view)
    return o.reshape(M, GW), lse.reshape(M, LANES)


def dilated_attention_group(qkv, Bsz, L, g, *, T=256):
    d = ATT_GROUPS[g][1]
    M = Bsz * L
    Ls = L // d
    R = ATT_RADIUS
    GW = HEADS_PER_GROUP * HEAD_DIM
    assert L % d == 0 and Ls % T == 0 and T % (2 * R) == 0
    nrb, nhb, hpt = Ls // T, Ls // R, T // R
    ncol = 3 * N_GROUPS
    view = qkv.reshape(M // d, d * 3 * D_ATT)

    def main(which):
        return pl.BlockSpec((T, GW), lambda b, r, t: (b * nrb + t, r * ncol + which * N_GROUPS + g))

    def prev(which):
        return pl.BlockSpec((R, GW), lambda b, r, t: (jnp.maximum(b * nhb + t * hpt - 1, b * nhb),
                                                      r * ncol + which * N_GROUPS + g))

    def nxt(which):
        return pl.BlockSpec((R, GW), lambda b, r, t: (jnp.minimum(b * nhb + (t + 1) * hpt, (b + 1) * nhb - 1),
                                                      r * ncol + which * N_GROUPS + g))

    o, lse = pl.pallas_call(
        partial(_attn_kernel, T, Ls),
        grid=(Bsz, d, nrb),
        in_specs=[main(0), prev(1), main(1), nxt(1), prev(2), main(2), nxt(2)],
        out_specs=[pl.BlockSpec((T, GW), lambda b, r, t: (b * nrb + t, r)),
                   pl.BlockSpec((T, LANES), lambda b, r, t: (b * nrb + t, r))],
        out_shape=[jax.ShapeDtypeStruct((M // d, d * GW), BF16),
                   jax.ShapeDtypeStruct((M // d, d * LANES), F32)],
        scratch_shapes=[pltpu.VMEM((T + 2 * R, GW), BF16), pltpu.VMEM((T + 2 * R, GW), BF16)],
        compiler_params=_params("parallel", "parallel", "arbitrary"),
        name=f"dilated_attention_g{g}",
    )(view, view, view, view, view, view, view)
    return o.reshape(M, GW), lse.reshape(M, LANES)


def _merge_kernel(x_ref, a_ref, o0_ref, o1_ref, o2_ref, l0_ref, l1_ref, l2_ref, ga_ref, gb_ref,
                  post_g_ref, whp_ref, wap_ref, wo_ref, o_ref):
    lses = [l0_ref[...], l1_ref[...], l2_ref[...]]
    mx = jnp.maximum(jnp.maximum(lses[0], lses[1]), lses[2])
    es = [jnp.exp(l - mx) for l in lses]
    den = es[0] + es[1] + es[2]
    wts = [e / den for e in es]
    outs = [o0_ref, o1_ref, o2_ref]
    heads = []
    for h in range(HEADS_PER_GROUP):
        cols = slice(h * HEAD_DIM, (h + 1) * HEAD_DIM)
        heads.append(sum(wts[g][:, h:h + 1] * outs[g][:, cols].astype(F32) for g in range(N_GROUPS)))
    att = jnp.concatenate(heads, axis=1).astype(BF16)
    a = _dot(a_ref[...], whp_ref[...])
    b = _dot(att, wap_ref[...])
    merged = (jax.nn.sigmoid(ga_ref[...]) * a + jax.nn.sigmoid(gb_ref[...]) * b).astype(BF16)
    mix = _dot(merged, wo_ref[...])
    o_ref[...] = x_ref[...] + _rms(mix, post_g_ref[...])


def merge_out(x, a_in, att_outs, att_lses, gates, post_g, whp, wap, wo, *, tm=256):
    M, D = x.shape
    const = lambda i: (0, 0)
    rows = lambda a: pl.BlockSpec((tm, a.shape[1]), lambda i: (i, 0))
    return pl.pallas_call(
        _merge_kernel,
        grid=(M // tm,),
        in_specs=[rows(x), rows(a_in)] + [rows(o) for o in att_outs] + [rows(l) for l in att_lses] + [
            pl.BlockSpec((tm, D), lambda i: (i, 0)),
            pl.BlockSpec((tm, D), lambda i: (i, 1)),
            pl.BlockSpec((1, D), const),
            pl.BlockSpec(whp.shape, const),
            pl.BlockSpec(wap.shape, const),
            pl.BlockSpec(wo.shape, const),
        ],
        out_specs=pl.BlockSpec((tm, D), lambda i: (i, 0)),
        out_shape=jax.ShapeDtypeStruct((M, D), F32),
        compiler_params=_params("parallel"),
        name="merge_out",
    )(x, a_in, *att_outs, *att_lses, gates, gates, post_g, whp, wap, wo)


def _dft_constants(L):
    N = 2 * L
    N2 = FFT_INNER
    N1 = N // N2
    h = N1 // 2
    idx1 = np.arange(N1)
    ang1 = -2.0 * np.pi * ((idx1[:, None] * idx1[None, :]) % N1) / N1
    f1r, f1i = np.cos(ang1), np.sin(ang1)
    s_data = np.block([[f1r[:, :h], -f1i[:, :h]], [f1i[:, :h], f1r[:, :h]]])
    s_filt = np.concatenate([f1r, f1i], axis=0)
    ar, ai = f1r[:h, :], -f1i[:h, :]
    t_fin = np.block([[ar, -ai], [ai, ar]])
    idx2 = np.arange(N2)
    ang2 = -2.0 * np.pi * ((idx2[:, None] * idx2[None, :]) % N2) / N2
    angt = -2.0 * np.pi * (idx1[:, None] * idx2[None, :]) / N
    return dict(
        N1=N1,
        s_data=jnp.asarray(s_data, BF16), s_filt=jnp.asarray(s_filt, BF16), t_fin=jnp.asarray(t_fin, BF16),
        f2r=jnp.asarray(np.cos(ang2), F32), f2i=jnp.asarray(np.sin(ang2), F32),
        twr=jnp.asarray(np.cos(angt).reshape(N1, 1, N2), F32),
        twi=jnp.asarray(np.sin(angt).reshape(N1, 1, N2), F32),
    )


def _shortconv_kernel(tl, L, x_ref, prev_ref, next_ref, w_ref, b_ref, o_ref):
    i = pl.program_id(1)
    x = x_ref[...].astype(F32)
    row = lax.broadcasted_iota(jnp.int32, x.shape, 0)
    pos0 = (i * tl) % L
    halo = prev_ref.shape[0]
    prev_row = jnp.where(pos0 == 0, 0.0, prev_ref[halo - 1:halo, :].astype(F32))
    next_row = jnp.where(pos0 + tl == L, 0.0, next_ref[0:1, :].astype(F32))
    xm = jnp.where(row == 0, prev_row, pltpu.roll(x, 1, 0))
    xp = jnp.where(row == tl - 1, next_row, pltpu.roll(x, tl - 1, 0))
    y = xm * w_ref[0:1, :] + x * w_ref[1:2, :] + xp * w_ref[2:3, :] + b_ref[...]
    o_ref[0] = y.astype(o_ref.dtype)


def hyena_shortconv(hy, w, b, L, *, tl=512):
    M = hy.shape[0]
    C = hy.shape[1] // 3
    halo = SLAB
    nhalo = M // halo
    return pl.pallas_call(
        partial(_shortconv_kernel, tl, L),
        grid=(3, M // tl),
        in_specs=[
            pl.BlockSpec((tl, C), lambda j, i: (i, j)),
            pl.BlockSpec((halo, C), lambda j, i: (jnp.maximum(i * (tl // halo) - 1, 0), j)),
            pl.BlockSpec((halo, C), lambda j, i: (jnp.minimum((i + 1) * (tl // halo), nhalo - 1), j)),
            pl.BlockSpec((SHORT_CONV, C), lambda j, i: (0, j)),
            pl.BlockSpec((1, C), lambda j, i: (0, j)),
        ],
        out_specs=pl.BlockSpec((1, tl, C), lambda j, i: (j, i, 0)),
        out_shape=jax.ShapeDtypeStruct((3, M, C), BF16),
        compiler_params=_params("parallel", "parallel"),
        name="hyena_shortconv",
    )(hy, hy, hy, w, b)


def _filter_kernel(L, tl, w1_ref, b1_ref, w2_ref, b2_ref, w3_ref, b3_ref, w4hi_ref, w4lo_ref,
                   freq_ref, fvec_ref, delta_ref, kf_ref):
    i = pl.program_id(0)
    HALF = LANES // 2
    wide = delta_ref.shape[1]
    C = wide // (2 * HYENA_ORDER)
    row = i * tl + lax.broadcasted_iota(jnp.int32, (tl, LANES), 0)
    lane = lax.broadcasted_iota(jnp.int32, (tl, LANES), 1)
    posf = jnp.where(lane < HALF, row, L - row).astype(F32)
    t = posf / (L - 1)
    a = fvec_ref[...] * ((2.0 * math.pi / L) * posf)
    lh = lane % HALF
    feats = jnp.where(lh < FILTER_BANDS, jnp.cos(a),
                      jnp.where(lh < 2 * FILTER_BANDS, -jnp.sin(a),
                                jnp.where(lh == 2 * FILTER_BANDS, t, 0.0)))
    freq = freq_ref[...]
    dense = lambda v, w_ref, b_ref: jnp.sin(freq * (
        jnp.dot(v, w_ref[...], precision=HIGHEST, preferred_element_type=F32) + b_ref[...]))
    h = dense(dense(dense(feats, w1_ref, b1_ref), w2_ref, b2_ref), w3_ref, b3_ref)
    h_hi = h.astype(BF16)
    h_lo = (h - h_hi.astype(F32)).astype(BF16)
    taps = _dot(h_hi, w4hi_ref[...]) + _dot(h_lo, w4hi_ref[...]) + _dot(h_hi, w4lo_ref[...])
    row_w = i * tl + lax.broadcasted_iota(jnp.int32, (tl, wide), 0)
    col_w = lax.broadcasted_iota(jnp.int32, (tl, wide), 1)
    backward = col_w >= wide // 2
    t_w = jnp.where(backward, L - row_w, row_w).astype(F32) / (L - 1)
    taps = taps * jnp.exp(-t_w * delta_ref[...])
    taps = jnp.where(backward & (row_w == 0), 0.0, taps)
    for n in range(HYENA_ORDER):
        for direction in range(2):
            c0 = (direction * HYENA_ORDER + n) * C
            kf_ref[n, direction] = taps[:, c0:c0 + C].astype(kf_ref.dtype)


def hyena_filter_taps(L, fw1, fb1, fw2, fb2, fw3, fb3, fw4, ffreq, *, tl=256):
    C = D_HYENA
    H = fw2.shape[0]
    HALF = LANES // 2
    assert H <= HALF and 2 * FILTER_BANDS + 1 <= HALF

    def both(a, rows):
        blk = jnp.zeros((HALF if rows else 1, HALF), F32).at[:a.shape[0], :a.shape[1]].set(a)
        if not rows:
            return jnp.concatenate([blk, blk], axis=1)
        zero = jnp.zeros_like(blk)
        return jnp.concatenate([jnp.concatenate([blk, zero], axis=1), jnp.concatenate([zero, blk], axis=1)], axis=0)

    w1 = both(jnp.concatenate([fw1[1:], fw1[:1]], axis=0), True)
    w2, w3 = both(fw2, True), both(fw3, True)
    b1, b2, b3, freq = (both(v[None, :], False) for v in (fb1, fb2, fb3, ffreq))
    w4 = fw4.reshape(H, HYENA_ORDER, 2, C)
    wide = 2 * HYENA_ORDER * C
    w4p = jnp.zeros((LANES, wide), F32)
    w4p = w4p.at[:H, :wide // 2].set(w4[:, :, 0].reshape(H, HYENA_ORDER * C))
    w4p = w4p.at[HALF:HALF + H, wide // 2:].set(w4[:, :, 1].reshape(H, HYENA_ORDER * C))
    w4hi = w4p.astype(BF16)
    w4lo = (w4p - w4hi.astype(F32)).astype(BF16)
    bands = np.linspace(1e-4, FILTER_BANDS - 1, FILTER_BANDS, dtype=np.float32)
    fvec = np.zeros((1, LANES), np.float32)
    for base in (0, HALF):
        fvec[0, base:base + FILTER_BANDS] = bands
        fvec[0, base + FILTER_BANDS:base + 2 * FILTER_BANDS] = bands
    max_decay = math.log(DECAY_TARGET) / FAST_DECAY_PCT
    min_decay = math.log(DECAY_TARGET) / SLOW_DECAY_PCT
    deltas = np.abs(np.linspace(min_decay, max_decay, C, dtype=np.float32))
    delta_w = np.tile(deltas[None, :], (1, 2 * HYENA_ORDER))
    const = lambda i: (0, 0)
    args = (w1, b1, w2, b2, w3, b3, w4hi, w4lo, freq, jnp.asarray(fvec), jnp.asarray(delta_w))
    return pl.pallas_call(
        partial(_filter_kernel, L, tl),
        grid=(L // tl,),
        in_specs=[pl.BlockSpec(a.shape, const) for a in args],
        out_specs=pl.BlockSpec((HYENA_ORDER, 2, tl, C), lambda i: (0, 0, i, 0)),
        out_shape=jax.ShapeDtypeStruct((HYENA_ORDER, 2, L, C), BF16),
        compiler_params=_params("parallel"),
        name="hyena_filter_taps",
    )(*args)


FFT_ROWS = 16
FFT_COLS = 512


def _stage1_kernel(s_ref, x_ref, y_ref):
    _, h, rows, cols = x_ref.shape
    s = s_ref[...]
    xt = pltpu.einshape("abc->bac", x_ref[...].reshape(2 * h, rows, cols))
    yt = jnp.stack([_dot(s, xt[b]).astype(y_ref.dtype) for b in range(rows)], axis=0)
    y_ref[...] = pltpu.einshape("bac->abc", yt).reshape(y_ref.shape)


def fft_stage1(x, which, s):
    _, _, h, N2, C = x.shape
    N1 = 2 * h
    return pl.pallas_call(
        _stage1_kernel,
        grid=(N2 // FFT_ROWS, C // FFT_COLS),
        in_specs=[pl.BlockSpec((2 * N1, N1), lambda j, c: (0, 0)),
                  pl.BlockSpec((None, 2, h, FFT_ROWS, FFT_COLS), lambda j, c: (which, 0, 0, j, c))],
        out_specs=pl.BlockSpec((2, N1, FFT_ROWS, FFT_COLS), lambda j, c: (0, 0, j, c)),
        out_shape=jax.ShapeDtypeStruct((2, N1, N2, C), BF16),
        compiler_params=_params("parallel", "parallel"),
        name="fft_stage1",
    )(s, x)


def _inner_dft(f2r_ref, f2i_ref, twr_ref, twi_ref):
    twr, twi = twr_ref[0], twi_ref[0]
    f2r, f2i = f2r_ref[...], f2i_ref[...]
    return f2r * twr - f2i * twi, f2r * twi + f2i * twr


def _mid_filter_kernel(f2r_ref, f2i_ref, twr_ref, twi_ref, y_ref, h_ref):
    gr, gi = _inner_dft(f2r_ref, f2i_ref, twr_ref, twi_ref)
    grb, gib = gr.astype(BF16), gi.astype(BF16)
    yr, yi = y_ref[0, 0], y_ref[1, 0]
    h_ref[0, 0] = (_dot(grb, yr) - _dot(gib, yi)).astype(h_ref.dtype)
    h_ref[1, 0] = (_dot(gib, yr) + _dot(grb, yi)).astype(h_ref.dtype)


def _mid_kernel(inv_n, f2r_ref, f2i_ref, twr_ref, twi_ref, y_ref, h_ref, u_ref):
    gr, gi = _inner_dft(f2r_ref, f2i_ref, twr_ref, twi_ref)
    grb, gib = gr.astype(BF16), gi.astype(BF16)
    yr, yi = y_ref[0, 0], y_ref[1, 0]
    zr = _dot(grb, yr) - _dot(gib, yi)
    zi = _dot(gib, yr) + _dot(grb, yi)
    hr, hi = h_ref[0, 0].astype(F32), h_ref[1, 0].astype(F32)
    pr = (zr * hr - zi * hi).astype(BF16)
    pi = (zr * hi + zi * hr).astype(BF16)
    irb = (gr.T * inv_n).astype(BF16)
    iib = (gi.T * (-inv_n)).astype(BF16)
    u_ref[0, 0] = (_dot(irb, pr) - _dot(iib, pi)).astype(u_ref.dtype)
    u_ref[1, 0] = (_dot(iib, pr) + _dot(irb, pi)).astype(u_ref.dtype)


def _mid_specs(N1, C):
    N2 = FFT_INNER
    blk = pl.BlockSpec((2, 1, N2, C), lambda k: (0, k, 0, 0))
    consts = [pl.BlockSpec((N2, N2), lambda k: (0, 0)), pl.BlockSpec((N2, N2), lambda k: (0, 0)),
              pl.BlockSpec((1, 1, N2), lambda k: (k, 0, 0)), pl.BlockSpec((1, 1, N2), lambda k: (k, 0, 0))]
    return blk, consts


def fft_mid_filter(y, dc):
    _, N1, _, C = y.shape
    blk, consts = _mid_specs(N1, C)
    return pl.pallas_call(
        _mid_filter_kernel,
        grid=(N1,),
        in_specs=consts + [blk],
        out_specs=blk,
        out_shape=jax.ShapeDtypeStruct((2, N1, FFT_INNER, C), BF16),
        compiler_params=_params("parallel"),
        name="fft_mid_filter",
    )(dc['f2r'], dc['f2i'], dc['twr'], dc['twi'], y)


def fft_mid(y, hspec, dc):
    _, N1, _, C = y.shape
    blk, consts = _mid_specs(N1, C)
    return pl.pallas_call(
        partial(_mid_kernel, 1.0 / (N1 * FFT_INNER)),
        grid=(N1,),
        in_specs=consts + [blk, blk],
        out_specs=blk,
        out_shape=jax.ShapeDtypeStruct((2, N1, FFT_INNER, C), BF16),
        compiler_params=_params("parallel"),
        name="fft_mid",
    )(dc['f2r'], dc['f2i'], dc['twr'], dc['twi'], y, hspec)


def _final_kernel(t_ref, u_ref, z_ref, gate_ref, bias_ref, o_ref):
    _, n1, rows, cols = u_ref.shape
    t = t_ref[...]
    ut = pltpu.einshape("abc->bac", u_ref[...].reshape(2 * n1, rows, cols))
    convt = jnp.stack([_dot(t, ut[b]) for b in range(rows)], axis=0)
    conv = pltpu.einshape("bac->abc", convt).reshape(o_ref.shape)
    z = z_ref[...].astype(F32)
    o_ref[...] = (gate_ref[...].astype(F32) * (conv + bias_ref[...] * z)).astype(o_ref.dtype)


def fft_final(u, t_fin, z, z_which, gate, gate_which, bias, bias_which):
    _, N1, N2, C = u.shape
    h = N1 // 2

    def half(which):
        return pl.BlockSpec((None, 2, h, FFT_ROWS, FFT_COLS), lambda j, c: (which, 0, 0, j, c))

    return pl.pallas_call(
        _final_kernel,
        grid=(N2 // FFT_ROWS, C // FFT_COLS),
        in_specs=[pl.BlockSpec((N1, 2 * N1), lambda j, c: (0, 0)),
                  pl.BlockSpec((2, N1, FFT_ROWS, FFT_COLS), lambda j, c: (0, 0, j, c)),
                  half(z_which), half(gate_which),
                  pl.BlockSpec((None, 1, FFT_COLS), lambda j, c: (bias_which, 0, c))],
        out_specs=half(0),
        out_shape=jax.ShapeDtypeStruct((1, 2, h, N2, C), BF16),
        compiler_params=_params("parallel", "parallel"),
        name="fft_final",
    )(t_fin, u, z, gate, bias.reshape(bias.shape[0], 1, C))


def hyena_branch(hy, L, p):
    M = hy.shape[0]
    assert M == 2 * L, "the batch pair rides as real/imaginary parts"
    C = hy.shape[1] // 3
    dc = _dft_constants(L)
    split = (2, dc['N1'] // 2, FFT_INNER, C)
    taps = hyena_filter_taps(L, *p['filt']).reshape(HYENA_ORDER, *split)
    zs = hyena_shortconv(hy, p['hy_conv_w'], p['hy_conv_b'], L).reshape(3, *split)
    z, z_which = zs, 0
    for n in range(HYENA_ORDER):
        hspec = fft_mid_filter(fft_stage1(taps, n, dc['s_filt']), dc)
        u = fft_mid(fft_stage1(z, z_which, dc['s_data']), hspec, dc)
        z, z_which = fft_final(u, dc['t_fin'], z, z_which, zs, n + 1, p['hy_bias'], n), 0
    return z.reshape(M, C)


def _layer(x, p):
    Bsz, L, D = x.shape
    M = Bsz * L
    x0 = x.reshape(M, D)
    x1 = ffn_block(x0, p['ffn1_pre_g'], p['ffn1_post_g'], p['ffn1_w_gate'], p['ffn1_w_up'], p['ffn1_w_down'])
    w_hy, w_qkv, w_gate = p['w_in']
    hy_in = norm_matmul(x1, p['mix_pre_g'], w_hy, BF16, tm=1024, tn=1024, name="inproj_hyena")
    qkv = norm_matmul(x1, p['mix_pre_g'], w_qkv, BF16, tm=512, tn=D_ATT, rope_len=L, name="inproj_qkv")
    gates = norm_matmul(x1, p['mix_pre_g'], w_gate, F32, tm=1024, tn=1024, name="inproj_gates")
    a2 = hyena_branch(hy_in, L, p)
    att = [dilated_attention_group(qkv, Bsz, L, g) if ATT_GROUPS[g][1] == 1 else
           dilated_attention_slabs(qkv, Bsz, L, g) for g in range(N_GROUPS)]
    x2 = merge_out(x1, a2, [o for o, _ in att], [l for _, l in att], gates, p['mix_post_g'],
                   p['w_hy_proj'], p['w_att_proj'], p['w_out'])
    x3 = ffn_block(x2, p['ffn2_pre_g'], p['ffn2_post_g'], p['ffn2_w_gate'], p['ffn2_w_up'], p['ffn2_w_down'])
    return x3.reshape(Bsz, L, D)


def kernel(x_prompt, x_sample, ffn1_pre_g, ffn1_post_g, ffn1_w_gate, ffn1_w_up, ffn1_w_down, mix_pre_g, mix_post_g, w_in, hy_conv_w, hy_conv_b, filt_w1, filt_b1, filt_w2, filt_b2, filt_w3, filt_b3, filt_w4, filt_freq, hy_bias, w_hy_proj, w_att_proj, w_out, ffn2_pre_g, ffn2_post_g, ffn2_w_gate, ffn2_w_up, ffn2_w_down):
    assert ffn1_w_gate.shape[0] == 1
    p = {
        'ffn1_pre_g': ffn1_pre_g, 'ffn1_post_g': ffn1_post_g,
        'ffn1_w_gate': ffn1_w_gate[0].astype(BF16), 'ffn1_w_up': ffn1_w_up[0].astype(BF16),
        'ffn1_w_down': ffn1_w_down[0].astype(BF16),
        'mix_pre_g': mix_pre_g, 'mix_post_g': mix_post_g,
        'w_in': tuple(w_in[0][:, a:b].astype(BF16) for a, b in (
            (0, 3 * D_HYENA), (3 * D_HYENA, 3 * D_HYENA + 3 * D_ATT), (3 * D_HYENA + 3 * D_ATT, D_IN_PROJ))),
        'hy_conv_w': hy_conv_w[0], 'hy_conv_b': hy_conv_b,
        'filt': (filt_w1[0], filt_b1[0], filt_w2[0], filt_b2[0], filt_w3[0], filt_b3[0], filt_w4[0], filt_freq[0]),
        'hy_bias': hy_bias[0],
        'w_hy_proj': w_hy_proj[0].astype(BF16), 'w_att_proj': w_att_proj[0].astype(BF16),
        'w_out': w_out[0].astype(BF16),
        'ffn2_pre_g': ffn2_pre_g, 'ffn2_post_g': ffn2_post_g,
        'ffn2_w_gate': ffn2_w_gate[0].astype(BF16), 'ffn2_w_up': ffn2_w_up[0].astype(BF16),
        'ffn2_w_down': ffn2_w_down[0].astype(BF16),
    }
    return (_layer(x_prompt, p), _layer(x_sample, p))
```

```python
import math
from functools import partial

import numpy as np
import jax
import jax.numpy as jnp
from jax import lax
from jax.experimental import pallas as pl
from jax.experimental.pallas import tpu as pltpu

D_MODEL = 2048
D_HYENA = 1024
HYENA_ORDER = 2
SHORT_CONV = 3
FILTER_EMB = 33
FILTER_BANDS = (FILTER_EMB - 1) // 2
FAST_DECAY_PCT = 0.3
SLOW_DECAY_PCT = 1.5
DECAY_TARGET = 1e-2
HEAD_DIM = 128
HEADS_PER_GROUP = 4
ATT_GROUPS = ((128, 1), (512, 4), (2048, 16))
N_GROUPS = len(ATT_GROUPS)
D_ATT = N_GROUPS * HEADS_PER_GROUP * HEAD_DIM
D_ATT_OUT = HEADS_PER_GROUP * HEAD_DIM
ROPE_DIM = HEAD_DIM // 4
ROPE_THETA = 500000.0
N_BRANCH = 2
D_IN_PROJ = 3 * D_HYENA + 3 * D_ATT + N_BRANCH * D_MODEL
D_FF = 5632
EPS = 1e-6
NEG_INF = -1e30

LANES = 128
FFT_INNER = 256
VMEM_BYTES_V7X = 64 * 1024 * 1024
VMEM_LIMIT_BYTES = VMEM_BYTES_V7X - 4 * 1024 * 1024
BF16 = jnp.bfloat16
F32 = jnp.float32
HIGHEST = lax.Precision.HIGHEST


def _rms(x, g):
    return x * lax.rsqrt(jnp.mean(x * x, axis=-1, keepdims=True) + EPS) * g


def _dot(a, b):
    return jnp.dot(a, b, preferred_element_type=F32)


def _params(*sem):
    return pltpu.CompilerParams(dimension_semantics=sem, vmem_limit_bytes=VMEM_LIMIT_BYTES)


def _ffn_kernel(x_ref, pre_g_ref, post_g_ref, wg_ref, wu_ref, wd_ref, o_ref, xn_ref):
    j = pl.program_id(1)

    @pl.when(j == 0)
    def _():
        xn_ref[...] = _rms(x_ref[...], pre_g_ref[...]).astype(BF16)
        o_ref[...] = jnp.zeros_like(o_ref)

    xn = xn_ref[...]
    gate = _dot(xn, wg_ref[...])
    up = _dot(xn, wu_ref[...])
    h = (gate * jax.nn.sigmoid(gate) * up).astype(BF16)
    o_ref[...] += _dot(h, wd_ref[...])

    @pl.when(j == pl.num_programs(1) - 1)
    def _():
        o_ref[...] = x_ref[...] + 0.5 * _rms(o_ref[...], post_g_ref[...])


def ffn_block(x, pre_g, post_g, wg, wu, wd, *, tm=512, tf=512):
    M, D = x.shape
    FF = wg.shape[1]
    return pl.pallas_call(
        _ffn_kernel,
        grid=(M // tm, FF // tf),
        in_specs=[
            pl.BlockSpec((tm, D), lambda i, j: (i, 0)),
            pl.BlockSpec((1, D), lambda i, j: (0, 0)),
            pl.BlockSpec((1, D), lambda i, j: (0, 0)),
            pl.BlockSpec((D, tf), lambda i, j: (0, j)),
            pl.BlockSpec((D, tf), lambda i, j: (0, j)),
            pl.BlockSpec((tf, D), lambda i, j: (j, 0)),
        ],
        out_specs=pl.BlockSpec((tm, D), lambda i, j: (i, 0)),
        out_shape=jax.ShapeDtypeStruct((M, D), F32),
        scratch_shapes=[pltpu.VMEM((tm, D), BF16)],
        compiler_params=_params("parallel", "arbitrary"),
        name="ffn_block",
    )(x, pre_g, post_g, wg, wu, wd)


def rope_tables(L):
    half = ROPE_DIM // 2
    inv_freq = jnp.power(ROPE_THETA, -jnp.arange(half, dtype=F32) / half)
    ang = jnp.arange(L, dtype=F32)[:, None] * inv_freq[None, :]
    cos, sin = jnp.cos(ang), jnp.sin(ang)
    rest = HEAD_DIM - ROPE_DIM
    c = jnp.concatenate([cos, cos, jnp.ones((L, rest), F32)], axis=1)
    s_lo = jnp.concatenate([-sin, jnp.zeros((L, half + rest), F32)], axis=1)
    s_hi = jnp.concatenate([jnp.zeros((L, half), F32), sin, jnp.zeros((L, rest), F32)], axis=1)
    scale = HEAD_DIM ** -0.5
    zero = jnp.zeros_like(c)
    return (jnp.stack([c * scale, c, jnp.ones_like(c)]), jnp.stack([s_lo * scale, s_lo, zero]),
            jnp.stack([s_hi * scale, s_hi, zero]))


def _norm_matmul_kernel(rotate, x_ref, g_ref, w_ref, *rest):
    if rotate:
        c_ref, slo_ref, shi_ref, o_ref, xn_ref = rest
    else:
        o_ref, xn_ref = rest

    @pl.when(pl.program_id(1) == 0)
    def _():
        xn_ref[...] = _rms(x_ref[...], g_ref[...]).astype(BF16)

    r = _dot(xn_ref[...], w_ref[...])
    if rotate:
        tn = r.shape[1]
        half = ROPE_DIM // 2
        wide = lambda t_ref: jnp.concatenate([t_ref[...]] * (tn // HEAD_DIM), axis=1)
        r = r * wide(c_ref) + pltpu.roll(r, tn - half, 1) * wide(slo_ref) + pltpu.roll(r, half, 1) * wide(shi_ref)
    o_ref[...] = r.astype(o_ref.dtype)


def norm_matmul(x, g, w, out_dtype, *, tm, tn, rope_len=None, name):
    M, D = x.shape
    N = w.shape[1]
    in_specs = [
        pl.BlockSpec((tm, D), lambda i, j: (i, 0)),
        pl.BlockSpec((1, D), lambda i, j: (0, 0)),
        pl.BlockSpec((D, tn), lambda i, j: (0, j)),
    ]
    args = [x, g, w]
    if rope_len is not None:
        assert tn == D_ATT and N == 3 * D_ATT and rope_len % tm == 0
        spec = pl.BlockSpec((None, tm, HEAD_DIM), lambda i, j: (j, i % (rope_len // tm), 0))
        in_specs += [spec, spec, spec]
        args += list(rope_tables(rope_len))
    return pl.pallas_call(
        partial(_norm_matmul_kernel, rope_len is not None),
        grid=(M // tm, N // tn),
        in_specs=in_specs,
        out_specs=pl.BlockSpec((tm, tn), lambda i, j: (i, j)),
        out_shape=jax.ShapeDtypeStruct((M, N), out_dtype),
        scratch_shapes=[pltpu.VMEM((tm, D), BF16)],
        compiler_params=_params("parallel", "arbitrary"),
        name=name,
    )(*args)


ATT_RADIUS = 64
assert all(w // (2 * d) == ATT_RADIUS for w, d in ATT_GROUPS)


def _attn_kernel(T, Ls, q_ref, kp_ref, kc_ref, kn_ref, vp_ref, vc_ref, vn_ref, o_ref, lse_ref, kbuf, vbuf):
    t = pl.program_id(2)
    R = ATT_RADIUS
    SB = 2 * R
    for buf, prv, cur, nxt in ((kbuf, kp_ref, kc_ref, kn_ref), (vbuf, vp_ref, vc_ref, vn_ref)):
        buf[0:R] = prv[...]
        buf[R:R + T] = cur[...]
        buf[R + T:R + T + R] = nxt[...]
    qi = lax.broadcasted_iota(jnp.int32, (SB, 2 * SB), 0)
    kk = lax.broadcasted_iota(jnp.int32, (SB, 2 * SB), 1)
    band = (kk >= qi) & (kk <= qi + 2 * R)
    lane = lax.broadcasted_iota(jnp.int32, (SB, LANES), 1)
    for sb in range(T // SB):
        kpos = t * T + (sb * SB - R) + kk
        mask = band & (kpos >= 0) & (kpos < Ls)
        lse = jnp.zeros((SB, LANES), F32)
        for h in range(HEADS_PER_GROUP):
            cols = slice(h * HEAD_DIM, (h + 1) * HEAD_DIM)
            q = q_ref[sb * SB:(sb + 1) * SB, cols]
            k = kbuf[sb * SB:(sb + 2) * SB, cols]
            v = vbuf[sb * SB:(sb + 2) * SB, cols]
            s = lax.dot_general(q, k, (((1,), (1,)), ((), ())), preferred_element_type=F32)
            s = jnp.where(mask, s, NEG_INF)
            m = jnp.max(s, axis=1, keepdims=True)
            p = jnp.exp(s - m)
            l = jnp.sum(p, axis=1, keepdims=True)
            o = _dot(p.astype(BF16), v) / l
            o_ref[sb * SB:(sb + 1) * SB, cols] = o.astype(o_ref.dtype)
            lse = jnp.where(lane == h, m + jnp.log(l), lse)
        lse_ref[sb * SB:(sb + 1) * SB, :] = lse


SLAB = 16


def _softmax_heads(q, k, v, mask):
    lane = lax.broadcasted_iota(jnp.int32, (q.shape[0], LANES), 1)
    lse = jnp.zeros((q.shape[0], LANES), F32)
    outs = []
    for h in range(HEADS_PER_GROUP):
        cols = slice(h * HEAD_DIM, (h + 1) * HEAD_DIM)
        s = lax.dot_general(q[:, cols], k[:, cols], (((1,), (1,)), ((), ())), preferred_element_type=F32)
        s = jnp.where(mask, s, NEG_INF)
        m = jnp.max(s, axis=1, keepdims=True)
        p = jnp.exp(s - m)
        l = jnp.sum(p, axis=1, keepdims=True)
        outs.append(_dot(p.astype(BF16), v[:, cols]) / l)
        lse = jnp.where(lane == h, m + jnp.log(l), lse)
    return jnp.concatenate(outs, axis=1), lse


def _attn_slab_kernel(d, NS, Ls, q_ref, kp_ref, kc_ref, kn_ref, vp_ref, vc_ref, vn_ref, o_ref, lse_ref,
                      qs, ks, vs, os_, ls):
    t = pl.program_id(1)
    R = ATT_RADIUS
    SB = 2 * R
    J = SLAB // d
    NH = kp_ref.shape[0]
    QA = SB // J
    KA = QA + 2 * NH
    n_sb = NS // QA
    to_class_major = lambda ref: pltpu.einshape("abc->bac", ref[...])
    qs[...] = to_class_major(q_ref)
    for buf, prv, cur, nxt in ((ks, kp_ref, kc_ref, kn_ref), (vs, vp_ref, vc_ref, vn_ref)):
        buf[:, 0:NH] = to_class_major(prv)
        buf[:, NH:NH + NS] = to_class_major(cur)
        buf[:, NH + NS:NH + NS + NH] = to_class_major(nxt)
    qi = lax.broadcasted_iota(jnp.int32, (SB, 2 * SB), 0)
    kk = lax.broadcasted_iota(jnp.int32, (SB, 2 * SB), 1)
    sq_rel = J * (qi % QA) + qi // QA
    sk_rel = J * (kk % KA - NH) + kk // KA
    band = jnp.abs(sk_rel - sq_rel) <= R

    def body(it, carry):
        r = it // n_sb
        a0 = pl.multiple_of((it % n_sb) * QA, QA)
        sk = J * (t * NS + a0) + sk_rel
        mask = band & (sk >= 0) & (sk < Ls)
        gather = lambda buf, n: jnp.concatenate([buf[j * d + r, pl.ds(a0, n), :] for j in range(J)], axis=0)
        o, lse = _softmax_heads(gather(qs, QA), gather(ks, KA), gather(vs, KA), mask)
        o = o.astype(os_.dtype)
        for j in range(J):
            os_[j * d + r, pl.ds(a0, QA), :] = o[j * QA:(j + 1) * QA]
            ls[j * d + r, pl.ds(a0, QA), :] = lse[j * QA:(j + 1) * QA]
        return carry

    lax.fori_loop(0, d * n_sb, body, 0)
    o_ref[...] = pltpu.einshape("bac->abc", os_[...])
    lse_ref[...] = pltpu.einshape("bac->abc", ls[...])


def dilated_attention_slabs(qkv, Bsz, L, g, *, NS=128):
    d = ATT_GROUPS[g][1]
    M = Bsz * L
    R = ATT_RADIUS
    GW = HEADS_PER_GROUP * HEAD_DIM
    NH = R * d // SLAB
    TB = NS * SLAB
    assert SLAB % d == 0 and L % TB == 0 and NS % NH == 0 and (2 * R) % (SLAB // d) == 0
    nmb, nhb, hpm = L // TB, L // (NH * SLAB), NS // NH
    view = qkv.reshape(M // SLAB, SLAB, 3 * D_ATT)
    col = lambda which: which * N_GROUPS + g

    def main(which):
        return pl.BlockSpec((NS, SLAB, GW), lambda b, t: (b * nmb + t, 0, col(which)))

    def prev(which):
        return pl.BlockSpec((NH, SLAB, GW), lambda b, t: (jnp.maximum(b * nhb + t * hpm - 1, b * nhb), 0, col(which)))

    def nxt(which):
        return pl.BlockSpec((NH, SLAB, GW),
                            lambda b, t: (jnp.minimum(b * nhb + (t + 1) * hpm, (b + 1) * nhb - 1), 0, col(which)))

    o, lse = pl.pallas_call(
        partial(_attn_slab_kernel, d, NS, L // d),
        grid=(Bsz, nmb),
        in_specs=[main(0), prev(1), main(1), nxt(1), prev(2), main(2), nxt(2)],
        out_specs=[pl.BlockSpec((NS, SLAB, GW), lambda b, t: (b * nmb + t, 0, 0)),
                   pl.BlockSpec((NS, SLAB, LANES), lambda b, t: (b * nmb + t, 0, 0))],
        out_shape=[jax.ShapeDtypeStruct((M // SLAB, SLAB, GW), BF16),
                   jax.ShapeDtypeStruct((M // SLAB, SLAB, LANES), F32)],
        scratch_shapes=[pltpu.VMEM((SLAB, NS, GW), BF16),
                        pltpu.VMEM((SLAB, NS + 2 * NH, GW), BF16), pltpu.VMEM((SLAB, NS + 2 * NH, GW), BF16),
                        pltpu.VMEM((SLAB, NS, GW), BF16), pltpu.VMEM((SLAB, NS, LANES), F32)],
        compiler_params=_params("parallel", "arbitrary"),
        name=f"dilated_attention_g{g}",
    )(view, view, view, view, view, view, view)
    return o.reshape(M, GW), lse.reshape(M, LANES)


def dilated_attention_group(qkv, Bsz, L, g, *, T=256):
    d = ATT_GROUPS[g][1]
    M = Bsz * L
    Ls = L // d
    R = ATT_RADIUS
    GW = HEADS_PER_GROUP * HEAD_DIM
    assert L % d == 0 and Ls % T == 0 and T % (2 * R) == 0
    nrb, nhb, hpt = Ls // T, Ls // R, T // R
    ncol = 3 * N_GROUPS
    view = qkv.reshape(M // d, d * 3 * D_ATT)

    def main(which):
        return pl.BlockSpec((T, GW), lambda b, r, t: (b * nrb + t, r * ncol + which * N_GROUPS + g))

    def prev(which):
        return pl.BlockSpec((R, GW), lambda b, r, t: (jnp.maximum(b * nhb + t * hpt - 1, b * nhb),
                                                      r * ncol + which * N_GROUPS + g))

    def nxt(which):
        return pl.BlockSpec((R, GW), lambda b, r, t: (jnp.minimum(b * nhb + (t + 1) * hpt, (b + 1) * nhb - 1),
                                                      r * ncol + which * N_GROUPS + g))

    o, lse = pl.pallas_call(
        partial(_attn_kernel, T, Ls),
        grid=(Bsz, d, nrb),
        in_specs=[main(0), prev(1), main(1), nxt(1), prev(2), main(2), nxt(2)],
        out_specs=[pl.BlockSpec((T, GW), lambda b, r, t: (b * nrb + t, r)),
                   pl.BlockSpec((T, LANES), lambda b, r, t: (b * nrb + t, r))],
        out_shape=[jax.ShapeDtypeStruct((M // d, d * GW), BF16),
                   jax.ShapeDtypeStruct((M // d, d * LANES), F32)],
        scratch_shapes=[pltpu.VMEM((T + 2 * R, GW), BF16), pltpu.VMEM((T + 2 * R, GW), BF16)],
        compiler_params=_params("parallel", "parallel", "arbitrary"),
        name=f"dilated_attention_g{g}",
    )(view, view, view, view, view, view, view)
    return o.reshape(M, GW), lse.reshape(M, LANES)


def _merge_kernel(x_ref, a_ref, o0_ref, o1_ref, o2_ref, l0_ref, l1_ref, l2_ref, ga_ref, gb_ref,
                  post_g_ref, whp_ref, wap_ref, wo_ref, o_ref):
    lses = [l0_ref[...], l1_ref[...], l2_ref[...]]
    mx = jnp.maximum(jnp.maximum(lses[0], lses[1]), lses[2])
    es = [jnp.exp(l - mx) for l in lses]
    den = es[0] + es[1] + es[2]
    wts = [e / den for e in es]
    outs = [o0_ref, o1_ref, o2_ref]
    heads = []
    for h in range(HEADS_PER_GROUP):
        cols = slice(h * HEAD_DIM, (h + 1) * HEAD_DIM)
        heads.append(sum(wts[g][:, h:h + 1] * outs[g][:, cols].astype(F32) for g in range(N_GROUPS)))
    att = jnp.concatenate(heads, axis=1).astype(BF16)
    a = _dot(a_ref[...], whp_ref[...])
    b = _dot(att, wap_ref[...])
    merged = (jax.nn.sigmoid(ga_ref[...]) * a + jax.nn.sigmoid(gb_ref[...]) * b).astype(BF16)
    mix = _dot(merged, wo_ref[...])
    o_ref[...] = x_ref[...] + _rms(mix, post_g_ref[...])


def merge_out(x, a_in, att_outs, att_lses, gates, post_g, whp, wap, wo, *, tm=256):
    M, D = x.shape
    const = lambda i: (0, 0)
    rows = lambda a: pl.BlockSpec((tm, a.shape[1]), lambda i: (i, 0))
    return pl.pallas_call(
        _merge_kernel,
        grid=(M // tm,),
        in_specs=[rows(x), rows(a_in)] + [rows(o) for o in att_outs] + [rows(l) for l in att_lses] + [
            pl.BlockSpec((tm, D), lambda i: (i, 0)),
            pl.BlockSpec((tm, D), lambda i: (i, 1)),
            pl.BlockSpec((1, D), const),
            pl.BlockSpec(whp.shape, const),
            pl.BlockSpec(wap.shape, const),
            pl.BlockSpec(wo.shape, const),
        ],
        out_specs=pl.BlockSpec((tm, D), lambda i: (i, 0)),
        out_shape=jax.ShapeDtypeStruct((M, D), F32),
        compiler_params=_params("parallel"),
        name="merge_out",
    )(x, a_in, *att_outs, *att_lses, gates, gates, post_g, whp, wap, wo)


def _dft_constants(L):
    N = 2 * L
    N2 = FFT_INNER
    N1 = N // N2
    h = N1 // 2
    idx1 = np.arange(N1)
    ang1 = -2.0 * np.pi * ((idx1[:, None] * idx1[None, :]) % N1) / N1
    f1r, f1i = np.cos(ang1), np.sin(ang1)
    s_data = np.block([[f1r[:, :h], -f1i[:, :h]], [f1i[:, :h], f1r[:, :h]]])
    s_filt = np.concatenate([f1r, f1i], axis=0)
    ar, ai = f1r[:h, :], -f1i[:h, :]
    t_fin = np.block([[ar, -ai], [ai, ar]])
    idx2 = np.arange(N2)
    ang2 = -2.0 * np.pi * ((idx2[:, None] * idx2[None, :]) % N2) / N2
    angt = -2.0 * np.pi * (idx1[:, None] * idx2[None, :]) / N
    return dict(
        N1=N1,
        s_data=jnp.asarray(s_data, BF16), s_filt=jnp.asarray(s_filt, BF16), t_fin=jnp.asarray(t_fin, BF16),
        f2r=jnp.asarray(np.cos(ang2), F32), f2i=jnp.asarray(np.sin(ang2), F32),
        twr=jnp.asarray(np.cos(angt).reshape(N1, 1, N2), F32),
        twi=jnp.asarray(np.sin(angt).reshape(N1, 1, N2), F32),
    )


def _shortconv_kernel(tl, L, x_ref, prev_ref, next_ref, w_ref, b_ref, o_ref):
    i = pl.program_id(1)
    x = x_ref[...].astype(F32)
    row = lax.broadcasted_iota(jnp.int32, x.shape, 0)
    pos0 = (i * tl) % L
    halo = prev_ref.shape[0]
    prev_row = jnp.where(pos0 == 0, 0.0, prev_ref[halo - 1:halo, :].astype(F32))
    next_row = jnp.where(pos0 + tl == L, 0.0, next_ref[0:1, :].astype(F32))
    xm = jnp.where(row == 0, prev_row, pltpu.roll(x, 1, 0))
    xp = jnp.where(row == tl - 1, next_row, pltpu.roll(x, tl - 1, 0))
    y = xm * w_ref[0:1, :] + x * w_ref[1:2, :] + xp * w_ref[2:3, :] + b_ref[...]
    o_ref[0] = y.astype(o_ref.dtype)


def hyena_shortconv(hy, w, b, L, *, tl=512):
    M = hy.shape[0]
    C = hy.shape[1] // 3
    halo = SLAB
    nhalo = M // halo
    return pl.pallas_call(
        partial(_shortconv_kernel, tl, L),
        grid=(3, M // tl),
        in_specs=[
            pl.BlockSpec((tl, C), lambda j, i: (i, j)),
            pl.BlockSpec((halo, C), lambda j, i: (jnp.maximum(i * (tl // halo) - 1, 0), j)),
            pl.BlockSpec((halo, C), lambda j, i: (jnp.minimum((i + 1) * (tl // halo), nhalo - 1), j)),
            pl.BlockSpec((SHORT_CONV, C), lambda j, i: (0, j)),
            pl.BlockSpec((1, C), lambda j, i: (0, j)),
        ],
        out_specs=pl.BlockSpec((1, tl, C), lambda j, i: (j, i, 0)),
        out_shape=jax.ShapeDtypeStruct((3, M, C), BF16),
        compiler_params=_params("parallel", "parallel"),
        name="hyena_shortconv",
    )(hy, hy, hy, w, b)


def _filter_kernel(L, tl, w1_ref, b1_ref, w2_ref, b2_ref, w3_ref, b3_ref, w4hi_ref, w4lo_ref,
                   freq_ref, fvec_ref, delta_ref, kf_ref):
    i = pl.program_id(0)
    HALF = LANES // 2
    wide = delta_ref.shape[1]
    C = wide // (2 * HYENA_ORDER)
    row = i * tl + lax.broadcasted_iota(jnp.int32, (tl, LANES), 0)
    lane = lax.broadcasted_iota(jnp.int32, (tl, LANES), 1)
    posf = jnp.where(lane < HALF, row, L - row).astype(F32)
    t = posf / (L - 1)
    a = fvec_ref[...] * ((2.0 * math.pi / L) * posf)
    lh = lane % HALF
    feats = jnp.where(lh < FILTER_BANDS, jnp.cos(a),
                      jnp.where(lh < 2 * FILTER_BANDS, -jnp.sin(a),
                                jnp.where(lh == 2 * FILTER_BANDS, t, 0.0)))
    freq = freq_ref[...]
    dense = lambda v, w_ref, b_ref: jnp.sin(freq * (
        jnp.dot(v, w_ref[...], precision=HIGHEST, preferred_element_type=F32) + b_ref[...]))
    h = dense(dense(dense(feats, w1_ref, b1_ref), w2_ref, b2_ref), w3_ref, b3_ref)
    h_hi = h.astype(BF16)
    h_lo = (h - h_hi.astype(F32)).astype(BF16)
    taps = _dot(h_hi, w4hi_ref[...]) + _dot(h_lo, w4hi_ref[...]) + _dot(h_hi, w4lo_ref[...])
    row_w = i * tl + lax.broadcasted_iota(jnp.int32, (tl, wide), 0)
    col_w = lax.broadcasted_iota(jnp.int32, (tl, wide), 1)
    backward = col_w >= wide // 2
    t_w = jnp.where(backward, L - row_w, row_w).astype(F32) / (L - 1)
    taps = taps * jnp.exp(-t_w * delta_ref[...])
    taps = jnp.where(backward & (row_w == 0), 0.0, taps)
    for n in range(HYENA_ORDER):
        for direction in range(2):
            c0 = (direction * HYENA_ORDER + n) * C
            kf_ref[n, direction] = taps[:, c0:c0 + C].astype(kf_ref.dtype)


def hyena_filter_taps(L, fw1, fb1, fw2, fb2, fw3, fb3, fw4, ffreq, *, tl=256):
    C = D_HYENA
    H = fw2.shape[0]
    HALF = LANES // 2
    assert H <= HALF and 2 * FILTER_BANDS + 1 <= HALF

    def both(a, rows):
        blk = jnp.zeros((HALF if rows else 1, HALF), F32).at[:a.shape[0], :a.shape[1]].set(a)
        if not rows:
            return jnp.concatenate([blk, blk], axis=1)
        zero = jnp.zeros_like(blk)
        return jnp.concatenate([jnp.concatenate([blk, zero], axis=1), jnp.concatenate([zero, blk], axis=1)], axis=0)

    w1 = both(jnp.concatenate([fw1[1:], fw1[:1]], axis=0), True)
    w2, w3 = both(fw2, True), both(fw3, True)
    b1, b2, b3, freq = (both(v[None, :], False) for v in (fb1, fb2, fb3, ffreq))
    w4 = fw4.reshape(H, HYENA_ORDER, 2, C)
    wide = 2 * HYENA_ORDER * C
    w4p = jnp.zeros((LANES, wide), F32)
    w4p = w4p.at[:H, :wide // 2].set(w4[:, :, 0].reshape(H, HYENA_ORDER * C))
    w4p = w4p.at[HALF:HALF + H, wide // 2:].set(w4[:, :, 1].reshape(H, HYENA_ORDER * C))
    w4hi = w4p.astype(BF16)
    w4lo = (w4p - w4hi.astype(F32)).astype(BF16)
    bands = np.linspace(1e-4, FILTER_BANDS - 1, FILTER_BANDS, dtype=np.float32)
    fvec = np.zeros((1, LANES), np.float32)
    for base in (0, HALF):
        fvec[0, base:base + FILTER_BANDS] = bands
        fvec[0, base + FILTER_BANDS:base + 2 * FILTER_BANDS] = bands
    max_decay = math.log(DECAY_TARGET) / FAST_DECAY_PCT
    min_decay = math.log(DECAY_TARGET) / SLOW_DECAY_PCT
    deltas = np.abs(np.linspace(min_decay, max_decay, C, dtype=np.float32))
    delta_w = np.tile(deltas[None, :], (1, 2 * HYENA_ORDER))
    const = lambda i: (0, 0)
    args = (w1, b1, w2, b2, w3, b3, w4hi, w4lo, freq, jnp.asarray(fvec), jnp.asarray(delta_w))
    return pl.pallas_call(
        partial(_filter_kernel, L, tl),
        grid=(L // tl,),
        in_specs=[pl.BlockSpec(a.shape, const) for a in args],
        out_specs=pl.BlockSpec((HYENA_ORDER, 2, tl, C), lambda i: (0, 0, i, 0)),
        out_shape=jax.ShapeDtypeStruct((HYENA_ORDER, 2, L, C), BF16),
        compiler_params=_params("parallel"),
        name="hyena_filter_taps",
    )(*args)


FFT_ROWS = 16
FFT_COLS = 512


def _stage1_kernel(s_ref, x_ref, y_ref):
    _, h, rows, cols = x_ref.shape
    s = s_ref[...]
    xt = pltpu.einshape("abc->bac", x_ref[...].reshape(2 * h, rows, cols))
    yt = jnp.stack([_dot(s, xt[b]).astype(y_ref.dtype) for b in range(rows)], axis=0)
    y_ref[...] = pltpu.einshape("bac->abc", yt).reshape(y_ref.shape)


def fft_stage1(x, which, s):
    _, _, h, N2, C = x.shape
    N1 = 2 * h
    return pl.pallas_call(
        _stage1_kernel,
        grid=(N2 // FFT_ROWS, C // FFT_COLS),
        in_specs=[pl.BlockSpec((2 * N1, N1), lambda j, c: (0, 0)),
                  pl.BlockSpec((None, 2, h, FFT_ROWS, FFT_COLS), lambda j, c: (which, 0, 0, j, c))],
        out_specs=pl.BlockSpec((2, N1, FFT_ROWS, FFT_COLS), lambda j, c: (0, 0, j, c)),
        out_shape=jax.ShapeDtypeStruct((2, N1, N2, C), BF16),
        compiler_params=_params("parallel", "parallel"),
        name="fft_stage1",
    )(s, x)


FFT_K1_PER_STEP = 2


def _mid_kernel(inv_n, f2r_ref, f2i_ref, twr_ref, twi_ref, yf_ref, y_ref, u_ref):
    f2r, f2i = f2r_ref[...], f2i_ref[...]
    for k in range(y_ref.shape[1]):
        twr, twi = twr_ref[k], twi_ref[k]
        gr, gi = f2r * twr - f2i * twi, f2r * twi + f2i * twr
        grb, gib = gr.astype(BF16), gi.astype(BF16)

        def inner(ref):
            re, im = ref[0, k], ref[1, k]
            return _dot(grb, re) - _dot(gib, im), _dot(gib, re) + _dot(grb, im)

        hr, hi = inner(yf_ref)
        zr, zi = inner(y_ref)
        pr = (zr * hr - zi * hi).astype(BF16)
        pi = (zr * hi + zi * hr).astype(BF16)
        irb = (gr.T * inv_n).astype(BF16)
        iib = (gi.T * (-inv_n)).astype(BF16)
        u_ref[0, k] = (_dot(irb, pr) - _dot(iib, pi)).astype(u_ref.dtype)
        u_ref[1, k] = (_dot(iib, pr) + _dot(irb, pi)).astype(u_ref.dtype)


def fft_mid(yf, y, dc):
    _, N1, N2, C = y.shape
    kb = FFT_K1_PER_STEP
    blk = pl.BlockSpec((2, kb, N2, C), lambda k: (0, k, 0, 0))
    const = pl.BlockSpec((N2, N2), lambda k: (0, 0))
    tw = pl.BlockSpec((kb, 1, N2), lambda k: (k, 0, 0))
    return pl.pallas_call(
        partial(_mid_kernel, 1.0 / (N1 * N2)),
        grid=(N1 // kb,),
        in_specs=[const, const, tw, tw, blk, blk],
        out_specs=blk,
        out_shape=jax.ShapeDtypeStruct((2, N1, N2, C), BF16),
        compiler_params=_params("parallel"),
        name="fft_mid",
    )(dc['f2r'], dc['f2i'], dc['twr'], dc['twi'], yf, y)


def _final_kernel(t_ref, u_ref, z_ref, gate_ref, bias_ref, o_ref):
    _, n1, rows, cols = u_ref.shape
    t = t_ref[...]
    ut = pltpu.einshape("abc->bac", u_ref[...].reshape(2 * n1, rows, cols))
    convt = jnp.stack([_dot(t, ut[b]) for b in range(rows)], axis=0)
    conv = pltpu.einshape("bac->abc", convt).reshape(o_ref.shape)
    z = z_ref[...].astype(F32)
    o_ref[...] = (gate_ref[...].astype(F32) * (conv + bias_ref[...] * z)).astype(o_ref.dtype)


def fft_final(u, t_fin, z, z_which, gate, gate_which, bias, bias_which):
    _, N1, N2, C = u.shape
    h = N1 // 2

    def half(which):
        return pl.BlockSpec((None, 2, h, FFT_ROWS, FFT_COLS), lambda j, c: (which, 0, 0, j, c))

    return pl.pallas_call(
        _final_kernel,
        grid=(N2 // FFT_ROWS, C // FFT_COLS),
        in_specs=[pl.BlockSpec((N1, 2 * N1), lambda j, c: (0, 0)),
                  pl.BlockSpec((2, N1, FFT_ROWS, FFT_COLS), lambda j, c: (0, 0, j, c)),
                  half(z_which), half(gate_which),
                  pl.BlockSpec((None, 1, FFT_COLS), lambda j, c: (bias_which, 0, c))],
        out_specs=half(0),
        out_shape=jax.ShapeDtypeStruct((1, 2, h, N2, C), BF16),
        compiler_params=_params("parallel", "parallel"),
        name="fft_final",
    )(t_fin, u, z, gate, bias.reshape(bias.shape[0], 1, C))


def hyena_branch(hy, L, p):
    M = hy.shape[0]
    assert M == 2 * L, "the batch pair rides as real/imaginary parts"
    C = hy.shape[1] // 3
    dc = _dft_constants(L)
    split = (2, dc['N1'] // 2, FFT_INNER, C)
    taps = hyena_filter_taps(L, *p['filt']).reshape(HYENA_ORDER, *split)
    zs = hyena_shortconv(hy, p['hy_conv_w'], p['hy_conv_b'], L).reshape(3, *split)
    z, z_which = zs, 0
    for n in range(HYENA_ORDER):
        u = fft_mid(fft_stage1(taps, n, dc['s_filt']), fft_stage1(z, z_which, dc['s_data']), dc)
        z, z_which = fft_final(u, dc['t_fin'], z, z_which, zs, n + 1, p['hy_bias'], n), 0
    return z.reshape(M, C)


def _layer(x, p):
    Bsz, L, D = x.shape
    M = Bsz * L
    x0 = x.reshape(M, D)
    x1 = ffn_block(x0, p['ffn1_pre_g'], p['ffn1_post_g'], p['ffn1_w_gate'], p['ffn1_w_up'], p['ffn1_w_down'])
    w_hy, w_qkv, w_gate = p['w_in']
    hy_in = norm_matmul(x1, p['mix_pre_g'], w_hy, BF16, tm=1024, tn=1024, name="inproj_hyena")
    qkv = norm_matmul(x1, p['mix_pre_g'], w_qkv, BF16, tm=512, tn=D_ATT, rope_len=L, name="inproj_qkv")
    gates = norm_matmul(x1, p['mix_pre_g'], w_gate, F32, tm=1024, tn=1024, name="inproj_gates")
    a2 = hyena_branch(hy_in, L, p)
    att = [dilated_attention_group(qkv, Bsz, L, g) if ATT_GROUPS[g][1] == 1 else
           dilated_attention_slabs(qkv, Bsz, L, g) for g in range(N_GROUPS)]
    x2 = merge_out(x1, a2, [o for o, _ in att], [l for _, l in att], gates, p['mix_post_g'],
                   p['w_hy_proj'], p['w_att_proj'], p['w_out'])
    x3 = ffn_block(x2, p['ffn2_pre_g'], p['ffn2_post_g'], p['ffn2_w_gate'], p['ffn2_w_up'], p['ffn2_w_down'])
    return x3.reshape(Bsz, L, D)


def kernel(x_prompt, x_sample, ffn1_pre_g, ffn1_post_g, ffn1_w_gate, ffn1_w_up, ffn1_w_down, mix_pre_g, mix_post_g, w_in, hy_conv_w, hy_conv_b, filt_w1, filt_b1, filt_w2, filt_b2, filt_w3, filt_b3, filt_w4, filt_freq, hy_bias, w_hy_proj, w_att_proj, w_out, ffn2_pre_g, ffn2_post_g, ffn2_w_gate, ffn2_w_up, ffn2_w_down):
    assert ffn1_w_gate.shape[0] == 1
    p = {
        'ffn1_pre_g': ffn1_pre_g, 'ffn1_post_g': ffn1_post_g,
        'ffn1_w_gate': ffn1_w_gate[0].astype(BF16), 'ffn1_w_up': ffn1_w_up[0].astype(BF16),
        'ffn1_w_down': ffn1_w_down[0].astype(BF16),
        'mix_pre_g': mix_pre_g, 'mix_post_g': mix_post_g,
        'w_in': tuple(w_in[0][:, a:b].astype(BF16) for a, b in (
            (0, 3 * D_HYENA), (3 * D_HYENA, 3 * D_HYENA + 3 * D_ATT), (3 * D_HYENA + 3 * D_ATT, D_IN_PROJ))),
        'hy_conv_w': hy_conv_w[0], 'hy_conv_b': hy_conv_b,
        'filt': (filt_w1[0], filt_b1[0], filt_w2[0], filt_b2[0], filt_w3[0], filt_b3[0], filt_w4[0], filt_freq[0]),
        'hy_bias': hy_bias[0],
        'w_hy_proj': w_hy_proj[0].astype(BF16), 'w_att_proj': w_att_proj[0].astype(BF16),
        'w_out': w_out[0].astype(BF16),
        'ffn2_pre_g': ffn2_pre_g, 'ffn2_post_g': ffn2_post_g,
        'ffn2_w_gate': ffn2_w_gate[0].astype(BF16), 'ffn2_w_up': ffn2_w_up[0].astype(BF16),
        'ffn2_w_down': ffn2_w_down[0].astype(BF16),
    }
    return (_layer(x_prompt, p), _layer(x_sample, p))
```

```python
import math
from functools import partial

import numpy as np
import jax
import jax.numpy as jnp
from jax import lax
from jax.experimental import pallas as pl
from jax.experimental.pallas import tpu as pltpu

D_MODEL = 2048
D_HYENA = 1024
HYENA_ORDER = 2
SHORT_CONV = 3
FILTER_EMB = 33
FILTER_BANDS = (FILTER_EMB - 1) // 2
FAST_DECAY_PCT = 0.3
SLOW_DECAY_PCT = 1.5
DECAY_TARGET = 1e-2
HEAD_DIM = 128
HEADS_PER_GROUP = 4
ATT_GROUPS = ((128, 1), (512, 4), (2048, 16))
N_GROUPS = len(ATT_GROUPS)
D_ATT = N_GROUPS * HEADS_PER_GROUP * HEAD_DIM
D_ATT_OUT = HEADS_PER_GROUP * HEAD_DIM
ROPE_DIM = HEAD_DIM // 4
ROPE_THETA = 500000.0
N_BRANCH = 2
D_IN_PROJ = 3 * D_HYENA + 3 * D_ATT + N_BRANCH * D_MODEL
D_FF = 5632
EPS = 1e-6
NEG_INF = -1e30

LANES = 128
FFT_INNER = 256
VMEM_BYTES_V7X = 64 * 1024 * 1024
VMEM_LIMIT_BYTES = VMEM_BYTES_V7X - 4 * 1024 * 1024
BF16 = jnp.bfloat16
F32 = jnp.float32
HIGHEST = lax.Precision.HIGHEST


def _rms(x, g):
    return x * lax.rsqrt(jnp.mean(x * x, axis=-1, keepdims=True) + EPS) * g


def _dot(a, b):
    return jnp.dot(a, b, preferred_element_type=F32)


def _params(*sem):
    return pltpu.CompilerParams(dimension_semantics=sem, vmem_limit_bytes=VMEM_LIMIT_BYTES)


def _ffn_kernel(x_ref, pre_g_ref, post_g_ref, wg_ref, wu_ref, wd_ref, o_ref, xn_ref):
    j = pl.program_id(1)

    @pl.when(j == 0)
    def _():
        xn_ref[...] = _rms(x_ref[...], pre_g_ref[...]).astype(BF16)
        o_ref[...] = jnp.zeros_like(o_ref)

    xn = xn_ref[...]
    gate = _dot(xn, wg_ref[...])
    up = _dot(xn, wu_ref[...])
    h = (gate * jax.nn.sigmoid(gate) * up).astype(BF16)
    o_ref[...] += _dot(h, wd_ref[...])

    @pl.when(j == pl.num_programs(1) - 1)
    def _():
        o_ref[...] = x_ref[...] + 0.5 * _rms(o_ref[...], post_g_ref[...])


def ffn_block(x, pre_g, post_g, wg, wu, wd, *, tm=512, tf=512):
    M, D = x.shape
    FF = wg.shape[1]
    return pl.pallas_call(
        _ffn_kernel,
        grid=(M // tm, FF // tf),
        in_specs=[
            pl.BlockSpec((tm, D), lambda i, j: (i, 0)),
            pl.BlockSpec((1, D), lambda i, j: (0, 0)),
            pl.BlockSpec((1, D), lambda i, j: (0, 0)),
            pl.BlockSpec((D, tf), lambda i, j: (0, j)),
            pl.BlockSpec((D, tf), lambda i, j: (0, j)),
            pl.BlockSpec((tf, D), lambda i, j: (j, 0)),
        ],
        out_specs=pl.BlockSpec((tm, D), lambda i, j: (i, 0)),
        out_shape=jax.ShapeDtypeStruct((M, D), F32),
        scratch_shapes=[pltpu.VMEM((tm, D), BF16)],
        compiler_params=_params("parallel", "arbitrary"),
        name="ffn_block",
    )(x, pre_g, post_g, wg, wu, wd)


def rope_tables(L):
    half = ROPE_DIM // 2
    inv_freq = jnp.power(ROPE_THETA, -jnp.arange(half, dtype=F32) / half)
    ang = jnp.arange(L, dtype=F32)[:, None] * inv_freq[None, :]
    cos, sin = jnp.cos(ang), jnp.sin(ang)
    rest = HEAD_DIM - ROPE_DIM
    c = jnp.concatenate([cos, cos, jnp.ones((L, rest), F32)], axis=1)
    s_lo = jnp.concatenate([-sin, jnp.zeros((L, half + rest), F32)], axis=1)
    s_hi = jnp.concatenate([jnp.zeros((L, half), F32), sin, jnp.zeros((L, rest), F32)], axis=1)
    scale = HEAD_DIM ** -0.5
    zero = jnp.zeros_like(c)
    return (jnp.stack([c * scale, c, jnp.ones_like(c)]), jnp.stack([s_lo * scale, s_lo, zero]),
            jnp.stack([s_hi * scale, s_hi, zero]))


def _norm_matmul_kernel(rotate, x_ref, g_ref, w_ref, *rest):
    if rotate:
        c_ref, slo_ref, shi_ref, o_ref, xn_ref = rest
    else:
        o_ref, xn_ref = rest

    @pl.when(pl.program_id(1) == 0)
    def _():
        xn_ref[...] = _rms(x_ref[...], g_ref[...]).astype(BF16)

    r = _dot(xn_ref[...], w_ref[...])
    if rotate:
        tn = r.shape[1]
        half = ROPE_DIM // 2
        wide = lambda t_ref: jnp.concatenate([t_ref[...]] * (tn // HEAD_DIM), axis=1)
        r = r * wide(c_ref) + pltpu.roll(r, tn - half, 1) * wide(slo_ref) + pltpu.roll(r, half, 1) * wide(shi_ref)
    o_ref[...] = r.astype(o_ref.dtype)


def norm_matmul(x, g, w, out_dtype, *, tm, tn, rope_len=None, name):
    M, D = x.shape
    N = w.shape[1]
    in_specs = [
        pl.BlockSpec((tm, D), lambda i, j: (i, 0)),
        pl.BlockSpec((1, D), lambda i, j: (0, 0)),
        pl.BlockSpec((D, tn), lambda i, j: (0, j)),
    ]
    args = [x, g, w]
    if rope_len is not None:
        assert tn == D_ATT and N == 3 * D_ATT and rope_len % tm == 0
        spec = pl.BlockSpec((None, tm, HEAD_DIM), lambda i, j: (j, i % (rope_len // tm), 0))
        in_specs += [spec, spec, spec]
        args += list(rope_tables(rope_len))
    return pl.pallas_call(
        partial(_norm_matmul_kernel, rope_len is not None),
        grid=(M // tm, N // tn),
        in_specs=in_specs,
        out_specs=pl.BlockSpec((tm, tn), lambda i, j: (i, j)),
        out_shape=jax.ShapeDtypeStruct((M, N), out_dtype),
        scratch_shapes=[pltpu.VMEM((tm, D), BF16)],
        compiler_params=_params("parallel", "arbitrary"),
        name=name,
    )(*args)


ATT_RADIUS = 64
assert all(w // (2 * d) == ATT_RADIUS for w, d in ATT_GROUPS)


def _attn_kernel(T, Ls, q_ref, kp_ref, kc_ref, kn_ref, vp_ref, vc_ref, vn_ref, o_ref, lse_ref, kbuf, vbuf):
    t = pl.program_id(2)
    R = ATT_RADIUS
    SB = 2 * R
    for buf, prv, cur, nxt in ((kbuf, kp_ref, kc_ref, kn_ref), (vbuf, vp_ref, vc_ref, vn_ref)):
        buf[0:R] = prv[...]
        buf[R:R + T] = cur[...]
        buf[R + T:R + T + R] = nxt[...]
    qi = lax.broadcasted_iota(jnp.int32, (SB, 2 * SB), 0)
    kk = lax.broadcasted_iota(jnp.int32, (SB, 2 * SB), 1)
    band = (kk >= qi) & (kk <= qi + 2 * R)
    lane = lax.broadcasted_iota(jnp.int32, (SB, LANES), 1)
    for sb in range(T // SB):
        kpos = t * T + (sb * SB - R) + kk
        mask = band & (kpos >= 0) & (kpos < Ls)
        lse = jnp.zeros((SB, LANES), F32)
        for h in range(HEADS_PER_GROUP):
            cols = slice(h * HEAD_DIM, (h + 1) * HEAD_DIM)
            q = q_ref[sb * SB:(sb + 1) * SB, cols]
            k = kbuf[sb * SB:(sb + 2) * SB, cols]
            v = vbuf[sb * SB:(sb + 2) * SB, cols]
            s = lax.dot_general(q, k, (((1,), (1,)), ((), ())), preferred_element_type=F32)
            s = jnp.where(mask, s, NEG_INF)
            m = jnp.max(s, axis=1, keepdims=True)
            p = jnp.exp(s - m)
            l = jnp.sum(p, axis=1, keepdims=True)
            o = _dot(p.astype(BF16), v) / l
            o_ref[sb * SB:(sb + 1) * SB, cols] = o.astype(o_ref.dtype)
            lse = jnp.where(lane == h, m + jnp.log(l), lse)
        lse_ref[sb * SB:(sb + 1) * SB, :] = lse


SLAB = 16


def _softmax_heads(q, k, v, mask):
    lane = lax.broadcasted_iota(jnp.int32, (q.shape[0], LANES), 1)
    lse = jnp.zeros((q.shape[0], LANES), F32)
    outs = []
    for h in range(HEADS_PER_GROUP):
        cols = slice(h * HEAD_DIM, (h + 1) * HEAD_DIM)
        s = lax.dot_general(q[:, cols], k[:, cols], (((1,), (1,)), ((), ())), preferred_element_type=F32)
        s = jnp.where(mask, s, NEG_INF)
        m = jnp.max(s, axis=1, keepdims=True)
        p = jnp.exp(s - m)
        l = jnp.sum(p, axis=1, keepdims=True)
        outs.append(_dot(p.astype(BF16), v[:, cols]) / l)
        lse = jnp.where(lane == h, m + jnp.log(l), lse)
    return jnp.concatenate(outs, axis=1), lse


def _attn_slab_kernel(d, NS, Ls, q_ref, kp_ref, kc_ref, kn_ref, vp_ref, vc_ref, vn_ref, o_ref, lse_ref,
                      qs, ks, vs, os_, ls):
    t = pl.program_id(1)
    R = ATT_RADIUS
    SB = 2 * R
    J = SLAB // d
    NH = kp_ref.shape[0]
    QA = SB // J
    KA = QA + 2 * NH
    n_sb = NS // QA
    to_class_major = lambda ref: pltpu.einshape("abc->bac", ref[...])
    qs[...] = to_class_major(q_ref)
    for buf, prv, cur, nxt in ((ks, kp_ref, kc_ref, kn_ref), (vs, vp_ref, vc_ref, vn_ref)):
        buf[:, 0:NH] = to_class_major(prv)
        buf[:, NH:NH + NS] = to_class_major(cur)
        buf[:, NH + NS:NH + NS + NH] = to_class_major(nxt)
    qi = lax.broadcasted_iota(jnp.int32, (SB, 2 * SB), 0)
    kk = lax.broadcasted_iota(jnp.int32, (SB, 2 * SB), 1)
    sq_rel = J * (qi % QA) + qi // QA
    sk_rel = J * (kk % KA - NH) + kk // KA
    band = jnp.abs(sk_rel - sq_rel) <= R

    def body(it, carry):
        r = it // n_sb
        a0 = pl.multiple_of((it % n_sb) * QA, QA)
        sk = J * (t * NS + a0) + sk_rel
        mask = band & (sk >= 0) & (sk < Ls)
        gather = lambda buf, n: jnp.concatenate([buf[j * d + r, pl.ds(a0, n), :] for j in range(J)], axis=0)
        o, lse = _softmax_heads(gather(qs, QA), gather(ks, KA), gather(vs, KA), mask)
        o = o.astype(os_.dtype)
        for j in range(J):
            os_[j * d + r, pl.ds(a0, QA), :] = o[j * QA:(j + 1) * QA]
            ls[j * d + r, pl.ds(a0, QA), :] = lse[j * QA:(j + 1) * QA]
        return carry

    lax.fori_loop(0, d * n_sb, body, 0)
    o_ref[...] = pltpu.einshape("bac->abc", os_[...])
    lse_ref[...] = pltpu.einshape("bac->abc", ls[...])


def dilated_attention_slabs(qkv, Bsz, L, g, *, NS=128):
    d = ATT_GROUPS[g][1]
    M = Bsz * L
    R = ATT_RADIUS
    GW = HEADS_PER_GROUP * HEAD_DIM
    NH = R * d // SLAB
    TB = NS * SLAB
    assert SLAB % d == 0 and L % TB == 0 and NS % NH == 0 and (2 * R) % (SLAB // d) == 0
    nmb, nhb, hpm = L // TB, L // (NH * SLAB), NS // NH
    view = qkv.reshape(M // SLAB, SLAB, 3 * D_ATT)
    col = lambda which: which * N_GROUPS + g

    def main(which):
        return pl.BlockSpec((NS, SLAB, GW), lambda b, t: (b * nmb + t, 0, col(which)))

    def prev(which):
        return pl.BlockSpec((NH, SLAB, GW), lambda b, t: (jnp.maximum(b * nhb + t * hpm - 1, b * nhb), 0, col(which)))

    def nxt(which):
        return pl.BlockSpec((NH, SLAB, GW),
                            lambda b, t: (jnp.minimum(b * nhb + (t + 1) * hpm, (b + 1) * nhb - 1), 0, col(which)))

    o, lse = pl.pallas_call(
        partial(_attn_slab_kernel, d, NS, L // d),
        grid=(Bsz, nmb),
        in_specs=[main(0), prev(1), main(1), nxt(1), prev(2), main(2), nxt(2)],
        out_specs=[pl.BlockSpec((NS, SLAB, GW), lambda b, t: (b * nmb + t, 0, 0)),
                   pl.BlockSpec((NS, SLAB, LANES), lambda b, t: (b * nmb + t, 0, 0))],
        out_shape=[jax.ShapeDtypeStruct((M // SLAB, SLAB, GW), BF16),
                   jax.ShapeDtypeStruct((M // SLAB, SLAB, LANES), F32)],
        scratch_shapes=[pltpu.VMEM((SLAB, NS, GW), BF16),
                        pltpu.VMEM((SLAB, NS + 2 * NH, GW), BF16), pltpu.VMEM((SLAB, NS + 2 * NH, GW), BF16),
                        pltpu.VMEM((SLAB, NS, GW), BF16), pltpu.VMEM((SLAB, NS, LANES), F32)],
        compiler_params=_params("parallel", "arbitrary"),
        name=f"dilated_attention_g{g}",
    )(view, view, view, view, view, view, view)
    return o.reshape(M, GW), lse.reshape(M, LANES)


def dilated_attention_group(qkv, Bsz, L, g, *, T=256):
    d = ATT_GROUPS[g][1]
    M = Bsz * L
    Ls = L // d
    R = ATT_RADIUS
    GW = HEADS_PER_GROUP * HEAD_DIM
    assert L % d == 0 and Ls % T == 0 and T % (2 * R) == 0
    nrb, nhb, hpt = Ls // T, Ls // R, T // R
    ncol = 3 * N_GROUPS
    view = qkv.reshape(M // d, d * 3 * D_ATT)

    def main(which):
        return pl.BlockSpec((T, GW), lambda b, r, t: (b * nrb + t, r * ncol + which * N_GROUPS + g))

    def prev(which):
        return pl.BlockSpec((R, GW), lambda b, r, t: (jnp.maximum(b * nhb + t * hpt - 1, b * nhb),
                                                      r * ncol + which * N_GROUPS + g))

    def nxt(which):
        return pl.BlockSpec((R, GW), lambda b, r, t: (jnp.minimum(b * nhb + (t + 1) * hpt, (b + 1) * nhb - 1),
                                                      r * ncol + which * N_GROUPS + g))

    o, lse = pl.pallas_call(
        partial(_attn_kernel, T, Ls),
        grid=(Bsz, d, nrb),
        in_specs=[main(0), prev(1), main(1), nxt(1), prev(2), main(2), nxt(2)],
        out_specs=[pl.BlockSpec((T, GW), lambda b, r, t: (b * nrb + t, r)),
                   pl.BlockSpec((T, LANES), lambda b, r, t: (b * nrb + t, r))],
        out_shape=[jax.ShapeDtypeStruct((M // d, d * GW), BF16),
                   jax.ShapeDtypeStruct((M // d, d * LANES), F32)],
        scratch_shapes=[pltpu.VMEM((T + 2 * R, GW), BF16), pltpu.VMEM((T + 2 * R, GW), BF16)],
        compiler_params=_params("parallel", "parallel", "arbitrary"),
        name=f"dilated_attention_g{g}",
    )(view, view, view, view, view, view, view)
    return o.reshape(M, GW), lse.reshape(M, LANES)


def _merge_kernel(x_ref, a_ref, o0_ref, o1_ref, o2_ref, l0_ref, l1_ref, l2_ref, ga_ref, gb_ref,
                  post_g_ref, whp_ref, wap_ref, wo_ref, o_ref):
    lses = [l0_ref[...], l1_ref[...], l2_ref[...]]
    mx = jnp.maximum(jnp.maximum(lses[0], lses[1]), lses[2])
    es = [jnp.exp(l - mx) for l in lses]
    den = es[0] + es[1] + es[2]
    wts = [e / den for e in es]
    outs = [o0_ref, o1_ref, o2_ref]
    heads = []
    for h in range(HEADS_PER_GROUP):
        cols = slice(h * HEAD_DIM, (h + 1) * HEAD_DIM)
        heads.append(sum(wts[g][:, h:h + 1] * outs[g][:, cols].astype(F32) for g in range(N_GROUPS)))
    att = jnp.concatenate(heads, axis=1).astype(BF16)
    a = _dot(a_ref[...], whp_ref[...])
    b = _dot(att, wap_ref[...])
    merged = (jax.nn.sigmoid(ga_ref[...]) * a + jax.nn.sigmoid(gb_ref[...]) * b).astype(BF16)
    mix = _dot(merged, wo_ref[...])
    o_ref[...] = x_ref[...] + _rms(mix, post_g_ref[...])


def merge_out(x, a_in, att_outs, att_lses, gates, post_g, whp, wap, wo, *, tm=256):
    M, D = x.shape
    const = lambda i: (0, 0)
    rows = lambda a: pl.BlockSpec((tm, a.shape[1]), lambda i: (i, 0))
    return pl.pallas_call(
        _merge_kernel,
        grid=(M // tm,),
        in_specs=[rows(x), rows(a_in)] + [rows(o) for o in att_outs] + [rows(l) for l in att_lses] + [
            pl.BlockSpec((tm, D), lambda i: (i, 0)),
            pl.BlockSpec((tm, D), lambda i: (i, 1)),
            pl.BlockSpec((1, D), const),
            pl.BlockSpec(whp.shape, const),
            pl.BlockSpec(wap.shape, const),
            pl.BlockSpec(wo.shape, const),
        ],
        out_specs=pl.BlockSpec((tm, D), lambda i: (i, 0)),
        out_shape=jax.ShapeDtypeStruct((M, D), F32),
        compiler_params=_params("parallel"),
        name="merge_out",
    )(x, a_in, *att_outs, *att_lses, gates, gates, post_g, whp, wap, wo)


def _dft_constants(L):
    N = 2 * L
    N2 = FFT_INNER
    N1 = N // N2
    h = N1 // 2
    idx1 = np.arange(N1)
    ang1 = -2.0 * np.pi * ((idx1[:, None] * idx1[None, :]) % N1) / N1
    f1r, f1i = np.cos(ang1), np.sin(ang1)
    s_data = np.block([[f1r[:, :h], -f1i[:, :h]], [f1i[:, :h], f1r[:, :h]]])
    s_filt = np.concatenate([f1r, f1i], axis=0)
    ar, ai = f1r[:h, :], -f1i[:h, :]
    t_fin = np.block([[ar, -ai], [ai, ar]])
    idx2 = np.arange(N2)
    ang2 = -2.0 * np.pi * ((idx2[:, None] * idx2[None, :]) % N2) / N2
    angt = -2.0 * np.pi * (idx1[:, None] * idx2[None, :]) / N
    return dict(
        N1=N1,
        s_data=jnp.asarray(s_data, BF16), s_filt=jnp.asarray(s_filt, BF16), t_fin=jnp.asarray(t_fin, BF16),
        f2r=jnp.asarray(np.cos(ang2), F32), f2i=jnp.asarray(np.sin(ang2), F32),
        twr=jnp.asarray(np.cos(angt).reshape(N1, 1, N2), F32),
        twi=jnp.asarray(np.sin(angt).reshape(N1, 1, N2), F32),
    )


def _hyena_in_kernel(tm, L, x_ref, prev_ref, next_ref, g_ref, w_ref, cw_ref, cb_ref, o_ref, xn_ref):
    i = pl.program_id(0)
    H = prev_ref.shape[0]

    @pl.when(pl.program_id(1) == 0)
    def _():
        g = g_ref[...]
        xn_ref[0:H] = _rms(prev_ref[...], g).astype(BF16)
        xn_ref[H:H + tm] = _rms(x_ref[...], g).astype(BF16)
        xn_ref[H + tm:H + tm + H] = _rms(next_ref[...], g).astype(BF16)

    r = _dot(xn_ref[...], w_ref[...])
    row = lax.broadcasted_iota(jnp.int32, r.shape, 0)
    pos0 = (i * tm) % L
    outside = ((row < H) & (pos0 == 0)) | ((row >= H + tm) & (pos0 + tm == L))
    r = jnp.where(outside, 0.0, r)
    n = r.shape[0]
    y = pltpu.roll(r, 1, 0) * cw_ref[0:1, :] + r * cw_ref[1:2, :] + pltpu.roll(r, n - 1, 0) * cw_ref[2:3, :]
    o_ref[0] = (y[H:H + tm] + cb_ref[...]).astype(o_ref.dtype)


def hyena_inputs(x, g, w, conv_w, conv_b, L, *, tm=1024):
    M, D = x.shape
    C = w.shape[1] // 3
    H = SLAB
    nh = M // H
    assert L % tm == 0 and tm % H == 0
    return pl.pallas_call(
        partial(_hyena_in_kernel, tm, L),
        grid=(M // tm, 3),
        in_specs=[
            pl.BlockSpec((tm, D), lambda i, j: (i, 0)),
            pl.BlockSpec((H, D), lambda i, j: (jnp.maximum(i * (tm // H) - 1, 0), 0)),
            pl.BlockSpec((H, D), lambda i, j: (jnp.minimum((i + 1) * (tm // H), nh - 1), 0)),
            pl.BlockSpec((1, D), lambda i, j: (0, 0)),
            pl.BlockSpec((D, C), lambda i, j: (0, j)),
            pl.BlockSpec((SHORT_CONV, C), lambda i, j: (0, j)),
            pl.BlockSpec((1, C), lambda i, j: (0, j)),
        ],
        out_specs=pl.BlockSpec((1, tm, C), lambda i, j: (j, i, 0)),
        out_shape=jax.ShapeDtypeStruct((3, M, C), BF16),
        scratch_shapes=[pltpu.VMEM((tm + 2 * H, D), BF16)],
        compiler_params=_params("parallel", "arbitrary"),
        name="hyena_inputs",
    )(x, x, x, g, w, conv_w, conv_b)


def _filter_kernel(L, tl, w1_ref, b1_ref, w2_ref, b2_ref, w3_ref, b3_ref, w4hi_ref, w4lo_ref,
                   freq_ref, fvec_ref, delta_ref, kf_ref):
    i = pl.program_id(0)
    HALF = LANES // 2
    wide = delta_ref.shape[1]
    C = wide // (2 * HYENA_ORDER)
    row = i * tl + lax.broadcasted_iota(jnp.int32, (tl, LANES), 0)
    lane = lax.broadcasted_iota(jnp.int32, (tl, LANES), 1)
    posf = jnp.where(lane < HALF, row, L - row).astype(F32)
    t = posf / (L - 1)
    a = fvec_ref[...] * ((2.0 * math.pi / L) * posf)
    lh = lane % HALF
    feats = jnp.where(lh < FILTER_BANDS, jnp.cos(a),
                      jnp.where(lh < 2 * FILTER_BANDS, -jnp.sin(a),
                                jnp.where(lh == 2 * FILTER_BANDS, t, 0.0)))
    freq = freq_ref[...]
    dense = lambda v, w_ref, b_ref: jnp.sin(freq * (
        jnp.dot(v, w_ref[...], precision=HIGHEST, preferred_element_type=F32) + b_ref[...]))
    h = dense(dense(dense(feats, w1_ref, b1_ref), w2_ref, b2_ref), w3_ref, b3_ref)
    h_hi = h.astype(BF16)
    h_lo = (h - h_hi.astype(F32)).astype(BF16)
    taps = _dot(h_hi, w4hi_ref[...]) + _dot(h_lo, w4hi_ref[...]) + _dot(h_hi, w4lo_ref[...])
    row_w = i * tl + lax.broadcasted_iota(jnp.int32, (tl, wide), 0)
    col_w = lax.broadcasted_iota(jnp.int32, (tl, wide), 1)
    backward = col_w >= wide // 2
    t_w = jnp.where(backward, L - row_w, row_w).astype(F32) / (L - 1)
    taps = taps * jnp.exp(-t_w * delta_ref[...])
    taps = jnp.where(backward & (row_w == 0), 0.0, taps)
    for n in range(HYENA_ORDER):
        for direction in range(2):
            c0 = (direction * HYENA_ORDER + n) * C
            kf_ref[n, direction] = taps[:, c0:c0 + C].astype(kf_ref.dtype)


def hyena_filter_taps(L, fw1, fb1, fw2, fb2, fw3, fb3, fw4, ffreq, *, tl=256):
    C = D_HYENA
    H = fw2.shape[0]
    HALF = LANES // 2
    assert H <= HALF and 2 * FILTER_BANDS + 1 <= HALF

    def both(a, rows):
        blk = jnp.zeros((HALF if rows else 1, HALF), F32).at[:a.shape[0], :a.shape[1]].set(a)
        if not rows:
            return jnp.concatenate([blk, blk], axis=1)
        zero = jnp.zeros_like(blk)
        return jnp.concatenate([jnp.concatenate([blk, zero], axis=1), jnp.concatenate([zero, blk], axis=1)], axis=0)

    w1 = both(jnp.concatenate([fw1[1:], fw1[:1]], axis=0), True)
    w2, w3 = both(fw2, True), both(fw3, True)
    b1, b2, b3, freq = (both(v[None, :], False) for v in (fb1, fb2, fb3, ffreq))
    w4 = fw4.reshape(H, HYENA_ORDER, 2, C)
    wide = 2 * HYENA_ORDER * C
    w4p = jnp.zeros((LANES, wide), F32)
    w4p = w4p.at[:H, :wide // 2].set(w4[:, :, 0].reshape(H, HYENA_ORDER * C))
    w4p = w4p.at[HALF:HALF + H, wide // 2:].set(w4[:, :, 1].reshape(H, HYENA_ORDER * C))
    w4hi = w4p.astype(BF16)
    w4lo = (w4p - w4hi.astype(F32)).astype(BF16)
    bands = np.linspace(1e-4, FILTER_BANDS - 1, FILTER_BANDS, dtype=np.float32)
    fvec = np.zeros((1, LANES), np.float32)
    for base in (0, HALF):
        fvec[0, base:base + FILTER_BANDS] = bands
        fvec[0, base + FILTER_BANDS:base + 2 * FILTER_BANDS] = bands
    max_decay = math.log(DECAY_TARGET) / FAST_DECAY_PCT
    min_decay = math.log(DECAY_TARGET) / SLOW_DECAY_PCT
    deltas = np.abs(np.linspace(min_decay, max_decay, C, dtype=np.float32))
    delta_w = np.tile(deltas[None, :], (1, 2 * HYENA_ORDER))
    const = lambda i: (0, 0)
    args = (w1, b1, w2, b2, w3, b3, w4hi, w4lo, freq, jnp.asarray(fvec), jnp.asarray(delta_w))
    return pl.pallas_call(
        partial(_filter_kernel, L, tl),
        grid=(L // tl,),
        in_specs=[pl.BlockSpec(a.shape, const) for a in args],
        out_specs=pl.BlockSpec((HYENA_ORDER, 2, tl, C), lambda i: (0, 0, i, 0)),
        out_shape=jax.ShapeDtypeStruct((HYENA_ORDER, 2, L, C), BF16),
        compiler_params=_params("parallel"),
        name="hyena_filter_taps",
    )(*args)


FFT_ROWS = 16
FFT_COLS = 512


def _stage1_kernel(s_ref, x_ref, y_ref):
    _, h, rows, cols = x_ref.shape
    s = s_ref[...]
    xt = pltpu.einshape("abc->bac", x_ref[...].reshape(2 * h, rows, cols))
    yt = jnp.stack([_dot(s, xt[b]).astype(y_ref.dtype) for b in range(rows)], axis=0)
    y_ref[...] = pltpu.einshape("bac->abc", yt).reshape(y_ref.shape)


def fft_stage1(x, which, s):
    _, _, h, N2, C = x.shape
    N1 = 2 * h
    return pl.pallas_call(
        _stage1_kernel,
        grid=(N2 // FFT_ROWS, C // FFT_COLS),
        in_specs=[pl.BlockSpec((2 * N1, N1), lambda j, c: (0, 0)),
                  pl.BlockSpec((None, 2, h, FFT_ROWS, FFT_COLS), lambda j, c: (which, 0, 0, j, c))],
        out_specs=pl.BlockSpec((2, N1, FFT_ROWS, FFT_COLS), lambda j, c: (0, 0, j, c)),
        out_shape=jax.ShapeDtypeStruct((2, N1, N2, C), BF16),
        compiler_params=_params("parallel", "parallel"),
        name="fft_stage1",
    )(s, x)


FFT_K1_PER_STEP = 2


def _mid_kernel(inv_n, f2r_ref, f2i_ref, twr_ref, twi_ref, yf_ref, y_ref, u_ref):
    f2r, f2i = f2r_ref[...], f2i_ref[...]
    for k in range(y_ref.shape[1]):
        twr, twi = twr_ref[k], twi_ref[k]
        gr, gi = f2r * twr - f2i * twi, f2r * twi + f2i * twr
        grb, gib = gr.astype(BF16), gi.astype(BF16)

        def inner(ref):
            re, im = ref[0, k], ref[1, k]
            return _dot(grb, re) - _dot(gib, im), _dot(gib, re) + _dot(grb, im)

        hr, hi = inner(yf_ref)
        zr, zi = inner(y_ref)
        pr = (zr * hr - zi * hi).astype(BF16)
        pi = (zr * hi + zi * hr).astype(BF16)
        irb = (gr.T * inv_n).astype(BF16)
        iib = (gi.T * (-inv_n)).astype(BF16)
        u_ref[0, k] = (_dot(irb, pr) - _dot(iib, pi)).astype(u_ref.dtype)
        u_ref[1, k] = (_dot(iib, pr) + _dot(irb, pi)).astype(u_ref.dtype)


def fft_mid(yf, y, dc):
    _, N1, N2, C = y.shape
    kb = FFT_K1_PER_STEP
    blk = pl.BlockSpec((2, kb, N2, C), lambda k: (0, k, 0, 0))
    const = pl.BlockSpec((N2, N2), lambda k: (0, 0))
    tw = pl.BlockSpec((kb, 1, N2), lambda k: (k, 0, 0))
    return pl.pallas_call(
        partial(_mid_kernel, 1.0 / (N1 * N2)),
        grid=(N1 // kb,),
        in_specs=[const, const, tw, tw, blk, blk],
        out_specs=blk,
        out_shape=jax.ShapeDtypeStruct((2, N1, N2, C), BF16),
        compiler_params=_params("parallel"),
        name="fft_mid",
    )(dc['f2r'], dc['f2i'], dc['twr'], dc['twi'], yf, y)


def _final_kernel(t_ref, u_ref, z_ref, gate_ref, bias_ref, o_ref):
    _, n1, rows, cols = u_ref.shape
    t = t_ref[...]
    ut = pltpu.einshape("abc->bac", u_ref[...].reshape(2 * n1, rows, cols))
    convt = jnp.stack([_dot(t, ut[b]) for b in range(rows)], axis=0)
    conv = pltpu.einshape("bac->abc", convt).reshape(o_ref.shape)
    z = z_ref[...].astype(F32)
    o_ref[...] = (gate_ref[...].astype(F32) * (conv + bias_ref[...] * z)).astype(o_ref.dtype)


def fft_final(u, t_fin, z, z_which, gate, gate_which, bias, bias_which):
    _, N1, N2, C = u.shape
    h = N1 // 2

    def half(which):
        return pl.BlockSpec((None, 2, h, FFT_ROWS, FFT_COLS), lambda j, c: (which, 0, 0, j, c))

    return pl.pallas_call(
        _final_kernel,
        grid=(N2 // FFT_ROWS, C // FFT_COLS),
        in_specs=[pl.BlockSpec((N1, 2 * N1), lambda j, c: (0, 0)),
                  pl.BlockSpec((2, N1, FFT_ROWS, FFT_COLS), lambda j, c: (0, 0, j, c)),
                  half(z_which), half(gate_which),
                  pl.BlockSpec((None, 1, FFT_COLS), lambda j, c: (bias_which, 0, c))],
        out_specs=half(0),
        out_shape=jax.ShapeDtypeStruct((1, 2, h, N2, C), BF16),
        compiler_params=_params("parallel", "parallel"),
        name="fft_final",
    )(t_fin, u, z, gate, bias.reshape(bias.shape[0], 1, C))


def hyena_branch(zs, L, p):
    _, M, C = zs.shape
    assert M == 2 * L, "the batch pair rides as real/imaginary parts"
    dc = _dft_constants(L)
    split = (2, dc['N1'] // 2, FFT_INNER, C)
    taps = hyena_filter_taps(L, *p['filt']).reshape(HYENA_ORDER, *split)
    zs = zs.reshape(3, *split)
    z, z_which = zs, 0
    for n in range(HYENA_ORDER):
        u = fft_mid(fft_stage1(taps, n, dc['s_filt']), fft_stage1(z, z_which, dc['s_data']), dc)
        z, z_which = fft_final(u, dc['t_fin'], z, z_which, zs, n + 1, p['hy_bias'], n), 0
    return z.reshape(M, C)


def _layer(x, p):
    Bsz, L, D = x.shape
    M = Bsz * L
    x0 = x.reshape(M, D)
    x1 = ffn_block(x0, p['ffn1_pre_g'], p['ffn1_post_g'], p['ffn1_w_gate'], p['ffn1_w_up'], p['ffn1_w_down'])
    w_hy, w_qkv, w_gate = p['w_in']
    hy_in = hyena_inputs(x1, p['mix_pre_g'], w_hy, p['hy_conv_w'], p['hy_conv_b'], L)
    qkv = norm_matmul(x1, p['mix_pre_g'], w_qkv, BF16, tm=512, tn=D_ATT, rope_len=L, name="inproj_qkv")
    gates = norm_matmul(x1, p['mix_pre_g'], w_gate, F32, tm=1024, tn=1024, name="inproj_gates")
    a2 = hyena_branch(hy_in, L, p)
    att = [dilated_attention_group(qkv, Bsz, L, g) if ATT_GROUPS[g][1] == 1 else
           dilated_attention_slabs(qkv, Bsz, L, g) for g in range(N_GROUPS)]
    x2 = merge_out(x1, a2, [o for o, _ in att], [l for _, l in att], gates, p['mix_post_g'],
                   p['w_hy_proj'], p['w_att_proj'], p['w_out'])
    x3 = ffn_block(x2, p['ffn2_pre_g'], p['ffn2_post_g'], p['ffn2_w_gate'], p['ffn2_w_up'], p['ffn2_w_down'])
    return x3.reshape(Bsz, L, D)


def kernel(x_prompt, x_sample, ffn1_pre_g, ffn1_post_g, ffn1_w_gate, ffn1_w_up, ffn1_w_down, mix_pre_g, mix_post_g, w_in, hy_conv_w, hy_conv_b, filt_w1, filt_b1, filt_w2, filt_b2, filt_w3, filt_b3, filt_w4, filt_freq, hy_bias, w_hy_proj, w_att_proj, w_out, ffn2_pre_g, ffn2_post_g, ffn2_w_gate, ffn2_w_up, ffn2_w_down):
    assert ffn1_w_gate.shape[0] == 1
    p = {
        'ffn1_pre_g': ffn1_pre_g, 'ffn1_post_g': ffn1_post_g,
        'ffn1_w_gate': ffn1_w_gate[0].astype(BF16), 'ffn1_w_up': ffn1_w_up[0].astype(BF16),
        'ffn1_w_down': ffn1_w_down[0].astype(BF16),
        'mix_pre_g': mix_pre_g, 'mix_post_g': mix_post_g,
        'w_in': tuple(w_in[0][:, a:b].astype(BF16) for a, b in (
            (0, 3 * D_HYENA), (3 * D_HYENA, 3 * D_HYENA + 3 * D_ATT), (3 * D_HYENA + 3 * D_ATT, D_IN_PROJ))),
        'hy_conv_w': hy_conv_w[0], 'hy_conv_b': hy_conv_b,
        'filt': (filt_w1[0], filt_b1[0], filt_w2[0], filt_b2[0], filt_w3[0], filt_b3[0], filt_w4[0], filt_freq[0]),
        'hy_bias': hy_bias[0],
        'w_hy_proj': w_hy_proj[0].astype(BF16), 'w_att_proj': w_att_proj[0].astype(BF16),
        'w_out': w_out[0].astype(BF16),
        'ffn2_pre_g': ffn2_pre_g, 'ffn2_post_g': ffn2_post_g,
        'ffn2_w_gate': ffn2_w_gate[0].astype(BF16), 'ffn2_w_up': ffn2_w_up[0].astype(BF16),
        'ffn2_w_down': ffn2_w_down[0].astype(BF16),
    }
    return (_layer(x_prompt, p), _layer(x_sample, p))
```

```python
import math
from functools import partial

import numpy as np
import jax
import jax.numpy as jnp
from jax import lax
from jax.experimental import pallas as pl
from jax.experimental.pallas import tpu as pltpu

D_MODEL = 2048
D_HYENA = 1024
HYENA_ORDER = 2
SHORT_CONV = 3
FILTER_EMB = 33
FILTER_BANDS = (FILTER_EMB - 1) // 2
FAST_DECAY_PCT = 0.3
SLOW_DECAY_PCT = 1.5
DECAY_TARGET = 1e-2
HEAD_DIM = 128
HEADS_PER_GROUP = 4
ATT_GROUPS = ((128, 1), (512, 4), (2048, 16))
N_GROUPS = len(ATT_GROUPS)
D_ATT = N_GROUPS * HEADS_PER_GROUP * HEAD_DIM
D_ATT_OUT = HEADS_PER_GROUP * HEAD_DIM
ROPE_DIM = HEAD_DIM // 4
ROPE_THETA = 500000.0
N_BRANCH = 2
D_IN_PROJ = 3 * D_HYENA + 3 * D_ATT + N_BRANCH * D_MODEL
D_FF = 5632
EPS = 1e-6
NEG_INF = -1e30

LANES = 128
FFT_INNER = 256
VMEM_BYTES_V7X = 64 * 1024 * 1024
VMEM_LIMIT_BYTES = VMEM_BYTES_V7X - 4 * 1024 * 1024
BF16 = jnp.bfloat16
F32 = jnp.float32
HIGHEST = lax.Precision.HIGHEST


def _rms(x, g):
    return x * lax.rsqrt(jnp.mean(x * x, axis=-1, keepdims=True) + EPS) * g


def _dot(a, b):
    return jnp.dot(a, b, preferred_element_type=F32)


def _params(*sem):
    return pltpu.CompilerParams(dimension_semantics=sem, vmem_limit_bytes=VMEM_LIMIT_BYTES)


FFN_SPLIT = 2


def _ffn_kernel(x_ref, pre_g_ref, post_g_ref, wg_ref, wu_ref, wd_ref, o_ref, xn_ref):
    j = pl.program_id(1)
    tf = wg_ref.shape[1]

    @pl.when(j == 0)
    def _():
        xn_ref[...] = _rms(x_ref[...], pre_g_ref[...]).astype(BF16)
        o_ref[...] = jnp.zeros_like(o_ref)

    xn = xn_ref[...]
    hs = []
    for c in range(FFN_SPLIT):
        cols = slice(c * (tf // FFN_SPLIT), (c + 1) * (tf // FFN_SPLIT))
        gate = _dot(xn, wg_ref[:, cols])
        up = _dot(xn, wu_ref[:, cols])
        hs.append((gate * jax.nn.sigmoid(gate) * up).astype(BF16))
    o_ref[...] += _dot(jnp.concatenate(hs, axis=1), wd_ref[...])

    @pl.when(j == pl.num_programs(1) - 1)
    def _():
        o_ref[...] = x_ref[...] + 0.5 * _rms(o_ref[...], post_g_ref[...])


def ffn_block(x, pre_g, post_g, wg, wu, wd, *, tm=512, tf=512):
    M, D = x.shape
    FF = wg.shape[1]
    return pl.pallas_call(
        _ffn_kernel,
        grid=(M // tm, FF // tf),
        in_specs=[
            pl.BlockSpec((tm, D), lambda i, j: (i, 0)),
            pl.BlockSpec((1, D), lambda i, j: (0, 0)),
            pl.BlockSpec((1, D), lambda i, j: (0, 0)),
            pl.BlockSpec((D, tf), lambda i, j: (0, j)),
            pl.BlockSpec((D, tf), lambda i, j: (0, j)),
            pl.BlockSpec((tf, D), lambda i, j: (j, 0)),
        ],
        out_specs=pl.BlockSpec((tm, D), lambda i, j: (i, 0)),
        out_shape=jax.ShapeDtypeStruct((M, D), F32),
        scratch_shapes=[pltpu.VMEM((tm, D), BF16)],
        compiler_params=_params("parallel", "arbitrary"),
        name="ffn_block",
    )(x, pre_g, post_g, wg, wu, wd)


def rope_tables(L):
    half = ROPE_DIM // 2
    inv_freq = jnp.power(ROPE_THETA, -jnp.arange(half, dtype=F32) / half)
    ang = jnp.arange(L, dtype=F32)[:, None] * inv_freq[None, :]
    cos, sin = jnp.cos(ang), jnp.sin(ang)
    rest = HEAD_DIM - ROPE_DIM
    c = jnp.concatenate([cos, cos, jnp.ones((L, rest), F32)], axis=1)
    s_lo = jnp.concatenate([-sin, jnp.zeros((L, half + rest), F32)], axis=1)
    s_hi = jnp.concatenate([jnp.zeros((L, half), F32), sin, jnp.zeros((L, rest), F32)], axis=1)
    scale = HEAD_DIM ** -0.5
    return jnp.stack([c * scale, c]), jnp.stack([s_lo * scale, s_lo]), jnp.stack([s_hi * scale, s_hi])


def _norm_matmul_kernel(rotate, x_ref, g_ref, w_ref, *rest):
    if rotate:
        c_ref, slo_ref, shi_ref, o_ref, xn_ref = rest
    else:
        o_ref, xn_ref = rest

    @pl.when(pl.program_id(1) == 0)
    def _():
        xn_ref[...] = _rms(x_ref[...], g_ref[...]).astype(BF16)

    r = _dot(xn_ref[...], w_ref[...])
    if rotate:
        tn = r.shape[1]
        half = ROPE_DIM // 2
        wide = lambda t_ref: jnp.concatenate([t_ref[...]] * (tn // HEAD_DIM), axis=1)
        r = r * wide(c_ref) + pltpu.roll(r, tn - half, 1) * wide(slo_ref) + pltpu.roll(r, half, 1) * wide(shi_ref)
    o_ref[...] = r.astype(o_ref.dtype)


def norm_matmul(x, g, w, out_dtype, *, tm, tn, rope_len=None, name):
    M, D = x.shape
    N = w.shape[1]
    in_specs = [
        pl.BlockSpec((tm, D), lambda i, j: (i, 0)),
        pl.BlockSpec((1, D), lambda i, j: (0, 0)),
        pl.BlockSpec((D, tn), lambda i, j: (0, j)),
    ]
    args = [x, g, w]
    if rope_len is not None:
        assert tn == D_ATT and N == 2 * D_ATT and rope_len % tm == 0
        spec = pl.BlockSpec((None, tm, HEAD_DIM), lambda i, j: (j, i % (rope_len // tm), 0))
        in_specs += [spec, spec, spec]
        args += list(rope_tables(rope_len))
    return pl.pallas_call(
        partial(_norm_matmul_kernel, rope_len is not None),
        grid=(M // tm, N // tn),
        in_specs=in_specs,
        out_specs=pl.BlockSpec((tm, tn), lambda i, j: (i, j)),
        out_shape=jax.ShapeDtypeStruct((M, N), out_dtype),
        scratch_shapes=[pltpu.VMEM((tm, D), BF16)],
        compiler_params=_params("parallel", "arbitrary"),
        name=name,
    )(*args)


ATT_RADIUS = 64
assert all(w // (2 * d) == ATT_RADIUS for w, d in ATT_GROUPS)


def _attn_kernel(T, Ls, q_ref, kp_ref, kc_ref, kn_ref, vp_ref, vc_ref, vn_ref, o_ref, lse_ref, kbuf, vbuf):
    t = pl.program_id(2)
    R = ATT_RADIUS
    SB = 2 * R
    for buf, prv, cur, nxt in ((kbuf, kp_ref, kc_ref, kn_ref), (vbuf, vp_ref, vc_ref, vn_ref)):
        buf[0:R] = prv[...]
        buf[R:R + T] = cur[...]
        buf[R + T:R + T + R] = nxt[...]
    qi = lax.broadcasted_iota(jnp.int32, (SB, 2 * SB), 0)
    kk = lax.broadcasted_iota(jnp.int32, (SB, 2 * SB), 1)
    band = (kk >= qi) & (kk <= qi + 2 * R)
    lane = lax.broadcasted_iota(jnp.int32, (SB, LANES), 1)
    for sb in range(T // SB):
        kpos = t * T + (sb * SB - R) + kk
        mask = band & (kpos >= 0) & (kpos < Ls)
        lse = jnp.zeros((SB, LANES), F32)
        for h in range(HEADS_PER_GROUP):
            cols = slice(h * HEAD_DIM, (h + 1) * HEAD_DIM)
            q = q_ref[sb * SB:(sb + 1) * SB, cols]
            k = kbuf[sb * SB:(sb + 2) * SB, cols]
            v = vbuf[sb * SB:(sb + 2) * SB, cols]
            s = lax.dot_general(q, k, (((1,), (1,)), ((), ())), preferred_element_type=F32)
            s = jnp.where(mask, s, NEG_INF)
            m = jnp.max(s, axis=1, keepdims=True)
            p = jnp.exp(s - m)
            l = jnp.sum(p, axis=1, keepdims=True)
            o = _dot(p.astype(BF16), v) / l
            o_ref[sb * SB:(sb + 1) * SB, cols] = o.astype(o_ref.dtype)
            lse = jnp.where(lane == h, m + jnp.log(l), lse)
        lse_ref[sb * SB:(sb + 1) * SB, :] = lse


SLAB = 16


def _softmax_heads(q, k, v, mask):
    lane = lax.broadcasted_iota(jnp.int32, (q.shape[0], LANES), 1)
    lse = jnp.zeros((q.shape[0], LANES), F32)
    outs = []
    for h in range(HEADS_PER_GROUP):
        cols = slice(h * HEAD_DIM, (h + 1) * HEAD_DIM)
        s = lax.dot_general(q[:, cols], k[:, cols], (((1,), (1,)), ((), ())), preferred_element_type=F32)
        s = jnp.where(mask, s, NEG_INF)
        m = jnp.max(s, axis=1, keepdims=True)
        p = jnp.exp(s - m)
        l = jnp.sum(p, axis=1, keepdims=True)
        outs.append(_dot(p.astype(BF16), v[:, cols]) / l)
        lse = jnp.where(lane == h, m + jnp.log(l), lse)
    return jnp.concatenate(outs, axis=1), lse


def _attn_slab_kernel(d, NS, Ls, q_ref, kp_ref, kc_ref, kn_ref, vp_ref, vc_ref, vn_ref, o_ref, lse_ref,
                      qs, ks, vs, os_, ls):
    t = pl.program_id(1)
    R = ATT_RADIUS
    SB = 2 * R
    J = SLAB // d
    NH = kp_ref.shape[0]
    QA = SB // J
    KA = QA + 2 * NH
    n_sb = NS // QA
    to_class_major = lambda ref: pltpu.einshape("abc->bac", ref[...])
    qs[...] = to_class_major(q_ref)
    for buf, prv, cur, nxt in ((ks, kp_ref, kc_ref, kn_ref), (vs, vp_ref, vc_ref, vn_ref)):
        buf[:, 0:NH] = to_class_major(prv)
        buf[:, NH:NH + NS] = to_class_major(cur)
        buf[:, NH + NS:NH + NS + NH] = to_class_major(nxt)
    qi = lax.broadcasted_iota(jnp.int32, (SB, 2 * SB), 0)
    kk = lax.broadcasted_iota(jnp.int32, (SB, 2 * SB), 1)
    sq_rel = J * (qi % QA) + qi // QA
    sk_rel = J * (kk % KA - NH) + kk // KA
    band = jnp.abs(sk_rel - sq_rel) <= R

    def body(it, carry):
        r = it // n_sb
        a0 = pl.multiple_of((it % n_sb) * QA, QA)
        sk = J * (t * NS + a0) + sk_rel
        mask = band & (sk >= 0) & (sk < Ls)
        gather = lambda buf, n: jnp.concatenate([buf[j * d + r, pl.ds(a0, n), :] for j in range(J)], axis=0)
        o, lse = _softmax_heads(gather(qs, QA), gather(ks, KA), gather(vs, KA), mask)
        o = o.astype(os_.dtype)
        for j in range(J):
            os_[j * d + r, pl.ds(a0, QA), :] = o[j * QA:(j + 1) * QA]
            ls[j * d + r, pl.ds(a0, QA), :] = lse[j * QA:(j + 1) * QA]
        return carry

    lax.fori_loop(0, d * n_sb, body, 0)
    o_ref[...] = pltpu.einshape("bac->abc", os_[...])
    lse_ref[...] = pltpu.einshape("bac->abc", ls[...])


def dilated_attention_slabs(srcs, Bsz, L, g, *, NS=128):
    d = ATT_GROUPS[g][1]
    M = Bsz * L
    R = ATT_RADIUS
    GW = HEADS_PER_GROUP * HEAD_DIM
    NH = R * d // SLAB
    TB = NS * SLAB
    assert SLAB % d == 0 and L % TB == 0 and NS % NH == 0 and (2 * R) % (SLAB // d) == 0
    nmb, nhb, hpm = L // TB, L // (NH * SLAB), NS // NH
    views = [a.reshape(M // SLAB, SLAB, a.shape[1]) for a, _ in srcs]
    col = lambda which: srcs[which][1]

    def main(which):
        return pl.BlockSpec((NS, SLAB, GW), lambda b, t: (b * nmb + t, 0, col(which)))

    def prev(which):
        return pl.BlockSpec((NH, SLAB, GW), lambda b, t: (jnp.maximum(b * nhb + t * hpm - 1, b * nhb), 0, col(which)))

    def nxt(which):
        return pl.BlockSpec((NH, SLAB, GW),
                            lambda b, t: (jnp.minimum(b * nhb + (t + 1) * hpm, (b + 1) * nhb - 1), 0, col(which)))

    o, lse = pl.pallas_call(
        partial(_attn_slab_kernel, d, NS, L // d),
        grid=(Bsz, nmb),
        in_specs=[main(0), prev(1), main(1), nxt(1), prev(2), main(2), nxt(2)],
        out_specs=[pl.BlockSpec((NS, SLAB, GW), lambda b, t: (b * nmb + t, 0, 0)),
                   pl.BlockSpec((NS, SLAB, LANES), lambda b, t: (b * nmb + t, 0, 0))],
        out_shape=[jax.ShapeDtypeStruct((M // SLAB, SLAB, GW), BF16),
                   jax.ShapeDtypeStruct((M // SLAB, SLAB, LANES), F32)],
        scratch_shapes=[pltpu.VMEM((SLAB, NS, GW), BF16),
                        pltpu.VMEM((SLAB, NS + 2 * NH, GW), BF16), pltpu.VMEM((SLAB, NS + 2 * NH, GW), BF16),
                        pltpu.VMEM((SLAB, NS, GW), BF16), pltpu.VMEM((SLAB, NS, LANES), F32)],
        compiler_params=_params("parallel", "arbitrary"),
        name=f"dilated_attention_g{g}",
    )(views[0], views[1], views[1], views[1], views[2], views[2], views[2])
    return o.reshape(M, GW), lse.reshape(M, LANES)


def dilated_attention_group(srcs, Bsz, L, g, *, T=256):
    d = ATT_GROUPS[g][1]
    M = Bsz * L
    Ls = L // d
    R = ATT_RADIUS
    GW = HEADS_PER_GROUP * HEAD_DIM
    assert d == 1 and Ls % T == 0 and T % (2 * R) == 0
    nrb, nhb, hpt = Ls // T, Ls // R, T // R

    def main(which):
        return pl.BlockSpec((T, GW), lambda b, r, t: (b * nrb + t, srcs[which][1]))

    def prev(which):
        return pl.BlockSpec((R, GW), lambda b, r, t: (jnp.maximum(b * nhb + t * hpt - 1, b * nhb), srcs[which][1]))

    def nxt(which):
        return pl.BlockSpec((R, GW), lambda b, r, t: (jnp.minimum(b * nhb + (t + 1) * hpt, (b + 1) * nhb - 1),
                                                      srcs[which][1]))

    o, lse = pl.pallas_call(
        partial(_attn_kernel, T, Ls),
        grid=(Bsz, d, nrb),
        in_specs=[main(0), prev(1), main(1), nxt(1), prev(2), main(2), nxt(2)],
        out_specs=[pl.BlockSpec((T, GW), lambda b, r, t: (b * nrb + t, r)),
                   pl.BlockSpec((T, LANES), lambda b, r, t: (b * nrb + t, r))],
        out_shape=[jax.ShapeDtypeStruct((M // d, d * GW), BF16),
                   jax.ShapeDtypeStruct((M // d, d * LANES), F32)],
        scratch_shapes=[pltpu.VMEM((T + 2 * R, GW), BF16), pltpu.VMEM((T + 2 * R, GW), BF16)],
        compiler_params=_params("parallel", "parallel", "arbitrary"),
        name=f"dilated_attention_g{g}",
    )(srcs[0][0], srcs[1][0], srcs[1][0], srcs[1][0], srcs[2][0], srcs[2][0], srcs[2][0])
    return o.reshape(M, GW), lse.reshape(M, LANES)


def _merge_kernel(x_ref, a_ref, o0_ref, o1_ref, o2_ref, l0_ref, l1_ref, l2_ref, ga_ref, gb_ref,
                  post_g_ref, whp_ref, wap_ref, wo_ref, o_ref):
    lses = [l0_ref[...], l1_ref[...], l2_ref[...]]
    mx = jnp.maximum(jnp.maximum(lses[0], lses[1]), lses[2])
    es = [jnp.exp(l - mx) for l in lses]
    den = es[0] + es[1] + es[2]
    wts = [e / den for e in es]
    outs = [o0_ref, o1_ref, o2_ref]
    heads = []
    for h in range(HEADS_PER_GROUP):
        cols = slice(h * HEAD_DIM, (h + 1) * HEAD_DIM)
        heads.append(sum(wts[g][:, h:h + 1] * outs[g][:, cols].astype(F32) for g in range(N_GROUPS)))
    att = jnp.concatenate(heads, axis=1).astype(BF16)
    a = _dot(a_ref[...], whp_ref[...])
    b = _dot(att, wap_ref[...])
    gate_a, gate_b = ga_ref[...].astype(F32), gb_ref[...].astype(F32)
    merged = (jax.nn.sigmoid(gate_a) * a + jax.nn.sigmoid(gate_b) * b).astype(BF16)
    mix = _dot(merged, wo_ref[...])
    o_ref[...] = x_ref[...] + _rms(mix, post_g_ref[...])


def merge_out(x, a_in, att_outs, att_lses, gates, post_g, whp, wap, wo, *, tm=256):
    M, D = x.shape
    const = lambda i: (0, 0)
    rows = lambda a: pl.BlockSpec((tm, a.shape[1]), lambda i: (i, 0))
    return pl.pallas_call(
        _merge_kernel,
        grid=(M // tm,),
        in_specs=[rows(x), rows(a_in)] + [rows(o) for o in att_outs] + [rows(l) for l in att_lses] + [
            pl.BlockSpec((tm, D), lambda i: (i, 0)),
            pl.BlockSpec((tm, D), lambda i: (i, 1)),
            pl.BlockSpec((1, D), const),
            pl.BlockSpec(whp.shape, const),
            pl.BlockSpec(wap.shape, const),
            pl.BlockSpec(wo.shape, const),
        ],
        out_specs=pl.BlockSpec((tm, D), lambda i: (i, 0)),
        out_shape=jax.ShapeDtypeStruct((M, D), F32),
        compiler_params=_params("parallel"),
        name="merge_out",
    )(x, a_in, *att_outs, *att_lses, gates, gates, post_g, whp, wap, wo)


def _dft_constants(L):
    N = 2 * L
    N2 = FFT_INNER
    N1 = N // N2
    h = N1 // 2
    idx1 = np.arange(N1)
    ang1 = -2.0 * np.pi * ((idx1[:, None] * idx1[None, :]) % N1) / N1
    f1r, f1i = np.cos(ang1), np.sin(ang1)
    s_data = np.block([[f1r[:, :h], -f1i[:, :h]], [f1i[:, :h], f1r[:, :h]]])
    s_filt = np.concatenate([f1r, f1i], axis=0)
    ar, ai = f1r[:h, :], -f1i[:h, :]
    t_fin = np.block([[ar, -ai], [ai, ar]])
    idx2 = np.arange(N2)
    ang2 = -2.0 * np.pi * ((idx2[:, None] * idx2[None, :]) % N2) / N2
    angt = -2.0 * np.pi * (idx1[:, None] * idx2[None, :]) / N
    return dict(
        N1=N1,
        s_data=jnp.asarray(s_data, BF16), s_filt=jnp.asarray(s_filt, BF16), t_fin=jnp.asarray(t_fin, BF16),
        f2r=jnp.asarray(np.cos(ang2), F32), f2i=jnp.asarray(np.sin(ang2), F32),
        twr=jnp.asarray(np.cos(angt).reshape(N1, 1, N2), F32),
        twi=jnp.asarray(np.sin(angt).reshape(N1, 1, N2), F32),
    )


def _hyena_in_kernel(tm, L, x_ref, prev_ref, next_ref, g_ref, w_ref, cw_ref, cb_ref, o_ref, xn_ref):
    i = pl.program_id(0)
    H = prev_ref.shape[0]

    @pl.when(pl.program_id(1) == 0)
    def _():
        g = g_ref[...]
        xn_ref[0:H] = _rms(prev_ref[...], g).astype(BF16)
        xn_ref[H:H + tm] = _rms(x_ref[...], g).astype(BF16)
        xn_ref[H + tm:H + tm + H] = _rms(next_ref[...], g).astype(BF16)

    r = _dot(xn_ref[...], w_ref[...])
    row = lax.broadcasted_iota(jnp.int32, r.shape, 0)
    pos0 = (i * tm) % L
    outside = ((row < H) & (pos0 == 0)) | ((row >= H + tm) & (pos0 + tm == L))
    r = jnp.where(outside, 0.0, r)
    n = r.shape[0]
    y = pltpu.roll(r, 1, 0) * cw_ref[0:1, :] + r * cw_ref[1:2, :] + pltpu.roll(r, n - 1, 0) * cw_ref[2:3, :]
    o_ref[0] = (y[H:H + tm] + cb_ref[...]).astype(o_ref.dtype)


def hyena_inputs(x, g, w, conv_w, conv_b, L, *, tm=1024):
    M, D = x.shape
    C = w.shape[1] // 3
    H = SLAB
    nh = M // H
    assert L % tm == 0 and tm % H == 0
    return pl.pallas_call(
        partial(_hyena_in_kernel, tm, L),
        grid=(M // tm, 3),
        in_specs=[
            pl.BlockSpec((tm, D), lambda i, j: (i, 0)),
            pl.BlockSpec((H, D), lambda i, j: (jnp.maximum(i * (tm // H) - 1, 0), 0)),
            pl.BlockSpec((H, D), lambda i, j: (jnp.minimum((i + 1) * (tm // H), nh - 1), 0)),
            pl.BlockSpec((1, D), lambda i, j: (0, 0)),
            pl.BlockSpec((D, C), lambda i, j: (0, j)),
            pl.BlockSpec((SHORT_CONV, C), lambda i, j: (0, j)),
            pl.BlockSpec((1, C), lambda i, j: (0, j)),
        ],
        out_specs=pl.BlockSpec((1, tm, C), lambda i, j: (j, i, 0)),
        out_shape=jax.ShapeDtypeStruct((3, M, C), BF16),
        scratch_shapes=[pltpu.VMEM((tm + 2 * H, D), BF16)],
        compiler_params=_params("parallel", "arbitrary"),
        name="hyena_inputs",
    )(x, x, x, g, w, conv_w, conv_b)


def _filter_kernel(L, tl, w1_ref, b1_ref, w2_ref, b2_ref, w3_ref, b3_ref, w4hi_ref, w4lo_ref,
                   freq_ref, fvec_ref, delta_ref, kf_ref):
    i = pl.program_id(0)
    HALF = LANES // 2
    wide = delta_ref.shape[1]
    C = wide // (2 * HYENA_ORDER)
    row = i * tl + lax.broadcasted_iota(jnp.int32, (tl, LANES), 0)
    lane = lax.broadcasted_iota(jnp.int32, (tl, LANES), 1)
    posf = jnp.where(lane < HALF, row, L - row).astype(F32)
    t = posf / (L - 1)
    a = fvec_ref[...] * ((2.0 * math.pi / L) * posf)
    lh = lane % HALF
    feats = jnp.where(lh < FILTER_BANDS, jnp.cos(a),
                      jnp.where(lh < 2 * FILTER_BANDS, -jnp.sin(a),
                                jnp.where(lh == 2 * FILTER_BANDS, t, 0.0)))
    freq = freq_ref[...]
    dense = lambda v, w_ref, b_ref: jnp.sin(freq * (
        jnp.dot(v, w_ref[...], precision=HIGHEST, preferred_element_type=F32) + b_ref[...]))
    h = dense(dense(dense(feats, w1_ref, b1_ref), w2_ref, b2_ref), w3_ref, b3_ref)
    h_hi = h.astype(BF16)
    h_lo = (h - h_hi.astype(F32)).astype(BF16)
    taps = _dot(h_hi, w4hi_ref[...]) + _dot(h_lo, w4hi_ref[...]) + _dot(h_hi, w4lo_ref[...])
    row_w = i * tl + lax.broadcasted_iota(jnp.int32, (tl, wide), 0)
    col_w = lax.broadcasted_iota(jnp.int32, (tl, wide), 1)
    backward = col_w >= wide // 2
    t_w = jnp.where(backward, L - row_w, row_w).astype(F32) / (L - 1)
    taps = taps * jnp.exp(-t_w * delta_ref[...])
    taps = jnp.where(backward & (row_w == 0), 0.0, taps)
    for n in range(HYENA_ORDER):
        for direction in range(2):
            c0 = (direction * HYENA_ORDER + n) * C
            kf_ref[n, direction] = taps[:, c0:c0 + C].astype(kf_ref.dtype)


def hyena_filter_taps(L, fw1, fb1, fw2, fb2, fw3, fb3, fw4, ffreq, *, tl=256):
    C = D_HYENA
    H = fw2.shape[0]
    HALF = LANES // 2
    assert H <= HALF and 2 * FILTER_BANDS + 1 <= HALF

    def both(a, rows):
        blk = jnp.zeros((HALF if rows else 1, HALF), F32).at[:a.shape[0], :a.shape[1]].set(a)
        if not rows:
            return jnp.concatenate([blk, blk], axis=1)
        zero = jnp.zeros_like(blk)
        return jnp.concatenate([jnp.concatenate([blk, zero], axis=1), jnp.concatenate([zero, blk], axis=1)], axis=0)

    w1 = both(jnp.concatenate([fw1[1:], fw1[:1]], axis=0), True)
    w2, w3 = both(fw2, True), both(fw3, True)
    b1, b2, b3, freq = (both(v[None, :], False) for v in (fb1, fb2, fb3, ffreq))
    w4 = fw4.reshape(H, HYENA_ORDER, 2, C)
    wide = 2 * HYENA_ORDER * C
    w4p = jnp.zeros((LANES, wide), F32)
    w4p = w4p.at[:H, :wide // 2].set(w4[:, :, 0].reshape(H, HYENA_ORDER * C))
    w4p = w4p.at[HALF:HALF + H, wide // 2:].set(w4[:, :, 1].reshape(H, HYENA_ORDER * C))
    w4hi = w4p.astype(BF16)
    w4lo = (w4p - w4hi.astype(F32)).astype(BF16)
    bands = np.linspace(1e-4, FILTER_BANDS - 1, FILTER_BANDS, dtype=np.float32)
    fvec = np.zeros((1, LANES), np.float32)
    for base in (0, HALF):
        fvec[0, base:base + FILTER_BANDS] = bands
        fvec[0, base + FILTER_BANDS:base + 2 * FILTER_BANDS] = bands
    max_decay = math.log(DECAY_TARGET) / FAST_DECAY_PCT
    min_decay = math.log(DECAY_TARGET) / SLOW_DECAY_PCT
    deltas = np.abs(np.linspace(min_decay, max_decay, C, dtype=np.float32))
    delta_w = np.tile(deltas[None, :], (1, 2 * HYENA_ORDER))
    const = lambda i: (0, 0)
    args = (w1, b1, w2, b2, w3, b3, w4hi, w4lo, freq, jnp.asarray(fvec), jnp.asarray(delta_w))
    return pl.pallas_call(
        partial(_filter_kernel, L, tl),
        grid=(L // tl,),
        in_specs=[pl.BlockSpec(a.shape, const) for a in args],
        out_specs=pl.BlockSpec((HYENA_ORDER, 2, tl, C), lambda i: (0, 0, i, 0)),
        out_shape=jax.ShapeDtypeStruct((HYENA_ORDER, 2, L, C), BF16),
        compiler_params=_params("parallel"),
        name="hyena_filter_taps",
    )(*args)


FFT_ROWS = 16
FFT_COLS = 512


def _stage1_kernel(s_ref, x_ref, y_ref):
    _, h, rows, cols = x_ref.shape
    s = s_ref[...]
    xt = pltpu.einshape("abc->bac", x_ref[...].reshape(2 * h, rows, cols))
    yt = jnp.stack([_dot(s, xt[b]).astype(y_ref.dtype) for b in range(rows)], axis=0)
    y_ref[...] = pltpu.einshape("bac->abc", yt).reshape(y_ref.shape)


def fft_stage1(x, which, s):
    _, _, h, N2, C = x.shape
    N1 = 2 * h
    return pl.pallas_call(
        _stage1_kernel,
        grid=(N2 // FFT_ROWS, C // FFT_COLS),
        in_specs=[pl.BlockSpec((2 * N1, N1), lambda j, c: (0, 0)),
                  pl.BlockSpec((None, 2, h, FFT_ROWS, FFT_COLS), lambda j, c: (which, 0, 0, j, c))],
        out_specs=pl.BlockSpec((2, N1, FFT_ROWS, FFT_COLS), lambda j, c: (0, 0, j, c)),
        out_shape=jax.ShapeDtypeStruct((2, N1, N2, C), BF16),
        compiler_params=_params("parallel", "parallel"),
        name="fft_stage1",
    )(s, x)


FFT_K1_PER_STEP = 2


def _mid_kernel(inv_n, f2r_ref, f2i_ref, twr_ref, twi_ref, yf_ref, y_ref, u_ref):
    f2r, f2i = f2r_ref[...], f2i_ref[...]
    for k in range(y_ref.shape[1]):
        twr, twi = twr_ref[k], twi_ref[k]
        gr, gi = f2r * twr - f2i * twi, f2r * twi + f2i * twr
        grb, gib = gr.astype(BF16), gi.astype(BF16)

        def inner(ref):
            re, im = ref[0, k], ref[1, k]
            return _dot(grb, re) - _dot(gib, im), _dot(gib, re) + _dot(grb, im)

        hr, hi = inner(yf_ref)
        zr, zi = inner(y_ref)
        pr = (zr * hr - zi * hi).astype(BF16)
        pi = (zr * hi + zi * hr).astype(BF16)
        irb = (gr.T * inv_n).astype(BF16)
        iib = (gi.T * (-inv_n)).astype(BF16)
        u_ref[0, k] = (_dot(irb, pr) - _dot(iib, pi)).astype(u_ref.dtype)
        u_ref[1, k] = (_dot(iib, pr) + _dot(irb, pi)).astype(u_ref.dtype)


def fft_mid(yf, y, dc):
    _, N1, N2, C = y.shape
    kb = FFT_K1_PER_STEP
    blk = pl.BlockSpec((2, kb, N2, C), lambda k: (0, k, 0, 0))
    const = pl.BlockSpec((N2, N2), lambda k: (0, 0))
    tw = pl.BlockSpec((kb, 1, N2), lambda k: (k, 0, 0))
    return pl.pallas_call(
        partial(_mid_kernel, 1.0 / (N1 * N2)),
        grid=(N1 // kb,),
        in_specs=[const, const, tw, tw, blk, blk],
        out_specs=blk,
        out_shape=jax.ShapeDtypeStruct((2, N1, N2, C), BF16),
        compiler_params=_params("parallel"),
        name="fft_mid",
    )(dc['f2r'], dc['f2i'], dc['twr'], dc['twi'], yf, y)


def _final_kernel(t_ref, u_ref, z_ref, gate_ref, bias_ref, o_ref):
    _, n1, rows, cols = u_ref.shape
    t = t_ref[...]
    ut = pltpu.einshape("abc->bac", u_ref[...].reshape(2 * n1, rows, cols))
    convt = jnp.stack([_dot(t, ut[b]).astype(BF16) for b in range(rows)], axis=0)
    conv = pltpu.einshape("bac->abc", convt).reshape(o_ref.shape).astype(F32)
    z = z_ref[...].astype(F32)
    o_ref[...] = (gate_ref[...].astype(F32) * (conv + bias_ref[...] * z)).astype(o_ref.dtype)


def fft_final(u, t_fin, z, z_which, gate, gate_which, bias, bias_which):
    _, N1, N2, C = u.shape
    h = N1 // 2

    def half(which):
        return pl.BlockSpec((None, 2, h, FFT_ROWS, FFT_COLS), lambda j, c: (which, 0, 0, j, c))

    return pl.pallas_call(
        _final_kernel,
        grid=(N2 // FFT_ROWS, C // FFT_COLS),
        in_specs=[pl.BlockSpec((N1, 2 * N1), lambda j, c: (0, 0)),
                  pl.BlockSpec((2, N1, FFT_ROWS, FFT_COLS), lambda j, c: (0, 0, j, c)),
                  half(z_which), half(gate_which),
                  pl.BlockSpec((None, 1, FFT_COLS), lambda j, c: (bias_which, 0, c))],
        out_specs=half(0),
        out_shape=jax.ShapeDtypeStruct((1, 2, h, N2, C), BF16),
        compiler_params=_params("parallel", "parallel"),
        name="fft_final",
    )(t_fin, u, z, gate, bias.reshape(bias.shape[0], 1, C))


def hyena_branch(zs, L, p):
    _, M, C = zs.shape
    assert M == 2 * L, "the batch pair rides as real/imaginary parts"
    dc = _dft_constants(L)
    split = (2, dc['N1'] // 2, FFT_INNER, C)
    taps = hyena_filter_taps(L, *p['filt']).reshape(HYENA_ORDER, *split)
    zs = zs.reshape(3, *split)
    z, z_which = zs, 0
    for n in range(HYENA_ORDER):
        u = fft_mid(fft_stage1(taps, n, dc['s_filt']), fft_stage1(z, z_which, dc['s_data']), dc)
        z, z_which = fft_final(u, dc['t_fin'], z, z_which, zs, n + 1, p['hy_bias'], n), 0
    return z.reshape(M, C)


def _split_w_in(w):
    c_q = 3 * D_HYENA
    c_v = c_q + 2 * D_ATT
    c_g = c_v + D_ATT
    return (w[:, :c_q].astype(BF16), w[:, c_q:c_v].astype(BF16),
            jnp.concatenate([w[:, c_g:], w[:, c_v:c_g]], axis=1).astype(BF16))


def _layer(x, p):
    Bsz, L, D = x.shape
    M = Bsz * L
    x0 = x.reshape(M, D)
    x1 = ffn_block(x0, p['ffn1_pre_g'], p['ffn1_post_g'], p['ffn1_w_gate'], p['ffn1_w_up'], p['ffn1_w_down'])
    w_hy, w_qk, w_gv = p['w_in']
    hy_in = hyena_inputs(x1, p['mix_pre_g'], w_hy, p['hy_conv_w'], p['hy_conv_b'], L)
    qk = norm_matmul(x1, p['mix_pre_g'], w_qk, BF16, tm=512, tn=D_ATT, rope_len=L, name="inproj_qk")
    gv = norm_matmul(x1, p['mix_pre_g'], w_gv, BF16, tm=1024, tn=w_gv.shape[1] // 4, name="inproj_gates_v")
    a2 = hyena_branch(hy_in, L, p)
    GW = HEADS_PER_GROUP * HEAD_DIM
    v_col0 = N_BRANCH * D_MODEL // GW
    att = []
    for g in range(N_GROUPS):
        srcs = ((qk, g), (qk, N_GROUPS + g), (gv, v_col0 + g))
        fn = dilated_attention_group if ATT_GROUPS[g][1] == 1 else dilated_attention_slabs
        att.append(fn(srcs, Bsz, L, g))
    x2 = merge_out(x1, a2, [o for o, _ in att], [l for _, l in att], gv, p['mix_post_g'],
                   p['w_hy_proj'], p['w_att_proj'], p['w_out'])
    x3 = ffn_block(x2, p['ffn2_pre_g'], p['ffn2_post_g'], p['ffn2_w_gate'], p['ffn2_w_up'], p['ffn2_w_down'])
    return x3.reshape(Bsz, L, D)


def kernel(x_prompt, x_sample, ffn1_pre_g, ffn1_post_g, ffn1_w_gate, ffn1_w_up, ffn1_w_down, mix_pre_g, mix_post_g, w_in, hy_conv_w, hy_conv_b, filt_w1, filt_b1, filt_w2, filt_b2, filt_w3, filt_b3, filt_w4, filt_freq, hy_bias, w_hy_proj, w_att_proj, w_out, ffn2_pre_g, ffn2_post_g, ffn2_w_gate, ffn2_w_up, ffn2_w_down):
    assert ffn1_w_gate.shape[0] == 1
    p = {
        'ffn1_pre_g': ffn1_pre_g, 'ffn1_post_g': ffn1_post_g,
        'ffn1_w_gate': ffn1_w_gate[0].astype(BF16), 'ffn1_w_up': ffn1_w_up[0].astype(BF16),
        'ffn1_w_down': ffn1_w_down[0].astype(BF16),
        'mix_pre_g': mix_pre_g, 'mix_post_g': mix_post_g,
        'w_in': _split_w_in(w_in[0]),
        'hy_conv_w': hy_conv_w[0], 'hy_conv_b': hy_conv_b,
        'filt': (filt_w1[0], filt_b1[0], filt_w2[0], filt_b2[0], filt_w3[0], filt_b3[0], filt_w4[0], filt_freq[0]),
        'hy_bias': hy_bias[0],
        'w_hy_proj': w_hy_proj[0].astype(BF16), 'w_att_proj': w_att_proj[0].astype(BF16),
        'w_out': w_out[0].astype(BF16),
        'ffn2_pre_g': ffn2_pre_g, 'ffn2_post_g': ffn2_post_g,
        'ffn2_w_gate': ffn2_w_gate[0].astype(BF16), 'ffn2_w_up': ffn2_w_up[0].astype(BF16),
        'ffn2_w_down': ffn2_w_down[0].astype(BF16),
    }
    return (_layer(x_prompt, p), _layer(x_sample, p))
```

```python
import math
from functools import partial

import numpy as np
import jax
import jax.numpy as jnp
from jax import lax
from jax.experimental import pallas as pl
from jax.experimental.pallas import tpu as pltpu

D_MODEL = 2048
D_HYENA = 1024
HYENA_ORDER = 2
SHORT_CONV = 3
FILTER_EMB = 33
FILTER_BANDS = (FILTER_EMB - 1) // 2
FAST_DECAY_PCT = 0.3
SLOW_DECAY_PCT = 1.5
DECAY_TARGET = 1e-2
HEAD_DIM = 128
HEADS_PER_GROUP = 4
ATT_GROUPS = ((128, 1), (512, 4), (2048, 16))
N_GROUPS = len(ATT_GROUPS)
D_ATT = N_GROUPS * HEADS_PER_GROUP * HEAD_DIM
D_ATT_OUT = HEADS_PER_GROUP * HEAD_DIM
ROPE_DIM = HEAD_DIM // 4
ROPE_THETA = 500000.0
N_BRANCH = 2
D_IN_PROJ = 3 * D_HYENA + 3 * D_ATT + N_BRANCH * D_MODEL
D_FF = 5632
EPS = 1e-6
NEG_INF = -1e30

LANES = 128
FFT_INNER = 256
VMEM_BYTES_V7X = 64 * 1024 * 1024
VMEM_LIMIT_BYTES = VMEM_BYTES_V7X - 4 * 1024 * 1024
BF16 = jnp.bfloat16
F32 = jnp.float32
HIGHEST = lax.Precision.HIGHEST


def _rms(x, g):
    return x * lax.rsqrt(jnp.mean(x * x, axis=-1, keepdims=True) + EPS) * g


def _dot(a, b):
    return jnp.dot(a, b, preferred_element_type=F32)


def _params(*sem):
    return pltpu.CompilerParams(dimension_semantics=sem, vmem_limit_bytes=VMEM_LIMIT_BYTES)


FFN_SPLIT = 2


def _ffn_kernel(norm_in, norm_out, x_ref, xn_src_ref, post_g_ref, next_g_ref, wg_ref, wu_ref, wd_ref, *rest):
    o_ref = rest[0]
    u_ref = rest[1] if norm_out else None
    xn_ref = rest[-1] if norm_in else xn_src_ref
    j = pl.program_id(1)
    tf = wg_ref.shape[1]

    @pl.when(j == 0)
    def _():
        if norm_in:
            xn_ref[...] = _rms(x_ref[...], xn_src_ref[...]).astype(BF16)
        o_ref[...] = jnp.zeros_like(o_ref)

    xn = xn_ref[...]
    hs = []
    for c in range(FFN_SPLIT):
        cols = slice(c * (tf // FFN_SPLIT), (c + 1) * (tf // FFN_SPLIT))
        gate = _dot(xn, wg_ref[:, cols])
        up = _dot(xn, wu_ref[:, cols])
        hs.append((gate * jax.nn.sigmoid(gate) * up).astype(BF16))
    o_ref[...] += _dot(jnp.concatenate(hs, axis=1), wd_ref[...])

    @pl.when(j == pl.num_programs(1) - 1)
    def _():
        y = x_ref[...] + 0.5 * _rms(o_ref[...], post_g_ref[...])
        o_ref[...] = y
        if norm_out:
            u_ref[...] = _rms(y, next_g_ref[...]).astype(u_ref.dtype)


def ffn_block(x, pre_g, post_g, wg, wu, wd, *, xn=None, next_g=None, tm=512, tf=512):
    M, D = x.shape
    FF = wg.shape[1]
    norm_in, norm_out = xn is None, next_g is not None
    rows = pl.BlockSpec((tm, D), lambda i, j: (i, 0))
    gain = pl.BlockSpec((1, D), lambda i, j: (0, 0))
    out = pl.pallas_call(
        partial(_ffn_kernel, norm_in, norm_out),
        grid=(M // tm, FF // tf),
        in_specs=[
            rows,
            gain if norm_in else rows,
            gain, gain,
            pl.BlockSpec((D, tf), lambda i, j: (0, j)),
            pl.BlockSpec((D, tf), lambda i, j: (0, j)),
            pl.BlockSpec((tf, D), lambda i, j: (j, 0)),
        ],
        out_specs=[rows, rows] if norm_out else [rows],
        out_shape=[jax.ShapeDtypeStruct((M, D), F32)] + ([jax.ShapeDtypeStruct((M, D), BF16)] if norm_out else []),
        scratch_shapes=[pltpu.VMEM((tm, D), BF16)] if norm_in else [],
        compiler_params=_params("parallel", "arbitrary"),
        name="ffn_block",
    )(x, pre_g if norm_in else xn, post_g, next_g if norm_out else post_g, wg, wu, wd)
    return out if norm_out else out[0]


def rope_tables(L):
    half = ROPE_DIM // 2
    inv_freq = jnp.power(ROPE_THETA, -jnp.arange(half, dtype=F32) / half)
    ang = jnp.arange(L, dtype=F32)[:, None] * inv_freq[None, :]
    cos, sin = jnp.cos(ang), jnp.sin(ang)
    rest = HEAD_DIM - ROPE_DIM
    c = jnp.concatenate([cos, cos, jnp.ones((L, rest), F32)], axis=1)
    s_lo = jnp.concatenate([-sin, jnp.zeros((L, half + rest), F32)], axis=1)
    s_hi = jnp.concatenate([jnp.zeros((L, half), F32), sin, jnp.zeros((L, rest), F32)], axis=1)
    scale = HEAD_DIM ** -0.5
    return jnp.stack([c * scale, c]), jnp.stack([s_lo * scale, s_lo]), jnp.stack([s_hi * scale, s_hi])


def _proj_kernel(rotate, xn_ref, w_ref, *rest):
    o_ref = rest[-1]
    r = _dot(xn_ref[...], w_ref[...])
    if rotate:
        c_ref, slo_ref, shi_ref = rest[:3]
        tn = r.shape[1]
        half = ROPE_DIM // 2
        wide = lambda t_ref: jnp.concatenate([t_ref[...]] * (tn // HEAD_DIM), axis=1)
        r = r * wide(c_ref) + pltpu.roll(r, tn - half, 1) * wide(slo_ref) + pltpu.roll(r, half, 1) * wide(shi_ref)
    o_ref[...] = r.astype(o_ref.dtype)


def projection(xn, w, out_dtype, *, tm, tn, rope_len=None, name):
    M, D = xn.shape
    N = w.shape[1]
    in_specs = [
        pl.BlockSpec((tm, D), lambda i, j: (i, 0)),
        pl.BlockSpec((D, tn), lambda i, j: (0, j)),
    ]
    args = [xn, w]
    if rope_len is not None:
        assert tn == D_ATT and N == 2 * D_ATT and rope_len % tm == 0
        spec = pl.BlockSpec((None, tm, HEAD_DIM), lambda i, j: (j, i % (rope_len // tm), 0))
        in_specs += [spec, spec, spec]
        args += list(rope_tables(rope_len))
    return pl.pallas_call(
        partial(_proj_kernel, rope_len is not None),
        grid=(M // tm, N // tn),
        in_specs=in_specs,
        out_specs=pl.BlockSpec((tm, tn), lambda i, j: (i, j)),
        out_shape=jax.ShapeDtypeStruct((M, N), out_dtype),
        compiler_params=_params("parallel", "arbitrary"),
        name=name,
    )(*args)


ATT_RADIUS = 64
assert all(w // (2 * d) == ATT_RADIUS for w, d in ATT_GROUPS)


def _attn_kernel(T, Ls, q_ref, kp_ref, kc_ref, kn_ref, vp_ref, vc_ref, vn_ref, o_ref, lse_ref, kbuf, vbuf):
    t = pl.program_id(2)
    R = ATT_RADIUS
    SB = 2 * R
    for buf, prv, cur, nxt in ((kbuf, kp_ref, kc_ref, kn_ref), (vbuf, vp_ref, vc_ref, vn_ref)):
        buf[0:R] = prv[...]
        buf[R:R + T] = cur[...]
        buf[R + T:R + T + R] = nxt[...]
    qi = lax.broadcasted_iota(jnp.int32, (SB, 2 * SB), 0)
    kk = lax.broadcasted_iota(jnp.int32, (SB, 2 * SB), 1)
    band = (kk >= qi) & (kk <= qi + 2 * R)
    lane = lax.broadcasted_iota(jnp.int32, (SB, LANES), 1)
    for sb in range(T // SB):
        kpos = t * T + (sb * SB - R) + kk
        mask = band & (kpos >= 0) & (kpos < Ls)
        lse = jnp.zeros((SB, LANES), F32)
        for h in range(HEADS_PER_GROUP):
            cols = slice(h * HEAD_DIM, (h + 1) * HEAD_DIM)
            q = q_ref[sb * SB:(sb + 1) * SB, cols]
            k = kbuf[sb * SB:(sb + 2) * SB, cols]
            v = vbuf[sb * SB:(sb + 2) * SB, cols]
            s = lax.dot_general(q, k, (((1,), (1,)), ((), ())), preferred_element_type=F32)
            s = jnp.where(mask, s, NEG_INF)
            m = jnp.max(s, axis=1, keepdims=True)
            p = jnp.exp(s - m)
            l = jnp.sum(p, axis=1, keepdims=True)
            o = _dot(p.astype(BF16), v) / l
            o_ref[sb * SB:(sb + 1) * SB, cols] = o.astype(o_ref.dtype)
            lse = jnp.where(lane == h, m + jnp.log(l), lse)
        lse_ref[sb * SB:(sb + 1) * SB, :] = lse


SLAB = 16


def _softmax_heads(q, k, v, mask):
    lane = lax.broadcasted_iota(jnp.int32, (q.shape[0], LANES), 1)
    lse = jnp.zeros((q.shape[0], LANES), F32)
    outs = []
    for h in range(HEADS_PER_GROUP):
        cols = slice(h * HEAD_DIM, (h + 1) * HEAD_DIM)
        s = lax.dot_general(q[:, cols], k[:, cols], (((1,), (1,)), ((), ())), preferred_element_type=F32)
        s = jnp.where(mask, s, NEG_INF)
        m = jnp.max(s, axis=1, keepdims=True)
        p = jnp.exp(s - m)
        l = jnp.sum(p, axis=1, keepdims=True)
        outs.append(_dot(p.astype(BF16), v[:, cols]) / l)
        lse = jnp.where(lane == h, m + jnp.log(l), lse)
    return jnp.concatenate(outs, axis=1), lse


def _attn_slab_kernel(d, NS, Ls, q_ref, kp_ref, kc_ref, kn_ref, vp_ref, vc_ref, vn_ref, o_ref, lse_ref,
                      qs, ks, vs, os_, ls):
    t = pl.program_id(1)
    R = ATT_RADIUS
    SB = 2 * R
    J = SLAB // d
    NH = kp_ref.shape[0]
    QA = SB // J
    KA = QA + 2 * NH
    n_sb = NS // QA
    to_class_major = lambda ref: pltpu.einshape("abc->bac", ref[...])
    qs[...] = to_class_major(q_ref)
    for buf, prv, cur, nxt in ((ks, kp_ref, kc_ref, kn_ref), (vs, vp_ref, vc_ref, vn_ref)):
        buf[:, 0:NH] = to_class_major(prv)
        buf[:, NH:NH + NS] = to_class_major(cur)
        buf[:, NH + NS:NH + NS + NH] = to_class_major(nxt)
    qi = lax.broadcasted_iota(jnp.int32, (SB, 2 * SB), 0)
    kk = lax.broadcasted_iota(jnp.int32, (SB, 2 * SB), 1)
    sq_rel = J * (qi % QA) + qi // QA
    sk_rel = J * (kk % KA - NH) + kk // KA
    band = jnp.abs(sk_rel - sq_rel) <= R

    def body(it, carry):
        r = it // n_sb
        a0 = pl.multiple_of((it % n_sb) * QA, QA)
        sk = J * (t * NS + a0) + sk_rel
        mask = band & (sk >= 0) & (sk < Ls)
        gather = lambda buf, n: jnp.concatenate([buf[j * d + r, pl.ds(a0, n), :] for j in range(J)], axis=0)
        o, lse = _softmax_heads(gather(qs, QA), gather(ks, KA), gather(vs, KA), mask)
        o = o.astype(os_.dtype)
        for j in range(J):
            os_[j * d + r, pl.ds(a0, QA), :] = o[j * QA:(j + 1) * QA]
            ls[j * d + r, pl.ds(a0, QA), :] = lse[j * QA:(j + 1) * QA]
        return carry

    lax.fori_loop(0, d * n_sb, body, 0)
    o_ref[...] = pltpu.einshape("bac->abc", os_[...])
    lse_ref[...] = pltpu.einshape("bac->abc", ls[...])


def dilated_attention_slabs(srcs, Bsz, L, g, *, NS=128):
    d = ATT_GROUPS[g][1]
    M = Bsz * L
    R = ATT_RADIUS
    GW = HEADS_PER_GROUP * HEAD_DIM
    NH = R * d // SLAB
    TB = NS * SLAB
    assert SLAB % d == 0 and L % TB == 0 and NS % NH == 0 and (2 * R) % (SLAB // d) == 0
    nmb, nhb, hpm = L // TB, L // (NH * SLAB), NS // NH
    views = [a.reshape(M // SLAB, SLAB, a.shape[1]) for a, _ in srcs]
    col = lambda which: srcs[which][1]

    def main(which):
        return pl.BlockSpec((NS, SLAB, GW), lambda b, t: (b * nmb + t, 0, col(which)))

    def prev(which):
        return pl.BlockSpec((NH, SLAB, GW), lambda b, t: (jnp.maximum(b * nhb + t * hpm - 1, b * nhb), 0, col(which)))

    def nxt(which):
        return pl.BlockSpec((NH, SLAB, GW),
                            lambda b, t: (jnp.minimum(b * nhb + (t + 1) * hpm, (b + 1) * nhb - 1), 0, col(which)))

    o, lse = pl.pallas_call(
        partial(_attn_slab_kernel, d, NS, L // d),
        grid=(Bsz, nmb),
        in_specs=[main(0), prev(1), main(1), nxt(1), prev(2), main(2), nxt(2)],
        out_specs=[pl.BlockSpec((NS, SLAB, GW), lambda b, t: (b * nmb + t, 0, 0)),
                   pl.BlockSpec((NS, SLAB, LANES), lambda b, t: (b * nmb + t, 0, 0))],
        out_shape=[jax.ShapeDtypeStruct((M // SLAB, SLAB, GW), BF16),
                   jax.ShapeDtypeStruct((M // SLAB, SLAB, LANES), F32)],
        scratch_shapes=[pltpu.VMEM((SLAB, NS, GW), BF16),
                        pltpu.VMEM((SLAB, NS + 2 * NH, GW), BF16), pltpu.VMEM((SLAB, NS + 2 * NH, GW), BF16),
                        pltpu.VMEM((SLAB, NS, GW), BF16), pltpu.VMEM((SLAB, NS, LANES), F32)],
        compiler_params=_params("parallel", "arbitrary"),
        name=f"dilated_attention_g{g}",
    )(views[0], views[1], views[1], views[1], views[2], views[2], views[2])
    return o.reshape(M, GW), lse.reshape(M, LANES)


def dilated_attention_group(srcs, Bsz, L, g, *, T=256):
    d = ATT_GROUPS[g][1]
    M = Bsz * L
    Ls = L // d
    R = ATT_RADIUS
    GW = HEADS_PER_GROUP * HEAD_DIM
    assert d == 1 and Ls % T == 0 and T % (2 * R) == 0
    nrb, nhb, hpt = Ls // T, Ls // R, T // R

    def main(which):
        return pl.BlockSpec((T, GW), lambda b, r, t: (b * nrb + t, srcs[which][1]))

    def prev(which):
        return pl.BlockSpec((R, GW), lambda b, r, t: (jnp.maximum(b * nhb + t * hpt - 1, b * nhb), srcs[which][1]))

    def nxt(which):
        return pl.BlockSpec((R, GW), lambda b, r, t: (jnp.minimum(b * nhb + (t + 1) * hpt, (b + 1) * nhb - 1),
                                                      srcs[which][1]))

    o, lse = pl.pallas_call(
        partial(_attn_kernel, T, Ls),
        grid=(Bsz, d, nrb),
        in_specs=[main(0), prev(1), main(1), nxt(1), prev(2), main(2), nxt(2)],
        out_specs=[pl.BlockSpec((T, GW), lambda b, r, t: (b * nrb + t, r)),
                   pl.BlockSpec((T, LANES), lambda b, r, t: (b * nrb + t, r))],
        out_shape=[jax.ShapeDtypeStruct((M // d, d * GW), BF16),
                   jax.ShapeDtypeStruct((M // d, d * LANES), F32)],
        scratch_shapes=[pltpu.VMEM((T + 2 * R, GW), BF16), pltpu.VMEM((T + 2 * R, GW), BF16)],
        compiler_params=_params("parallel", "parallel", "arbitrary"),
        name=f"dilated_attention_g{g}",
    )(srcs[0][0], srcs[1][0], srcs[1][0], srcs[1][0], srcs[2][0], srcs[2][0], srcs[2][0])
    return o.reshape(M, GW), lse.reshape(M, LANES)


def _merge_kernel(x_ref, a_ref, o0_ref, o1_ref, o2_ref, l0_ref, l1_ref, l2_ref, ga_ref, gb_ref,
                  post_g_ref, next_g_ref, whp_ref, wap_ref, wo_ref, o_ref, u_ref):
    lses = [l0_ref[...], l1_ref[...], l2_ref[...]]
    mx = jnp.maximum(jnp.maximum(lses[0], lses[1]), lses[2])
    es = [jnp.exp(l - mx) for l in lses]
    den = es[0] + es[1] + es[2]
    wts = [e / den for e in es]
    outs = [o0_ref, o1_ref, o2_ref]
    heads = []
    for h in range(HEADS_PER_GROUP):
        cols = slice(h * HEAD_DIM, (h + 1) * HEAD_DIM)
        heads.append(sum(wts[g][:, h:h + 1] * outs[g][:, cols].astype(F32) for g in range(N_GROUPS)))
    att = jnp.concatenate(heads, axis=1).astype(BF16)
    a = _dot(a_ref[...], whp_ref[...])
    b = _dot(att, wap_ref[...])
    gate_a, gate_b = ga_ref[...].astype(F32), gb_ref[...].astype(F32)
    merged = (jax.nn.sigmoid(gate_a) * a + jax.nn.sigmoid(gate_b) * b).astype(BF16)
    mix = _dot(merged, wo_ref[...])
    y = x_ref[...] + _rms(mix, post_g_ref[...])
    o_ref[...] = y
    u_ref[...] = _rms(y, next_g_ref[...]).astype(u_ref.dtype)


def merge_out(x, a_in, att_outs, att_lses, gates, post_g, next_g, whp, wap, wo, *, tm=256):
    M, D = x.shape
    const = lambda i: (0, 0)
    rows = lambda a: pl.BlockSpec((tm, a.shape[1]), lambda i: (i, 0))
    return pl.pallas_call(
        _merge_kernel,
        grid=(M // tm,),
        in_specs=[rows(x), rows(a_in)] + [rows(o) for o in att_outs] + [rows(l) for l in att_lses] + [
            pl.BlockSpec((tm, D), lambda i: (i, 0)),
            pl.BlockSpec((tm, D), lambda i: (i, 1)),
            pl.BlockSpec((1, D), const),
            pl.BlockSpec((1, D), const),
            pl.BlockSpec(whp.shape, const),
            pl.BlockSpec(wap.shape, const),
            pl.BlockSpec(wo.shape, const),
        ],
        out_specs=[pl.BlockSpec((tm, D), lambda i: (i, 0)), pl.BlockSpec((tm, D), lambda i: (i, 0))],
        out_shape=[jax.ShapeDtypeStruct((M, D), F32), jax.ShapeDtypeStruct((M, D), BF16)],
        compiler_params=_params("parallel"),
        name="merge_out",
    )(x, a_in, *att_outs, *att_lses, gates, gates, post_g, next_g, whp, wap, wo)


def _dft_constants(L):
    N = 2 * L
    N2 = FFT_INNER
    N1 = N // N2
    h = N1 // 2
    idx1 = np.arange(N1)
    ang1 = -2.0 * np.pi * ((idx1[:, None] * idx1[None, :]) % N1) / N1
    f1r, f1i = np.cos(ang1), np.sin(ang1)
    s_data = np.block([[f1r[:, :h], -f1i[:, :h]], [f1i[:, :h], f1r[:, :h]]])
    s_filt = np.concatenate([f1r, f1i], axis=0)
    ar, ai = f1r[:h, :], -f1i[:h, :]
    t_fin = np.block([[ar, -ai], [ai, ar]])
    idx2 = np.arange(N2)
    ang2 = -2.0 * np.pi * ((idx2[:, None] * idx2[None, :]) % N2) / N2
    angt = -2.0 * np.pi * (idx1[:, None] * idx2[None, :]) / N
    return dict(
        N1=N1,
        s_data=jnp.asarray(s_data, BF16), s_filt=jnp.asarray(s_filt, BF16), t_fin=jnp.asarray(t_fin, BF16),
        f2r=jnp.asarray(np.cos(ang2), F32), f2i=jnp.asarray(np.sin(ang2), F32),
        twr=jnp.asarray(np.cos(angt).reshape(N1, 1, N2), F32),
        twi=jnp.asarray(np.sin(angt).reshape(N1, 1, N2), F32),
    )


def _hyena_in_kernel(tm, L, x_ref, prev_ref, next_ref, w_ref, cw_ref, cb_ref, o_ref, xn_ref):
    i = pl.program_id(0)
    H = prev_ref.shape[0]

    @pl.when(pl.program_id(1) == 0)
    def _():
        xn_ref[0:H] = prev_ref[...]
        xn_ref[H:H + tm] = x_ref[...]
        xn_ref[H + tm:H + tm + H] = next_ref[...]

    r = _dot(xn_ref[...], w_ref[...])
    row = lax.broadcasted_iota(jnp.int32, r.shape, 0)
    pos0 = (i * tm) % L
    outside = ((row < H) & (pos0 == 0)) | ((row >= H + tm) & (pos0 + tm == L))
    r = jnp.where(outside, 0.0, r)
    n = r.shape[0]
    y = pltpu.roll(r, 1, 0) * cw_ref[0:1, :] + r * cw_ref[1:2, :] + pltpu.roll(r, n - 1, 0) * cw_ref[2:3, :]
    o_ref[0] = (y[H:H + tm] + cb_ref[...]).astype(o_ref.dtype)


def hyena_inputs(x, w, conv_w, conv_b, L, *, tm=1024):
    M, D = x.shape
    C = w.shape[1] // 3
    H = SLAB
    nh = M // H
    assert L % tm == 0 and tm % H == 0
    return pl.pallas_call(
        partial(_hyena_in_kernel, tm, L),
        grid=(M // tm, 3),
        in_specs=[
            pl.BlockSpec((tm, D), lambda i, j: (i, 0)),
            pl.BlockSpec((H, D), lambda i, j: (jnp.maximum(i * (tm // H) - 1, 0), 0)),
            pl.BlockSpec((H, D), lambda i, j: (jnp.minimum((i + 1) * (tm // H), nh - 1), 0)),
            pl.BlockSpec((D, C), lambda i, j: (0, j)),
            pl.BlockSpec((SHORT_CONV, C), lambda i, j: (0, j)),
            pl.BlockSpec((1, C), lambda i, j: (0, j)),
        ],
        out_specs=pl.BlockSpec((1, tm, C), lambda i, j: (j, i, 0)),
        out_shape=jax.ShapeDtypeStruct((3, M, C), BF16),
        scratch_shapes=[pltpu.VMEM((tm + 2 * H, D), BF16)],
        compiler_params=_params("parallel", "arbitrary"),
        name="hyena_inputs",
    )(x, x, x, w, conv_w, conv_b)


def _filter_kernel(L, tl, w1_ref, b1_ref, w2_ref, b2_ref, w3_ref, b3_ref, w4hi_ref, w4lo_ref,
                   freq_ref, fvec_ref, delta_ref, kf_ref):
    i = pl.program_id(0)
    HALF = LANES // 2
    wide = delta_ref.shape[1]
    C = wide // (2 * HYENA_ORDER)
    row = i * tl + lax.broadcasted_iota(jnp.int32, (tl, LANES), 0)
    lane = lax.broadcasted_iota(jnp.int32, (tl, LANES), 1)
    posf = jnp.where(lane < HALF, row, L - row).astype(F32)
    t = posf / (L - 1)
    a = fvec_ref[...] * ((2.0 * math.pi / L) * posf)
    lh = lane % HALF
    feats = jnp.where(lh < FILTER_BANDS, jnp.cos(a),
                      jnp.where(lh < 2 * FILTER_BANDS, -jnp.sin(a),
                                jnp.where(lh == 2 * FILTER_BANDS, t, 0.0)))
    freq = freq_ref[...]
    dense = lambda v, w_ref, b_ref: jnp.sin(freq * (
        jnp.dot(v, w_ref[...], precision=HIGHEST, preferred_element_type=F32) + b_ref[...]))
    h = dense(dense(dense(feats, w1_ref, b1_ref), w2_ref, b2_ref), w3_ref, b3_ref)
    h_hi = h.astype(BF16)
    h_lo = (h - h_hi.astype(F32)).astype(BF16)
    taps = _dot(h_hi, w4hi_ref[...]) + _dot(h_lo, w4hi_ref[...]) + _dot(h_hi, w4lo_ref[...])
    row_w = i * tl + lax.broadcasted_iota(jnp.int32, (tl, wide), 0)
    col_w = lax.broadcasted_iota(jnp.int32, (tl, wide), 1)
    backward = col_w >= wide // 2
    t_w = jnp.where(backward, L - row_w, row_w).astype(F32) / (L - 1)
    taps = taps * jnp.exp(-t_w * delta_ref[...])
    taps = jnp.where(backward & (row_w == 0), 0.0, taps)
    for n in range(HYENA_ORDER):
        for direction in range(2):
            c0 = (direction * HYENA_ORDER + n) * C
            kf_ref[n, direction] = taps[:, c0:c0 + C].astype(kf_ref.dtype)


def hyena_filter_taps(L, fw1, fb1, fw2, fb2, fw3, fb3, fw4, ffreq, *, tl=256):
    C = D_HYENA
    H = fw2.shape[0]
    HALF = LANES // 2
    assert H <= HALF and 2 * FILTER_BANDS + 1 <= HALF

    def both(a, rows):
        blk = jnp.zeros((HALF if rows else 1, HALF), F32).at[:a.shape[0], :a.shape[1]].set(a)
        if not rows:
            return jnp.concatenate([blk, blk], axis=1)
        zero = jnp.zeros_like(blk)
        return jnp.concatenate([jnp.concatenate([blk, zero], axis=1), jnp.concatenate([zero, blk], axis=1)], axis=0)

    w1 = both(jnp.concatenate([fw1[1:], fw1[:1]], axis=0), True)
    w2, w3 = both(fw2, True), both(fw3, True)
    b1, b2, b3, freq = (both(v[None, :], False) for v in (fb1, fb2, fb3, ffreq))
    w4 = fw4.reshape(H, HYENA_ORDER, 2, C)
    wide = 2 * HYENA_ORDER * C
    w4p = jnp.zeros((LANES, wide), F32)
    w4p = w4p.at[:H, :wide // 2].set(w4[:, :, 0].reshape(H, HYENA_ORDER * C))
    w4p = w4p.at[HALF:HALF + H, wide // 2:].set(w4[:, :, 1].reshape(H, HYENA_ORDER * C))
    w4hi = w4p.astype(BF16)
    w4lo = (w4p - w4hi.astype(F32)).astype(BF16)
    bands = np.linspace(1e-4, FILTER_BANDS - 1, FILTER_BANDS, dtype=np.float32)
    fvec = np.zeros((1, LANES), np.float32)
    for base in (0, HALF):
        fvec[0, base:base + FILTER_BANDS] = bands
        fvec[0, base + FILTER_BANDS:base + 2 * FILTER_BANDS] = bands
    max_decay = math.log(DECAY_TARGET) / FAST_DECAY_PCT
    min_decay = math.log(DECAY_TARGET) / SLOW_DECAY_PCT
    deltas = np.abs(np.linspace(min_decay, max_decay, C, dtype=np.float32))
    delta_w = np.tile(deltas[None, :], (1, 2 * HYENA_ORDER))
    const = lambda i: (0, 0)
    args = (w1, b1, w2, b2, w3, b3, w4hi, w4lo, freq, jnp.asarray(fvec), jnp.asarray(delta_w))
    return pl.pallas_call(
        partial(_filter_kernel, L, tl),
        grid=(L // tl,),
        in_specs=[pl.BlockSpec(a.shape, const) for a in args],
        out_specs=pl.BlockSpec((HYENA_ORDER, 2, tl, C), lambda i: (0, 0, i, 0)),
        out_shape=jax.ShapeDtypeStruct((HYENA_ORDER, 2, L, C), BF16),
        compiler_params=_params("parallel"),
        name="hyena_filter_taps",
    )(*args)


FFT_ROWS = 16
FFT_COLS = 512


def _stage1_kernel(s_ref, x_ref, y_ref):
    _, h, rows, cols = x_ref.shape
    s = s_ref[...]
    xt = pltpu.einshape("abc->bac", x_ref[...].reshape(2 * h, rows, cols))
    yt = jnp.stack([_dot(s, xt[b]).astype(y_ref.dtype) for b in range(rows)], axis=0)
    y_ref[...] = pltpu.einshape("bac->abc", yt).reshape(y_ref.shape)


def fft_stage1(x, which, s):
    _, _, h, N2, C = x.shape
    N1 = 2 * h
    return pl.pallas_call(
        _stage1_kernel,
        grid=(N2 // FFT_ROWS, C // FFT_COLS),
        in_specs=[pl.BlockSpec((2 * N1, N1), lambda j, c: (0, 0)),
                  pl.BlockSpec((None, 2, h, FFT_ROWS, FFT_COLS), lambda j, c: (which, 0, 0, j, c))],
        out_specs=pl.BlockSpec((2, N1, FFT_ROWS, FFT_COLS), lambda j, c: (0, 0, j, c)),
        out_shape=jax.ShapeDtypeStruct((2, N1, N2, C), BF16),
        compiler_params=_params("parallel", "parallel"),
        name="fft_stage1",
    )(s, x)


FFT_K1_PER_STEP = 2


def _mid_kernel(inv_n, f2r_ref, f2i_ref, twr_ref, twi_ref, yf_ref, y_ref, u_ref):
    f2r, f2i = f2r_ref[...], f2i_ref[...]
    for k in range(y_ref.shape[1]):
        twr, twi = twr_ref[k], twi_ref[k]
        gr, gi = f2r * twr - f2i * twi, f2r * twi + f2i * twr
        grb, gib = gr.astype(BF16), gi.astype(BF16)

        def inner(ref):
            re, im = ref[0, k], ref[1, k]
            return _dot(grb, re) - _dot(gib, im), _dot(gib, re) + _dot(grb, im)

        hr, hi = inner(yf_ref)
        zr, zi = inner(y_ref)
        pr = (zr * hr - zi * hi).astype(BF16)
        pi = (zr * hi + zi * hr).astype(BF16)
        irb = (gr.T * inv_n).astype(BF16)
        iib = (gi.T * (-inv_n)).astype(BF16)
        u_ref[0, k] = (_dot(irb, pr) - _dot(iib, pi)).astype(u_ref.dtype)
        u_ref[1, k] = (_dot(iib, pr) + _dot(irb, pi)).astype(u_ref.dtype)


def fft_mid(yf, y, dc):
    _, N1, N2, C = y.shape
    kb = FFT_K1_PER_STEP
    blk = pl.BlockSpec((2, kb, N2, C), lambda k: (0, k, 0, 0))
    const = pl.BlockSpec((N2, N2), lambda k: (0, 0))
    tw = pl.BlockSpec((kb, 1, N2), lambda k: (k, 0, 0))
    return pl.pallas_call(
        partial(_mid_kernel, 1.0 / (N1 * N2)),
        grid=(N1 // kb,),
        in_specs=[const, const, tw, tw, blk, blk],
        out_specs=blk,
        out_shape=jax.ShapeDtypeStruct((2, N1, N2, C), BF16),
        compiler_params=_params("parallel"),
        name="fft_mid",
    )(dc['f2r'], dc['f2i'], dc['twr'], dc['twi'], yf, y)


def _final_kernel(t_ref, u_ref, z_ref, gate_ref, bias_ref, o_ref):
    _, n1, rows, cols = u_ref.shape
    t = t_ref[...]
    ut = pltpu.einshape("abc->bac", u_ref[...].reshape(2 * n1, rows, cols))
    convt = jnp.stack([_dot(t, ut[b]).astype(BF16) for b in range(rows)], axis=0)
    conv = pltpu.einshape("bac->abc", convt).reshape(o_ref.shape).astype(F32)
    z = z_ref[...].astype(F32)
    o_ref[...] = (gate_ref[...].astype(F32) * (conv + bias_ref[...] * z)).astype(o_ref.dtype)


def fft_final(u, t_fin, z, z_which, gate, gate_which, bias, bias_which):
    _, N1, N2, C = u.shape
    h = N1 // 2

    def half(which):
        return pl.BlockSpec((None, 2, h, FFT_ROWS, FFT_COLS), lambda j, c: (which, 0, 0, j, c))

    return pl.pallas_call(
        _final_kernel,
        grid=(N2 // FFT_ROWS, C // FFT_COLS),
        in_specs=[pl.BlockSpec((N1, 2 * N1), lambda j, c: (0, 0)),
                  pl.BlockSpec((2, N1, FFT_ROWS, FFT_COLS), lambda j, c: (0, 0, j, c)),
                  half(z_which), half(gate_which),
                  pl.BlockSpec((None, 1, FFT_COLS), lambda j, c: (bias_which, 0, c))],
        out_specs=half(0),
        out_shape=jax.ShapeDtypeStruct((1, 2, h, N2, C), BF16),
        compiler_params=_params("parallel", "parallel"),
        name="fft_final",
    )(t_fin, u, z, gate, bias.reshape(bias.shape[0], 1, C))


def hyena_branch(zs, L, p):
    _, M, C = zs.shape
    assert M == 2 * L, "the batch pair rides as real/imaginary parts"
    dc = _dft_constants(L)
    split = (2, dc['N1'] // 2, FFT_INNER, C)
    taps = hyena_filter_taps(L, *p['filt']).reshape(HYENA_ORDER, *split)
    zs = zs.reshape(3, *split)
    z, z_which = zs, 0
    for n in range(HYENA_ORDER):
        u = fft_mid(fft_stage1(taps, n, dc['s_filt']), fft_stage1(z, z_which, dc['s_data']), dc)
        z, z_which = fft_final(u, dc['t_fin'], z, z_which, zs, n + 1, p['hy_bias'], n), 0
    return z.reshape(M, C)


def _split_w_in(w):
    c_q = 3 * D_HYENA
    c_v = c_q + 2 * D_ATT
    c_g = c_v + D_ATT
    return (w[:, :c_q].astype(BF16), w[:, c_q:c_v].astype(BF16),
            jnp.concatenate([w[:, c_g:], w[:, c_v:c_g]], axis=1).astype(BF16))


def _layer(x, p):
    Bsz, L, D = x.shape
    M = Bsz * L
    x0 = x.reshape(M, D)
    x1, u = ffn_block(x0, p['ffn1_pre_g'], p['ffn1_post_g'], p['ffn1_w_gate'], p['ffn1_w_up'], p['ffn1_w_down'],
                      next_g=p['mix_pre_g'])
    w_hy, w_qk, w_gv = p['w_in']
    hy_in = hyena_inputs(u, w_hy, p['hy_conv_w'], p['hy_conv_b'], L)
    qk = projection(u, w_qk, BF16, tm=512, tn=D_ATT, rope_len=L, name="inproj_qk")
    gv = projection(u, w_gv, BF16, tm=1024, tn=w_gv.shape[1] // 4, name="inproj_gates_v")
    a2 = hyena_branch(hy_in, L, p)
    GW = HEADS_PER_GROUP * HEAD_DIM
    v_col0 = N_BRANCH * D_MODEL // GW
    att = []
    for g in range(N_GROUPS):
        srcs = ((qk, g), (qk, N_GROUPS + g), (gv, v_col0 + g))
        fn = dilated_attention_group if ATT_GROUPS[g][1] == 1 else dilated_attention_slabs
        att.append(fn(srcs, Bsz, L, g))
    x2, xn2 = merge_out(x1, a2, [o for o, _ in att], [l for _, l in att], gv, p['mix_post_g'], p['ffn2_pre_g'],
                        p['w_hy_proj'], p['w_att_proj'], p['w_out'])
    x3 = ffn_block(x2, p['ffn2_pre_g'], p['ffn2_post_g'], p['ffn2_w_gate'], p['ffn2_w_up'], p['ffn2_w_down'],
                   xn=xn2)
    return x3.reshape(Bsz, L, D)


def kernel(x_prompt, x_sample, ffn1_pre_g, ffn1_post_g, ffn1_w_gate, ffn1_w_up, ffn1_w_down, mix_pre_g, mix_post_g, w_in, hy_conv_w, hy_conv_b, filt_w1, filt_b1, filt_w2, filt_b2, filt_w3, filt_b3, filt_w4, filt_freq, hy_bias, w_hy_proj, w_att_proj, w_out, ffn2_pre_g, ffn2_post_g, ffn2_w_gate, ffn2_w_up, ffn2_w_down):
    assert ffn1_w_gate.shape[0] == 1
    p = {
        'ffn1_pre_g': ffn1_pre_g, 'ffn1_post_g': ffn1_post_g,
        'ffn1_w_gate': ffn1_w_gate[0].astype(BF16), 'ffn1_w_up': ffn1_w_up[0].astype(BF16),
        'ffn1_w_down': ffn1_w_down[0].astype(BF16),
        'mix_pre_g': mix_pre_g, 'mix_post_g': mix_post_g,
        'w_in': _split_w_in(w_in[0]),
        'hy_conv_w': hy_conv_w[0], 'hy_conv_b': hy_conv_b,
        'filt': (filt_w1[0], filt_b1[0], filt_w2[0], filt_b2[0], filt_w3[0], filt_b3[0], filt_w4[0], filt_freq[0]),
        'hy_bias': hy_bias[0],
        'w_hy_proj': w_hy_proj[0].astype(BF16), 'w_att_proj': w_att_proj[0].astype(BF16),
        'w_out': w_out[0].astype(BF16),
        'ffn2_pre_g': ffn2_pre_g, 'ffn2_post_g': ffn2_post_g,
        'ffn2_w_gate': ffn2_w_gate[0].astype(BF16), 'ffn2_w_up': ffn2_w_up[0].astype(BF16),
        'ffn2_w_down': ffn2_w_down[0].astype(BF16),
    }
    return (_layer(x_prompt, p), _layer(x_sample, p))
```

```python
import math
from functools import partial

import numpy as np
import jax
import jax.numpy as jnp
from jax import lax
from jax.experimental import pallas as pl
from jax.experimental.pallas import tpu as pltpu

D_MODEL = 2048
D_HYENA = 1024
HYENA_ORDER = 2
SHORT_CONV = 3
FILTER_EMB = 33
FILTER_BANDS = (FILTER_EMB - 1) // 2
FAST_DECAY_PCT = 0.3
SLOW_DECAY_PCT = 1.5
DECAY_TARGET = 1e-2
HEAD_DIM = 128
HEADS_PER_GROUP = 4
ATT_GROUPS = ((128, 1), (512, 4), (2048, 16))
N_GROUPS = len(ATT_GROUPS)
D_ATT = N_GROUPS * HEADS_PER_GROUP * HEAD_DIM
D_ATT_OUT = HEADS_PER_GROUP * HEAD_DIM
ROPE_DIM = HEAD_DIM // 4
ROPE_THETA = 500000.0
N_BRANCH = 2
D_IN_PROJ = 3 * D_HYENA + 3 * D_ATT + N_BRANCH * D_MODEL
D_FF = 5632
EPS = 1e-6
NEG_INF = -1e30

LANES = 128
FFT_INNER = 256
VMEM_BYTES_V7X = 64 * 1024 * 1024
VMEM_LIMIT_BYTES = VMEM_BYTES_V7X - 4 * 1024 * 1024
BF16 = jnp.bfloat16
F32 = jnp.float32
HIGHEST = lax.Precision.HIGHEST


def _rms(x, g):
    return x * lax.rsqrt(jnp.mean(x * x, axis=-1, keepdims=True) + EPS) * g


def _dot(a, b):
    return jnp.dot(a, b, preferred_element_type=F32)


def _params(*sem):
    return pltpu.CompilerParams(dimension_semantics=sem, vmem_limit_bytes=VMEM_LIMIT_BYTES)


FFN_SPLIT = 2


def _ffn_kernel(norm_in, norm_out, x_ref, xn_src_ref, post_g_ref, next_g_ref, wg_ref, wu_ref, wd_ref, *rest):
    o_ref = rest[0]
    u_ref = rest[1] if norm_out else None
    xn_ref = rest[-1] if norm_in else xn_src_ref
    j = pl.program_id(1)
    tf = wg_ref.shape[1]

    @pl.when(j == 0)
    def _():
        if norm_in:
            xn_ref[...] = _rms(x_ref[...], xn_src_ref[...]).astype(BF16)
        o_ref[...] = jnp.zeros_like(o_ref)

    xn = xn_ref[...]
    hs = []
    for c in range(FFN_SPLIT):
        cols = slice(c * (tf // FFN_SPLIT), (c + 1) * (tf // FFN_SPLIT))
        gate = _dot(xn, wg_ref[:, cols])
        up = _dot(xn, wu_ref[:, cols])
        hs.append((gate * jax.nn.sigmoid(gate) * up).astype(BF16))
    o_ref[...] += _dot(jnp.concatenate(hs, axis=1), wd_ref[...])

    @pl.when(j == pl.num_programs(1) - 1)
    def _():
        y = x_ref[...] + 0.5 * _rms(o_ref[...], post_g_ref[...])
        o_ref[...] = y
        if norm_out:
            u_ref[...] = _rms(y, next_g_ref[...]).astype(u_ref.dtype)


def ffn_block(x, pre_g, post_g, wg, wu, wd, *, xn=None, next_g=None, tm=512, tf=512):
    M, D = x.shape
    FF = wg.shape[1]
    norm_in, norm_out = xn is None, next_g is not None
    rows = pl.BlockSpec((tm, D), lambda i, j: (i, 0))
    gain = pl.BlockSpec((1, D), lambda i, j: (0, 0))
    out = pl.pallas_call(
        partial(_ffn_kernel, norm_in, norm_out),
        grid=(M // tm, FF // tf),
        in_specs=[
            rows,
            gain if norm_in else rows,
            gain, gain,
            pl.BlockSpec((D, tf), lambda i, j: (0, j)),
            pl.BlockSpec((D, tf), lambda i, j: (0, j)),
            pl.BlockSpec((tf, D), lambda i, j: (j, 0)),
        ],
        out_specs=[rows, rows] if norm_out else [rows],
        out_shape=[jax.ShapeDtypeStruct((M, D), F32)] + ([jax.ShapeDtypeStruct((M, D), BF16)] if norm_out else []),
        scratch_shapes=[pltpu.VMEM((tm, D), BF16)] if norm_in else [],
        compiler_params=_params("parallel", "arbitrary"),
        name="ffn_block",
    )(x, pre_g if norm_in else xn, post_g, next_g if norm_out else post_g, wg, wu, wd)
    return out if norm_out else out[0]


def rope_tables(L):
    half = ROPE_DIM // 2
    inv_freq = jnp.power(ROPE_THETA, -jnp.arange(half, dtype=F32) / half)
    ang = jnp.arange(L, dtype=F32)[:, None] * inv_freq[None, :]
    cos, sin = jnp.cos(ang), jnp.sin(ang)
    rest = HEAD_DIM - ROPE_DIM
    c = jnp.concatenate([cos, cos, jnp.ones((L, rest), F32)], axis=1)
    s_lo = jnp.concatenate([-sin, jnp.zeros((L, half + rest), F32)], axis=1)
    s_hi = jnp.concatenate([jnp.zeros((L, half), F32), sin, jnp.zeros((L, rest), F32)], axis=1)
    scale = HEAD_DIM ** -0.5
    return jnp.stack([c * scale, c]), jnp.stack([s_lo * scale, s_lo]), jnp.stack([s_hi * scale, s_hi])


def _proj_kernel(rotate, xn_ref, w_ref, *rest):
    o_ref = rest[-1]
    r = _dot(xn_ref[...], w_ref[...])
    if rotate:
        c_ref, slo_ref, shi_ref = rest[:3]
        tn = r.shape[1]
        half = ROPE_DIM // 2
        wide = lambda t_ref: jnp.concatenate([t_ref[...]] * (tn // HEAD_DIM), axis=1)
        r = r * wide(c_ref) + pltpu.roll(r, tn - half, 1) * wide(slo_ref) + pltpu.roll(r, half, 1) * wide(shi_ref)
    o_ref[...] = r.astype(o_ref.dtype)


def projection(xn, w, out_dtype, *, tm, tn, rope_len=None, name):
    M, D = xn.shape
    N = w.shape[1]
    in_specs = [
        pl.BlockSpec((tm, D), lambda i, j: (i, 0)),
        pl.BlockSpec((D, tn), lambda i, j: (0, j)),
    ]
    args = [xn, w]
    if rope_len is not None:
        assert tn == D_ATT and N == 2 * D_ATT and rope_len % tm == 0
        spec = pl.BlockSpec((None, tm, HEAD_DIM), lambda i, j: (j, i % (rope_len // tm), 0))
        in_specs += [spec, spec, spec]
        args += list(rope_tables(rope_len))
    return pl.pallas_call(
        partial(_proj_kernel, rope_len is not None),
        grid=(M // tm, N // tn),
        in_specs=in_specs,
        out_specs=pl.BlockSpec((tm, tn), lambda i, j: (i, j)),
        out_shape=jax.ShapeDtypeStruct((M, N), out_dtype),
        compiler_params=_params("parallel", "arbitrary"),
        name=name,
    )(*args)


ATT_RADIUS = 64
assert all(w // (2 * d) == ATT_RADIUS for w, d in ATT_GROUPS)


def _attn_kernel(T, Ls, q_ref, kp_ref, kc_ref, kn_ref, vp_ref, vc_ref, vn_ref, o_ref, lse_ref, kbuf, vbuf):
    t = pl.program_id(2)
    R = ATT_RADIUS
    SB = 2 * R
    for buf, prv, cur, nxt in ((kbuf, kp_ref, kc_ref, kn_ref), (vbuf, vp_ref, vc_ref, vn_ref)):
        buf[0:R] = prv[...]
        buf[R:R + T] = cur[...]
        buf[R + T:R + T + R] = nxt[...]
    qi = lax.broadcasted_iota(jnp.int32, (SB, 2 * SB), 0)
    kk = lax.broadcasted_iota(jnp.int32, (SB, 2 * SB), 1)
    band = (kk >= qi) & (kk <= qi + 2 * R)
    lane = lax.broadcasted_iota(jnp.int32, (SB, LANES), 1)
    for sb in range(T // SB):
        kpos = t * T + (sb * SB - R) + kk
        mask = band & (kpos >= 0) & (kpos < Ls)
        lse = jnp.zeros((SB, LANES), F32)
        for h in range(HEADS_PER_GROUP):
            cols = slice(h * HEAD_DIM, (h + 1) * HEAD_DIM)
            q = q_ref[sb * SB:(sb + 1) * SB, cols]
            k = kbuf[sb * SB:(sb + 2) * SB, cols]
            v = vbuf[sb * SB:(sb + 2) * SB, cols]
            s = lax.dot_general(q, k, (((1,), (1,)), ((), ())), preferred_element_type=F32)
            s = jnp.where(mask, s, NEG_INF)
            m = jnp.max(s, axis=1, keepdims=True)
            p = jnp.exp(s - m)
            l = jnp.sum(p, axis=1, keepdims=True)
            o = _dot(p.astype(BF16), v) / l
            o_ref[sb * SB:(sb + 1) * SB, cols] = o.astype(o_ref.dtype)
            lse = jnp.where(lane == h, m + jnp.log(l), lse)
        lse_ref[sb * SB:(sb + 1) * SB, :] = lse


SLAB = 16


def _softmax_heads(q, k, v, mask):
    lane = lax.broadcasted_iota(jnp.int32, (q.shape[0], LANES), 1)
    lse = jnp.zeros((q.shape[0], LANES), F32)
    outs = []
    for h in range(HEADS_PER_GROUP):
        cols = slice(h * HEAD_DIM, (h + 1) * HEAD_DIM)
        s = lax.dot_general(q[:, cols], k[:, cols], (((1,), (1,)), ((), ())), preferred_element_type=F32)
        s = jnp.where(mask, s, NEG_INF)
        m = jnp.max(s, axis=1, keepdims=True)
        p = jnp.exp(s - m)
        l = jnp.sum(p, axis=1, keepdims=True)
        outs.append(_dot(p.astype(BF16), v[:, cols]) / l)
        lse = jnp.where(lane == h, m + jnp.log(l), lse)
    return jnp.concatenate(outs, axis=1), lse


def _attn_slab_kernel(d, NS, Ls, q_ref, kp_ref, kc_ref, kn_ref, vp_ref, vc_ref, vn_ref, o_ref, lse_ref,
                      qs, ks, vs, os_, ls):
    t = pl.program_id(1)
    R = ATT_RADIUS
    SB = 2 * R
    J = SLAB // d
    NH = kp_ref.shape[0]
    QA = SB // J
    KA = QA + 2 * NH
    n_sb = NS // QA
    to_class_major = lambda ref: pltpu.einshape("abc->bac", ref[...])
    qs[...] = to_class_major(q_ref)
    for buf, prv, cur, nxt in ((ks, kp_ref, kc_ref, kn_ref), (vs, vp_ref, vc_ref, vn_ref)):
        buf[:, 0:NH] = to_class_major(prv)
        buf[:, NH:NH + NS] = to_class_major(cur)
        buf[:, NH + NS:NH + NS + NH] = to_class_major(nxt)
    qi = lax.broadcasted_iota(jnp.int32, (SB, 2 * SB), 0)
    kk = lax.broadcasted_iota(jnp.int32, (SB, 2 * SB), 1)
    sq_rel = J * (qi % QA) + qi // QA
    sk_rel = J * (kk % KA - NH) + kk // KA
    band = jnp.abs(sk_rel - sq_rel) <= R

    def body(it, carry):
        r = it // n_sb
        a0 = pl.multiple_of((it % n_sb) * QA, QA)
        sk = J * (t * NS + a0) + sk_rel
        mask = band & (sk >= 0) & (sk < Ls)
        gather = lambda buf, n: jnp.concatenate([buf[j * d + r, pl.ds(a0, n), :] for j in range(J)], axis=0)
        o, lse = _softmax_heads(gather(qs, QA), gather(ks, KA), gather(vs, KA), mask)
        o = o.astype(os_.dtype)
        for j in range(J):
            os_[j * d + r, pl.ds(a0, QA), :] = o[j * QA:(j + 1) * QA]
            ls[j * d + r, pl.ds(a0, QA), :] = lse[j * QA:(j + 1) * QA]
        return carry

    lax.fori_loop(0, d * n_sb, body, 0)
    o_ref[...] = pltpu.einshape("bac->abc", os_[...])
    lse_ref[...] = pltpu.einshape("bac->abc", ls[...])


def dilated_attention_slabs(srcs, Bsz, L, g, *, NS=128):
    d = ATT_GROUPS[g][1]
    M = Bsz * L
    R = ATT_RADIUS
    GW = HEADS_PER_GROUP * HEAD_DIM
    NH = R * d // SLAB
    TB = NS * SLAB
    assert SLAB % d == 0 and L % TB == 0 and NS % NH == 0 and (2 * R) % (SLAB // d) == 0
    nmb, nhb, hpm = L // TB, L // (NH * SLAB), NS // NH
    views = [a.reshape(M // SLAB, SLAB, a.shape[1]) for a, _ in srcs]
    col = lambda which: srcs[which][1]

    def main(which):
        return pl.BlockSpec((NS, SLAB, GW), lambda b, t: (b * nmb + t, 0, col(which)))

    def prev(which):
        return pl.BlockSpec((NH, SLAB, GW), lambda b, t: (jnp.maximum(b * nhb + t * hpm - 1, b * nhb), 0, col(which)))

    def nxt(which):
        return pl.BlockSpec((NH, SLAB, GW),
                            lambda b, t: (jnp.minimum(b * nhb + (t + 1) * hpm, (b + 1) * nhb - 1), 0, col(which)))

    o, lse = pl.pallas_call(
        partial(_attn_slab_kernel, d, NS, L // d),
        grid=(Bsz, nmb),
        in_specs=[main(0), prev(1), main(1), nxt(1), prev(2), main(2), nxt(2)],
        out_specs=[pl.BlockSpec((NS, SLAB, GW), lambda b, t: (b * nmb + t, 0, 0)),
                   pl.BlockSpec((NS, SLAB, LANES), lambda b, t: (b * nmb + t, 0, 0))],
        out_shape=[jax.ShapeDtypeStruct((M // SLAB, SLAB, GW), BF16),
                   jax.ShapeDtypeStruct((M // SLAB, SLAB, LANES), F32)],
        scratch_shapes=[pltpu.VMEM((SLAB, NS, GW), BF16),
                        pltpu.VMEM((SLAB, NS + 2 * NH, GW), BF16), pltpu.VMEM((SLAB, NS + 2 * NH, GW), BF16),
                        pltpu.VMEM((SLAB, NS, GW), BF16), pltpu.VMEM((SLAB, NS, LANES), F32)],
        compiler_params=_params("parallel", "arbitrary"),
        name=f"dilated_attention_g{g}",
    )(views[0], views[1], views[1], views[1], views[2], views[2], views[2])
    return o.reshape(M, GW), lse.reshape(M, LANES)


def dilated_attention_group(srcs, Bsz, L, g, *, T=256):
    d = ATT_GROUPS[g][1]
    M = Bsz * L
    Ls = L // d
    R = ATT_RADIUS
    GW = HEADS_PER_GROUP * HEAD_DIM
    assert d == 1 and Ls % T == 0 and T % (2 * R) == 0
    nrb, nhb, hpt = Ls // T, Ls // R, T // R

    def main(which):
        return pl.BlockSpec((T, GW), lambda b, r, t: (b * nrb + t, srcs[which][1]))

    def prev(which):
        return pl.BlockSpec((R, GW), lambda b, r, t: (jnp.maximum(b * nhb + t * hpt - 1, b * nhb), srcs[which][1]))

    def nxt(which):
        return pl.BlockSpec((R, GW), lambda b, r, t: (jnp.minimum(b * nhb + (t + 1) * hpt, (b + 1) * nhb - 1),
                                                      srcs[which][1]))

    o, lse = pl.pallas_call(
        partial(_attn_kernel, T, Ls),
        grid=(Bsz, d, nrb),
        in_specs=[main(0), prev(1), main(1), nxt(1), prev(2), main(2), nxt(2)],
        out_specs=[pl.BlockSpec((T, GW), lambda b, r, t: (b * nrb + t, r)),
                   pl.BlockSpec((T, LANES), lambda b, r, t: (b * nrb + t, r))],
        out_shape=[jax.ShapeDtypeStruct((M // d, d * GW), BF16),
                   jax.ShapeDtypeStruct((M // d, d * LANES), F32)],
        scratch_shapes=[pltpu.VMEM((T + 2 * R, GW), BF16), pltpu.VMEM((T + 2 * R, GW), BF16)],
        compiler_params=_params("parallel", "parallel", "arbitrary"),
        name=f"dilated_attention_g{g}",
    )(srcs[0][0], srcs[1][0], srcs[1][0], srcs[1][0], srcs[2][0], srcs[2][0], srcs[2][0])
    return o.reshape(M, GW), lse.reshape(M, LANES)


def _merge_kernel(x_ref, a_ref, o0_ref, o1_ref, o2_ref, l0_ref, l1_ref, l2_ref, ga_ref, gb_ref,
                  post_g_ref, next_g_ref, whp_ref, wap_ref, wo_ref, o_ref, u_ref):
    lses = [l0_ref[...], l1_ref[...], l2_ref[...]]
    mx = jnp.maximum(jnp.maximum(lses[0], lses[1]), lses[2])
    es = [jnp.exp(l - mx) for l in lses]
    den = es[0] + es[1] + es[2]
    wts = [e / den for e in es]
    outs = [o0_ref, o1_ref, o2_ref]
    heads = []
    for h in range(HEADS_PER_GROUP):
        cols = slice(h * HEAD_DIM, (h + 1) * HEAD_DIM)
        heads.append(sum(wts[g][:, h:h + 1] * outs[g][:, cols].astype(F32) for g in range(N_GROUPS)))
    att = jnp.concatenate(heads, axis=1).astype(BF16)
    a = _dot(a_ref[...], whp_ref[...])
    b = _dot(att, wap_ref[...])
    gate_a, gate_b = ga_ref[...].astype(F32), gb_ref[...].astype(F32)
    merged = (jax.nn.sigmoid(gate_a) * a + jax.nn.sigmoid(gate_b) * b).astype(BF16)
    mix = _dot(merged, wo_ref[...])
    y = x_ref[...] + _rms(mix, post_g_ref[...])
    o_ref[...] = y
    u_ref[...] = _rms(y, next_g_ref[...]).astype(u_ref.dtype)


def merge_out(x, a_in, att_outs, att_lses, gates, post_g, next_g, whp, wap, wo, *, tm=512):
    M, D = x.shape
    const = lambda i: (0, 0)
    rows = lambda a: pl.BlockSpec((tm, a.shape[1]), lambda i: (i, 0))
    return pl.pallas_call(
        _merge_kernel,
        grid=(M // tm,),
        in_specs=[rows(x), rows(a_in)] + [rows(o) for o in att_outs] + [rows(l) for l in att_lses] + [
            pl.BlockSpec((tm, D), lambda i: (i, 0)),
            pl.BlockSpec((tm, D), lambda i: (i, 1)),
            pl.BlockSpec((1, D), const),
            pl.BlockSpec((1, D), const),
            pl.BlockSpec(whp.shape, const, pipeline_mode=pl.Buffered(1)),
            pl.BlockSpec(wap.shape, const, pipeline_mode=pl.Buffered(1)),
            pl.BlockSpec(wo.shape, const, pipeline_mode=pl.Buffered(1)),
        ],
        out_specs=[pl.BlockSpec((tm, D), lambda i: (i, 0)), pl.BlockSpec((tm, D), lambda i: (i, 0))],
        out_shape=[jax.ShapeDtypeStruct((M, D), F32), jax.ShapeDtypeStruct((M, D), BF16)],
        compiler_params=_params("parallel"),
        name="merge_out",
    )(x, a_in, *att_outs, *att_lses, gates, gates, post_g, next_g, whp, wap, wo)


def _dft_constants(L):
    N = 2 * L
    N2 = FFT_INNER
    N1 = N // N2
    h = N1 // 2
    idx1 = np.arange(N1)
    ang1 = -2.0 * np.pi * ((idx1[:, None] * idx1[None, :]) % N1) / N1
    f1r, f1i = np.cos(ang1), np.sin(ang1)
    s_data = np.block([[f1r[:, :h], -f1i[:, :h]], [f1i[:, :h], f1r[:, :h]]])
    s_filt = np.concatenate([f1r, f1i], axis=0)
    ar, ai = f1r[:h, :], -f1i[:h, :]
    t_fin = np.block([[ar, -ai], [ai, ar]])
    idx2 = np.arange(N2)
    ang2 = -2.0 * np.pi * ((idx2[:, None] * idx2[None, :]) % N2) / N2
    angt = -2.0 * np.pi * (idx1[:, None] * idx2[None, :]) / N
    return dict(
        N1=N1,
        s_data=jnp.asarray(s_data, BF16), s_filt=jnp.asarray(s_filt, BF16), t_fin=jnp.asarray(t_fin, BF16),
        f2r=jnp.asarray(np.cos(ang2), F32), f2i=jnp.asarray(np.sin(ang2), F32),
        twr=jnp.asarray(np.cos(angt).reshape(N1, 1, N2), F32),
        twi=jnp.asarray(np.sin(angt).reshape(N1, 1, N2), F32),
    )


def _hyena_in_kernel(tm, L, x_ref, prev_ref, next_ref, w_ref, cw_ref, cb_ref, o_ref, xn_ref):
    i = pl.program_id(0)
    H = prev_ref.shape[0]

    @pl.when(pl.program_id(1) == 0)
    def _():
        xn_ref[0:H] = prev_ref[...]
        xn_ref[H:H + tm] = x_ref[...]
        xn_ref[H + tm:H + tm + H] = next_ref[...]

    r = _dot(xn_ref[...], w_ref[...])
    row = lax.broadcasted_iota(jnp.int32, r.shape, 0)
    pos0 = (i * tm) % L
    outside = ((row < H) & (pos0 == 0)) | ((row >= H + tm) & (pos0 + tm == L))
    r = jnp.where(outside, 0.0, r)
    n = r.shape[0]
    y = pltpu.roll(r, 1, 0) * cw_ref[0:1, :] + r * cw_ref[1:2, :] + pltpu.roll(r, n - 1, 0) * cw_ref[2:3, :]
    o_ref[0] = (y[H:H + tm] + cb_ref[...]).astype(o_ref.dtype)


def hyena_inputs(x, w, conv_w, conv_b, L, *, tm=1024):
    M, D = x.shape
    C = w.shape[1] // 3
    H = SLAB
    nh = M // H
    assert L % tm == 0 and tm % H == 0
    return pl.pallas_call(
        partial(_hyena_in_kernel, tm, L),
        grid=(M // tm, 3),
        in_specs=[
            pl.BlockSpec((tm, D), lambda i, j: (i, 0)),
            pl.BlockSpec((H, D), lambda i, j: (jnp.maximum(i * (tm // H) - 1, 0), 0)),
            pl.BlockSpec((H, D), lambda i, j: (jnp.minimum((i + 1) * (tm // H), nh - 1), 0)),
            pl.BlockSpec((D, C), lambda i, j: (0, j)),
            pl.BlockSpec((SHORT_CONV, C), lambda i, j: (0, j)),
            pl.BlockSpec((1, C), lambda i, j: (0, j)),
        ],
        out_specs=pl.BlockSpec((1, tm, C), lambda i, j: (j, i, 0)),
        out_shape=jax.ShapeDtypeStruct((3, M, C), BF16),
        scratch_shapes=[pltpu.VMEM((tm + 2 * H, D), BF16)],
        compiler_params=_params("parallel", "arbitrary"),
        name="hyena_inputs",
    )(x, x, x, w, conv_w, conv_b)


def _filter_kernel(L, tl, w1_ref, b1_ref, w2_ref, b2_ref, w3_ref, b3_ref, w4hi_ref, w4lo_ref,
                   freq_ref, fvec_ref, delta_ref, kf_ref):
    i = pl.program_id(0)
    HALF = LANES // 2
    wide = delta_ref.shape[1]
    C = wide // (2 * HYENA_ORDER)
    row = i * tl + lax.broadcasted_iota(jnp.int32, (tl, LANES), 0)
    lane = lax.broadcasted_iota(jnp.int32, (tl, LANES), 1)
    posf = jnp.where(lane < HALF, row, L - row).astype(F32)
    t = posf / (L - 1)
    a = fvec_ref[...] * ((2.0 * math.pi / L) * posf)
    lh = lane % HALF
    feats = jnp.where(lh < FILTER_BANDS, jnp.cos(a),
                      jnp.where(lh < 2 * FILTER_BANDS, -jnp.sin(a),
                                jnp.where(lh == 2 * FILTER_BANDS, t, 0.0)))
    freq = freq_ref[...]
    dense = lambda v, w_ref, b_ref: jnp.sin(freq * (
        jnp.dot(v, w_ref[...], precision=HIGHEST, preferred_element_type=F32) + b_ref[...]))
    h = dense(dense(dense(feats, w1_ref, b1_ref), w2_ref, b2_ref), w3_ref, b3_ref)
    h_hi = h.astype(BF16)
    h_lo = (h - h_hi.astype(F32)).astype(BF16)
    taps = _dot(h_hi, w4hi_ref[...]) + _dot(h_lo, w4hi_ref[...]) + _dot(h_hi, w4lo_ref[...])
    row_w = i * tl + lax.broadcasted_iota(jnp.int32, (tl, wide), 0)
    col_w = lax.broadcasted_iota(jnp.int32, (tl, wide), 1)
    backward = col_w >= wide // 2
    t_w = jnp.where(backward, L - row_w, row_w).astype(F32) / (L - 1)
    taps = taps * jnp.exp(-t_w * delta_ref[...])
    taps = jnp.where(backward & (row_w == 0), 0.0, taps)
    for n in range(HYENA_ORDER):
        for direction in range(2):
            c0 = (direction * HYENA_ORDER + n) * C
            kf_ref[n, direction] = taps[:, c0:c0 + C].astype(kf_ref.dtype)


def hyena_filter_taps(L, fw1, fb1, fw2, fb2, fw3, fb3, fw4, ffreq, *, tl=256):
    C = D_HYENA
    H = fw2.shape[0]
    HALF = LANES // 2
    assert H <= HALF and 2 * FILTER_BANDS + 1 <= HALF

    def both(a, rows):
        blk = jnp.zeros((HALF if rows else 1, HALF), F32).at[:a.shape[0], :a.shape[1]].set(a)
        if not rows:
            return jnp.concatenate([blk, blk], axis=1)
        zero = jnp.zeros_like(blk)
        return jnp.concatenate([jnp.concatenate([blk, zero], axis=1), jnp.concatenate([zero, blk], axis=1)], axis=0)

    w1 = both(jnp.concatenate([fw1[1:], fw1[:1]], axis=0), True)
    w2, w3 = both(fw2, True), both(fw3, True)
    b1, b2, b3, freq = (both(v[None, :], False) for v in (fb1, fb2, fb3, ffreq))
    w4 = fw4.reshape(H, HYENA_ORDER, 2, C)
    wide = 2 * HYENA_ORDER * C
    w4p = jnp.zeros((LANES, wide), F32)
    w4p = w4p.at[:H, :wide // 2].set(w4[:, :, 0].reshape(H, HYENA_ORDER * C))
    w4p = w4p.at[HALF:HALF + H, wide // 2:].set(w4[:, :, 1].reshape(H, HYENA_ORDER * C))
    w4hi = w4p.astype(BF16)
    w4lo = (w4p - w4hi.astype(F32)).astype(BF16)
    bands = np.linspace(1e-4, FILTER_BANDS - 1, FILTER_BANDS, dtype=np.float32)
    fvec = np.zeros((1, LANES), np.float32)
    for base in (0, HALF):
        fvec[0, base:base + FILTER_BANDS] = bands
        fvec[0, base + FILTER_BANDS:base + 2 * FILTER_BANDS] = bands
    max_decay = math.log(DECAY_TARGET) / FAST_DECAY_PCT
    min_decay = math.log(DECAY_TARGET) / SLOW_DECAY_PCT
    deltas = np.abs(np.linspace(min_decay, max_decay, C, dtype=np.float32))
    delta_w = np.tile(deltas[None, :], (1, 2 * HYENA_ORDER))
    const = lambda i: (0, 0)
    args = (w1, b1, w2, b2, w3, b3, w4hi, w4lo, freq, jnp.asarray(fvec), jnp.asarray(delta_w))
    return pl.pallas_call(
        partial(_filter_kernel, L, tl),
        grid=(L // tl,),
        in_specs=[pl.BlockSpec(a.shape, const) for a in args],
        out_specs=pl.BlockSpec((HYENA_ORDER, 2, tl, C), lambda i: (0, 0, i, 0)),
        out_shape=jax.ShapeDtypeStruct((HYENA_ORDER, 2, L, C), BF16),
        compiler_params=_params("parallel"),
        name="hyena_filter_taps",
    )(*args)


FFT_ROWS = 16
FFT_COLS = 512
FFT_COLS_IN = 1024


def _stage1_kernel(s_ref, x_ref, y_ref):
    _, h, rows, cols = x_ref.shape
    s = s_ref[...]
    xt = pltpu.einshape("abc->bac", x_ref[...].reshape(2 * h, rows, cols))
    yt = jnp.stack([_dot(s, xt[b]).astype(y_ref.dtype) for b in range(rows)], axis=0)
    y_ref[...] = pltpu.einshape("bac->abc", yt).reshape(y_ref.shape)


def fft_stage1(x, which, s):
    _, _, h, N2, C = x.shape
    N1 = 2 * h
    return pl.pallas_call(
        _stage1_kernel,
        grid=(N2 // FFT_ROWS, C // FFT_COLS_IN),
        in_specs=[pl.BlockSpec((2 * N1, N1), lambda j, c: (0, 0)),
                  pl.BlockSpec((None, 2, h, FFT_ROWS, FFT_COLS_IN), lambda j, c: (which, 0, 0, j, c))],
        out_specs=pl.BlockSpec((2, N1, FFT_ROWS, FFT_COLS_IN), lambda j, c: (0, 0, j, c)),
        out_shape=jax.ShapeDtypeStruct((2, N1, N2, C), BF16),
        compiler_params=_params("parallel", "parallel"),
        name="fft_stage1",
    )(s, x)


FFT_K1_PER_STEP = 2


def _mid_kernel(inv_n, f2r_ref, f2i_ref, twr_ref, twi_ref, yf_ref, y_ref, u_ref):
    f2r, f2i = f2r_ref[...], f2i_ref[...]
    for k in range(y_ref.shape[1]):
        twr, twi = twr_ref[k], twi_ref[k]
        gr, gi = f2r * twr - f2i * twi, f2r * twi + f2i * twr
        grb, gib = gr.astype(BF16), gi.astype(BF16)

        def inner(ref):
            re, im = ref[0, k], ref[1, k]
            return _dot(grb, re) - _dot(gib, im), _dot(gib, re) + _dot(grb, im)

        hr, hi = inner(yf_ref)
        zr, zi = inner(y_ref)
        pr = (zr * hr - zi * hi).astype(BF16)
        pi = (zr * hi + zi * hr).astype(BF16)
        irb = (gr.T * inv_n).astype(BF16)
        iib = (gi.T * (-inv_n)).astype(BF16)
        u_ref[0, k] = (_dot(irb, pr) - _dot(iib, pi)).astype(u_ref.dtype)
        u_ref[1, k] = (_dot(iib, pr) + _dot(irb, pi)).astype(u_ref.dtype)


def fft_mid(yf, y, dc):
    _, N1, N2, C = y.shape
    kb = FFT_K1_PER_STEP
    blk = pl.BlockSpec((2, kb, N2, C), lambda k: (0, k, 0, 0))
    const = pl.BlockSpec((N2, N2), lambda k: (0, 0))
    tw = pl.BlockSpec((kb, 1, N2), lambda k: (k, 0, 0))
    return pl.pallas_call(
        partial(_mid_kernel, 1.0 / (N1 * N2)),
        grid=(N1 // kb,),
        in_specs=[const, const, tw, tw, blk, blk],
        out_specs=blk,
        out_shape=jax.ShapeDtypeStruct((2, N1, N2, C), BF16),
        compiler_params=_params("parallel"),
        name="fft_mid",
    )(dc['f2r'], dc['f2i'], dc['twr'], dc['twi'], yf, y)


def _final_kernel(t_ref, u_ref, z_ref, gate_ref, bias_ref, o_ref):
    _, n1, rows, cols = u_ref.shape
    t = t_ref[...]
    ut = pltpu.einshape("abc->bac", u_ref[...].reshape(2 * n1, rows, cols))
    convt = jnp.stack([_dot(t, ut[b]).astype(BF16) for b in range(rows)], axis=0)
    conv = pltpu.einshape("bac->abc", convt).reshape(o_ref.shape).astype(F32)
    z = z_ref[...].astype(F32)
    o_ref[...] = (gate_ref[...].astype(F32) * (conv + bias_ref[...] * z)).astype(o_ref.dtype)


def fft_final(u, t_fin, z, z_which, gate, gate_which, bias, bias_which):
    _, N1, N2, C = u.shape
    h = N1 // 2

    def half(which):
        return pl.BlockSpec((None, 2, h, FFT_ROWS, FFT_COLS), lambda j, c: (which, 0, 0, j, c))

    return pl.pallas_call(
        _final_kernel,
        grid=(N2 // FFT_ROWS, C // FFT_COLS),
        in_specs=[pl.BlockSpec((N1, 2 * N1), lambda j, c: (0, 0)),
                  pl.BlockSpec((2, N1, FFT_ROWS, FFT_COLS), lambda j, c: (0, 0, j, c)),
                  half(z_which), half(gate_which),
                  pl.BlockSpec((None, 1, FFT_COLS), lambda j, c: (bias_which, 0, c))],
        out_specs=half(0),
        out_shape=jax.ShapeDtypeStruct((1, 2, h, N2, C), BF16),
        compiler_params=_params("parallel", "parallel"),
        name="fft_final",
    )(t_fin, u, z, gate, bias.reshape(bias.shape[0], 1, C))


def hyena_branch(zs, L, p):
    _, M, C = zs.shape
    assert M == 2 * L, "the batch pair rides as real/imaginary parts"
    dc = _dft_constants(L)
    split = (2, dc['N1'] // 2, FFT_INNER, C)
    taps = hyena_filter_taps(L, *p['filt']).reshape(HYENA_ORDER, *split)
    zs = zs.reshape(3, *split)
    z, z_which = zs, 0
    for n in range(HYENA_ORDER):
        u = fft_mid(fft_stage1(taps, n, dc['s_filt']), fft_stage1(z, z_which, dc['s_data']), dc)
        z, z_which = fft_final(u, dc['t_fin'], z, z_which, zs, n + 1, p['hy_bias'], n), 0
    return z.reshape(M, C)


def _split_w_in(w):
    c_q = 3 * D_HYENA
    c_v = c_q + 2 * D_ATT
    c_g = c_v + D_ATT
    return (w[:, :c_q].astype(BF16), w[:, c_q:c_v].astype(BF16),
            jnp.concatenate([w[:, c_g:], w[:, c_v:c_g]], axis=1).astype(BF16))


def _layer(x, p):
    Bsz, L, D = x.shape
    M = Bsz * L
    x0 = x.reshape(M, D)
    x1, u = ffn_block(x0, p['ffn1_pre_g'], p['ffn1_post_g'], p['ffn1_w_gate'], p['ffn1_w_up'], p['ffn1_w_down'],
                      next_g=p['mix_pre_g'])
    w_hy, w_qk, w_gv = p['w_in']
    hy_in = hyena_inputs(u, w_hy, p['hy_conv_w'], p['hy_conv_b'], L)
    qk = projection(u, w_qk, BF16, tm=1024, tn=D_ATT, rope_len=L, name="inproj_qk")
    gv = projection(u, w_gv, BF16, tm=1024, tn=w_gv.shape[1] // 4, name="inproj_gates_v")
    a2 = hyena_branch(hy_in, L, p)
    GW = HEADS_PER_GROUP * HEAD_DIM
    v_col0 = N_BRANCH * D_MODEL // GW
    att = []
    for g in range(N_GROUPS):
        srcs = ((qk, g), (qk, N_GROUPS + g), (gv, v_col0 + g))
        fn = dilated_attention_group if ATT_GROUPS[g][1] == 1 else dilated_attention_slabs
        att.append(fn(srcs, Bsz, L, g))
    x2, xn2 = merge_out(x1, a2, [o for o, _ in att], [l for _, l in att], gv, p['mix_post_g'], p['ffn2_pre_g'],
                        p['w_hy_proj'], p['w_att_proj'], p['w_out'])
    x3 = ffn_block(x2, p['ffn2_pre_g'], p['ffn2_post_g'], p['ffn2_w_gate'], p['ffn2_w_up'], p['ffn2_w_down'],
                   xn=xn2)
    return x3.reshape(Bsz, L, D)


def kernel(x_prompt, x_sample, ffn1_pre_g, ffn1_post_g, ffn1_w_gate, ffn1_w_up, ffn1_w_down, mix_pre_g, mix_post_g, w_in, hy_conv_w, hy_conv_b, filt_w1, filt_b1, filt_w2, filt_b2, filt_w3, filt_b3, filt_w4, filt_freq, hy_bias, w_hy_proj, w_att_proj, w_out, ffn2_pre_g, ffn2_post_g, ffn2_w_gate, ffn2_w_up, ffn2_w_down):
    assert ffn1_w_gate.shape[0] == 1
    p = {
        'ffn1_pre_g': ffn1_pre_g, 'ffn1_post_g': ffn1_post_g,
        'ffn1_w_gate': ffn1_w_gate[0].astype(BF16), 'ffn1_w_up': ffn1_w_up[0].astype(BF16),
        'ffn1_w_down': ffn1_w_down[0].astype(BF16),
        'mix_pre_g': mix_pre_g, 'mix_post_g': mix_post_g,
        'w_in': _split_w_in(w_in[0]),
        'hy_conv_w': hy_conv_w[0], 'hy_conv_b': hy_conv_b,
        'filt': (filt_w1[0], filt_b1[0], filt_w2[0], filt_b2[0], filt_w3[0], filt_b3[0], filt_w4[0], filt_freq[0]),
        'hy_bias': hy_bias[0],
        'w_hy_proj': w_hy_proj[0].astype(BF16), 'w_att_proj': w_att_proj[0].astype(BF16),
        'w_out': w_out[0].astype(BF16),
        'ffn2_pre_g': ffn2_pre_g, 'ffn2_post_g': ffn2_post_g,
        'ffn2_w_gate': ffn2_w_gate[0].astype(BF16), 'ffn2_w_up': ffn2_w_up[0].astype(BF16),
        'ffn2_w_down': ffn2_w_down[0].astype(BF16),
    }
    return (_layer(x_prompt, p), _layer(x_sample, p))
```

```python
import math
from functools import partial

import numpy as np
import jax
import jax.numpy as jnp
from jax import lax
from jax.experimental import pallas as pl
from jax.experimental.pallas import tpu as pltpu

D_MODEL = 2048
D_HYENA = 1024
HYENA_ORDER = 2
SHORT_CONV = 3
FILTER_EMB = 33
FILTER_BANDS = (FILTER_EMB - 1) // 2
FAST_DECAY_PCT = 0.3
SLOW_DECAY_PCT = 1.5
DECAY_TARGET = 1e-2
HEAD_DIM = 128
HEADS_PER_GROUP = 4
ATT_GROUPS = ((128, 1), (512, 4), (2048, 16))
N_GROUPS = len(ATT_GROUPS)
D_ATT = N_GROUPS * HEADS_PER_GROUP * HEAD_DIM
D_ATT_OUT = HEADS_PER_GROUP * HEAD_DIM
ROPE_DIM = HEAD_DIM // 4
ROPE_THETA = 500000.0
N_BRANCH = 2
D_IN_PROJ = 3 * D_HYENA + 3 * D_ATT + N_BRANCH * D_MODEL
D_FF = 5632
EPS = 1e-6
NEG_INF = -1e30

LANES = 128
FFT_INNER = 256
VMEM_BYTES_V7X = 64 * 1024 * 1024
VMEM_LIMIT_BYTES = VMEM_BYTES_V7X - 4 * 1024 * 1024
BF16 = jnp.bfloat16
F32 = jnp.float32
HIGHEST = lax.Precision.HIGHEST


def _rms(x, g):
    return x * lax.rsqrt(jnp.mean(x * x, axis=-1, keepdims=True) + EPS) * g


def _dot(a, b):
    return jnp.dot(a, b, preferred_element_type=F32)


def _params(*sem):
    return pltpu.CompilerParams(dimension_semantics=sem, vmem_limit_bytes=VMEM_LIMIT_BYTES)


FFN_SPLIT = 2


def _ffn_kernel(norm_in, norm_out, x_ref, xn_src_ref, post_g_ref, next_g_ref, wg_ref, wu_ref, wd_ref, *rest):
    o_ref = rest[0]
    u_ref = rest[1] if norm_out else None
    xn_ref = rest[-1] if norm_in else xn_src_ref
    j = pl.program_id(1)
    tf = wg_ref.shape[1]

    @pl.when(j == 0)
    def _():
        if norm_in:
            xn_ref[...] = _rms(x_ref[...], xn_src_ref[...]).astype(BF16)
        o_ref[...] = jnp.zeros_like(o_ref)

    xn = xn_ref[...]
    hs = []
    for c in range(FFN_SPLIT):
        cols = slice(c * (tf // FFN_SPLIT), (c + 1) * (tf // FFN_SPLIT))
        gate = _dot(xn, wg_ref[:, cols])
        up = _dot(xn, wu_ref[:, cols])
        hs.append((gate * jax.nn.sigmoid(gate) * up).astype(BF16))
    o_ref[...] += _dot(jnp.concatenate(hs, axis=1), wd_ref[...])

    @pl.when(j == pl.num_programs(1) - 1)
    def _():
        y = x_ref[...] + 0.5 * _rms(o_ref[...], post_g_ref[...])
        o_ref[...] = y
        if norm_out:
            u_ref[...] = _rms(y, next_g_ref[...]).astype(u_ref.dtype)


def ffn_block(x, pre_g, post_g, wg, wu, wd, *, xn=None, next_g=None, tm=512, tf=512):
    M, D = x.shape
    FF = wg.shape[1]
    norm_in, norm_out = xn is None, next_g is not None
    rows = pl.BlockSpec((tm, D), lambda i, j: (i, 0))
    gain = pl.BlockSpec((1, D), lambda i, j: (0, 0))
    out = pl.pallas_call(
        partial(_ffn_kernel, norm_in, norm_out),
        grid=(M // tm, FF // tf),
        in_specs=[
            rows,
            gain if norm_in else rows,
            gain, gain,
            pl.BlockSpec((D, tf), lambda i, j: (0, j)),
            pl.BlockSpec((D, tf), lambda i, j: (0, j)),
            pl.BlockSpec((tf, D), lambda i, j: (j, 0)),
        ],
        out_specs=[rows, rows] if norm_out else [rows],
        out_shape=[jax.ShapeDtypeStruct((M, D), F32)] + ([jax.ShapeDtypeStruct((M, D), BF16)] if norm_out else []),
        scratch_shapes=[pltpu.VMEM((tm, D), BF16)] if norm_in else [],
        compiler_params=_params("parallel", "arbitrary"),
        name="ffn_block",
    )(x, pre_g if norm_in else xn, post_g, next_g if norm_out else post_g, wg, wu, wd)
    return out if norm_out else out[0]


def rope_tables(L):
    half = ROPE_DIM // 2
    inv_freq = jnp.power(ROPE_THETA, -jnp.arange(half, dtype=F32) / half)
    ang = jnp.arange(L, dtype=F32)[:, None] * inv_freq[None, :]
    cos, sin = jnp.cos(ang), jnp.sin(ang)
    rest = HEAD_DIM - ROPE_DIM
    c = jnp.concatenate([cos, cos, jnp.ones((L, rest), F32)], axis=1)
    s_lo = jnp.concatenate([-sin, jnp.zeros((L, half + rest), F32)], axis=1)
    s_hi = jnp.concatenate([jnp.zeros((L, half), F32), sin, jnp.zeros((L, rest), F32)], axis=1)
    scale = HEAD_DIM ** -0.5
    return jnp.stack([c * scale, c]), jnp.stack([s_lo * scale, s_lo]), jnp.stack([s_hi * scale, s_hi])


def _proj_kernel(rotate, xn_ref, w_ref, *rest):
    o_ref = rest[-1]
    r = _dot(xn_ref[...], w_ref[...])
    if rotate:
        c_ref, slo_ref, shi_ref = rest[:3]
        tn = r.shape[1]
        half = ROPE_DIM // 2
        wide = lambda t_ref: jnp.concatenate([t_ref[...]] * (tn // HEAD_DIM), axis=1)
        r = r * wide(c_ref) + pltpu.roll(r, tn - half, 1) * wide(slo_ref) + pltpu.roll(r, half, 1) * wide(shi_ref)
    o_ref[...] = r.astype(o_ref.dtype)


def projection(xn, w, out_dtype, *, tm, tn, rope_len=None, name):
    M, D = xn.shape
    N = w.shape[1]
    in_specs = [
        pl.BlockSpec((tm, D), lambda i, j: (i, 0)),
        pl.BlockSpec((D, tn), lambda i, j: (0, j)),
    ]
    args = [xn, w]
    if rope_len is not None:
        assert tn == D_ATT and N == 2 * D_ATT and rope_len % tm == 0
        spec = pl.BlockSpec((None, tm, HEAD_DIM), lambda i, j: (j, i % (rope_len // tm), 0))
        in_specs += [spec, spec, spec]
        args += list(rope_tables(rope_len))
    return pl.pallas_call(
        partial(_proj_kernel, rope_len is not None),
        grid=(M // tm, N // tn),
        in_specs=in_specs,
        out_specs=pl.BlockSpec((tm, tn), lambda i, j: (i, j)),
        out_shape=jax.ShapeDtypeStruct((M, N), out_dtype),
        compiler_params=_params("parallel", "arbitrary"),
        name=name,
    )(*args)


ATT_RADIUS = 64
assert all(w // (2 * d) == ATT_RADIUS for w, d in ATT_GROUPS)


def _attn_kernel(T, Ls, q_ref, kp_ref, kc_ref, kn_ref, vp_ref, vc_ref, vn_ref, o_ref, lse_ref, kbuf, vbuf):
    t = pl.program_id(2)
    R = ATT_RADIUS
    SB = 2 * R
    for buf, prv, cur, nxt in ((kbuf, kp_ref, kc_ref, kn_ref), (vbuf, vp_ref, vc_ref, vn_ref)):
        buf[0:R] = prv[...]
        buf[R:R + T] = cur[...]
        buf[R + T:R + T + R] = nxt[...]
    qi = lax.broadcasted_iota(jnp.int32, (SB, 2 * SB), 0)
    kk = lax.broadcasted_iota(jnp.int32, (SB, 2 * SB), 1)
    band = (kk >= qi) & (kk <= qi + 2 * R)
    lane = lax.broadcasted_iota(jnp.int32, (SB, LANES), 1)
    for sb in range(T // SB):
        kpos = t * T + (sb * SB - R) + kk
        mask = band & (kpos >= 0) & (kpos < Ls)
        lse = jnp.zeros((SB, LANES), F32)
        for h in range(HEADS_PER_GROUP):
            cols = slice(h * HEAD_DIM, (h + 1) * HEAD_DIM)
            q = q_ref[sb * SB:(sb + 1) * SB, cols]
            k = kbuf[sb * SB:(sb + 2) * SB, cols]
            v = vbuf[sb * SB:(sb + 2) * SB, cols]
            s = lax.dot_general(q, k, (((1,), (1,)), ((), ())), preferred_element_type=F32)
            s = jnp.where(mask, s, NEG_INF)
            m = jnp.max(s, axis=1, keepdims=True)
            p = jnp.exp(s - m)
            l = jnp.sum(p, axis=1, keepdims=True)
            o = _dot(p.astype(BF16), v) / l
            o_ref[sb * SB:(sb + 1) * SB, cols] = o.astype(o_ref.dtype)
            lse = jnp.where(lane == h, m + jnp.log(l), lse)
        lse_ref[sb * SB:(sb + 1) * SB, :] = lse


SLAB = 16


def _softmax_heads(q, k, v, mask):
    lane = lax.broadcasted_iota(jnp.int32, (q.shape[0], LANES), 1)
    lse = jnp.zeros((q.shape[0], LANES), F32)
    outs = []
    for h in range(HEADS_PER_GROUP):
        cols = slice(h * HEAD_DIM, (h + 1) * HEAD_DIM)
        s = lax.dot_general(q[:, cols], k[:, cols], (((1,), (1,)), ((), ())), preferred_element_type=F32)
        s = jnp.where(mask, s, NEG_INF)
        m = jnp.max(s, axis=1, keepdims=True)
        p = jnp.exp(s - m)
        l = jnp.sum(p, axis=1, keepdims=True)
        outs.append(_dot(p.astype(BF16), v[:, cols]) / l)
        lse = jnp.where(lane == h, m + jnp.log(l), lse)
    return jnp.concatenate(outs, axis=1), lse


def _attn_slab_kernel(d, NS, Ls, q_ref, kp_ref, kc_ref, kn_ref, vp_ref, vc_ref, vn_ref, o_ref, lse_ref,
                      qs, ks, vs, os_, ls):
    t = pl.program_id(1)
    R = ATT_RADIUS
    SB = 2 * R
    J = SLAB // d
    NH = kp_ref.shape[0]
    QA = SB // J
    KA = QA + 2 * NH
    n_sb = NS // QA
    to_class_major = lambda ref: pltpu.einshape("abc->bac", ref[...])
    qs[...] = to_class_major(q_ref)
    for buf, prv, cur, nxt in ((ks, kp_ref, kc_ref, kn_ref), (vs, vp_ref, vc_ref, vn_ref)):
        buf[:, 0:NH] = to_class_major(prv)
        buf[:, NH:NH + NS] = to_class_major(cur)
        buf[:, NH + NS:NH + NS + NH] = to_class_major(nxt)
    qi = lax.broadcasted_iota(jnp.int32, (SB, 2 * SB), 0)
    kk = lax.broadcasted_iota(jnp.int32, (SB, 2 * SB), 1)
    sq_rel = J * (qi % QA) + qi // QA
    sk_rel = J * (kk % KA - NH) + kk // KA
    band = jnp.abs(sk_rel - sq_rel) <= R

    def body(it, carry):
        r = it // n_sb
        a0 = pl.multiple_of((it % n_sb) * QA, QA)
        sk = J * (t * NS + a0) + sk_rel
        mask = band & (sk >= 0) & (sk < Ls)
        gather = lambda buf, n: jnp.concatenate([buf[j * d + r, pl.ds(a0, n), :] for j in range(J)], axis=0)
        o, lse = _softmax_heads(gather(qs, QA), gather(ks, KA), gather(vs, KA), mask)
        o = o.astype(os_.dtype)
        for j in range(J):
            os_[j * d + r, pl.ds(a0, QA), :] = o[j * QA:(j + 1) * QA]
            ls[j * d + r, pl.ds(a0, QA), :] = lse[j * QA:(j + 1) * QA]
        return carry

    lax.fori_loop(0, d * n_sb, body, 0, unroll=4)
    o_ref[...] = pltpu.einshape("bac->abc", os_[...])
    lse_ref[...] = pltpu.einshape("bac->abc", ls[...])


def dilated_attention_slabs(srcs, Bsz, L, g, *, NS=128):
    d = ATT_GROUPS[g][1]
    M = Bsz * L
    R = ATT_RADIUS
    GW = HEADS_PER_GROUP * HEAD_DIM
    NH = R * d // SLAB
    TB = NS * SLAB
    assert SLAB % d == 0 and L % TB == 0 and NS % NH == 0 and (2 * R) % (SLAB // d) == 0
    nmb, nhb, hpm = L // TB, L // (NH * SLAB), NS // NH
    views = [a.reshape(M // SLAB, SLAB, a.shape[1]) for a, _ in srcs]
    col = lambda which: srcs[which][1]

    def main(which):
        return pl.BlockSpec((NS, SLAB, GW), lambda b, t: (b * nmb + t, 0, col(which)))

    def prev(which):
        return pl.BlockSpec((NH, SLAB, GW), lambda b, t: (jnp.maximum(b * nhb + t * hpm - 1, b * nhb), 0, col(which)))

    def nxt(which):
        return pl.BlockSpec((NH, SLAB, GW),
                            lambda b, t: (jnp.minimum(b * nhb + (t + 1) * hpm, (b + 1) * nhb - 1), 0, col(which)))

    o, lse = pl.pallas_call(
        partial(_attn_slab_kernel, d, NS, L // d),
        grid=(Bsz, nmb),
        in_specs=[main(0), prev(1), main(1), nxt(1), prev(2), main(2), nxt(2)],
        out_specs=[pl.BlockSpec((NS, SLAB, GW), lambda b, t: (b * nmb + t, 0, 0)),
                   pl.BlockSpec((NS, SLAB, LANES), lambda b, t: (b * nmb + t, 0, 0))],
        out_shape=[jax.ShapeDtypeStruct((M // SLAB, SLAB, GW), BF16),
                   jax.ShapeDtypeStruct((M // SLAB, SLAB, LANES), F32)],
        scratch_shapes=[pltpu.VMEM((SLAB, NS, GW), BF16),
                        pltpu.VMEM((SLAB, NS + 2 * NH, GW), BF16), pltpu.VMEM((SLAB, NS + 2 * NH, GW), BF16),
                        pltpu.VMEM((SLAB, NS, GW), BF16), pltpu.VMEM((SLAB, NS, LANES), F32)],
        compiler_params=_params("parallel", "arbitrary"),
        name=f"dilated_attention_g{g}",
    )(views[0], views[1], views[1], views[1], views[2], views[2], views[2])
    return o.reshape(M, GW), lse.reshape(M, LANES)


def dilated_attention_group(srcs, Bsz, L, g, *, T=512):
    d = ATT_GROUPS[g][1]
    M = Bsz * L
    Ls = L // d
    R = ATT_RADIUS
    GW = HEADS_PER_GROUP * HEAD_DIM
    assert d == 1 and Ls % T == 0 and T % (2 * R) == 0
    nrb, nhb, hpt = Ls // T, Ls // R, T // R

    def main(which):
        return pl.BlockSpec((T, GW), lambda b, r, t: (b * nrb + t, srcs[which][1]))

    def prev(which):
        return pl.BlockSpec((R, GW), lambda b, r, t: (jnp.maximum(b * nhb + t * hpt - 1, b * nhb), srcs[which][1]))

    def nxt(which):
        return pl.BlockSpec((R, GW), lambda b, r, t: (jnp.minimum(b * nhb + (t + 1) * hpt, (b + 1) * nhb - 1),
                                                      srcs[which][1]))

    o, lse = pl.pallas_call(
        partial(_attn_kernel, T, Ls),
        grid=(Bsz, d, nrb),
        in_specs=[main(0), prev(1), main(1), nxt(1), prev(2), main(2), nxt(2)],
        out_specs=[pl.BlockSpec((T, GW), lambda b, r, t: (b * nrb + t, r)),
                   pl.BlockSpec((T, LANES), lambda b, r, t: (b * nrb + t, r))],
        out_shape=[jax.ShapeDtypeStruct((M // d, d * GW), BF16),
                   jax.ShapeDtypeStruct((M // d, d * LANES), F32)],
        scratch_shapes=[pltpu.VMEM((T + 2 * R, GW), BF16), pltpu.VMEM((T + 2 * R, GW), BF16)],
        compiler_params=_params("parallel", "parallel", "arbitrary"),
        name=f"dilated_attention_g{g}",
    )(srcs[0][0], srcs[1][0], srcs[1][0], srcs[1][0], srcs[2][0], srcs[2][0], srcs[2][0])
    return o.reshape(M, GW), lse.reshape(M, LANES)


def _merge_kernel(x_ref, a_ref, o0_ref, o1_ref, o2_ref, l0_ref, l1_ref, l2_ref, ga_ref, gb_ref,
                  post_g_ref, next_g_ref, whp_ref, wap_ref, wo_ref, o_ref, u_ref):
    lses = [l0_ref[...], l1_ref[...], l2_ref[...]]
    mx = jnp.maximum(jnp.maximum(lses[0], lses[1]), lses[2])
    es = [jnp.exp(l - mx) for l in lses]
    den = es[0] + es[1] + es[2]
    wts = [e / den for e in es]
    outs = [o0_ref, o1_ref, o2_ref]
    heads = []
    for h in range(HEADS_PER_GROUP):
        cols = slice(h * HEAD_DIM, (h + 1) * HEAD_DIM)
        heads.append(sum(wts[g][:, h:h + 1] * outs[g][:, cols].astype(F32) for g in range(N_GROUPS)))
    att = jnp.concatenate(heads, axis=1).astype(BF16)
    a = _dot(a_ref[...], whp_ref[...])
    b = _dot(att, wap_ref[...])
    gate_a, gate_b = ga_ref[...].astype(F32), gb_ref[...].astype(F32)
    merged = (jax.nn.sigmoid(gate_a) * a + jax.nn.sigmoid(gate_b) * b).astype(BF16)
    mix = _dot(merged, wo_ref[...])
    y = x_ref[...] + _rms(mix, post_g_ref[...])
    o_ref[...] = y
    u_ref[...] = _rms(y, next_g_ref[...]).astype(u_ref.dtype)


def merge_out(x, a_in, att_outs, att_lses, gates, post_g, next_g, whp, wap, wo, *, tm=512):
    M, D = x.shape
    const = lambda i: (0, 0)
    rows = lambda a: pl.BlockSpec((tm, a.shape[1]), lambda i: (i, 0))
    return pl.pallas_call(
        _merge_kernel,
        grid=(M // tm,),
        in_specs=[rows(x), rows(a_in)] + [rows(o) for o in att_outs] + [rows(l) for l in att_lses] + [
            pl.BlockSpec((tm, D), lambda i: (i, 0)),
            pl.BlockSpec((tm, D), lambda i: (i, 1)),
            pl.BlockSpec((1, D), const),
            pl.BlockSpec((1, D), const),
            pl.BlockSpec(whp.shape, const, pipeline_mode=pl.Buffered(1)),
            pl.BlockSpec(wap.shape, const, pipeline_mode=pl.Buffered(1)),
            pl.BlockSpec(wo.shape, const, pipeline_mode=pl.Buffered(1)),
        ],
        out_specs=[pl.BlockSpec((tm, D), lambda i: (i, 0)), pl.BlockSpec((tm, D), lambda i: (i, 0))],
        out_shape=[jax.ShapeDtypeStruct((M, D), F32), jax.ShapeDtypeStruct((M, D), BF16)],
        compiler_params=_params("parallel"),
        name="merge_out",
    )(x, a_in, *att_outs, *att_lses, gates, gates, post_g, next_g, whp, wap, wo)


def _dft_constants(L):
    N = 2 * L
    N2 = FFT_INNER
    N1 = N // N2
    h = N1 // 2
    idx1 = np.arange(N1)
    ang1 = -2.0 * np.pi * ((idx1[:, None] * idx1[None, :]) % N1) / N1
    f1r, f1i = np.cos(ang1), np.sin(ang1)
    s_data = np.block([[f1r[:, :h], -f1i[:, :h]], [f1i[:, :h], f1r[:, :h]]])
    s_filt = np.concatenate([f1r, f1i], axis=0)
    ar, ai = f1r[:h, :], -f1i[:h, :]
    t_fin = np.block([[ar, -ai], [ai, ar]])
    idx2 = np.arange(N2)
    ang2 = -2.0 * np.pi * ((idx2[:, None] * idx2[None, :]) % N2) / N2
    angt = -2.0 * np.pi * (idx1[:, None] * idx2[None, :]) / N
    return dict(
        N1=N1,
        s_data=jnp.asarray(s_data, BF16), s_filt=jnp.asarray(s_filt, BF16), t_fin=jnp.asarray(t_fin, BF16),
        f2r=jnp.asarray(np.cos(ang2), F32), f2i=jnp.asarray(np.sin(ang2), F32),
        twr=jnp.asarray(np.cos(angt).reshape(N1, 1, N2), F32),
        twi=jnp.asarray(np.sin(angt).reshape(N1, 1, N2), F32),
    )


def _hyena_in_kernel(tm, L, x_ref, prev_ref, next_ref, w_ref, cw_ref, cb_ref, o_ref, xn_ref):
    i = pl.program_id(0)
    H = prev_ref.shape[0]

    @pl.when(pl.program_id(1) == 0)
    def _():
        xn_ref[0:H] = prev_ref[...]
        xn_ref[H:H + tm] = x_ref[...]
        xn_ref[H + tm:H + tm + H] = next_ref[...]

    r = _dot(xn_ref[...], w_ref[...])
    row = lax.broadcasted_iota(jnp.int32, r.shape, 0)
    pos0 = (i * tm) % L
    outside = ((row < H) & (pos0 == 0)) | ((row >= H + tm) & (pos0 + tm == L))
    r = jnp.where(outside, 0.0, r)
    n = r.shape[0]
    y = pltpu.roll(r, 1, 0) * cw_ref[0:1, :] + r * cw_ref[1:2, :] + pltpu.roll(r, n - 1, 0) * cw_ref[2:3, :]
    o_ref[0] = (y[H:H + tm] + cb_ref[...]).astype(o_ref.dtype)


def hyena_inputs(x, w, conv_w, conv_b, L, *, tm=1024):
    M, D = x.shape
    C = w.shape[1] // 3
    H = SLAB
    nh = M // H
    assert L % tm == 0 and tm % H == 0
    return pl.pallas_call(
        partial(_hyena_in_kernel, tm, L),
        grid=(M // tm, 3),
        in_specs=[
            pl.BlockSpec((tm, D), lambda i, j: (i, 0)),
            pl.BlockSpec((H, D), lambda i, j: (jnp.maximum(i * (tm // H) - 1, 0), 0)),
            pl.BlockSpec((H, D), lambda i, j: (jnp.minimum((i + 1) * (tm // H), nh - 1), 0)),
            pl.BlockSpec((D, C), lambda i, j: (0, j)),
            pl.BlockSpec((SHORT_CONV, C), lambda i, j: (0, j)),
            pl.BlockSpec((1, C), lambda i, j: (0, j)),
        ],
        out_specs=pl.BlockSpec((1, tm, C), lambda i, j: (j, i, 0)),
        out_shape=jax.ShapeDtypeStruct((3, M, C), BF16),
        scratch_shapes=[pltpu.VMEM((tm + 2 * H, D), BF16)],
        compiler_params=_params("parallel", "arbitrary"),
        name="hyena_inputs",
    )(x, x, x, w, conv_w, conv_b)


def _filter_kernel(L, tl, w1_ref, b1_ref, w2_ref, b2_ref, w3_ref, b3_ref, w4hi_ref, w4lo_ref,
                   freq_ref, fvec_ref, delta_ref, kf_ref):
    i = pl.program_id(0)
    HALF = LANES // 2
    wide = delta_ref.shape[1]
    C = wide // (2 * HYENA_ORDER)
    row = i * tl + lax.broadcasted_iota(jnp.int32, (tl, LANES), 0)
    lane = lax.broadcasted_iota(jnp.int32, (tl, LANES), 1)
    posf = jnp.where(lane < HALF, row, L - row).astype(F32)
    t = posf / (L - 1)
    a = fvec_ref[...] * ((2.0 * math.pi / L) * posf)
    lh = lane % HALF
    feats = jnp.where(lh < FILTER_BANDS, jnp.cos(a),
                      jnp.where(lh < 2 * FILTER_BANDS, -jnp.sin(a),
                                jnp.where(lh == 2 * FILTER_BANDS, t, 0.0)))
    freq = freq_ref[...]
    dense = lambda v, w_ref, b_ref: jnp.sin(freq * (
        jnp.dot(v, w_ref[...], precision=HIGHEST, preferred_element_type=F32) + b_ref[...]))
    h = dense(dense(dense(feats, w1_ref, b1_ref), w2_ref, b2_ref), w3_ref, b3_ref)
    h_hi = h.astype(BF16)
    h_lo = (h - h_hi.astype(F32)).astype(BF16)
    taps = _dot(h_hi, w4hi_ref[...]) + _dot(h_lo, w4hi_ref[...]) + _dot(h_hi, w4lo_ref[...])
    row_w = i * tl + lax.broadcasted_iota(jnp.int32, (tl, wide), 0)
    col_w = lax.broadcasted_iota(jnp.int32, (tl, wide), 1)
    backward = col_w >= wide // 2
    t_w = jnp.where(backward, L - row_w, row_w).astype(F32) / (L - 1)
    taps = taps * jnp.exp(-t_w * delta_ref[...])
    taps = jnp.where(backward & (row_w == 0), 0.0, taps)
    for n in range(HYENA_ORDER):
        for direction in range(2):
            c0 = (direction * HYENA_ORDER + n) * C
            kf_ref[n, direction] = taps[:, c0:c0 + C].astype(kf_ref.dtype)


def hyena_filter_taps(L, fw1, fb1, fw2, fb2, fw3, fb3, fw4, ffreq, *, tl=512):
    C = D_HYENA
    H = fw2.shape[0]
    HALF = LANES // 2
    assert H <= HALF and 2 * FILTER_BANDS + 1 <= HALF

    def both(a, rows):
        blk = jnp.zeros((HALF if rows else 1, HALF), F32).at[:a.shape[0], :a.shape[1]].set(a)
        if not rows:
            return jnp.concatenate([blk, blk], axis=1)
        zero = jnp.zeros_like(blk)
        return jnp.concatenate([jnp.concatenate([blk, zero], axis=1), jnp.concatenate([zero, blk], axis=1)], axis=0)

    w1 = both(jnp.concatenate([fw1[1:], fw1[:1]], axis=0), True)
    w2, w3 = both(fw2, True), both(fw3, True)
    b1, b2, b3, freq = (both(v[None, :], False) for v in (fb1, fb2, fb3, ffreq))
    w4 = fw4.reshape(H, HYENA_ORDER, 2, C)
    wide = 2 * HYENA_ORDER * C
    w4p = jnp.zeros((LANES, wide), F32)
    w4p = w4p.at[:H, :wide // 2].set(w4[:, :, 0].reshape(H, HYENA_ORDER * C))
    w4p = w4p.at[HALF:HALF + H, wide // 2:].set(w4[:, :, 1].reshape(H, HYENA_ORDER * C))
    w4hi = w4p.astype(BF16)
    w4lo = (w4p - w4hi.astype(F32)).astype(BF16)
    bands = np.linspace(1e-4, FILTER_BANDS - 1, FILTER_BANDS, dtype=np.float32)
    fvec = np.zeros((1, LANES), np.float32)
    for base in (0, HALF):
        fvec[0, base:base + FILTER_BANDS] = bands
        fvec[0, base + FILTER_BANDS:base + 2 * FILTER_BANDS] = bands
    max_decay = math.log(DECAY_TARGET) / FAST_DECAY_PCT
    min_decay = math.log(DECAY_TARGET) / SLOW_DECAY_PCT
    deltas = np.abs(np.linspace(min_decay, max_decay, C, dtype=np.float32))
    delta_w = np.tile(deltas[None, :], (1, 2 * HYENA_ORDER))
    const = lambda i: (0, 0)
    args = (w1, b1, w2, b2, w3, b3, w4hi, w4lo, freq, jnp.asarray(fvec), jnp.asarray(delta_w))
    return pl.pallas_call(
        partial(_filter_kernel, L, tl),
        grid=(L // tl,),
        in_specs=[pl.BlockSpec(a.shape, const) for a in args],
        out_specs=pl.BlockSpec((HYENA_ORDER, 2, tl, C), lambda i: (0, 0, i, 0)),
        out_shape=jax.ShapeDtypeStruct((HYENA_ORDER, 2, L, C), BF16),
        compiler_params=_params("parallel"),
        name="hyena_filter_taps",
    )(*args)


FFT_ROWS = 16
FFT_COLS = 512
FFT_COLS_IN = 1024


def _stage1_kernel(s_ref, x_ref, y_ref):
    _, h, rows, cols = x_ref.shape
    s = s_ref[...]
    xt = pltpu.einshape("abc->bac", x_ref[...].reshape(2 * h, rows, cols))
    yt = jnp.stack([_dot(s, xt[b]).astype(y_ref.dtype) for b in range(rows)], axis=0)
    y_ref[...] = pltpu.einshape("bac->abc", yt).reshape(y_ref.shape)


def fft_stage1(x, which, s):
    _, _, h, N2, C = x.shape
    N1 = 2 * h
    return pl.pallas_call(
        _stage1_kernel,
        grid=(N2 // FFT_ROWS, C // FFT_COLS_IN),
        in_specs=[pl.BlockSpec((2 * N1, N1), lambda j, c: (0, 0)),
                  pl.BlockSpec((None, 2, h, FFT_ROWS, FFT_COLS_IN), lambda j, c: (which, 0, 0, j, c))],
        out_specs=pl.BlockSpec((2, N1, FFT_ROWS, FFT_COLS_IN), lambda j, c: (0, 0, j, c)),
        out_shape=jax.ShapeDtypeStruct((2, N1, N2, C), BF16),
        compiler_params=_params("parallel", "parallel"),
        name="fft_stage1",
    )(s, x)


FFT_K1_PER_STEP = 4


def _mid_kernel(inv_n, f2r_ref, f2i_ref, twr_ref, twi_ref, yf_ref, y_ref, u_ref):
    f2r, f2i = f2r_ref[...], f2i_ref[...]
    for k in range(y_ref.shape[1]):
        twr, twi = twr_ref[k], twi_ref[k]
        gr, gi = f2r * twr - f2i * twi, f2r * twi + f2i * twr
        grb, gib = gr.astype(BF16), gi.astype(BF16)

        def inner(ref):
            re, im = ref[0, k], ref[1, k]
            return _dot(grb, re) - _dot(gib, im), _dot(gib, re) + _dot(grb, im)

        hr, hi = inner(yf_ref)
        zr, zi = inner(y_ref)
        pr = (zr * hr - zi * hi).astype(BF16)
        pi = (zr * hi + zi * hr).astype(BF16)
        irb = (gr.T * inv_n).astype(BF16)
        iib = (gi.T * (-inv_n)).astype(BF16)
        u_ref[0, k] = (_dot(irb, pr) - _dot(iib, pi)).astype(u_ref.dtype)
        u_ref[1, k] = (_dot(iib, pr) + _dot(irb, pi)).astype(u_ref.dtype)


def fft_mid(yf, y, dc):
    _, N1, N2, C = y.shape
    kb = FFT_K1_PER_STEP
    blk = pl.BlockSpec((2, kb, N2, C), lambda k: (0, k, 0, 0))
    const = pl.BlockSpec((N2, N2), lambda k: (0, 0))
    tw = pl.BlockSpec((kb, 1, N2), lambda k: (k, 0, 0))
    return pl.pallas_call(
        partial(_mid_kernel, 1.0 / (N1 * N2)),
        grid=(N1 // kb,),
        in_specs=[const, const, tw, tw, blk, blk],
        out_specs=blk,
        out_shape=jax.ShapeDtypeStruct((2, N1, N2, C), BF16),
        compiler_params=_params("parallel"),
        name="fft_mid",
    )(dc['f2r'], dc['f2i'], dc['twr'], dc['twi'], yf, y)


def _final_kernel(t_ref, u_ref, z_ref, gate_ref, bias_ref, o_ref):
    _, n1, rows, cols = u_ref.shape
    t = t_ref[...]
    ut = pltpu.einshape("abc->bac", u_ref[...].reshape(2 * n1, rows, cols))
    convt = jnp.stack([_dot(t, ut[b]).astype(BF16) for b in range(rows)], axis=0)
    conv = pltpu.einshape("bac->abc", convt).reshape(o_ref.shape).astype(F32)
    z = z_ref[...].astype(F32)
    o_ref[...] = (gate_ref[...].astype(F32) * (conv + bias_ref[...] * z)).astype(o_ref.dtype)


def fft_final(u, t_fin, z, z_which, gate, gate_which, bias, bias_which):
    _, N1, N2, C = u.shape
    h = N1 // 2

    def half(which):
        return pl.BlockSpec((None, 2, h, FFT_ROWS, FFT_COLS), lambda j, c: (which, 0, 0, j, c))

    return pl.pallas_call(
        _final_kernel,
        grid=(N2 // FFT_ROWS, C // FFT_COLS),
        in_specs=[pl.BlockSpec((N1, 2 * N1), lambda j, c: (0, 0)),
                  pl.BlockSpec((2, N1, FFT_ROWS, FFT_COLS), lambda j, c: (0, 0, j, c)),
                  half(z_which), half(gate_which),
                  pl.BlockSpec((None, 1, FFT_COLS), lambda j, c: (bias_which, 0, c))],
        out_specs=half(0),
        out_shape=jax.ShapeDtypeStruct((1, 2, h, N2, C), BF16),
        compiler_params=_params("parallel", "parallel"),
        name="fft_final",
    )(t_fin, u, z, gate, bias.reshape(bias.shape[0], 1, C))


def hyena_branch(zs, L, p):
    _, M, C = zs.shape
    assert M == 2 * L, "the batch pair rides as real/imaginary parts"
    dc = _dft_constants(L)
    split = (2, dc['N1'] // 2, FFT_INNER, C)
    taps = hyena_filter_taps(L, *p['filt']).reshape(HYENA_ORDER, *split)
    zs = zs.reshape(3, *split)
    z, z_which = zs, 0
    for n in range(HYENA_ORDER):
        u = fft_mid(fft_stage1(taps, n, dc['s_filt']), fft_stage1(z, z_which, dc['s_data']), dc)
        z, z_which = fft_final(u, dc['t_fin'], z, z_which, zs, n + 1, p['hy_bias'], n), 0
    return z.reshape(M, C)


def _split_w_in(w):
    c_q = 3 * D_HYENA
    c_v = c_q + 2 * D_ATT
    c_g = c_v + D_ATT
    return (w[:, :c_q].astype(BF16), w[:, c_q:c_v].astype(BF16),
            jnp.concatenate([w[:, c_g:], w[:, c_v:c_g]], axis=1).astype(BF16))


def _layer(x, p):
    Bsz, L, D = x.shape
    M = Bsz * L
    x0 = x.reshape(M, D)
    x1, u = ffn_block(x0, p['ffn1_pre_g'], p['ffn1_post_g'], p['ffn1_w_gate'], p['ffn1_w_up'], p['ffn1_w_down'],
                      next_g=p['mix_pre_g'])
    w_hy, w_qk, w_gv = p['w_in']
    hy_in = hyena_inputs(u, w_hy, p['hy_conv_w'], p['hy_conv_b'], L)
    qk = projection(u, w_qk, BF16, tm=1024, tn=D_ATT, rope_len=L, name="inproj_qk")
    gv = projection(u, w_gv, BF16, tm=1024, tn=w_gv.shape[1] // 4, name="inproj_gates_v")
    a2 = hyena_branch(hy_in, L, p)
    GW = HEADS_PER_GROUP * HEAD_DIM
    v_col0 = N_BRANCH * D_MODEL // GW
    att = []
    for g in range(N_GROUPS):
        srcs = ((qk, g), (qk, N_GROUPS + g), (gv, v_col0 + g))
        fn = dilated_attention_group if ATT_GROUPS[g][1] == 1 else dilated_attention_slabs
        att.append(fn(srcs, Bsz, L, g))
    x2, xn2 = merge_out(x1, a2, [o for o, _ in att], [l for _, l in att], gv, p['mix_post_g'], p['ffn2_pre_g'],
                        p['w_hy_proj'], p['w_att_proj'], p['w_out'])
    x3 = ffn_block(x2, p['ffn2_pre_g'], p['ffn2_post_g'], p['ffn2_w_gate'], p['ffn2_w_up'], p['ffn2_w_down'],
                   xn=xn2)
    return x3.reshape(Bsz, L, D)


def kernel(x_prompt, x_sample, ffn1_pre_g, ffn1_post_g, ffn1_w_gate, ffn1_w_up, ffn1_w_down, mix_pre_g, mix_post_g, w_in, hy_conv_w, hy_conv_b, filt_w1, filt_b1, filt_w2, filt_b2, filt_w3, filt_b3, filt_w4, filt_freq, hy_bias, w_hy_proj, w_att_proj, w_out, ffn2_pre_g, ffn2_post_g, ffn2_w_gate, ffn2_w_up, ffn2_w_down):
    assert ffn1_w_gate.shape[0] == 1
    p = {
        'ffn1_pre_g': ffn1_pre_g, 'ffn1_post_g': ffn1_post_g,
        'ffn1_w_gate': ffn1_w_gate[0].astype(BF16), 'ffn1_w_up': ffn1_w_up[0].astype(BF16),
        'ffn1_w_down': ffn1_w_down[0].astype(BF16),
        'mix_pre_g': mix_pre_g, 'mix_post_g': mix_post_g,
        'w_in': _split_w_in(w_in[0]),
        'hy_conv_w': hy_conv_w[0], 'hy_conv_b': hy_conv_b,
        'filt': (filt_w1[0], filt_b1[0], filt_w2[0], filt_b2[0], filt_w3[0], filt_b3[0], filt_w4[0], filt_freq[0]),
        'hy_bias': hy_bias[0],
        'w_hy_proj': w_hy_proj[0].astype(BF16), 'w_att_proj': w_att_proj[0].astype(BF16),
        'w_out': w_out[0].astype(BF16),
        'ffn2_pre_g': ffn2_pre_g, 'ffn2_post_g': ffn2_post_g,
        'ffn2_w_gate': ffn2_w_gate[0].astype(BF16), 'ffn2_w_up': ffn2_w_up[0].astype(BF16),
        'ffn2_w_down': ffn2_w_down[0].astype(BF16),
    }
    return (_layer(x_prompt, p), _layer(x_sample, p))
```

```python
import math
from functools import partial

import numpy as np
import jax
import jax.numpy as jnp
from jax import lax
from jax.experimental import pallas as pl
from jax.experimental.pallas import tpu as pltpu

D_MODEL = 2048
D_HYENA = 1024
HYENA_ORDER = 2
SHORT_CONV = 3
FILTER_EMB = 33
FILTER_BANDS = (FILTER_EMB - 1) // 2
FAST_DECAY_PCT = 0.3
SLOW_DECAY_PCT = 1.5
DECAY_TARGET = 1e-2
HEAD_DIM = 128
HEADS_PER_GROUP = 4
ATT_GROUPS = ((128, 1), (512, 4), (2048, 16))
N_GROUPS = len(ATT_GROUPS)
D_ATT = N_GROUPS * HEADS_PER_GROUP * HEAD_DIM
D_ATT_OUT = HEADS_PER_GROUP * HEAD_DIM
ROPE_DIM = HEAD_DIM // 4
ROPE_THETA = 500000.0
N_BRANCH = 2
D_IN_PROJ = 3 * D_HYENA + 3 * D_ATT + N_BRANCH * D_MODEL
D_FF = 5632
EPS = 1e-6
NEG_INF = -1e30

LANES = 128
MXU_WIDTH = 256
FFT_INNER = MXU_WIDTH
VMEM_BYTES_V7X = 64 * 1024 * 1024
VMEM_LIMIT_BYTES = VMEM_BYTES_V7X - 4 * 1024 * 1024
BF16 = jnp.bfloat16
F32 = jnp.float32
HIGHEST = lax.Precision.HIGHEST


def _rms(x, g):
    return x * lax.rsqrt(jnp.mean(x * x, axis=-1, keepdims=True) + EPS) * g


def _dot(a, b):
    return jnp.dot(a, b, preferred_element_type=F32)


def _params(*sem):
    return pltpu.CompilerParams(dimension_semantics=sem, vmem_limit_bytes=VMEM_LIMIT_BYTES)


FFN_GROUP = MXU_WIDTH


def ffn_gate_up_weights(wg, wu):
    D, FF = wg.shape
    both = jnp.stack([wg.reshape(D, FF // FFN_GROUP, FFN_GROUP), wu.reshape(D, FF // FFN_GROUP, FFN_GROUP)], axis=2)
    return both.reshape(D, 2 * FF).astype(BF16)


def _ffn_kernel(norm_in, norm_out, x_ref, xn_src_ref, post_g_ref, next_g_ref, wgu_ref, wd_ref, *rest):
    o_ref = rest[0]
    u_ref = rest[1] if norm_out else None
    xn_ref = rest[-1] if norm_in else xn_src_ref
    j = pl.program_id(1)
    tf = wd_ref.shape[0]

    @pl.when(j == 0)
    def _():
        if norm_in:
            xn_ref[...] = _rms(x_ref[...], xn_src_ref[...]).astype(BF16)
        o_ref[...] = jnp.zeros_like(o_ref)

    xn = xn_ref[...]
    hs = []
    for c in range(tf // FFN_GROUP):
        gu = _dot(xn, wgu_ref[:, 2 * c * FFN_GROUP:2 * (c + 1) * FFN_GROUP])
        gate, up = gu[:, :FFN_GROUP], gu[:, FFN_GROUP:]
        hs.append((gate * jax.nn.sigmoid(gate) * up).astype(BF16))
    o_ref[...] += _dot(jnp.concatenate(hs, axis=1), wd_ref[...])

    @pl.when(j == pl.num_programs(1) - 1)
    def _():
        y = x_ref[...] + 0.5 * _rms(o_ref[...], post_g_ref[...])
        o_ref[...] = y
        if norm_out:
            u_ref[...] = _rms(y, next_g_ref[...]).astype(u_ref.dtype)


def ffn_block(x, pre_g, post_g, wgu, wd, *, xn=None, next_g=None, tm=512, tf=512):
    M, D = x.shape
    FF = wd.shape[0]
    norm_in, norm_out = xn is None, next_g is not None
    rows = pl.BlockSpec((tm, D), lambda i, j: (i, 0))
    gain = pl.BlockSpec((1, D), lambda i, j: (0, 0))
    out = pl.pallas_call(
        partial(_ffn_kernel, norm_in, norm_out),
        grid=(M // tm, FF // tf),
        in_specs=[
            rows,
            gain if norm_in else rows,
            gain, gain,
            pl.BlockSpec((D, 2 * tf), lambda i, j: (0, j)),
            pl.BlockSpec((tf, D), lambda i, j: (j, 0)),
        ],
        out_specs=[rows, rows] if norm_out else [rows],
        out_shape=[jax.ShapeDtypeStruct((M, D), F32)] + ([jax.ShapeDtypeStruct((M, D), BF16)] if norm_out else []),
        scratch_shapes=[pltpu.VMEM((tm, D), BF16)] if norm_in else [],
        compiler_params=_params("parallel", "arbitrary"),
        name="ffn_block",
    )(x, pre_g if norm_in else xn, post_g, next_g if norm_out else post_g, wgu, wd)
    return out if norm_out else out[0]


def rope_tables(L):
    half = ROPE_DIM // 2
    inv_freq = jnp.power(ROPE_THETA, -jnp.arange(half, dtype=F32) / half)
    ang = jnp.arange(L, dtype=F32)[:, None] * inv_freq[None, :]
    cos, sin = jnp.cos(ang), jnp.sin(ang)
    rest = HEAD_DIM - ROPE_DIM
    c = jnp.concatenate([cos, cos, jnp.ones((L, rest), F32)], axis=1)
    s_lo = jnp.concatenate([-sin, jnp.zeros((L, half + rest), F32)], axis=1)
    s_hi = jnp.concatenate([jnp.zeros((L, half), F32), sin, jnp.zeros((L, rest), F32)], axis=1)
    scale = HEAD_DIM ** -0.5
    return jnp.stack([c * scale, c]), jnp.stack([s_lo * scale, s_lo]), jnp.stack([s_hi * scale, s_hi])


def _proj_kernel(rotate, xn_ref, w_ref, *rest):
    o_ref = rest[-1]
    r = _dot(xn_ref[...], w_ref[...])
    if rotate:
        c_ref, slo_ref, shi_ref = rest[:3]
        tn = r.shape[1]
        half = ROPE_DIM // 2
        wide = lambda t_ref: jnp.concatenate([t_ref[...]] * (tn // HEAD_DIM), axis=1)
        r = r * wide(c_ref) + pltpu.roll(r, tn - half, 1) * wide(slo_ref) + pltpu.roll(r, half, 1) * wide(shi_ref)
    o_ref[...] = r.astype(o_ref.dtype)


def projection(xn, w, out_dtype, *, tm, tn, rope_len=None, name):
    M, D = xn.shape
    N = w.shape[1]
    in_specs = [
        pl.BlockSpec((tm, D), lambda i, j: (i, 0)),
        pl.BlockSpec((D, tn), lambda i, j: (0, j)),
    ]
    args = [xn, w]
    if rope_len is not None:
        assert tn == D_ATT and N == 2 * D_ATT and rope_len % tm == 0
        spec = pl.BlockSpec((None, tm, HEAD_DIM), lambda i, j: (j, i % (rope_len // tm), 0))
        in_specs += [spec, spec, spec]
        args += list(rope_tables(rope_len))
    return pl.pallas_call(
        partial(_proj_kernel, rope_len is not None),
        grid=(M // tm, N // tn),
        in_specs=in_specs,
        out_specs=pl.BlockSpec((tm, tn), lambda i, j: (i, j)),
        out_shape=jax.ShapeDtypeStruct((M, N), out_dtype),
        compiler_params=_params("parallel", "arbitrary"),
        name=name,
    )(*args)


ATT_RADIUS = 64
assert all(w // (2 * d) == ATT_RADIUS for w, d in ATT_GROUPS)


def _attn_kernel(T, Ls, q_ref, kp_ref, kc_ref, kn_ref, vp_ref, vc_ref, vn_ref, o_ref, lse_ref, kbuf, vbuf):
    t = pl.program_id(2)
    R = ATT_RADIUS
    SB = 2 * R
    for buf, prv, cur, nxt in ((kbuf, kp_ref, kc_ref, kn_ref), (vbuf, vp_ref, vc_ref, vn_ref)):
        buf[0:R] = prv[...]
        buf[R:R + T] = cur[...]
        buf[R + T:R + T + R] = nxt[...]
    qi = lax.broadcasted_iota(jnp.int32, (SB, 2 * SB), 0)
    kk = lax.broadcasted_iota(jnp.int32, (SB, 2 * SB), 1)
    band = (kk >= qi) & (kk <= qi + 2 * R)
    lane = lax.broadcasted_iota(jnp.int32, (SB, LANES), 1)
    for sb in range(T // SB):
        kpos = t * T + (sb * SB - R) + kk
        mask = band & (kpos >= 0) & (kpos < Ls)
        lse = jnp.zeros((SB, LANES), F32)
        for h in range(HEADS_PER_GROUP):
            cols = slice(h * HEAD_DIM, (h + 1) * HEAD_DIM)
            q = q_ref[sb * SB:(sb + 1) * SB, cols]
            k = kbuf[sb * SB:(sb + 2) * SB, cols]
            v = vbuf[sb * SB:(sb + 2) * SB, cols]
            s = lax.dot_general(q, k, (((1,), (1,)), ((), ())), preferred_element_type=F32)
            s = jnp.where(mask, s, NEG_INF)
            m = jnp.max(s, axis=1, keepdims=True)
            p = jnp.exp(s - m)
            l = jnp.sum(p, axis=1, keepdims=True)
            o = _dot(p.astype(BF16), v) / l
            o_ref[sb * SB:(sb + 1) * SB, cols] = o.astype(o_ref.dtype)
            lse = jnp.where(lane == h, m + jnp.log(l), lse)
        lse_ref[sb * SB:(sb + 1) * SB, :] = lse


SLAB = 16


def _softmax_heads(q, k, v, mask):
    lane = lax.broadcasted_iota(jnp.int32, (q.shape[0], LANES), 1)
    lse = jnp.zeros((q.shape[0], LANES), F32)
    outs = []
    for h in range(HEADS_PER_GROUP):
        cols = slice(h * HEAD_DIM, (h + 1) * HEAD_DIM)
        s = lax.dot_general(q[:, cols], k[:, cols], (((1,), (1,)), ((), ())), preferred_element_type=F32)
        s = jnp.where(mask, s, NEG_INF)
        m = jnp.max(s, axis=1, keepdims=True)
        p = jnp.exp(s - m)
        l = jnp.sum(p, axis=1, keepdims=True)
        outs.append(_dot(p.astype(BF16), v[:, cols]) / l)
        lse = jnp.where(lane == h, m + jnp.log(l), lse)
    return jnp.concatenate(outs, axis=1), lse


def _attn_slab_kernel(d, NS, Ls, q_ref, kp_ref, kc_ref, kn_ref, vp_ref, vc_ref, vn_ref, o_ref, lse_ref,
                      qs, ks, vs, os_, ls):
    t = pl.program_id(1)
    R = ATT_RADIUS
    SB = 2 * R
    J = SLAB // d
    NH = kp_ref.shape[0]
    QA = SB // J
    KA = QA + 2 * NH
    n_sb = NS // QA
    to_class_major = lambda ref: pltpu.einshape("abc->bac", ref[...])
    qs[...] = to_class_major(q_ref)
    for buf, prv, cur, nxt in ((ks, kp_ref, kc_ref, kn_ref), (vs, vp_ref, vc_ref, vn_ref)):
        buf[:, 0:NH] = to_class_major(prv)
        buf[:, NH:NH + NS] = to_class_major(cur)
        buf[:, NH + NS:NH + NS + NH] = to_class_major(nxt)
    qi = lax.broadcasted_iota(jnp.int32, (SB, 2 * SB), 0)
    kk = lax.broadcasted_iota(jnp.int32, (SB, 2 * SB), 1)
    sq_rel = J * (qi % QA) + qi // QA
    sk_rel = J * (kk % KA - NH) + kk // KA
    band = jnp.abs(sk_rel - sq_rel) <= R

    def body(it, carry):
        r = it // n_sb
        a0 = pl.multiple_of((it % n_sb) * QA, QA)
        sk = J * (t * NS + a0) + sk_rel
        mask = band & (sk >= 0) & (sk < Ls)
        gather = lambda buf, n: jnp.concatenate([buf[j * d + r, pl.ds(a0, n), :] for j in range(J)], axis=0)
        o, lse = _softmax_heads(gather(qs, QA), gather(ks, KA), gather(vs, KA), mask)
        o = o.astype(os_.dtype)
        for j in range(J):
            os_[j * d + r, pl.ds(a0, QA), :] = o[j * QA:(j + 1) * QA]
            ls[j * d + r, pl.ds(a0, QA), :] = lse[j * QA:(j + 1) * QA]
        return carry

    lax.fori_loop(0, d * n_sb, body, 0, unroll=4)
    o_ref[...] = pltpu.einshape("bac->abc", os_[...])
    lse_ref[...] = pltpu.einshape("bac->abc", ls[...])


def dilated_attention_slabs(srcs, Bsz, L, g, *, NS=128):
    d = ATT_GROUPS[g][1]
    M = Bsz * L
    R = ATT_RADIUS
    GW = HEADS_PER_GROUP * HEAD_DIM
    NH = R * d // SLAB
    TB = NS * SLAB
    assert SLAB % d == 0 and L % TB == 0 and NS % NH == 0 and (2 * R) % (SLAB // d) == 0
    nmb, nhb, hpm = L // TB, L // (NH * SLAB), NS // NH
    views = [a.reshape(M // SLAB, SLAB, a.shape[1]) for a, _ in srcs]
    col = lambda which: srcs[which][1]

    def main(which):
        return pl.BlockSpec((NS, SLAB, GW), lambda b, t: (b * nmb + t, 0, col(which)))

    def prev(which):
        return pl.BlockSpec((NH, SLAB, GW), lambda b, t: (jnp.maximum(b * nhb + t * hpm - 1, b * nhb), 0, col(which)))

    def nxt(which):
        return pl.BlockSpec((NH, SLAB, GW),
                            lambda b, t: (jnp.minimum(b * nhb + (t + 1) * hpm, (b + 1) * nhb - 1), 0, col(which)))

    o, lse = pl.pallas_call(
        partial(_attn_slab_kernel, d, NS, L // d),
        grid=(Bsz, nmb),
        in_specs=[main(0), prev(1), main(1), nxt(1), prev(2), main(2), nxt(2)],
        out_specs=[pl.BlockSpec((NS, SLAB, GW), lambda b, t: (b * nmb + t, 0, 0)),
                   pl.BlockSpec((NS, SLAB, LANES), lambda b, t: (b * nmb + t, 0, 0))],
        out_shape=[jax.ShapeDtypeStruct((M // SLAB, SLAB, GW), BF16),
                   jax.ShapeDtypeStruct((M // SLAB, SLAB, LANES), F32)],
        scratch_shapes=[pltpu.VMEM((SLAB, NS, GW), BF16),
                        pltpu.VMEM((SLAB, NS + 2 * NH, GW), BF16), pltpu.VMEM((SLAB, NS + 2 * NH, GW), BF16),
                        pltpu.VMEM((SLAB, NS, GW), BF16), pltpu.VMEM((SLAB, NS, LANES), F32)],
        compiler_params=_params("parallel", "arbitrary"),
        name=f"dilated_attention_g{g}",
    )(views[0], views[1], views[1], views[1], views[2], views[2], views[2])
    return o.reshape(M, GW), lse.reshape(M, LANES)


def dilated_attention_group(srcs, Bsz, L, g, *, T=512):
    d = ATT_GROUPS[g][1]
    M = Bsz * L
    Ls = L // d
    R = ATT_RADIUS
    GW = HEADS_PER_GROUP * HEAD_DIM
    assert d == 1 and Ls % T == 0 and T % (2 * R) == 0
    nrb, nhb, hpt = Ls // T, Ls // R, T // R

    def main(which):
        return pl.BlockSpec((T, GW), lambda b, r, t: (b * nrb + t, srcs[which][1]))

    def prev(which):
        return pl.BlockSpec((R, GW), lambda b, r, t: (jnp.maximum(b * nhb + t * hpt - 1, b * nhb), srcs[which][1]))

    def nxt(which):
        return pl.BlockSpec((R, GW), lambda b, r, t: (jnp.minimum(b * nhb + (t + 1) * hpt, (b + 1) * nhb - 1),
                                                      srcs[which][1]))

    o, lse = pl.pallas_call(
        partial(_attn_kernel, T, Ls),
        grid=(Bsz, d, nrb),
        in_specs=[main(0), prev(1), main(1), nxt(1), prev(2), main(2), nxt(2)],
        out_specs=[pl.BlockSpec((T, GW), lambda b, r, t: (b * nrb + t, r)),
                   pl.BlockSpec((T, LANES), lambda b, r, t: (b * nrb + t, r))],
        out_shape=[jax.ShapeDtypeStruct((M // d, d * GW), BF16),
                   jax.ShapeDtypeStruct((M // d, d * LANES), F32)],
        scratch_shapes=[pltpu.VMEM((T + 2 * R, GW), BF16), pltpu.VMEM((T + 2 * R, GW), BF16)],
        compiler_params=_params("parallel", "parallel", "arbitrary"),
        name=f"dilated_attention_g{g}",
    )(srcs[0][0], srcs[1][0], srcs[1][0], srcs[1][0], srcs[2][0], srcs[2][0], srcs[2][0])
    return o.reshape(M, GW), lse.reshape(M, LANES)


def _merge_kernel(x_ref, a_ref, o0_ref, o1_ref, o2_ref, l0_ref, l1_ref, l2_ref, ga_ref, gb_ref,
                  post_g_ref, next_g_ref, whp_ref, wap_ref, wo_ref, o_ref, u_ref):
    lses = [l0_ref[...], l1_ref[...], l2_ref[...]]
    mx = jnp.maximum(jnp.maximum(lses[0], lses[1]), lses[2])
    es = [jnp.exp(l - mx) for l in lses]
    den = es[0] + es[1] + es[2]
    wts = [e / den for e in es]
    outs = [o0_ref, o1_ref, o2_ref]
    heads = []
    for h in range(HEADS_PER_GROUP):
        cols = slice(h * HEAD_DIM, (h + 1) * HEAD_DIM)
        heads.append(sum(wts[g][:, h:h + 1] * outs[g][:, cols].astype(F32) for g in range(N_GROUPS)))
    att = jnp.concatenate(heads, axis=1).astype(BF16)
    a = _dot(a_ref[...], whp_ref[...])
    b = _dot(att, wap_ref[...])
    gate_a, gate_b = ga_ref[...].astype(F32), gb_ref[...].astype(F32)
    merged = (jax.nn.sigmoid(gate_a) * a + jax.nn.sigmoid(gate_b) * b).astype(BF16)
    mix = _dot(merged, wo_ref[...])
    y = x_ref[...] + _rms(mix, post_g_ref[...])
    o_ref[...] = y
    u_ref[...] = _rms(y, next_g_ref[...]).astype(u_ref.dtype)


def merge_out(x, a_in, att_outs, att_lses, gates, post_g, next_g, whp, wap, wo, *, tm=512):
    M, D = x.shape
    const = lambda i: (0, 0)
    rows = lambda a: pl.BlockSpec((tm, a.shape[1]), lambda i: (i, 0))
    return pl.pallas_call(
        _merge_kernel,
        grid=(M // tm,),
        in_specs=[rows(x), rows(a_in)] + [rows(o) for o in att_outs] + [rows(l) for l in att_lses] + [
            pl.BlockSpec((tm, D), lambda i: (i, 0)),
            pl.BlockSpec((tm, D), lambda i: (i, 1)),
            pl.BlockSpec((1, D), const),
            pl.BlockSpec((1, D), const),
            pl.BlockSpec(whp.shape, const, pipeline_mode=pl.Buffered(1)),
            pl.BlockSpec(wap.shape, const, pipeline_mode=pl.Buffered(1)),
            pl.BlockSpec(wo.shape, const, pipeline_mode=pl.Buffered(1)),
        ],
        out_specs=[pl.BlockSpec((tm, D), lambda i: (i, 0)), pl.BlockSpec((tm, D), lambda i: (i, 0))],
        out_shape=[jax.ShapeDtypeStruct((M, D), F32), jax.ShapeDtypeStruct((M, D), BF16)],
        compiler_params=_params("parallel"),
        name="merge_out",
    )(x, a_in, *att_outs, *att_lses, gates, gates, post_g, next_g, whp, wap, wo)


def _dft_constants(L):
    N = 2 * L
    N2 = FFT_INNER
    N1 = N // N2
    h = N1 // 2
    idx1 = np.arange(N1)
    ang1 = -2.0 * np.pi * ((idx1[:, None] * idx1[None, :]) % N1) / N1
    f1r, f1i = np.cos(ang1), np.sin(ang1)
    s_data = np.block([[f1r[:, :h], -f1i[:, :h]], [f1i[:, :h], f1r[:, :h]]])
    s_filt = np.concatenate([f1r, f1i], axis=0)
    ar, ai = f1r[:h, :], -f1i[:h, :]
    t_fin = np.block([[ar, -ai], [ai, ar]])
    idx2 = np.arange(N2)
    ang2 = -2.0 * np.pi * ((idx2[:, None] * idx2[None, :]) % N2) / N2
    angt = -2.0 * np.pi * (idx1[:, None] * idx2[None, :]) / N
    return dict(
        N1=N1,
        s_data=jnp.asarray(s_data, BF16), s_filt=jnp.asarray(s_filt, BF16), t_fin=jnp.asarray(t_fin, BF16),
        f2r=jnp.asarray(np.cos(ang2), F32), f2i=jnp.asarray(np.sin(ang2), F32),
        twr=jnp.asarray(np.cos(angt).reshape(N1, 1, N2), F32),
        twi=jnp.asarray(np.sin(angt).reshape(N1, 1, N2), F32),
    )


def _hyena_in_kernel(tm, L, x_ref, prev_ref, next_ref, w_ref, cw_ref, cb_ref, o_ref, xn_ref):
    i = pl.program_id(0)
    H = prev_ref.shape[0]

    @pl.when(pl.program_id(1) == 0)
    def _():
        pos0 = (i * tm) % L
        zero = jnp.zeros_like(prev_ref)
        xn_ref[0:H] = jnp.where(pos0 == 0, zero, prev_ref[...])
        xn_ref[H:H + tm] = x_ref[...]
        xn_ref[H + tm:H + tm + H] = jnp.where(pos0 + tm == L, zero, next_ref[...])

    r = _dot(xn_ref[...], w_ref[...])
    n = r.shape[0]
    y = pltpu.roll(r, 1, 0) * cw_ref[0:1, :] + r * cw_ref[1:2, :] + pltpu.roll(r, n - 1, 0) * cw_ref[2:3, :]
    o_ref[0] = (y[H:H + tm] + cb_ref[...]).astype(o_ref.dtype)


def hyena_inputs(x, w, conv_w, conv_b, L, *, tm=1024):
    M, D = x.shape
    C = w.shape[1] // 3
    H = SLAB
    nh = M // H
    assert L % tm == 0 and tm % H == 0
    return pl.pallas_call(
        partial(_hyena_in_kernel, tm, L),
        grid=(M // tm, 3),
        in_specs=[
            pl.BlockSpec((tm, D), lambda i, j: (i, 0)),
            pl.BlockSpec((H, D), lambda i, j: (jnp.maximum(i * (tm // H) - 1, 0), 0)),
            pl.BlockSpec((H, D), lambda i, j: (jnp.minimum((i + 1) * (tm // H), nh - 1), 0)),
            pl.BlockSpec((D, C), lambda i, j: (0, j)),
            pl.BlockSpec((SHORT_CONV, C), lambda i, j: (0, j)),
            pl.BlockSpec((1, C), lambda i, j: (0, j)),
        ],
        out_specs=pl.BlockSpec((1, tm, C), lambda i, j: (j, i, 0)),
        out_shape=jax.ShapeDtypeStruct((3, M, C), BF16),
        scratch_shapes=[pltpu.VMEM((tm + 2 * H, D), BF16)],
        compiler_params=_params("parallel", "arbitrary"),
        name="hyena_inputs",
    )(x, x, x, w, conv_w, conv_b)


def _filter_kernel(L, tl, w1_ref, b1_ref, w2_ref, b2_ref, w3_ref, b3_ref, w4hi_ref, w4lo_ref,
                   freq_ref, fvec_ref, delta_ref, kf_ref):
    i = pl.program_id(0)
    HALF = LANES // 2
    wide = delta_ref.shape[1]
    C = wide // (2 * HYENA_ORDER)
    row = i * tl + lax.broadcasted_iota(jnp.int32, (tl, LANES), 0)
    lane = lax.broadcasted_iota(jnp.int32, (tl, LANES), 1)
    posf = jnp.where(lane < HALF, row, L - row).astype(F32)
    t = posf / (L - 1)
    a = fvec_ref[...] * ((2.0 * math.pi / L) * posf)
    lh = lane % HALF
    feats = jnp.where(lh < FILTER_BANDS, jnp.cos(a),
                      jnp.where(lh < 2 * FILTER_BANDS, -jnp.sin(a),
                                jnp.where(lh == 2 * FILTER_BANDS, t, 0.0)))
    freq = freq_ref[...]
    dense = lambda v, w_ref, b_ref: jnp.sin(freq * (
        jnp.dot(v, w_ref[...], precision=HIGHEST, preferred_element_type=F32) + b_ref[...]))
    h = dense(dense(dense(feats, w1_ref, b1_ref), w2_ref, b2_ref), w3_ref, b3_ref)
    h_hi = h.astype(BF16)
    h_lo = (h - h_hi.astype(F32)).astype(BF16)
    taps = _dot(h_hi, w4hi_ref[...]) + _dot(h_lo, w4hi_ref[...]) + _dot(h_hi, w4lo_ref[...])
    row_w = i * tl + lax.broadcasted_iota(jnp.int32, (tl, wide), 0)
    col_w = lax.broadcasted_iota(jnp.int32, (tl, wide), 1)
    backward = col_w >= wide // 2
    t_w = jnp.where(backward, L - row_w, row_w).astype(F32) / (L - 1)
    taps = taps * jnp.exp(-t_w * delta_ref[...])
    taps = jnp.where(backward & (row_w == 0), 0.0, taps)
    for n in range(HYENA_ORDER):
        for direction in range(2):
            c0 = (direction * HYENA_ORDER + n) * C
            kf_ref[n, direction] = taps[:, c0:c0 + C].astype(kf_ref.dtype)


def hyena_filter_taps(L, fw1, fb1, fw2, fb2, fw3, fb3, fw4, ffreq, *, tl=512):
    C = D_HYENA
    H = fw2.shape[0]
    HALF = LANES // 2
    assert H <= HALF and 2 * FILTER_BANDS + 1 <= HALF

    def both(a, rows):
        blk = jnp.zeros((HALF if rows else 1, HALF), F32).at[:a.shape[0], :a.shape[1]].set(a)
        if not rows:
            return jnp.concatenate([blk, blk], axis=1)
        zero = jnp.zeros_like(blk)
        return jnp.concatenate([jnp.concatenate([blk, zero], axis=1), jnp.concatenate([zero, blk], axis=1)], axis=0)

    w1 = both(jnp.concatenate([fw1[1:], fw1[:1]], axis=0), True)
    w2, w3 = both(fw2, True), both(fw3, True)
    b1, b2, b3, freq = (both(v[None, :], False) for v in (fb1, fb2, fb3, ffreq))
    w4 = fw4.reshape(H, HYENA_ORDER, 2, C)
    wide = 2 * HYENA_ORDER * C
    w4p = jnp.zeros((LANES, wide), F32)
    w4p = w4p.at[:H, :wide // 2].set(w4[:, :, 0].reshape(H, HYENA_ORDER * C))
    w4p = w4p.at[HALF:HALF + H, wide // 2:].set(w4[:, :, 1].reshape(H, HYENA_ORDER * C))
    w4hi = w4p.astype(BF16)
    w4lo = (w4p - w4hi.astype(F32)).astype(BF16)
    bands = np.linspace(1e-4, FILTER_BANDS - 1, FILTER_BANDS, dtype=np.float32)
    fvec = np.zeros((1, LANES), np.float32)
    for base in (0, HALF):
        fvec[0, base:base + FILTER_BANDS] = bands
        fvec[0, base + FILTER_BANDS:base + 2 * FILTER_BANDS] = bands
    max_decay = math.log(DECAY_TARGET) / FAST_DECAY_PCT
    min_decay = math.log(DECAY_TARGET) / SLOW_DECAY_PCT
    deltas = np.abs(np.linspace(min_decay, max_decay, C, dtype=np.float32))
    delta_w = np.tile(deltas[None, :], (1, 2 * HYENA_ORDER))
    const = lambda i: (0, 0)
    args = (w1, b1, w2, b2, w3, b3, w4hi, w4lo, freq, jnp.asarray(fvec), jnp.asarray(delta_w))
    return pl.pallas_call(
        partial(_filter_kernel, L, tl),
        grid=(L // tl,),
        in_specs=[pl.BlockSpec(a.shape, const) for a in args],
        out_specs=pl.BlockSpec((HYENA_ORDER, 2, tl, C), lambda i: (0, 0, i, 0)),
        out_shape=jax.ShapeDtypeStruct((HYENA_ORDER, 2, L, C), BF16),
        compiler_params=_params("parallel"),
        name="hyena_filter_taps",
    )(*args)


FFT_ROWS = 16
FFT_COLS = 512
FFT_COLS_IN = 1024


def _stage1_kernel(s_ref, x_ref, y_ref):
    _, h, rows, cols = x_ref.shape
    s = s_ref[...]
    xt = pltpu.einshape("abc->bac", x_ref[...].reshape(2 * h, rows, cols))
    yt = jnp.stack([_dot(s, xt[b]).astype(y_ref.dtype) for b in range(rows)], axis=0)
    y_ref[...] = pltpu.einshape("bac->abc", yt).reshape(y_ref.shape)


def fft_stage1(x, which, s):
    _, _, h, N2, C = x.shape
    N1 = 2 * h
    return pl.pallas_call(
        _stage1_kernel,
        grid=(N2 // FFT_ROWS, C // FFT_COLS_IN),
        in_specs=[pl.BlockSpec((2 * N1, N1), lambda j, c: (0, 0)),
                  pl.BlockSpec((None, 2, h, FFT_ROWS, FFT_COLS_IN), lambda j, c: (which, 0, 0, j, c))],
        out_specs=pl.BlockSpec((2, N1, FFT_ROWS, FFT_COLS_IN), lambda j, c: (0, 0, j, c)),
        out_shape=jax.ShapeDtypeStruct((2, N1, N2, C), BF16),
        compiler_params=_params("parallel", "parallel"),
        name="fft_stage1",
    )(s, x)


FFT_K1_PER_STEP = 4


def _mid_kernel(inv_n, f2r_ref, f2i_ref, twr_ref, twi_ref, yf_ref, y_ref, u_ref):
    f2r, f2i = f2r_ref[...], f2i_ref[...]
    for k in range(y_ref.shape[1]):
        twr, twi = twr_ref[k], twi_ref[k]
        gr, gi = f2r * twr - f2i * twi, f2r * twi + f2i * twr
        grb, gib = gr.astype(BF16), gi.astype(BF16)

        def inner(ref):
            re, im = ref[0, k], ref[1, k]
            return _dot(grb, re) - _dot(gib, im), _dot(gib, re) + _dot(grb, im)

        hr, hi = inner(yf_ref)
        zr, zi = inner(y_ref)
        pr = (zr * hr - zi * hi).astype(BF16)
        pi = (zr * hi + zi * hr).astype(BF16)
        irb = (gr.T * inv_n).astype(BF16)
        iib = (gi.T * (-inv_n)).astype(BF16)
        u_ref[0, k] = (_dot(irb, pr) - _dot(iib, pi)).astype(u_ref.dtype)
        u_ref[1, k] = (_dot(iib, pr) + _dot(irb, pi)).astype(u_ref.dtype)


def fft_mid(yf, y, dc):
    _, N1, N2, C = y.shape
    kb = FFT_K1_PER_STEP
    blk = pl.BlockSpec((2, kb, N2, C), lambda k: (0, k, 0, 0))
    const = pl.BlockSpec((N2, N2), lambda k: (0, 0))
    tw = pl.BlockSpec((kb, 1, N2), lambda k: (k, 0, 0))
    return pl.pallas_call(
        partial(_mid_kernel, 1.0 / (N1 * N2)),
        grid=(N1 // kb,),
        in_specs=[const, const, tw, tw, blk, blk],
        out_specs=blk,
        out_shape=jax.ShapeDtypeStruct((2, N1, N2, C), BF16),
        compiler_params=_params("parallel"),
        name="fft_mid",
    )(dc['f2r'], dc['f2i'], dc['twr'], dc['twi'], yf, y)


def _final_kernel(t_ref, u_ref, z_ref, gate_ref, bias_ref, o_ref):
    _, n1, rows, cols = u_ref.shape
    t = t_ref[...]
    ut = pltpu.einshape("abc->bac", u_ref[...].reshape(2 * n1, rows, cols))
    convt = jnp.stack([_dot(t, ut[b]).astype(BF16) for b in range(rows)], axis=0)
    conv = pltpu.einshape("bac->abc", convt).reshape(o_ref.shape).astype(F32)
    z = z_ref[...].astype(F32)
    o_ref[...] = (gate_ref[...].astype(F32) * (conv + bias_ref[...] * z)).astype(o_ref.dtype)


def fft_final(u, t_fin, z, z_which, gate, gate_which, bias, bias_which):
    _, N1, N2, C = u.shape
    h = N1 // 2

    def half(which):
        return pl.BlockSpec((None, 2, h, FFT_ROWS, FFT_COLS), lambda j, c: (which, 0, 0, j, c))

    return pl.pallas_call(
        _final_kernel,
        grid=(N2 // FFT_ROWS, C // FFT_COLS),
        in_specs=[pl.BlockSpec((N1, 2 * N1), lambda j, c: (0, 0)),
                  pl.BlockSpec((2, N1, FFT_ROWS, FFT_COLS), lambda j, c: (0, 0, j, c)),
                  half(z_which), half(gate_which),
                  pl.BlockSpec((None, 1, FFT_COLS), lambda j, c: (bias_which, 0, c))],
        out_specs=half(0),
        out_shape=jax.ShapeDtypeStruct((1, 2, h, N2, C), BF16),
        compiler_params=_params("parallel", "parallel"),
        name="fft_final",
    )(t_fin, u, z, gate, bias.reshape(bias.shape[0], 1, C))


def hyena_branch(zs, L, p):
    _, M, C = zs.shape
    assert M == 2 * L, "the batch pair rides as real/imaginary parts"
    dc = _dft_constants(L)
    split = (2, dc['N1'] // 2, FFT_INNER, C)
    taps = hyena_filter_taps(L, *p['filt']).reshape(HYENA_ORDER, *split)
    zs = zs.reshape(3, *split)
    z, z_which = zs, 0
    for n in range(HYENA_ORDER):
        u = fft_mid(fft_stage1(taps, n, dc['s_filt']), fft_stage1(z, z_which, dc['s_data']), dc)
        z, z_which = fft_final(u, dc['t_fin'], z, z_which, zs, n + 1, p['hy_bias'], n), 0
    return z.reshape(M, C)


def _split_w_in(w):
    c_q = 3 * D_HYENA
    c_v = c_q + 2 * D_ATT
    c_g = c_v + D_ATT
    return (w[:, :c_q].astype(BF16), w[:, c_q:c_v].astype(BF16),
            jnp.concatenate([w[:, c_g:], w[:, c_v:c_g]], axis=1).astype(BF16))


def _layer(x, p):
    Bsz, L, D = x.shape
    M = Bsz * L
    x0 = x.reshape(M, D)
    x1, u = ffn_block(x0, p['ffn1_pre_g'], p['ffn1_post_g'], p['ffn1_w_gate_up'], p['ffn1_w_down'],
                      next_g=p['mix_pre_g'])
    w_hy, w_qk, w_gv = p['w_in']
    hy_in = hyena_inputs(u, w_hy, p['hy_conv_w'], p['hy_conv_b'], L)
    qk = projection(u, w_qk, BF16, tm=1024, tn=D_ATT, rope_len=L, name="inproj_qk")
    gv = projection(u, w_gv, BF16, tm=1024, tn=w_gv.shape[1] // 4, name="inproj_gates_v")
    a2 = hyena_branch(hy_in, L, p)
    GW = HEADS_PER_GROUP * HEAD_DIM
    v_col0 = N_BRANCH * D_MODEL // GW
    att = []
    for g in range(N_GROUPS):
        srcs = ((qk, g), (qk, N_GROUPS + g), (gv, v_col0 + g))
        fn = dilated_attention_group if ATT_GROUPS[g][1] == 1 else dilated_attention_slabs
        att.append(fn(srcs, Bsz, L, g))
    x2, xn2 = merge_out(x1, a2, [o for o, _ in att], [l for _, l in att], gv, p['mix_post_g'], p['ffn2_pre_g'],
                        p['w_hy_proj'], p['w_att_proj'], p['w_out'])
    x3 = ffn_block(x2, p['ffn2_pre_g'], p['ffn2_post_g'], p['ffn2_w_gate_up'], p['ffn2_w_down'],
                   xn=xn2)
    return x3.reshape(Bsz, L, D)


def kernel(x_prompt, x_sample, ffn1_pre_g, ffn1_post_g, ffn1_w_gate, ffn1_w_up, ffn1_w_down, mix_pre_g, mix_post_g, w_in, hy_conv_w, hy_conv_b, filt_w1, filt_b1, filt_w2, filt_b2, filt_w3, filt_b3, filt_w4, filt_freq, hy_bias, w_hy_proj, w_att_proj, w_out, ffn2_pre_g, ffn2_post_g, ffn2_w_gate, ffn2_w_up, ffn2_w_down):
    assert ffn1_w_gate.shape[0] == 1
    p = {
        'ffn1_pre_g': ffn1_pre_g, 'ffn1_post_g': ffn1_post_g,
        'ffn1_w_gate_up': ffn_gate_up_weights(ffn1_w_gate[0], ffn1_w_up[0]),
        'ffn1_w_down': ffn1_w_down[0].astype(BF16),
        'mix_pre_g': mix_pre_g, 'mix_post_g': mix_post_g,
        'w_in': _split_w_in(w_in[0]),
        'hy_conv_w': hy_conv_w[0], 'hy_conv_b': hy_conv_b,
        'filt': (filt_w1[0], filt_b1[0], filt_w2[0], filt_b2[0], filt_w3[0], filt_b3[0], filt_w4[0], filt_freq[0]),
        'hy_bias': hy_bias[0],
        'w_hy_proj': w_hy_proj[0].astype(BF16), 'w_att_proj': w_att_proj[0].astype(BF16),
        'w_out': w_out[0].astype(BF16),
        'ffn2_pre_g': ffn2_pre_g, 'ffn2_post_g': ffn2_post_g,
        'ffn2_w_gate_up': ffn_gate_up_weights(ffn2_w_gate[0], ffn2_w_up[0]),
        'ffn2_w_down': ffn2_w_down[0].astype(BF16),
    }
    return (_layer(x_prompt, p), _layer(x_sample, p))
```

```python
import math
from functools import partial

import numpy as np
import jax
import jax.numpy as jnp
from jax import lax
from jax.experimental import pallas as pl
from jax.experimental.pallas import tpu as pltpu

D_MODEL = 2048
D_HYENA = 1024
HYENA_ORDER = 2
SHORT_CONV = 3
FILTER_EMB = 33
FILTER_BANDS = (FILTER_EMB - 1) // 2
FAST_DECAY_PCT = 0.3
SLOW_DECAY_PCT = 1.5
DECAY_TARGET = 1e-2
HEAD_DIM = 128
HEADS_PER_GROUP = 4
ATT_GROUPS = ((128, 1), (512, 4), (2048, 16))
N_GROUPS = len(ATT_GROUPS)
D_ATT = N_GROUPS * HEADS_PER_GROUP * HEAD_DIM
D_ATT_OUT = HEADS_PER_GROUP * HEAD_DIM
ROPE_DIM = HEAD_DIM // 4
ROPE_THETA = 500000.0
N_BRANCH = 2
D_IN_PROJ = 3 * D_HYENA + 3 * D_ATT + N_BRANCH * D_MODEL
D_FF = 5632
EPS = 1e-6
NEG_INF = -1e30

LANES = 128
FFT_INNER = 256
VMEM_BYTES_V7X = 64 * 1024 * 1024
VMEM_LIMIT_BYTES = VMEM_BYTES_V7X - 4 * 1024 * 1024
BF16 = jnp.bfloat16
F32 = jnp.float32
HIGHEST = lax.Precision.HIGHEST


def _rms(x, g):
    return x * lax.rsqrt(jnp.mean(x * x, axis=-1, keepdims=True) + EPS) * g


def _dot(a, b):
    return jnp.dot(a, b, preferred_element_type=F32)


def _params(*sem):
    return pltpu.CompilerParams(dimension_semantics=sem, vmem_limit_bytes=VMEM_LIMIT_BYTES)


FFN_SPLIT = 2


def _ffn_kernel(norm_in, norm_out, x_ref, xn_src_ref, post_g_ref, next_g_ref, wg_ref, wu_ref, wd_ref, *rest):
    o_ref = rest[0]
    u_ref = rest[1] if norm_out else None
    xn_ref = rest[-1] if norm_in else xn_src_ref
    j = pl.program_id(1)
    tf = wg_ref.shape[1]

    @pl.when(j == 0)
    def _():
        if norm_in:
            xn_ref[...] = _rms(x_ref[...], xn_src_ref[...]).astype(BF16)
        o_ref[...] = jnp.zeros_like(o_ref)

    xn = xn_ref[...]
    hs = []
    for c in range(FFN_SPLIT):
        cols = slice(c * (tf // FFN_SPLIT), (c + 1) * (tf // FFN_SPLIT))
        gate = _dot(xn, wg_ref[:, cols])
        up = _dot(xn, wu_ref[:, cols])
        hs.append((gate * jax.nn.sigmoid(gate) * up).astype(BF16))
    o_ref[...] += _dot(jnp.concatenate(hs, axis=1), wd_ref[...])

    @pl.when(j == pl.num_programs(1) - 1)
    def _():
        y = x_ref[...] + 0.5 * _rms(o_ref[...], post_g_ref[...])
        o_ref[...] = y
        if norm_out:
            u_ref[...] = _rms(y, next_g_ref[...]).astype(u_ref.dtype)


def ffn_block(x, pre_g, post_g, wg, wu, wd, *, xn=None, next_g=None, tm=512, tf=512):
    M, D = x.shape
    FF = wg.shape[1]
    norm_in, norm_out = xn is None, next_g is not None
    rows = pl.BlockSpec((tm, D), lambda i, j: (i, 0))
    gain = pl.BlockSpec((1, D), lambda i, j: (0, 0))
    out = pl.pallas_call(
        partial(_ffn_kernel, norm_in, norm_out),
        grid=(M // tm, FF // tf),
        in_specs=[
            rows,
            gain if norm_in else rows,
            gain, gain,
            pl.BlockSpec((D, tf), lambda i, j: (0, j)),
            pl.BlockSpec((D, tf), lambda i, j: (0, j)),
            pl.BlockSpec((tf, D), lambda i, j: (j, 0)),
        ],
        out_specs=[rows, rows] if norm_out else [rows],
        out_shape=[jax.ShapeDtypeStruct((M, D), F32)] + ([jax.ShapeDtypeStruct((M, D), BF16)] if norm_out else []),
        scratch_shapes=[pltpu.VMEM((tm, D), BF16)] if norm_in else [],
        compiler_params=_params("parallel", "arbitrary"),
        name="ffn_block",
    )(x, pre_g if norm_in else xn, post_g, next_g if norm_out else post_g, wg, wu, wd)
    return out if norm_out else out[0]


def rope_tables(L):
    half = ROPE_DIM // 2
    inv_freq = jnp.power(ROPE_THETA, -jnp.arange(half, dtype=F32) / half)
    ang = jnp.arange(L, dtype=F32)[:, None] * inv_freq[None, :]
    cos, sin = jnp.cos(ang), jnp.sin(ang)
    rest = HEAD_DIM - ROPE_DIM
    c = jnp.concatenate([cos, cos, jnp.ones((L, rest), F32)], axis=1)
    s_lo = jnp.concatenate([-sin, jnp.zeros((L, half + rest), F32)], axis=1)
    s_hi = jnp.concatenate([jnp.zeros((L, half), F32), sin, jnp.zeros((L, rest), F32)], axis=1)
    scale = HEAD_DIM ** -0.5
    return jnp.stack([c * scale, c]), jnp.stack([s_lo * scale, s_lo]), jnp.stack([s_hi * scale, s_hi])


def _proj_kernel(rotate, xn_ref, w_ref, *rest):
    o_ref = rest[-1]
    r = _dot(xn_ref[...], w_ref[...])
    if rotate:
        c_ref, slo_ref, shi_ref = rest[:3]
        tn = r.shape[1]
        half = ROPE_DIM // 2
        wide = lambda t_ref: jnp.concatenate([t_ref[...]] * (tn // HEAD_DIM), axis=1)
        r = r * wide(c_ref) + pltpu.roll(r, tn - half, 1) * wide(slo_ref) + pltpu.roll(r, half, 1) * wide(shi_ref)
    o_ref[...] = r.astype(o_ref.dtype)


def projection(xn, w, out_dtype, *, tm, tn, rope_len=None, name):
    M, D = xn.shape
    N = w.shape[1]
    in_specs = [
        pl.BlockSpec((tm, D), lambda i, j: (i, 0)),
        pl.BlockSpec((D, tn), lambda i, j: (0, j)),
    ]
    args = [xn, w]
    if rope_len is not None:
        assert tn == D_ATT and N == 2 * D_ATT and rope_len % tm == 0
        spec = pl.BlockSpec((None, tm, HEAD_DIM), lambda i, j: (j, i % (rope_len // tm), 0))
        in_specs += [spec, spec, spec]
        args += list(rope_tables(rope_len))
    return pl.pallas_call(
        partial(_proj_kernel, rope_len is not None),
        grid=(M // tm, N // tn),
        in_specs=in_specs,
        out_specs=pl.BlockSpec((tm, tn), lambda i, j: (i, j)),
        out_shape=jax.ShapeDtypeStruct((M, N), out_dtype),
        compiler_params=_params("parallel", "arbitrary"),
        name=name,
    )(*args)


ATT_RADIUS = 64
assert all(w // (2 * d) == ATT_RADIUS for w, d in ATT_GROUPS)


def _attn_kernel(T, Ls, q_ref, kp_ref, kc_ref, kn_ref, vp_ref, vc_ref, vn_ref, o_ref, lse_ref, kbuf, vbuf):
    t = pl.program_id(2)
    R = ATT_RADIUS
    SB = 2 * R
    for buf, prv, cur, nxt in ((kbuf, kp_ref, kc_ref, kn_ref), (vbuf, vp_ref, vc_ref, vn_ref)):
        buf[0:R] = prv[...]
        buf[R:R + T] = cur[...]
        buf[R + T:R + T + R] = nxt[...]
    qi = lax.broadcasted_iota(jnp.int32, (SB, 2 * SB), 0)
    kk = lax.broadcasted_iota(jnp.int32, (SB, 2 * SB), 1)
    band = (kk >= qi) & (kk <= qi + 2 * R)
    lane = lax.broadcasted_iota(jnp.int32, (SB, LANES), 1)
    for sb in range(T // SB):
        kpos = t * T + (sb * SB - R) + kk
        mask = band & (kpos >= 0) & (kpos < Ls)
        lse = jnp.zeros((SB, LANES), F32)
        for h in range(HEADS_PER_GROUP):
            cols = slice(h * HEAD_DIM, (h + 1) * HEAD_DIM)
            q = q_ref[sb * SB:(sb + 1) * SB, cols]
            k = kbuf[sb * SB:(sb + 2) * SB, cols]
            v = vbuf[sb * SB:(sb + 2) * SB, cols]
            s = lax.dot_general(q, k, (((1,), (1,)), ((), ())), preferred_element_type=F32)
            s = jnp.where(mask, s, NEG_INF)
            m = jnp.max(s, axis=1, keepdims=True)
            p = jnp.exp(s - m)
            l = jnp.sum(p, axis=1, keepdims=True)
            o = _dot(p.astype(BF16), v) / l
            o_ref[sb * SB:(sb + 1) * SB, cols] = o.astype(o_ref.dtype)
            lse = jnp.where(lane == h, m + jnp.log(l), lse)
        lse_ref[sb * SB:(sb + 1) * SB, :] = lse


SLAB = 16


def _softmax_heads(q, k, v, mask):
    lane = lax.broadcasted_iota(jnp.int32, (q.shape[0], LANES), 1)
    lse = jnp.zeros((q.shape[0], LANES), F32)
    outs = []
    for h in range(HEADS_PER_GROUP):
        cols = slice(h * HEAD_DIM, (h + 1) * HEAD_DIM)
        s = lax.dot_general(q[:, cols], k[:, cols], (((1,), (1,)), ((), ())), preferred_element_type=F32)
        s = jnp.where(mask, s, NEG_INF)
        m = jnp.max(s, axis=1, keepdims=True)
        p = jnp.exp(s - m)
        l = jnp.sum(p, axis=1, keepdims=True)
        outs.append(_dot(p.astype(BF16), v[:, cols]) / l)
        lse = jnp.where(lane == h, m + jnp.log(l), lse)
    return jnp.concatenate(outs, axis=1), lse


def _attn_slab_kernel(d, NS, Ls, q_ref, kp_ref, kc_ref, kn_ref, vp_ref, vc_ref, vn_ref, o_ref, lse_ref,
                      qs, ks, vs, os_, ls):
    t = pl.program_id(1)
    R = ATT_RADIUS
    SB = 2 * R
    J = SLAB // d
    NH = kp_ref.shape[0]
    QA = SB // J
    KA = QA + 2 * NH
    n_sb = NS // QA
    to_class_major = lambda ref: pltpu.einshape("abc->bac", ref[...])
    qs[...] = to_class_major(q_ref)
    for buf, prv, cur, nxt in ((ks, kp_ref, kc_ref, kn_ref), (vs, vp_ref, vc_ref, vn_ref)):
        buf[:, 0:NH] = to_class_major(prv)
        buf[:, NH:NH + NS] = to_class_major(cur)
        buf[:, NH + NS:NH + NS + NH] = to_class_major(nxt)
    qi = lax.broadcasted_iota(jnp.int32, (SB, 2 * SB), 0)
    kk = lax.broadcasted_iota(jnp.int32, (SB, 2 * SB), 1)
    sq_rel = J * (qi % QA) + qi // QA
    sk_rel = J * (kk % KA - NH) + kk // KA
    band = jnp.abs(sk_rel - sq_rel) <= R

    def body(it, carry):
        r = it // n_sb
        a0 = pl.multiple_of((it % n_sb) * QA, QA)
        sk = J * (t * NS + a0) + sk_rel
        mask = band & (sk >= 0) & (sk < Ls)
        gather = lambda buf, n: jnp.concatenate([buf[j * d + r, pl.ds(a0, n), :] for j in range(J)], axis=0)
        o, lse = _softmax_heads(gather(qs, QA), gather(ks, KA), gather(vs, KA), mask)
        o = o.astype(os_.dtype)
        for j in range(J):
            os_[j * d + r, pl.ds(a0, QA), :] = o[j * QA:(j + 1) * QA]
            ls[j * d + r, pl.ds(a0, QA), :] = lse[j * QA:(j + 1) * QA]
        return carry

    lax.fori_loop(0, d * n_sb, body, 0, unroll=True)
    o_ref[...] = pltpu.einshape("bac->abc", os_[...])
    lse_ref[...] = pltpu.einshape("bac->abc", ls[...])


def dilated_attention_slabs(srcs, Bsz, L, g, *, NS=128):
    d = ATT_GROUPS[g][1]
    M = Bsz * L
    R = ATT_RADIUS
    GW = HEADS_PER_GROUP * HEAD_DIM
    NH = R * d // SLAB
    TB = NS * SLAB
    assert SLAB % d == 0 and L % TB == 0 and NS % NH == 0 and (2 * R) % (SLAB // d) == 0
    nmb, nhb, hpm = L // TB, L // (NH * SLAB), NS // NH
    views = [a.reshape(M // SLAB, SLAB, a.shape[1]) for a, _ in srcs]
    col = lambda which: srcs[which][1]

    def main(which):
        return pl.BlockSpec((NS, SLAB, GW), lambda b, t: (b * nmb + t, 0, col(which)))

    def prev(which):
        return pl.BlockSpec((NH, SLAB, GW), lambda b, t: (jnp.maximum(b * nhb + t * hpm - 1, b * nhb), 0, col(which)))

    def nxt(which):
        return pl.BlockSpec((NH, SLAB, GW),
                            lambda b, t: (jnp.minimum(b * nhb + (t + 1) * hpm, (b + 1) * nhb - 1), 0, col(which)))

    o, lse = pl.pallas_call(
        partial(_attn_slab_kernel, d, NS, L // d),
        grid=(Bsz, nmb),
        in_specs=[main(0), prev(1), main(1), nxt(1), prev(2), main(2), nxt(2)],
        out_specs=[pl.BlockSpec((NS, SLAB, GW), lambda b, t: (b * nmb + t, 0, 0)),
                   pl.BlockSpec((NS, SLAB, LANES), lambda b, t: (b * nmb + t, 0, 0))],
        out_shape=[jax.ShapeDtypeStruct((M // SLAB, SLAB, GW), BF16),
                   jax.ShapeDtypeStruct((M // SLAB, SLAB, LANES), F32)],
        scratch_shapes=[pltpu.VMEM((SLAB, NS, GW), BF16),
                        pltpu.VMEM((SLAB, NS + 2 * NH, GW), BF16), pltpu.VMEM((SLAB, NS + 2 * NH, GW), BF16),
                        pltpu.VMEM((SLAB, NS, GW), BF16), pltpu.VMEM((SLAB, NS, LANES), F32)],
        compiler_params=_params("parallel", "arbitrary"),
        name=f"dilated_attention_g{g}",
    )(views[0], views[1], views[1], views[1], views[2], views[2], views[2])
    return o.reshape(M, GW), lse.reshape(M, LANES)


def dilated_attention_group(srcs, Bsz, L, g, *, T=512):
    d = ATT_GROUPS[g][1]
    M = Bsz * L
    Ls = L // d
    R = ATT_RADIUS
    GW = HEADS_PER_GROUP * HEAD_DIM
    assert d == 1 and Ls % T == 0 and T % (2 * R) == 0
    nrb, nhb, hpt = Ls // T, Ls // R, T // R

    def main(which):
        return pl.BlockSpec((T, GW), lambda b, r, t: (b * nrb + t, srcs[which][1]))

    def prev(which):
        return pl.BlockSpec((R, GW), lambda b, r, t: (jnp.maximum(b * nhb + t * hpt - 1, b * nhb), srcs[which][1]))

    def nxt(which):
        return pl.BlockSpec((R, GW), lambda b, r, t: (jnp.minimum(b * nhb + (t + 1) * hpt, (b + 1) * nhb - 1),
                                                      srcs[which][1]))

    o, lse = pl.pallas_call(
        partial(_attn_kernel, T, Ls),
        grid=(Bsz, d, nrb),
        in_specs=[main(0), prev(1), main(1), nxt(1), prev(2), main(2), nxt(2)],
        out_specs=[pl.BlockSpec((T, GW), lambda b, r, t: (b * nrb + t, r)),
                   pl.BlockSpec((T, LANES), lambda b, r, t: (b * nrb + t, r))],
        out_shape=[jax.ShapeDtypeStruct((M // d, d * GW), BF16),
                   jax.ShapeDtypeStruct((M // d, d * LANES), F32)],
        scratch_shapes=[pltpu.VMEM((T + 2 * R, GW), BF16), pltpu.VMEM((T + 2 * R, GW), BF16)],
        compiler_params=_params("parallel", "parallel", "arbitrary"),
        name=f"dilated_attention_g{g}",
    )(srcs[0][0], srcs[1][0], srcs[1][0], srcs[1][0], srcs[2][0], srcs[2][0], srcs[2][0])
    return o.reshape(M, GW), lse.reshape(M, LANES)


def _merge_kernel(x_ref, a_ref, o0_ref, o1_ref, o2_ref, l0_ref, l1_ref, l2_ref, ga_ref, gb_ref,
                  post_g_ref, next_g_ref, whp_ref, wap_ref, wo_ref, o_ref, u_ref):
    lses = [l0_ref[...], l1_ref[...], l2_ref[...]]
    mx = jnp.maximum(jnp.maximum(lses[0], lses[1]), lses[2])
    es = [jnp.exp(l - mx) for l in lses]
    den = es[0] + es[1] + es[2]
    wts = [e / den for e in es]
    outs = [o0_ref, o1_ref, o2_ref]
    heads = []
    for h in range(HEADS_PER_GROUP):
        cols = slice(h * HEAD_DIM, (h + 1) * HEAD_DIM)
        heads.append(sum(wts[g][:, h:h + 1] * outs[g][:, cols].astype(F32) for g in range(N_GROUPS)))
    att = jnp.concatenate(heads, axis=1).astype(BF16)
    a = _dot(a_ref[...], whp_ref[...])
    b = _dot(att, wap_ref[...])
    gate_a, gate_b = ga_ref[...].astype(F32), gb_ref[...].astype(F32)
    merged = (jax.nn.sigmoid(gate_a) * a + jax.nn.sigmoid(gate_b) * b).astype(BF16)
    mix = _dot(merged, wo_ref[...])
    y = x_ref[...] + _rms(mix, post_g_ref[...])
    o_ref[...] = y
    u_ref[...] = _rms(y, next_g_ref[...]).astype(u_ref.dtype)


def merge_out(x, a_in, att_outs, att_lses, gates, post_g, next_g, whp, wap, wo, *, tm=512):
    M, D = x.shape
    const = lambda i: (0, 0)
    rows = lambda a: pl.BlockSpec((tm, a.shape[1]), lambda i: (i, 0))
    return pl.pallas_call(
        _merge_kernel,
        grid=(M // tm,),
        in_specs=[rows(x), rows(a_in)] + [rows(o) for o in att_outs] + [rows(l) for l in att_lses] + [
            pl.BlockSpec((tm, D), lambda i: (i, 0)),
            pl.BlockSpec((tm, D), lambda i: (i, 1)),
            pl.BlockSpec((1, D), const),
            pl.BlockSpec((1, D), const),
            pl.BlockSpec(whp.shape, const, pipeline_mode=pl.Buffered(1)),
            pl.BlockSpec(wap.shape, const, pipeline_mode=pl.Buffered(1)),
            pl.BlockSpec(wo.shape, const, pipeline_mode=pl.Buffered(1)),
        ],
        out_specs=[pl.BlockSpec((tm, D), lambda i: (i, 0)), pl.BlockSpec((tm, D), lambda i: (i, 0))],
        out_shape=[jax.ShapeDtypeStruct((M, D), F32), jax.ShapeDtypeStruct((M, D), BF16)],
        compiler_params=_params("parallel"),
        name="merge_out",
    )(x, a_in, *att_outs, *att_lses, gates, gates, post_g, next_g, whp, wap, wo)


def _dft_constants(L):
    N = 2 * L
    N2 = FFT_INNER
    N1 = N // N2
    h = N1 // 2
    idx1 = np.arange(N1)
    ang1 = -2.0 * np.pi * ((idx1[:, None] * idx1[None, :]) % N1) / N1
    f1r, f1i = np.cos(ang1), np.sin(ang1)
    s_data = np.block([[f1r[:, :h], -f1i[:, :h]], [f1i[:, :h], f1r[:, :h]]])
    s_filt = np.concatenate([f1r, f1i], axis=0)
    ar, ai = f1r[:h, :], -f1i[:h, :]
    t_fin = np.block([[ar, -ai], [ai, ar]])
    idx2 = np.arange(N2)
    ang2 = -2.0 * np.pi * ((idx2[:, None] * idx2[None, :]) % N2) / N2
    angt = -2.0 * np.pi * (idx1[:, None] * idx2[None, :]) / N
    return dict(
        N1=N1,
        s_data=jnp.asarray(s_data, BF16), s_filt=jnp.asarray(s_filt, BF16), t_fin=jnp.asarray(t_fin, BF16),
        f2r=jnp.asarray(np.cos(ang2), F32), f2i=jnp.asarray(np.sin(ang2), F32),
        twr=jnp.asarray(np.cos(angt).reshape(N1, 1, N2), F32),
        twi=jnp.asarray(np.sin(angt).reshape(N1, 1, N2), F32),
    )


def _hyena_in_kernel(tm, L, x_ref, prev_ref, next_ref, w_ref, cw_ref, cb_ref, o_ref, xn_ref):
    i = pl.program_id(0)
    H = prev_ref.shape[0]

    @pl.when(pl.program_id(1) == 0)
    def _():
        xn_ref[0:H] = prev_ref[...]
        xn_ref[H:H + tm] = x_ref[...]
        xn_ref[H + tm:H + tm + H] = next_ref[...]

    r = _dot(xn_ref[...], w_ref[...])
    row = lax.broadcasted_iota(jnp.int32, r.shape, 0)
    pos0 = (i * tm) % L
    outside = ((row < H) & (pos0 == 0)) | ((row >= H + tm) & (pos0 + tm == L))
    r = jnp.where(outside, 0.0, r)
    n = r.shape[0]
    y = pltpu.roll(r, 1, 0) * cw_ref[0:1, :] + r * cw_ref[1:2, :] + pltpu.roll(r, n - 1, 0) * cw_ref[2:3, :]
    o_ref[0] = (y[H:H + tm] + cb_ref[...]).astype(o_ref.dtype)


def hyena_inputs(x, w, conv_w, conv_b, L, *, tm=1024):
    M, D = x.shape
    C = w.shape[1] // 3
    H = SLAB
    nh = M // H
    assert L % tm == 0 and tm % H == 0
    return pl.pallas_call(
        partial(_hyena_in_kernel, tm, L),
        grid=(M // tm, 3),
        in_specs=[
            pl.BlockSpec((tm, D), lambda i, j: (i, 0)),
            pl.BlockSpec((H, D), lambda i, j: (jnp.maximum(i * (tm // H) - 1, 0), 0)),
            pl.BlockSpec((H, D), lambda i, j: (jnp.minimum((i + 1) * (tm // H), nh - 1), 0)),
            pl.BlockSpec((D, C), lambda i, j: (0, j)),
            pl.BlockSpec((SHORT_CONV, C), lambda i, j: (0, j)),
            pl.BlockSpec((1, C), lambda i, j: (0, j)),
        ],
        out_specs=pl.BlockSpec((1, tm, C), lambda i, j: (j, i, 0)),
        out_shape=jax.ShapeDtypeStruct((3, M, C), BF16),
        scratch_shapes=[pltpu.VMEM((tm + 2 * H, D), BF16)],
        compiler_params=_params("parallel", "arbitrary"),
        name="hyena_inputs",
    )(x, x, x, w, conv_w, conv_b)


def _filter_kernel(L, tl, w1_ref, b1_ref, w2_ref, b2_ref, w3_ref, b3_ref, w4hi_ref, w4lo_ref,
                   freq_ref, fvec_ref, delta_ref, kf_ref):
    i = pl.program_id(0)
    HALF = LANES // 2
    wide = delta_ref.shape[1]
    C = wide // (2 * HYENA_ORDER)
    row = i * tl + lax.broadcasted_iota(jnp.int32, (tl, LANES), 0)
    lane = lax.broadcasted_iota(jnp.int32, (tl, LANES), 1)
    posf = jnp.where(lane < HALF, row, L - row).astype(F32)
    t = posf / (L - 1)
    a = fvec_ref[...] * ((2.0 * math.pi / L) * posf)
    lh = lane % HALF
    feats = jnp.where(lh < FILTER_BANDS, jnp.cos(a),
                      jnp.where(lh < 2 * FILTER_BANDS, -jnp.sin(a),
                                jnp.where(lh == 2 * FILTER_BANDS, t, 0.0)))
    freq = freq_ref[...]
    dense = lambda v, w_ref, b_ref: jnp.sin(freq * (
        jnp.dot(v, w_ref[...], precision=HIGHEST, preferred_element_type=F32) + b_ref[...]))
    h = dense(dense(dense(feats, w1_ref, b1_ref), w2_ref, b2_ref), w3_ref, b3_ref)
    h_hi = h.astype(BF16)
    h_lo = (h - h_hi.astype(F32)).astype(BF16)
    taps = _dot(h_hi, w4hi_ref[...]) + _dot(h_lo, w4hi_ref[...]) + _dot(h_hi, w4lo_ref[...])
    row_w = i * tl + lax.broadcasted_iota(jnp.int32, (tl, wide), 0)
    col_w = lax.broadcasted_iota(jnp.int32, (tl, wide), 1)
    backward = col_w >= wide // 2
    t_w = jnp.where(backward, L - row_w, row_w).astype(F32) / (L - 1)
    taps = taps * jnp.exp(-t_w * delta_ref[...])
    taps = jnp.where(backward & (row_w == 0), 0.0, taps)
    for n in range(HYENA_ORDER):
        for direction in range(2):
            c0 = (direction * HYENA_ORDER + n) * C
            kf_ref[n, direction] = taps[:, c0:c0 + C].astype(kf_ref.dtype)


def hyena_filter_taps(L, fw1, fb1, fw2, fb2, fw3, fb3, fw4, ffreq, *, tl=512):
    C = D_HYENA
    H = fw2.shape[0]
    HALF = LANES // 2
    assert H <= HALF and 2 * FILTER_BANDS + 1 <= HALF

    def both(a, rows):
        blk = jnp.zeros((HALF if rows else 1, HALF), F32).at[:a.shape[0], :a.shape[1]].set(a)
        if not rows:
            return jnp.concatenate([blk, blk], axis=1)
        zero = jnp.zeros_like(blk)
        return jnp.concatenate([jnp.concatenate([blk, zero], axis=1), jnp.concatenate([zero, blk], axis=1)], axis=0)

    w1 = both(jnp.concatenate([fw1[1:], fw1[:1]], axis=0), True)
    w2, w3 = both(fw2, True), both(fw3, True)
    b1, b2, b3, freq = (both(v[None, :], False) for v in (fb1, fb2, fb3, ffreq))
    w4 = fw4.reshape(H, HYENA_ORDER, 2, C)
    wide = 2 * HYENA_ORDER * C
    w4p = jnp.zeros((LANES, wide), F32)
    w4p = w4p.at[:H, :wide // 2].set(w4[:, :, 0].reshape(H, HYENA_ORDER * C))
    w4p = w4p.at[HALF:HALF + H, wide // 2:].set(w4[:, :, 1].reshape(H, HYENA_ORDER * C))
    w4hi = w4p.astype(BF16)
    w4lo = (w4p - w4hi.astype(F32)).astype(BF16)
    bands = np.linspace(1e-4, FILTER_BANDS - 1, FILTER_BANDS, dtype=np.float32)
    fvec = np.zeros((1, LANES), np.float32)
    for base in (0, HALF):
        fvec[0, base:base + FILTER_BANDS] = bands
        fvec[0, base + FILTER_BANDS:base + 2 * FILTER_BANDS] = bands
    max_decay = math.log(DECAY_TARGET) / FAST_DECAY_PCT
    min_decay = math.log(DECAY_TARGET) / SLOW_DECAY_PCT
    deltas = np.abs(np.linspace(min_decay, max_decay, C, dtype=np.float32))
    delta_w = np.tile(deltas[None, :], (1, 2 * HYENA_ORDER))
    const = lambda i: (0, 0)
    args = (w1, b1, w2, b2, w3, b3, w4hi, w4lo, freq, jnp.asarray(fvec), jnp.asarray(delta_w))
    return pl.pallas_call(
        partial(_filter_kernel, L, tl),
        grid=(L // tl,),
        in_specs=[pl.BlockSpec(a.shape, const) for a in args],
        out_specs=pl.BlockSpec((HYENA_ORDER, 2, tl, C), lambda i: (0, 0, i, 0)),
        out_shape=jax.ShapeDtypeStruct((HYENA_ORDER, 2, L, C), BF16),
        compiler_params=_params("parallel"),
        name="hyena_filter_taps",
    )(*args)


FFT_ROWS = 16
FFT_COLS = 512
FFT_COLS_IN = 1024


def _stage1_kernel(s_ref, x_ref, y_ref):
    _, h, rows, cols = x_ref.shape
    s = s_ref[...]
    xt = pltpu.einshape("abc->bac", x_ref[...].reshape(2 * h, rows, cols))
    yt = jnp.stack([_dot(s, xt[b]).astype(y_ref.dtype) for b in range(rows)], axis=0)
    y_ref[...] = pltpu.einshape("bac->abc", yt).reshape(y_ref.shape)


def fft_stage1(x, which, s):
    _, _, h, N2, C = x.shape
    N1 = 2 * h
    return pl.pallas_call(
        _stage1_kernel,
        grid=(N2 // FFT_ROWS, C // FFT_COLS_IN),
        in_specs=[pl.BlockSpec((2 * N1, N1), lambda j, c: (0, 0)),
                  pl.BlockSpec((None, 2, h, FFT_ROWS, FFT_COLS_IN), lambda j, c: (which, 0, 0, j, c))],
        out_specs=pl.BlockSpec((2, N1, FFT_ROWS, FFT_COLS_IN), lambda j, c: (0, 0, j, c)),
        out_shape=jax.ShapeDtypeStruct((2, N1, N2, C), BF16),
        compiler_params=_params("parallel", "parallel"),
        name="fft_stage1",
    )(s, x)


FFT_K1_PER_STEP = 4


def _mid_kernel(inv_n, f2r_ref, f2i_ref, twr_ref, twi_ref, yf_ref, y_ref, u_ref):
    f2r, f2i = f2r_ref[...], f2i_ref[...]
    for k in range(y_ref.shape[1]):
        twr, twi = twr_ref[k], twi_ref[k]
        gr, gi = f2r * twr - f2i * twi, f2r * twi + f2i * twr
        grb, gib = gr.astype(BF16), gi.astype(BF16)

        def inner(ref):
            re, im = ref[0, k], ref[1, k]
            return _dot(grb, re) - _dot(gib, im), _dot(gib, re) + _dot(grb, im)

        hr, hi = inner(yf_ref)
        zr, zi = inner(y_ref)
        pr = (zr * hr - zi * hi).astype(BF16)
        pi = (zr * hi + zi * hr).astype(BF16)
        irb = (gr.T * inv_n).astype(BF16)
        iib = (gi.T * (-inv_n)).astype(BF16)
        u_ref[0, k] = (_dot(irb, pr) - _dot(iib, pi)).astype(u_ref.dtype)
        u_ref[1, k] = (_dot(iib, pr) + _dot(irb, pi)).astype(u_ref.dtype)


def fft_mid(yf, y, dc):
    _, N1, N2, C = y.shape
    kb = FFT_K1_PER_STEP
    blk = pl.BlockSpec((2, kb, N2, C), lambda k: (0, k, 0, 0))
    const = pl.BlockSpec((N2, N2), lambda k: (0, 0))
    tw = pl.BlockSpec((kb, 1, N2), lambda k: (k, 0, 0))
    return pl.pallas_call(
        partial(_mid_kernel, 1.0 / (N1 * N2)),
        grid=(N1 // kb,),
        in_specs=[const, const, tw, tw, blk, blk],
        out_specs=blk,
        out_shape=jax.ShapeDtypeStruct((2, N1, N2, C), BF16),
        compiler_params=_params("parallel"),
        name="fft_mid",
    )(dc['f2r'], dc['f2i'], dc['twr'], dc['twi'], yf, y)


def _final_kernel(t_ref, u_ref, z_ref, gate_ref, bias_ref, o_ref):
    _, n1, rows, cols = u_ref.shape
    t = t_ref[...]
    ut = pltpu.einshape("abc->bac", u_ref[...].reshape(2 * n1, rows, cols))
    convt = jnp.stack([_dot(t, ut[b]).astype(BF16) for b in range(rows)], axis=0)
    conv = pltpu.einshape("bac->abc", convt).reshape(o_ref.shape).astype(F32)
    z = z_ref[...].astype(F32)
    o_ref[...] = (gate_ref[...].astype(F32) * (conv + bias_ref[...] * z)).astype(o_ref.dtype)


def fft_final(u, t_fin, z, z_which, gate, gate_which, bias, bias_which):
    _, N1, N2, C = u.shape
    h = N1 // 2

    def half(which):
        return pl.BlockSpec((None, 2, h, FFT_ROWS, FFT_COLS), lambda j, c: (which, 0, 0, j, c))

    return pl.pallas_call(
        _final_kernel,
        grid=(N2 // FFT_ROWS, C // FFT_COLS),
        in_specs=[pl.BlockSpec((N1, 2 * N1), lambda j, c: (0, 0)),
                  pl.BlockSpec((2, N1, FFT_ROWS, FFT_COLS), lambda j, c: (0, 0, j, c)),
                  half(z_which), half(gate_which),
                  pl.BlockSpec((None, 1, FFT_COLS), lambda j, c: (bias_which, 0, c))],
        out_specs=half(0),
        out_shape=jax.ShapeDtypeStruct((1, 2, h, N2, C), BF16),
        compiler_params=_params("parallel", "parallel"),
        name="fft_final",
    )(t_fin, u, z, gate, bias.reshape(bias.shape[0], 1, C))


def hyena_branch(zs, L, p):
    _, M, C = zs.shape
    assert M == 2 * L, "the batch pair rides as real/imaginary parts"
    dc = _dft_constants(L)
    split = (2, dc['N1'] // 2, FFT_INNER, C)
    taps = hyena_filter_taps(L, *p['filt']).reshape(HYENA_ORDER, *split)
    zs = zs.reshape(3, *split)
    z, z_which = zs, 0
    for n in range(HYENA_ORDER):
        u = fft_mid(fft_stage1(taps, n, dc['s_filt']), fft_stage1(z, z_which, dc['s_data']), dc)
        z, z_which = fft_final(u, dc['t_fin'], z, z_which, zs, n + 1, p['hy_bias'], n), 0
    return z.reshape(M, C)


def _split_w_in(w):
    c_q = 3 * D_HYENA
    c_v = c_q + 2 * D_ATT
    c_g = c_v + D_ATT
    return (w[:, :c_q].astype(BF16), w[:, c_q:c_v].astype(BF16),
            jnp.concatenate([w[:, c_g:], w[:, c_v:c_g]], axis=1).astype(BF16))


def _layer(x, p):
    Bsz, L, D = x.shape
    M = Bsz * L
    x0 = x.reshape(M, D)
    x1, u = ffn_block(x0, p['ffn1_pre_g'], p['ffn1_post_g'], p['ffn1_w_gate'], p['ffn1_w_up'], p['ffn1_w_down'],
                      next_g=p['mix_pre_g'])
    w_hy, w_qk, w_gv = p['w_in']
    hy_in = hyena_inputs(u, w_hy, p['hy_conv_w'], p['hy_conv_b'], L)
    qk = projection(u, w_qk, BF16, tm=1024, tn=D_ATT, rope_len=L, name="inproj_qk")
    gv = projection(u, w_gv, BF16, tm=1024, tn=w_gv.shape[1] // 4, name="inproj_gates_v")
    a2 = hyena_branch(hy_in, L, p)
    GW = HEADS_PER_GROUP * HEAD_DIM
    v_col0 = N_BRANCH * D_MODEL // GW
    att = []
    for g in range(N_GROUPS):
        srcs = ((qk, g), (qk, N_GROUPS + g), (gv, v_col0 + g))
        fn = dilated_attention_group if ATT_GROUPS[g][1] == 1 else dilated_attention_slabs
        att.append(fn(srcs, Bsz, L, g))
    x2, xn2 = merge_out(x1, a2, [o for o, _ in att], [l for _, l in att], gv, p['mix_post_g'], p['ffn2_pre_g'],
                        p['w_hy_proj'], p['w_att_proj'], p['w_out'])
    x3 = ffn_block(x2, p['ffn2_pre_g'], p['ffn2_post_g'], p['ffn2_w_gate'], p['ffn2_w_up'], p['ffn2_w_down'],
                   xn=xn2)
    return x3.reshape(Bsz, L, D)


def kernel(x_prompt, x_sample, ffn1_pre_g, ffn1_post_g, ffn1_w_gate, ffn1_w_up, ffn1_w_down, mix_pre_g, mix_post_g, w_in, hy_conv_w, hy_conv_b, filt_w1, filt_b1, filt_w2, filt_b2, filt_w3, filt_b3, filt_w4, filt_freq, hy_bias, w_hy_proj, w_att_proj, w_out, ffn2_pre_g, ffn2_post_g, ffn2_w_gate, ffn2_w_up, ffn2_w_down):
    assert ffn1_w_gate.shape[0] == 1
    p = {
        'ffn1_pre_g': ffn1_pre_g, 'ffn1_post_g': ffn1_post_g,
        'ffn1_w_gate': ffn1_w_gate[0].astype(BF16), 'ffn1_w_up': ffn1_w_up[0].astype(BF16),
        'ffn1_w_down': ffn1_w_down[0].astype(BF16),
        'mix_pre_g': mix_pre_g, 'mix_post_g': mix_post_g,
        'w_in': _split_w_in(w_in[0]),
        'hy_conv_w': hy_conv_w[0], 'hy_conv_b': hy_conv_b,
        'filt': (filt_w1[0], filt_b1[0], filt_w2[0], filt_b2[0], filt_w3[0], filt_b3[0], filt_w4[0], filt_freq[0]),
        'hy_bias': hy_bias[0],
        'w_hy_proj': w_hy_proj[0].astype(BF16), 'w_att_proj': w_att_proj[0].astype(BF16),
        'w_out': w_out[0].astype(BF16),
        'ffn2_pre_g': ffn2_pre_g, 'ffn2_post_g': ffn2_post_g,
        'ffn2_w_gate': ffn2_w_gate[0].astype(BF16), 'ffn2_w_up': ffn2_w_up[0].astype(BF16),
        'ffn2_w_down': ffn2_w_down[0].astype(BF16),
    }
    return (_layer(x_prompt, p), _layer(x_sample, p))
```

```python
import math
from functools import partial

import numpy as np
import jax
import jax.numpy as jnp
from jax import lax
from jax.experimental import pallas as pl
from jax.experimental.pallas import tpu as pltpu

D_MODEL = 2048
D_HYENA = 1024
HYENA_ORDER = 2
SHORT_CONV = 3
FILTER_EMB = 33
FILTER_BANDS = (FILTER_EMB - 1) // 2
FAST_DECAY_PCT = 0.3
SLOW_DECAY_PCT = 1.5
DECAY_TARGET = 1e-2
HEAD_DIM = 128
HEADS_PER_GROUP = 4
ATT_GROUPS = ((128, 1), (512, 4), (2048, 16))
N_GROUPS = len(ATT_GROUPS)
D_ATT = N_GROUPS * HEADS_PER_GROUP * HEAD_DIM
D_ATT_OUT = HEADS_PER_GROUP * HEAD_DIM
ROPE_DIM = HEAD_DIM // 4
ROPE_THETA = 500000.0
N_BRANCH = 2
D_IN_PROJ = 3 * D_HYENA + 3 * D_ATT + N_BRANCH * D_MODEL
D_FF = 5632
EPS = 1e-6
NEG_INF = -1e30

LANES = 128
FFT_INNER = 256
VMEM_BYTES_V7X = 64 * 1024 * 1024
VMEM_LIMIT_BYTES = VMEM_BYTES_V7X - 4 * 1024 * 1024
BF16 = jnp.bfloat16
F32 = jnp.float32
HIGHEST = lax.Precision.HIGHEST


def _rms(x, g):
    return x * lax.rsqrt(jnp.mean(x * x, axis=-1, keepdims=True) + EPS) * g


def _dot(a, b):
    return jnp.dot(a, b, preferred_element_type=F32)


def _params(*sem):
    return pltpu.CompilerParams(dimension_semantics=sem, vmem_limit_bytes=VMEM_LIMIT_BYTES)


FFN_SPLIT = 2


def _ffn_kernel(norm_in, norm_out, x_ref, xn_src_ref, post_g_ref, next_g_ref, wg_ref, wu_ref, wd_ref, *rest):
    o_ref = rest[0]
    u_ref = rest[1] if norm_out else None
    xn_ref = rest[-1] if norm_in else xn_src_ref
    j = pl.program_id(1)
    tf = wg_ref.shape[1]

    @pl.when(j == 0)
    def _():
        if norm_in:
            xn_ref[...] = _rms(x_ref[...], xn_src_ref[...]).astype(BF16)
        o_ref[...] = jnp.zeros_like(o_ref)

    xn = xn_ref[...]
    hs = []
    for c in range(FFN_SPLIT):
        cols = slice(c * (tf // FFN_SPLIT), (c + 1) * (tf // FFN_SPLIT))
        gate = _dot(xn, wg_ref[:, cols])
        up = _dot(xn, wu_ref[:, cols])
        hs.append((gate * jax.nn.sigmoid(gate) * up).astype(BF16))
    o_ref[...] += _dot(jnp.concatenate(hs, axis=1), wd_ref[...])

    @pl.when(j == pl.num_programs(1) - 1)
    def _():
        y = x_ref[...] + 0.5 * _rms(o_ref[...], post_g_ref[...])
        o_ref[...] = y
        if norm_out:
            u_ref[...] = _rms(y, next_g_ref[...]).astype(u_ref.dtype)


def ffn_block(x, pre_g, post_g, wg, wu, wd, *, xn=None, next_g=None, tm=512, tf=512):
    M, D = x.shape
    FF = wg.shape[1]
    norm_in, norm_out = xn is None, next_g is not None
    rows = pl.BlockSpec((tm, D), lambda i, j: (i, 0))
    gain = pl.BlockSpec((1, D), lambda i, j: (0, 0))
    out = pl.pallas_call(
        partial(_ffn_kernel, norm_in, norm_out),
        grid=(M // tm, FF // tf),
        in_specs=[
            rows,
            gain if norm_in else rows,
            gain, gain,
            pl.BlockSpec((D, tf), lambda i, j: (0, j)),
            pl.BlockSpec((D, tf), lambda i, j: (0, j)),
            pl.BlockSpec((tf, D), lambda i, j: (j, 0)),
        ],
        out_specs=[rows, rows] if norm_out else [rows],
        out_shape=[jax.ShapeDtypeStruct((M, D), F32)] + ([jax.ShapeDtypeStruct((M, D), BF16)] if norm_out else []),
        scratch_shapes=[pltpu.VMEM((tm, D), BF16)] if norm_in else [],
        compiler_params=_params("parallel", "arbitrary"),
        name="ffn_block",
    )(x, pre_g if norm_in else xn, post_g, next_g if norm_out else post_g, wg, wu, wd)
    return out if norm_out else out[0]


def rope_tables(L):
    half = ROPE_DIM // 2
    inv_freq = jnp.power(ROPE_THETA, -jnp.arange(half, dtype=F32) / half)
    ang = jnp.arange(L, dtype=F32)[:, None] * inv_freq[None, :]
    cos, sin = jnp.cos(ang), jnp.sin(ang)
    rest = HEAD_DIM - ROPE_DIM
    c = jnp.concatenate([cos, cos, jnp.ones((L, rest), F32)], axis=1)
    s_lo = jnp.concatenate([-sin, jnp.zeros((L, half + rest), F32)], axis=1)
    s_hi = jnp.concatenate([jnp.zeros((L, half), F32), sin, jnp.zeros((L, rest), F32)], axis=1)
    scale = HEAD_DIM ** -0.5
    return jnp.stack([c * scale, c]), jnp.stack([s_lo * scale, s_lo]), jnp.stack([s_hi * scale, s_hi])


def _proj_kernel(rotate, xn_ref, w_ref, *rest):
    o_ref = rest[-1]
    r = _dot(xn_ref[...], w_ref[...])
    if rotate:
        c_ref, slo_ref, shi_ref = rest[:3]
        tn = r.shape[1]
        half = ROPE_DIM // 2
        wide = lambda t_ref: jnp.concatenate([t_ref[...]] * (tn // HEAD_DIM), axis=1)
        r = r * wide(c_ref) + pltpu.roll(r, tn - half, 1) * wide(slo_ref) + pltpu.roll(r, half, 1) * wide(shi_ref)
    o_ref[...] = r.astype(o_ref.dtype)


def projection(xn, w, out_dtype, *, tm, tn, rope_len=None, rope=None, name):
    M, D = xn.shape
    N = w.shape[1]
    in_specs = [
        pl.BlockSpec((tm, D), lambda i, j: (i, 0)),
        pl.BlockSpec((D, tn), lambda i, j: (0, j)),
    ]
    args = [xn, w]
    if rope_len is not None:
        assert tn == D_ATT and N == 2 * D_ATT and rope_len % tm == 0
        spec = pl.BlockSpec((None, tm, HEAD_DIM), lambda i, j: (j, i % (rope_len // tm), 0))
        in_specs += [spec, spec, spec]
        assert all(t.shape[1] >= rope_len for t in rope)
        args += list(rope)
    return pl.pallas_call(
        partial(_proj_kernel, rope_len is not None),
        grid=(M // tm, N // tn),
        in_specs=in_specs,
        out_specs=pl.BlockSpec((tm, tn), lambda i, j: (i, j)),
        out_shape=jax.ShapeDtypeStruct((M, N), out_dtype),
        compiler_params=_params("parallel", "arbitrary"),
        name=name,
    )(*args)


ATT_RADIUS = 64
assert all(w // (2 * d) == ATT_RADIUS for w, d in ATT_GROUPS)


def _attn_kernel(T, Ls, q_ref, kp_ref, kc_ref, kn_ref, vp_ref, vc_ref, vn_ref, o_ref, lse_ref, kbuf, vbuf):
    t = pl.program_id(2)
    R = ATT_RADIUS
    SB = 2 * R
    for buf, prv, cur, nxt in ((kbuf, kp_ref, kc_ref, kn_ref), (vbuf, vp_ref, vc_ref, vn_ref)):
        buf[0:R] = prv[...]
        buf[R:R + T] = cur[...]
        buf[R + T:R + T + R] = nxt[...]
    qi = lax.broadcasted_iota(jnp.int32, (SB, 2 * SB), 0)
    kk = lax.broadcasted_iota(jnp.int32, (SB, 2 * SB), 1)
    band = (kk >= qi) & (kk <= qi + 2 * R)
    lane = lax.broadcasted_iota(jnp.int32, (SB, LANES), 1)
    for sb in range(T // SB):
        kpos = t * T + (sb * SB - R) + kk
        mask = band & (kpos >= 0) & (kpos < Ls)
        lse = jnp.zeros((SB, LANES), F32)
        for h in range(HEADS_PER_GROUP):
            cols = slice(h * HEAD_DIM, (h + 1) * HEAD_DIM)
            q = q_ref[sb * SB:(sb + 1) * SB, cols]
            k = kbuf[sb * SB:(sb + 2) * SB, cols]
            v = vbuf[sb * SB:(sb + 2) * SB, cols]
            s = lax.dot_general(q, k, (((1,), (1,)), ((), ())), preferred_element_type=F32)
            s = jnp.where(mask, s, NEG_INF)
            m = jnp.max(s, axis=1, keepdims=True)
            p = jnp.exp(s - m)
            l = jnp.sum(p, axis=1, keepdims=True)
            o = _dot(p.astype(BF16), v) / l
            o_ref[sb * SB:(sb + 1) * SB, cols] = o.astype(o_ref.dtype)
            lse = jnp.where(lane == h, m + jnp.log(l), lse)
        lse_ref[sb * SB:(sb + 1) * SB, :] = lse


SLAB = 16


def _softmax_heads(q, k, v, mask):
    lane = lax.broadcasted_iota(jnp.int32, (q.shape[0], LANES), 1)
    lse = jnp.zeros((q.shape[0], LANES), F32)
    outs = []
    for h in range(HEADS_PER_GROUP):
        cols = slice(h * HEAD_DIM, (h + 1) * HEAD_DIM)
        s = lax.dot_general(q[:, cols], k[:, cols], (((1,), (1,)), ((), ())), preferred_element_type=F32)
        s = jnp.where(mask, s, NEG_INF)
        m = jnp.max(s, axis=1, keepdims=True)
        p = jnp.exp(s - m)
        l = jnp.sum(p, axis=1, keepdims=True)
        outs.append(_dot(p.astype(BF16), v[:, cols]) / l)
        lse = jnp.where(lane == h, m + jnp.log(l), lse)
    return jnp.concatenate(outs, axis=1), lse


def _attn_slab_kernel(d, NS, Ls, q_ref, kp_ref, kc_ref, kn_ref, vp_ref, vc_ref, vn_ref, o_ref, lse_ref,
                      qs, ks, vs, os_, ls):
    t = pl.program_id(1)
    R = ATT_RADIUS
    SB = 2 * R
    J = SLAB // d
    NH = kp_ref.shape[0]
    QA = SB // J
    KA = QA + 2 * NH
    n_sb = NS // QA
    to_class_major = lambda ref: pltpu.einshape("abc->bac", ref[...])
    qs[...] = to_class_major(q_ref)
    for buf, prv, cur, nxt in ((ks, kp_ref, kc_ref, kn_ref), (vs, vp_ref, vc_ref, vn_ref)):
        buf[:, 0:NH] = to_class_major(prv)
        buf[:, NH:NH + NS] = to_class_major(cur)
        buf[:, NH + NS:NH + NS + NH] = to_class_major(nxt)
    qi = lax.broadcasted_iota(jnp.int32, (SB, 2 * SB), 0)
    kk = lax.broadcasted_iota(jnp.int32, (SB, 2 * SB), 1)
    sq_rel = J * (qi % QA) + qi // QA
    sk_rel = J * (kk % KA - NH) + kk // KA
    band = jnp.abs(sk_rel - sq_rel) <= R

    def body(it, carry):
        r = it // n_sb
        a0 = pl.multiple_of((it % n_sb) * QA, QA)
        sk = J * (t * NS + a0) + sk_rel
        mask = band & (sk >= 0) & (sk < Ls)
        gather = lambda buf, n: jnp.concatenate([buf[j * d + r, pl.ds(a0, n), :] for j in range(J)], axis=0)
        o, lse = _softmax_heads(gather(qs, QA), gather(ks, KA), gather(vs, KA), mask)
        o = o.astype(os_.dtype)
        for j in range(J):
            os_[j * d + r, pl.ds(a0, QA), :] = o[j * QA:(j + 1) * QA]
            ls[j * d + r, pl.ds(a0, QA), :] = lse[j * QA:(j + 1) * QA]
        return carry

    lax.fori_loop(0, d * n_sb, body, 0, unroll=True)
    o_ref[...] = pltpu.einshape("bac->abc", os_[...])
    lse_ref[...] = pltpu.einshape("bac->abc", ls[...])


def dilated_attention_slabs(srcs, Bsz, L, g, *, NS=128):
    d = ATT_GROUPS[g][1]
    M = Bsz * L
    R = ATT_RADIUS
    GW = HEADS_PER_GROUP * HEAD_DIM
    NH = R * d // SLAB
    TB = NS * SLAB
    assert SLAB % d == 0 and L % TB == 0 and NS % NH == 0 and (2 * R) % (SLAB // d) == 0
    nmb, nhb, hpm = L // TB, L // (NH * SLAB), NS // NH
    views = [a.reshape(M // SLAB, SLAB, a.shape[1]) for a, _ in srcs]
    col = lambda which: srcs[which][1]

    def main(which):
        return pl.BlockSpec((NS, SLAB, GW), lambda b, t: (b * nmb + t, 0, col(which)))

    def prev(which):
        return pl.BlockSpec((NH, SLAB, GW), lambda b, t: (jnp.maximum(b * nhb + t * hpm - 1, b * nhb), 0, col(which)))

    def nxt(which):
        return pl.BlockSpec((NH, SLAB, GW),
                            lambda b, t: (jnp.minimum(b * nhb + (t + 1) * hpm, (b + 1) * nhb - 1), 0, col(which)))

    o, lse = pl.pallas_call(
        partial(_attn_slab_kernel, d, NS, L // d),
        grid=(Bsz, nmb),
        in_specs=[main(0), prev(1), main(1), nxt(1), prev(2), main(2), nxt(2)],
        out_specs=[pl.BlockSpec((NS, SLAB, GW), lambda b, t: (b * nmb + t, 0, 0)),
                   pl.BlockSpec((NS, SLAB, LANES), lambda b, t: (b * nmb + t, 0, 0))],
        out_shape=[jax.ShapeDtypeStruct((M // SLAB, SLAB, GW), BF16),
                   jax.ShapeDtypeStruct((M // SLAB, SLAB, LANES), F32)],
        scratch_shapes=[pltpu.VMEM((SLAB, NS, GW), BF16),
                        pltpu.VMEM((SLAB, NS + 2 * NH, GW), BF16), pltpu.VMEM((SLAB, NS + 2 * NH, GW), BF16),
                        pltpu.VMEM((SLAB, NS, GW), BF16), pltpu.VMEM((SLAB, NS, LANES), F32)],
        compiler_params=_params("parallel", "arbitrary"),
        name=f"dilated_attention_g{g}",
    )(views[0], views[1], views[1], views[1], views[2], views[2], views[2])
    return o.reshape(M, GW), lse.reshape(M, LANES)


def dilated_attention_group(srcs, Bsz, L, g, *, T=512):
    d = ATT_GROUPS[g][1]
    M = Bsz * L
    Ls = L // d
    R = ATT_RADIUS
    GW = HEADS_PER_GROUP * HEAD_DIM
    assert d == 1 and Ls % T == 0 and T % (2 * R) == 0
    nrb, nhb, hpt = Ls // T, Ls // R, T // R

    def main(which):
        return pl.BlockSpec((T, GW), lambda b, r, t: (b * nrb + t, srcs[which][1]))

    def prev(which):
        return pl.BlockSpec((R, GW), lambda b, r, t: (jnp.maximum(b * nhb + t * hpt - 1, b * nhb), srcs[which][1]))

    def nxt(which):
        return pl.BlockSpec((R, GW), lambda b, r, t: (jnp.minimum(b * nhb + (t + 1) * hpt, (b + 1) * nhb - 1),
                                                      srcs[which][1]))

    o, lse = pl.pallas_call(
        partial(_attn_kernel, T, Ls),
        grid=(Bsz, d, nrb),
        in_specs=[main(0), prev(1), main(1), nxt(1), prev(2), main(2), nxt(2)],
        out_specs=[pl.BlockSpec((T, GW), lambda b, r, t: (b * nrb + t, r)),
                   pl.BlockSpec((T, LANES), lambda b, r, t: (b * nrb + t, r))],
        out_shape=[jax.ShapeDtypeStruct((M // d, d * GW), BF16),
                   jax.ShapeDtypeStruct((M // d, d * LANES), F32)],
        scratch_shapes=[pltpu.VMEM((T + 2 * R, GW), BF16), pltpu.VMEM((T + 2 * R, GW), BF16)],
        compiler_params=_params("parallel", "parallel", "arbitrary"),
        name=f"dilated_attention_g{g}",
    )(srcs[0][0], srcs[1][0], srcs[1][0], srcs[1][0], srcs[2][0], srcs[2][0], srcs[2][0])
    return o.reshape(M, GW), lse.reshape(M, LANES)


def _merge_kernel(x_ref, a_ref, o0_ref, o1_ref, o2_ref, l0_ref, l1_ref, l2_ref, ga_ref, gb_ref,
                  post_g_ref, next_g_ref, whp_ref, wap_ref, wo_ref, o_ref, u_ref):
    lses = [l0_ref[...], l1_ref[...], l2_ref[...]]
    mx = jnp.maximum(jnp.maximum(lses[0], lses[1]), lses[2])
    es = [jnp.exp(l - mx) for l in lses]
    den = es[0] + es[1] + es[2]
    wts = [e / den for e in es]
    outs = [o0_ref, o1_ref, o2_ref]
    heads = []
    for h in range(HEADS_PER_GROUP):
        cols = slice(h * HEAD_DIM, (h + 1) * HEAD_DIM)
        heads.append(sum(wts[g][:, h:h + 1] * outs[g][:, cols].astype(F32) for g in range(N_GROUPS)))
    att = jnp.concatenate(heads, axis=1).astype(BF16)
    a = _dot(a_ref[...], whp_ref[...])
    b = _dot(att, wap_ref[...])
    gate_a, gate_b = ga_ref[...].astype(F32), gb_ref[...].astype(F32)
    merged = (jax.nn.sigmoid(gate_a) * a + jax.nn.sigmoid(gate_b) * b).astype(BF16)
    mix = _dot(merged, wo_ref[...])
    y = x_ref[...] + _rms(mix, post_g_ref[...])
    o_ref[...] = y
    u_ref[...] = _rms(y, next_g_ref[...]).astype(u_ref.dtype)


def merge_out(x, a_in, att_outs, att_lses, gates, post_g, next_g, whp, wap, wo, *, tm=512):
    M, D = x.shape
    const = lambda i: (0, 0)
    rows = lambda a: pl.BlockSpec((tm, a.shape[1]), lambda i: (i, 0))
    return pl.pallas_call(
        _merge_kernel,
        grid=(M // tm,),
        in_specs=[rows(x), rows(a_in)] + [rows(o) for o in att_outs] + [rows(l) for l in att_lses] + [
            pl.BlockSpec((tm, D), lambda i: (i, 0)),
            pl.BlockSpec((tm, D), lambda i: (i, 1)),
            pl.BlockSpec((1, D), const),
            pl.BlockSpec((1, D), const),
            pl.BlockSpec(whp.shape, const, pipeline_mode=pl.Buffered(1)),
            pl.BlockSpec(wap.shape, const, pipeline_mode=pl.Buffered(1)),
            pl.BlockSpec(wo.shape, const, pipeline_mode=pl.Buffered(1)),
        ],
        out_specs=[pl.BlockSpec((tm, D), lambda i: (i, 0)), pl.BlockSpec((tm, D), lambda i: (i, 0))],
        out_shape=[jax.ShapeDtypeStruct((M, D), F32), jax.ShapeDtypeStruct((M, D), BF16)],
        compiler_params=_params("parallel"),
        name="merge_out",
    )(x, a_in, *att_outs, *att_lses, gates, gates, post_g, next_g, whp, wap, wo)


def _dft_constants(L):
    N = 2 * L
    N2 = FFT_INNER
    N1 = N // N2
    h = N1 // 2
    idx1 = np.arange(N1)
    ang1 = -2.0 * np.pi * ((idx1[:, None] * idx1[None, :]) % N1) / N1
    f1r, f1i = np.cos(ang1), np.sin(ang1)
    s_data = np.block([[f1r[:, :h], -f1i[:, :h]], [f1i[:, :h], f1r[:, :h]]])
    s_filt = np.concatenate([f1r, f1i], axis=0)
    ar, ai = f1r[:h, :], -f1i[:h, :]
    t_fin = np.block([[ar, -ai], [ai, ar]])
    idx2 = np.arange(N2)
    ang2 = -2.0 * np.pi * ((idx2[:, None] * idx2[None, :]) % N2) / N2
    angt = -2.0 * np.pi * (idx1[:, None] * idx2[None, :]) / N
    return dict(
        N1=N1,
        s_data=jnp.asarray(s_data, BF16), s_filt=jnp.asarray(s_filt, BF16), t_fin=jnp.asarray(t_fin, BF16),
        f2r=jnp.asarray(np.cos(ang2), F32), f2i=jnp.asarray(np.sin(ang2), F32),
        twr=jnp.asarray(np.cos(angt).reshape(N1, 1, N2), F32),
        twi=jnp.asarray(np.sin(angt).reshape(N1, 1, N2), F32),
    )


def _hyena_in_kernel(tm, L, x_ref, prev_ref, next_ref, w_ref, cw_ref, cb_ref, o_ref, xn_ref):
    i = pl.program_id(0)
    H = prev_ref.shape[0]

    @pl.when(pl.program_id(1) == 0)
    def _():
        xn_ref[0:H] = prev_ref[...]
        xn_ref[H:H + tm] = x_ref[...]
        xn_ref[H + tm:H + tm + H] = next_ref[...]

    r = _dot(xn_ref[...], w_ref[...])
    row = lax.broadcasted_iota(jnp.int32, r.shape, 0)
    pos0 = (i * tm) % L
    outside = ((row < H) & (pos0 == 0)) | ((row >= H + tm) & (pos0 + tm == L))
    r = jnp.where(outside, 0.0, r)
    n = r.shape[0]
    y = pltpu.roll(r, 1, 0) * cw_ref[0:1, :] + r * cw_ref[1:2, :] + pltpu.roll(r, n - 1, 0) * cw_ref[2:3, :]
    o_ref[0] = (y[H:H + tm] + cb_ref[...]).astype(o_ref.dtype)


def hyena_inputs(x, w, conv_w, conv_b, L, *, tm=1024):
    M, D = x.shape
    C = w.shape[1] // 3
    H = SLAB
    nh = M // H
    assert L % tm == 0 and tm % H == 0
    return pl.pallas_call(
        partial(_hyena_in_kernel, tm, L),
        grid=(M // tm, 3),
        in_specs=[
            pl.BlockSpec((tm, D), lambda i, j: (i, 0)),
            pl.BlockSpec((H, D), lambda i, j: (jnp.maximum(i * (tm // H) - 1, 0), 0)),
            pl.BlockSpec((H, D), lambda i, j: (jnp.minimum((i + 1) * (tm // H), nh - 1), 0)),
            pl.BlockSpec((D, C), lambda i, j: (0, j)),
            pl.BlockSpec((SHORT_CONV, C), lambda i, j: (0, j)),
            pl.BlockSpec((1, C), lambda i, j: (0, j)),
        ],
        out_specs=pl.BlockSpec((1, tm, C), lambda i, j: (j, i, 0)),
        out_shape=jax.ShapeDtypeStruct((3, M, C), BF16),
        scratch_shapes=[pltpu.VMEM((tm + 2 * H, D), BF16)],
        compiler_params=_params("parallel", "arbitrary"),
        name="hyena_inputs",
    )(x, x, x, w, conv_w, conv_b)


def _filter_kernel(L, tl, w1_ref, b1_ref, w2_ref, b2_ref, w3_ref, b3_ref, w4hi_ref, w4lo_ref,
                   freq_ref, fvec_ref, delta_ref, kf_ref):
    i = pl.program_id(0)
    HALF = LANES // 2
    wide = delta_ref.shape[1]
    C = wide // (2 * HYENA_ORDER)
    row = i * tl + lax.broadcasted_iota(jnp.int32, (tl, LANES), 0)
    lane = lax.broadcasted_iota(jnp.int32, (tl, LANES), 1)
    posf = jnp.where(lane < HALF, row, L - row).astype(F32)
    t = posf / (L - 1)
    a = fvec_ref[...] * ((2.0 * math.pi / L) * posf)
    lh = lane % HALF
    feats = jnp.where(lh < FILTER_BANDS, jnp.cos(a),
                      jnp.where(lh < 2 * FILTER_BANDS, -jnp.sin(a),
                                jnp.where(lh == 2 * FILTER_BANDS, t, 0.0)))
    freq = freq_ref[...]
    dense = lambda v, w_ref, b_ref: jnp.sin(freq * (
        jnp.dot(v, w_ref[...], precision=HIGHEST, preferred_element_type=F32) + b_ref[...]))
    h = dense(dense(dense(feats, w1_ref, b1_ref), w2_ref, b2_ref), w3_ref, b3_ref)
    h_hi = h.astype(BF16)
    h_lo = (h - h_hi.astype(F32)).astype(BF16)
    taps = _dot(h_hi, w4hi_ref[...]) + _dot(h_lo, w4hi_ref[...]) + _dot(h_hi, w4lo_ref[...])
    row_w = i * tl + lax.broadcasted_iota(jnp.int32, (tl, wide), 0)
    col_w = lax.broadcasted_iota(jnp.int32, (tl, wide), 1)
    backward = col_w >= wide // 2
    t_w = jnp.where(backward, L - row_w, row_w).astype(F32) / (L - 1)
    taps = taps * jnp.exp(-t_w * delta_ref[...])
    taps = jnp.where(backward & (row_w == 0), 0.0, taps)
    for n in range(HYENA_ORDER):
        for direction in range(2):
            c0 = (direction * HYENA_ORDER + n) * C
            kf_ref[n, direction] = taps[:, c0:c0 + C].astype(kf_ref.dtype)


def hyena_filter_taps(L, fw1, fb1, fw2, fb2, fw3, fb3, fw4, ffreq, *, tl=512):
    C = D_HYENA
    H = fw2.shape[0]
    HALF = LANES // 2
    assert H <= HALF and 2 * FILTER_BANDS + 1 <= HALF

    def both(a, rows):
        blk = jnp.zeros((HALF if rows else 1, HALF), F32).at[:a.shape[0], :a.shape[1]].set(a)
        if not rows:
            return jnp.concatenate([blk, blk], axis=1)
        zero = jnp.zeros_like(blk)
        return jnp.concatenate([jnp.concatenate([blk, zero], axis=1), jnp.concatenate([zero, blk], axis=1)], axis=0)

    w1 = both(jnp.concatenate([fw1[1:], fw1[:1]], axis=0), True)
    w2, w3 = both(fw2, True), both(fw3, True)
    b1, b2, b3, freq = (both(v[None, :], False) for v in (fb1, fb2, fb3, ffreq))
    w4 = fw4.reshape(H, HYENA_ORDER, 2, C)
    wide = 2 * HYENA_ORDER * C
    w4p = jnp.zeros((LANES, wide), F32)
    w4p = w4p.at[:H, :wide // 2].set(w4[:, :, 0].reshape(H, HYENA_ORDER * C))
    w4p = w4p.at[HALF:HALF + H, wide // 2:].set(w4[:, :, 1].reshape(H, HYENA_ORDER * C))
    w4hi = w4p.astype(BF16)
    w4lo = (w4p - w4hi.astype(F32)).astype(BF16)
    bands = np.linspace(1e-4, FILTER_BANDS - 1, FILTER_BANDS, dtype=np.float32)
    fvec = np.zeros((1, LANES), np.float32)
    for base in (0, HALF):
        fvec[0, base:base + FILTER_BANDS] = bands
        fvec[0, base + FILTER_BANDS:base + 2 * FILTER_BANDS] = bands
    max_decay = math.log(DECAY_TARGET) / FAST_DECAY_PCT
    min_decay = math.log(DECAY_TARGET) / SLOW_DECAY_PCT
    deltas = np.abs(np.linspace(min_decay, max_decay, C, dtype=np.float32))
    delta_w = np.tile(deltas[None, :], (1, 2 * HYENA_ORDER))
    const = lambda i: (0, 0)
    args = (w1, b1, w2, b2, w3, b3, w4hi, w4lo, freq, jnp.asarray(fvec), jnp.asarray(delta_w))
    return pl.pallas_call(
        partial(_filter_kernel, L, tl),
        grid=(L // tl,),
        in_specs=[pl.BlockSpec(a.shape, const) for a in args],
        out_specs=pl.BlockSpec((HYENA_ORDER, 2, tl, C), lambda i: (0, 0, i, 0)),
        out_shape=jax.ShapeDtypeStruct((HYENA_ORDER, 2, L, C), BF16),
        compiler_params=_params("parallel"),
        name="hyena_filter_taps",
    )(*args)


FFT_ROWS = 16
FFT_COLS = 512
FFT_COLS_IN = 1024


def _stage1_kernel(s_ref, x_ref, y_ref):
    _, h, rows, cols = x_ref.shape
    s = s_ref[...]
    xt = pltpu.einshape("abc->bac", x_ref[...].reshape(2 * h, rows, cols))
    yt = jnp.stack([_dot(s, xt[b]).astype(y_ref.dtype) for b in range(rows)], axis=0)
    y_ref[...] = pltpu.einshape("bac->abc", yt).reshape(y_ref.shape)


def fft_stage1(x, which, s):
    _, _, h, N2, C = x.shape
    N1 = 2 * h
    return pl.pallas_call(
        _stage1_kernel,
        grid=(N2 // FFT_ROWS, C // FFT_COLS_IN),
        in_specs=[pl.BlockSpec((2 * N1, N1), lambda j, c: (0, 0)),
                  pl.BlockSpec((None, 2, h, FFT_ROWS, FFT_COLS_IN), lambda j, c: (which, 0, 0, j, c))],
        out_specs=pl.BlockSpec((2, N1, FFT_ROWS, FFT_COLS_IN), lambda j, c: (0, 0, j, c)),
        out_shape=jax.ShapeDtypeStruct((2, N1, N2, C), BF16),
        compiler_params=_params("parallel", "parallel"),
        name="fft_stage1",
    )(s, x)


FFT_K1_PER_STEP = 4


def _mid_kernel(inv_n, f2r_ref, f2i_ref, twr_ref, twi_ref, yf_ref, y_ref, u_ref):
    f2r, f2i = f2r_ref[...], f2i_ref[...]
    for k in range(y_ref.shape[1]):
        twr, twi = twr_ref[k], twi_ref[k]
        gr, gi = f2r * twr - f2i * twi, f2r * twi + f2i * twr
        grb, gib = gr.astype(BF16), gi.astype(BF16)

        def inner(ref):
            re, im = ref[0, k], ref[1, k]
            return _dot(grb, re) - _dot(gib, im), _dot(gib, re) + _dot(grb, im)

        hr, hi = inner(yf_ref)
        zr, zi = inner(y_ref)
        pr = (zr * hr - zi * hi).astype(BF16)
        pi = (zr * hi + zi * hr).astype(BF16)
        irb = (gr.T * inv_n).astype(BF16)
        iib = (gi.T * (-inv_n)).astype(BF16)
        u_ref[0, k] = (_dot(irb, pr) - _dot(iib, pi)).astype(u_ref.dtype)
        u_ref[1, k] = (_dot(iib, pr) + _dot(irb, pi)).astype(u_ref.dtype)


def fft_mid(yf, y, dc):
    _, N1, N2, C = y.shape
    kb = FFT_K1_PER_STEP
    blk = pl.BlockSpec((2, kb, N2, C), lambda k: (0, k, 0, 0))
    const = pl.BlockSpec((N2, N2), lambda k: (0, 0))
    tw = pl.BlockSpec((kb, 1, N2), lambda k: (k, 0, 0))
    return pl.pallas_call(
        partial(_mid_kernel, 1.0 / (N1 * N2)),
        grid=(N1 // kb,),
        in_specs=[const, const, tw, tw, blk, blk],
        out_specs=blk,
        out_shape=jax.ShapeDtypeStruct((2, N1, N2, C), BF16),
        compiler_params=_params("parallel"),
        name="fft_mid",
    )(dc['f2r'], dc['f2i'], dc['twr'], dc['twi'], yf, y)


def _final_kernel(t_ref, u_ref, z_ref, gate_ref, bias_ref, o_ref):
    _, n1, rows, cols = u_ref.shape
    t = t_ref[...]
    ut = pltpu.einshape("abc->bac", u_ref[...].reshape(2 * n1, rows, cols))
    convt = jnp.stack([_dot(t, ut[b]).astype(BF16) for b in range(rows)], axis=0)
    conv = pltpu.einshape("bac->abc", convt).reshape(o_ref.shape).astype(F32)
    z = z_ref[...].astype(F32)
    o_ref[...] = (gate_ref[...].astype(F32) * (conv + bias_ref[...] * z)).astype(o_ref.dtype)


def fft_final(u, t_fin, z, z_which, gate, gate_which, bias, bias_which):
    _, N1, N2, C = u.shape
    h = N1 // 2

    def half(which):
        return pl.BlockSpec((None, 2, h, FFT_ROWS, FFT_COLS), lambda j, c: (which, 0, 0, j, c))

    return pl.pallas_call(
        _final_kernel,
        grid=(N2 // FFT_ROWS, C // FFT_COLS),
        in_specs=[pl.BlockSpec((N1, 2 * N1), lambda j, c: (0, 0)),
                  pl.BlockSpec((2, N1, FFT_ROWS, FFT_COLS), lambda j, c: (0, 0, j, c)),
                  half(z_which), half(gate_which),
                  pl.BlockSpec((None, 1, FFT_COLS), lambda j, c: (bias_which, 0, c))],
        out_specs=half(0),
        out_shape=jax.ShapeDtypeStruct((1, 2, h, N2, C), BF16),
        compiler_params=_params("parallel", "parallel"),
        name="fft_final",
    )(t_fin, u, z, gate, bias.reshape(bias.shape[0], 1, C))


def hyena_branch(zs, L, p):
    _, M, C = zs.shape
    assert M == 2 * L, "the batch pair rides as real/imaginary parts"
    dc = _dft_constants(L)
    split = (2, dc['N1'] // 2, FFT_INNER, C)
    taps = hyena_filter_taps(L, *p['filt']).reshape(HYENA_ORDER, *split)
    zs = zs.reshape(3, *split)
    z, z_which = zs, 0
    for n in range(HYENA_ORDER):
        u = fft_mid(fft_stage1(taps, n, dc['s_filt']), fft_stage1(z, z_which, dc['s_data']), dc)
        z, z_which = fft_final(u, dc['t_fin'], z, z_which, zs, n + 1, p['hy_bias'], n), 0
    return z.reshape(M, C)


def _split_w_in(w):
    c_q = 3 * D_HYENA
    c_v = c_q + 2 * D_ATT
    c_g = c_v + D_ATT
    return tuple(w[:, a:b].astype(BF16) for a, b in ((0, c_q), (c_q, c_v), (c_v, c_g), (c_g, w.shape[1])))


def _layer(x, p):
    Bsz, L, D = x.shape
    M = Bsz * L
    x0 = x.reshape(M, D)
    x1, u = ffn_block(x0, p['ffn1_pre_g'], p['ffn1_post_g'], p['ffn1_w_gate'], p['ffn1_w_up'], p['ffn1_w_down'],
                      next_g=p['mix_pre_g'])
    w_hy, w_qk, w_v, w_gate = p['w_in']
    hy_in = hyena_inputs(u, w_hy, p['hy_conv_w'], p['hy_conv_b'], L)
    qk = projection(u, w_qk, BF16, tm=1024, tn=D_ATT, rope_len=L, rope=p['rope'], name="inproj_qk")
    v = projection(u, w_v, BF16, tm=1024, tn=D_ATT, name="inproj_v")
    gates = projection(u, w_gate, BF16, tm=1024, tn=D_MODEL, name="inproj_gates")
    a2 = hyena_branch(hy_in, L, p)
    att = []
    for g in range(N_GROUPS):
        srcs = ((qk, g), (qk, N_GROUPS + g), (v, g))
        fn = dilated_attention_group if ATT_GROUPS[g][1] == 1 else dilated_attention_slabs
        att.append(fn(srcs, Bsz, L, g))
    x2, xn2 = merge_out(x1, a2, [o for o, _ in att], [l for _, l in att], gates, p['mix_post_g'], p['ffn2_pre_g'],
                        p['w_hy_proj'], p['w_att_proj'], p['w_out'])
    x3 = ffn_block(x2, p['ffn2_pre_g'], p['ffn2_post_g'], p['ffn2_w_gate'], p['ffn2_w_up'], p['ffn2_w_down'],
                   xn=xn2)
    return x3.reshape(Bsz, L, D)


def kernel(x_prompt, x_sample, ffn1_pre_g, ffn1_post_g, ffn1_w_gate, ffn1_w_up, ffn1_w_down, mix_pre_g, mix_post_g, w_in, hy_conv_w, hy_conv_b, filt_w1, filt_b1, filt_w2, filt_b2, filt_w3, filt_b3, filt_w4, filt_freq, hy_bias, w_hy_proj, w_att_proj, w_out, ffn2_pre_g, ffn2_post_g, ffn2_w_gate, ffn2_w_up, ffn2_w_down):
    assert ffn1_w_gate.shape[0] == 1
    p = {
        'ffn1_pre_g': ffn1_pre_g, 'ffn1_post_g': ffn1_post_g,
        'ffn1_w_gate': ffn1_w_gate[0].astype(BF16), 'ffn1_w_up': ffn1_w_up[0].astype(BF16),
        'ffn1_w_down': ffn1_w_down[0].astype(BF16),
        'mix_pre_g': mix_pre_g, 'mix_post_g': mix_post_g,
        'w_in': _split_w_in(w_in[0]),
        'hy_conv_w': hy_conv_w[0], 'hy_conv_b': hy_conv_b,
        'filt': (filt_w1[0], filt_b1[0], filt_w2[0], filt_b2[0], filt_w3[0], filt_b3[0], filt_w4[0], filt_freq[0]),
        'hy_bias': hy_bias[0],
        'w_hy_proj': w_hy_proj[0].astype(BF16), 'w_att_proj': w_att_proj[0].astype(BF16),
        'w_out': w_out[0].astype(BF16),
        'ffn2_pre_g': ffn2_pre_g, 'ffn2_post_g': ffn2_post_g,
        'ffn2_w_gate': ffn2_w_gate[0].astype(BF16), 'ffn2_w_up': ffn2_w_up[0].astype(BF16),
        'ffn2_w_down': ffn2_w_down[0].astype(BF16),
    }
    p['rope'] = rope_tables(max(x_prompt.shape[1], x_sample.shape[1]))
    return (_layer(x_prompt, p), _layer(x_sample, p))
```

```python
import math
from functools import partial

import numpy as np
import jax
import jax.numpy as jnp
from jax import lax
from jax.experimental import pallas as pl
from jax.experimental.pallas import tpu as pltpu

D_MODEL = 2048
D_HYENA = 1024
HYENA_ORDER = 2
SHORT_CONV = 3
FILTER_EMB = 33
FILTER_BANDS = (FILTER_EMB - 1) // 2
FAST_DECAY_PCT = 0.3
SLOW_DECAY_PCT = 1.5
DECAY_TARGET = 1e-2
HEAD_DIM = 128
HEADS_PER_GROUP = 4
ATT_GROUPS = ((128, 1), (512, 4), (2048, 16))
N_GROUPS = len(ATT_GROUPS)
D_ATT = N_GROUPS * HEADS_PER_GROUP * HEAD_DIM
D_ATT_OUT = HEADS_PER_GROUP * HEAD_DIM
ROPE_DIM = HEAD_DIM // 4
ROPE_THETA = 500000.0
N_BRANCH = 2
D_IN_PROJ = 3 * D_HYENA + 3 * D_ATT + N_BRANCH * D_MODEL
D_FF = 5632
EPS = 1e-6
NEG_INF = -1e30

LANES = 128
FFT_INNER = 256
VMEM_BYTES_V7X = 64 * 1024 * 1024
VMEM_LIMIT_BYTES = VMEM_BYTES_V7X - 4 * 1024 * 1024
BF16 = jnp.bfloat16
F32 = jnp.float32
HIGHEST = lax.Precision.HIGHEST


def _rms(x, g):
    return x * lax.rsqrt(jnp.mean(x * x, axis=-1, keepdims=True) + EPS) * g


def _dot(a, b):
    return jnp.dot(a, b, preferred_element_type=F32)


def _params(*sem):
    return pltpu.CompilerParams(dimension_semantics=sem, vmem_limit_bytes=VMEM_LIMIT_BYTES)


FFN_SPLIT = 2


def _ffn_kernel(norm_in, norm_out, x_ref, xn_src_ref, post_g_ref, next_g_ref, wg_ref, wu_ref, wd_ref, *rest):
    o_ref = rest[0]
    u_ref = rest[1] if norm_out else None
    xn_ref = rest[-1] if norm_in else xn_src_ref
    j = pl.program_id(1)
    tf = wg_ref.shape[1]

    @pl.when(j == 0)
    def _():
        if norm_in:
            xn_ref[...] = _rms(x_ref[...], xn_src_ref[...]).astype(BF16)
        o_ref[...] = jnp.zeros_like(o_ref)

    xn = xn_ref[...]
    hs = []
    for c in range(FFN_SPLIT):
        cols = slice(c * (tf // FFN_SPLIT), (c + 1) * (tf // FFN_SPLIT))
        gate = _dot(xn, wg_ref[:, cols])
        up = _dot(xn, wu_ref[:, cols])
        hs.append((gate * jax.nn.sigmoid(gate) * up).astype(BF16))
    o_ref[...] += _dot(jnp.concatenate(hs, axis=1), wd_ref[...])

    @pl.when(j == pl.num_programs(1) - 1)
    def _():
        y = x_ref[...] + 0.5 * _rms(o_ref[...], post_g_ref[...])
        o_ref[...] = y
        if norm_out:
            u_ref[...] = _rms(y, next_g_ref[...]).astype(u_ref.dtype)


def ffn_block(x, pre_g, post_g, wg, wu, wd, *, xn=None, next_g=None, tm=512, tf=512):
    M, D = x.shape
    FF = wg.shape[1]
    norm_in, norm_out = xn is None, next_g is not None
    rows = pl.BlockSpec((tm, D), lambda i, j: (i, 0))
    gain = pl.BlockSpec((1, D), lambda i, j: (0, 0))
    out = pl.pallas_call(
        partial(_ffn_kernel, norm_in, norm_out),
        grid=(M // tm, FF // tf),
        in_specs=[
            rows,
            gain if norm_in else rows,
            gain, gain,
            pl.BlockSpec((D, tf), lambda i, j: (0, j)),
            pl.BlockSpec((D, tf), lambda i, j: (0, j)),
            pl.BlockSpec((tf, D), lambda i, j: (j, 0)),
        ],
        out_specs=[rows, rows] if norm_out else [rows],
        out_shape=[jax.ShapeDtypeStruct((M, D), F32)] + ([jax.ShapeDtypeStruct((M, D), BF16)] if norm_out else []),
        scratch_shapes=[pltpu.VMEM((tm, D), BF16)] if norm_in else [],
        compiler_params=_params("parallel", "arbitrary"),
        name="ffn_block",
    )(x, pre_g if norm_in else xn, post_g, next_g if norm_out else post_g, wg, wu, wd)
    return out if norm_out else out[0]


def rope_tables(L):
    half = ROPE_DIM // 2
    inv_freq = jnp.power(ROPE_THETA, -jnp.arange(half, dtype=F32) / half)
    ang = jnp.arange(L, dtype=F32)[:, None] * inv_freq[None, :]
    cos, sin = jnp.cos(ang), jnp.sin(ang)
    rest = HEAD_DIM - ROPE_DIM
    c = jnp.concatenate([cos, cos, jnp.ones((L, rest), F32)], axis=1)
    s_lo = jnp.concatenate([-sin, jnp.zeros((L, half + rest), F32)], axis=1)
    s_hi = jnp.concatenate([jnp.zeros((L, half), F32), sin, jnp.zeros((L, rest), F32)], axis=1)
    scale = HEAD_DIM ** -0.5
    return jnp.stack([c * scale, c]), jnp.stack([s_lo * scale, s_lo]), jnp.stack([s_hi * scale, s_hi])


def _proj_kernel(rotate, xn_ref, w_ref, *rest):
    o_ref = rest[-1]
    r = _dot(xn_ref[...], w_ref[...])
    if rotate:
        c_ref, slo_ref, shi_ref = rest[:3]
        tn = r.shape[1]
        half = ROPE_DIM // 2
        wide = lambda t_ref: jnp.concatenate([t_ref[...]] * (tn // HEAD_DIM), axis=1)
        r = r * wide(c_ref) + pltpu.roll(r, tn - half, 1) * wide(slo_ref) + pltpu.roll(r, half, 1) * wide(shi_ref)
    o_ref[...] = r.astype(o_ref.dtype)


def projection(xn, w, out_dtype, *, tm, tn, rope_len=None, rope=None, name):
    M, D = xn.shape
    N = w.shape[1]
    in_specs = [
        pl.BlockSpec((tm, D), lambda i, j: (i, 0)),
        pl.BlockSpec((D, tn), lambda i, j: (0, j)),
    ]
    args = [xn, w]
    if rope_len is not None:
        assert tn == D_ATT and N == 2 * D_ATT and rope_len % tm == 0
        spec = pl.BlockSpec((None, tm, HEAD_DIM), lambda i, j: (j, i % (rope_len // tm), 0))
        in_specs += [spec, spec, spec]
        assert all(t.shape[1] >= rope_len for t in rope)
        args += list(rope)
    return pl.pallas_call(
        partial(_proj_kernel, rope_len is not None),
        grid=(M // tm, N // tn),
        in_specs=in_specs,
        out_specs=pl.BlockSpec((tm, tn), lambda i, j: (i, j)),
        out_shape=jax.ShapeDtypeStruct((M, N), out_dtype),
        compiler_params=_params("parallel", "arbitrary"),
        name=name,
    )(*args)


ATT_RADIUS = 64
assert all(w // (2 * d) == ATT_RADIUS for w, d in ATT_GROUPS)


def _attn_kernel(T, Ls, q_ref, kp_ref, kc_ref, kn_ref, vp_ref, vc_ref, vn_ref, o_ref, lse_ref, kbuf, vbuf):
    t = pl.program_id(2)
    R = ATT_RADIUS
    SB = 2 * R
    for buf, prv, cur, nxt in ((kbuf, kp_ref, kc_ref, kn_ref), (vbuf, vp_ref, vc_ref, vn_ref)):
        buf[0:R] = prv[...]
        buf[R:R + T] = cur[...]
        buf[R + T:R + T + R] = nxt[...]
    qi = lax.broadcasted_iota(jnp.int32, (SB, 2 * SB), 0)
    kk = lax.broadcasted_iota(jnp.int32, (SB, 2 * SB), 1)
    band = (kk >= qi) & (kk <= qi + 2 * R)
    lane = lax.broadcasted_iota(jnp.int32, (SB, LANES), 1)
    for sb in range(T // SB):
        kpos = t * T + (sb * SB - R) + kk
        mask = band & (kpos >= 0) & (kpos < Ls)
        lse = jnp.zeros((SB, LANES), F32)
        for h in range(HEADS_PER_GROUP):
            cols = slice(h * HEAD_DIM, (h + 1) * HEAD_DIM)
            q = q_ref[sb * SB:(sb + 1) * SB, cols]
            k = kbuf[sb * SB:(sb + 2) * SB, cols]
            v = vbuf[sb * SB:(sb + 2) * SB, cols]
            s = lax.dot_general(q, k, (((1,), (1,)), ((), ())), preferred_element_type=F32)
            s = jnp.where(mask, s, NEG_INF)
            m = jnp.max(s, axis=1, keepdims=True)
            p = jnp.exp(s - m)
            l = jnp.sum(p, axis=1, keepdims=True)
            o = _dot(p.astype(BF16), v) / l
            o_ref[sb * SB:(sb + 1) * SB, cols] = o.astype(o_ref.dtype)
            lse = jnp.where(lane == h, m + jnp.log(l), lse)
        lse_ref[sb * SB:(sb + 1) * SB, :] = lse


SLAB = 16


def _softmax_heads(q, k, v, mask):
    lane = lax.broadcasted_iota(jnp.int32, (q.shape[0], LANES), 1)
    lse = jnp.zeros((q.shape[0], LANES), F32)
    outs = []
    for h in range(HEADS_PER_GROUP):
        cols = slice(h * HEAD_DIM, (h + 1) * HEAD_DIM)
        s = lax.dot_general(q[:, cols], k[:, cols], (((1,), (1,)), ((), ())), preferred_element_type=F32)
        s = jnp.where(mask, s, NEG_INF)
        m = jnp.max(s, axis=1, keepdims=True)
        p = jnp.exp(s - m)
        l = jnp.sum(p, axis=1, keepdims=True)
        outs.append(_dot(p.astype(BF16), v[:, cols]) / l)
        lse = jnp.where(lane == h, m + jnp.log(l), lse)
    return jnp.concatenate(outs, axis=1), lse


def _attn_slab_kernel(d, NS, Ls, q_ref, kp_ref, kc_ref, kn_ref, vp_ref, vc_ref, vn_ref, o_ref, lse_ref,
                      qs, ks, vs, os_, ls):
    t = pl.program_id(1)
    R = ATT_RADIUS
    SB = 2 * R
    J = SLAB // d
    NH = kp_ref.shape[0]
    QA = SB // J
    KA = QA + 2 * NH
    n_sb = NS // QA
    to_class_major = lambda ref: pltpu.einshape("abc->bac", ref[...])
    qs[...] = to_class_major(q_ref)
    for buf, prv, cur, nxt in ((ks, kp_ref, kc_ref, kn_ref), (vs, vp_ref, vc_ref, vn_ref)):
        buf[:, 0:NH] = to_class_major(prv)
        buf[:, NH:NH + NS] = to_class_major(cur)
        buf[:, NH + NS:NH + NS + NH] = to_class_major(nxt)
    qi = lax.broadcasted_iota(jnp.int32, (SB, 2 * SB), 0)
    kk = lax.broadcasted_iota(jnp.int32, (SB, 2 * SB), 1)
    sq_rel = J * (qi % QA) + qi // QA
    sk_rel = J * (kk % KA - NH) + kk // KA
    band = jnp.abs(sk_rel - sq_rel) <= R

    def body(it, carry):
        r = it // n_sb
        a0 = pl.multiple_of((it % n_sb) * QA, QA)
        sk = J * (t * NS + a0) + sk_rel
        mask = band & (sk >= 0) & (sk < Ls)
        gather = lambda buf, n: jnp.concatenate([buf[j * d + r, pl.ds(a0, n), :] for j in range(J)], axis=0)
        o, lse = _softmax_heads(gather(qs, QA), gather(ks, KA), gather(vs, KA), mask)
        o = o.astype(os_.dtype)
        for j in range(J):
            os_[j * d + r, pl.ds(a0, QA), :] = o[j * QA:(j + 1) * QA]
            ls[j * d + r, pl.ds(a0, QA), :] = lse[j * QA:(j + 1) * QA]
        return carry

    lax.fori_loop(0, d * n_sb, body, 0, unroll=True)
    o_ref[...] = pltpu.einshape("bac->abc", os_[...])
    lse_ref[...] = pltpu.einshape("bac->abc", ls[...])


def dilated_attention_slabs(srcs, Bsz, L, g, *, NS=128):
    d = ATT_GROUPS[g][1]
    M = Bsz * L
    R = ATT_RADIUS
    GW = HEADS_PER_GROUP * HEAD_DIM
    NH = R * d // SLAB
    TB = NS * SLAB
    assert SLAB % d == 0 and L % TB == 0 and NS % NH == 0 and (2 * R) % (SLAB // d) == 0
    nmb, nhb, hpm = L // TB, L // (NH * SLAB), NS // NH
    views = [a.reshape(M // SLAB, SLAB, a.shape[1]) for a, _ in srcs]
    col = lambda which: srcs[which][1]

    def main(which):
        return pl.BlockSpec((NS, SLAB, GW), lambda b, t: (b * nmb + t, 0, col(which)))

    def prev(which):
        return pl.BlockSpec((NH, SLAB, GW), lambda b, t: (jnp.maximum(b * nhb + t * hpm - 1, b * nhb), 0, col(which)))

    def nxt(which):
        return pl.BlockSpec((NH, SLAB, GW),
                            lambda b, t: (jnp.minimum(b * nhb + (t + 1) * hpm, (b + 1) * nhb - 1), 0, col(which)))

    o, lse = pl.pallas_call(
        partial(_attn_slab_kernel, d, NS, L // d),
        grid=(Bsz, nmb),
        in_specs=[main(0), prev(1), main(1), nxt(1), prev(2), main(2), nxt(2)],
        out_specs=[pl.BlockSpec((NS, SLAB, GW), lambda b, t: (b * nmb + t, 0, 0)),
                   pl.BlockSpec((NS, SLAB, LANES), lambda b, t: (b * nmb + t, 0, 0))],
        out_shape=[jax.ShapeDtypeStruct((M // SLAB, SLAB, GW), BF16),
                   jax.ShapeDtypeStruct((M // SLAB, SLAB, LANES), F32)],
        scratch_shapes=[pltpu.VMEM((SLAB, NS, GW), BF16),
                        pltpu.VMEM((SLAB, NS + 2 * NH, GW), BF16), pltpu.VMEM((SLAB, NS + 2 * NH, GW), BF16),
                        pltpu.VMEM((SLAB, NS, GW), BF16), pltpu.VMEM((SLAB, NS, LANES), F32)],
        compiler_params=_params("parallel", "arbitrary"),
        name=f"dilated_attention_g{g}",
    )(views[0], views[1], views[1], views[1], views[2], views[2], views[2])
    return o.reshape(M, GW), lse.reshape(M, LANES)


def dilated_attention_group(srcs, Bsz, L, g, *, T=1024):
    d = ATT_GROUPS[g][1]
    M = Bsz * L
    Ls = L // d
    R = ATT_RADIUS
    GW = HEADS_PER_GROUP * HEAD_DIM
    assert d == 1 and Ls % T == 0 and T % (2 * R) == 0
    nrb, nhb, hpt = Ls // T, Ls // R, T // R

    def main(which):
        return pl.BlockSpec((T, GW), lambda b, r, t: (b * nrb + t, srcs[which][1]))

    def prev(which):
        return pl.BlockSpec((R, GW), lambda b, r, t: (jnp.maximum(b * nhb + t * hpt - 1, b * nhb), srcs[which][1]))

    def nxt(which):
        return pl.BlockSpec((R, GW), lambda b, r, t: (jnp.minimum(b * nhb + (t + 1) * hpt, (b + 1) * nhb - 1),
                                                      srcs[which][1]))

    o, lse = pl.pallas_call(
        partial(_attn_kernel, T, Ls),
        grid=(Bsz, d, nrb),
        in_specs=[main(0), prev(1), main(1), nxt(1), prev(2), main(2), nxt(2)],
        out_specs=[pl.BlockSpec((T, GW), lambda b, r, t: (b * nrb + t, r)),
                   pl.BlockSpec((T, LANES), lambda b, r, t: (b * nrb + t, r))],
        out_shape=[jax.ShapeDtypeStruct((M // d, d * GW), BF16),
                   jax.ShapeDtypeStruct((M // d, d * LANES), F32)],
        scratch_shapes=[pltpu.VMEM((T + 2 * R, GW), BF16), pltpu.VMEM((T + 2 * R, GW), BF16)],
        compiler_params=_params("parallel", "parallel", "arbitrary"),
        name=f"dilated_attention_g{g}",
    )(srcs[0][0], srcs[1][0], srcs[1][0], srcs[1][0], srcs[2][0], srcs[2][0], srcs[2][0])
    return o.reshape(M, GW), lse.reshape(M, LANES)


def _merge_kernel(x_ref, a_ref, o0_ref, o1_ref, o2_ref, l0_ref, l1_ref, l2_ref, ga_ref, gb_ref,
                  post_g_ref, next_g_ref, whp_ref, wap_ref, wo_ref, o_ref, u_ref):
    lses = [l0_ref[...], l1_ref[...], l2_ref[...]]
    mx = jnp.maximum(jnp.maximum(lses[0], lses[1]), lses[2])
    es = [jnp.exp(l - mx) for l in lses]
    den = es[0] + es[1] + es[2]
    wts = [e / den for e in es]
    outs = [o0_ref, o1_ref, o2_ref]
    heads = []
    for h in range(HEADS_PER_GROUP):
        cols = slice(h * HEAD_DIM, (h + 1) * HEAD_DIM)
        heads.append(sum(wts[g][:, h:h + 1] * outs[g][:, cols].astype(F32) for g in range(N_GROUPS)))
    att = jnp.concatenate(heads, axis=1).astype(BF16)
    a = _dot(a_ref[...], whp_ref[...])
    b = _dot(att, wap_ref[...])
    gate_a, gate_b = ga_ref[...].astype(F32), gb_ref[...].astype(F32)
    merged = (jax.nn.sigmoid(gate_a) * a + jax.nn.sigmoid(gate_b) * b).astype(BF16)
    mix = _dot(merged, wo_ref[...])
    y = x_ref[...] + _rms(mix, post_g_ref[...])
    o_ref[...] = y
    u_ref[...] = _rms(y, next_g_ref[...]).astype(u_ref.dtype)


def merge_out(x, a_in, att_outs, att_lses, gates, post_g, next_g, whp, wap, wo, *, tm=512):
    M, D = x.shape
    const = lambda i: (0, 0)
    rows = lambda a: pl.BlockSpec((tm, a.shape[1]), lambda i: (i, 0))
    return pl.pallas_call(
        _merge_kernel,
        grid=(M // tm,),
        in_specs=[rows(x), rows(a_in)] + [rows(o) for o in att_outs] + [rows(l) for l in att_lses] + [
            pl.BlockSpec((tm, D), lambda i: (i, 0)),
            pl.BlockSpec((tm, D), lambda i: (i, 1)),
            pl.BlockSpec((1, D), const),
            pl.BlockSpec((1, D), const),
            pl.BlockSpec(whp.shape, const, pipeline_mode=pl.Buffered(1)),
            pl.BlockSpec(wap.shape, const, pipeline_mode=pl.Buffered(1)),
            pl.BlockSpec(wo.shape, const, pipeline_mode=pl.Buffered(1)),
        ],
        out_specs=[pl.BlockSpec((tm, D), lambda i: (i, 0)), pl.BlockSpec((tm, D), lambda i: (i, 0))],
        out_shape=[jax.ShapeDtypeStruct((M, D), F32), jax.ShapeDtypeStruct((M, D), BF16)],
        compiler_params=_params("parallel"),
        name="merge_out",
    )(x, a_in, *att_outs, *att_lses, gates, gates, post_g, next_g, whp, wap, wo)


def _dft_constants(L):
    N = 2 * L
    N2 = FFT_INNER
    N1 = N // N2
    h = N1 // 2
    idx1 = np.arange(N1)
    ang1 = -2.0 * np.pi * ((idx1[:, None] * idx1[None, :]) % N1) / N1
    f1r, f1i = np.cos(ang1), np.sin(ang1)
    s_data = np.block([[f1r[:, :h], -f1i[:, :h]], [f1i[:, :h], f1r[:, :h]]])
    s_filt = np.concatenate([f1r, f1i], axis=0)
    ar, ai = f1r[:h, :], -f1i[:h, :]
    t_fin = np.block([[ar, -ai], [ai, ar]])
    idx2 = np.arange(N2)
    ang2 = -2.0 * np.pi * ((idx2[:, None] * idx2[None, :]) % N2) / N2
    angt = -2.0 * np.pi * (idx1[:, None] * idx2[None, :]) / N
    return dict(
        N1=N1,
        s_data=jnp.asarray(s_data, BF16), s_filt=jnp.asarray(s_filt, BF16), t_fin=jnp.asarray(t_fin, BF16),
        f2r=jnp.asarray(np.cos(ang2), F32), f2i=jnp.asarray(np.sin(ang2), F32),
        twr=jnp.asarray(np.cos(angt).reshape(N1, 1, N2), F32),
        twi=jnp.asarray(np.sin(angt).reshape(N1, 1, N2), F32),
    )


def _hyena_in_kernel(tm, L, x_ref, prev_ref, next_ref, w_ref, cw_ref, cb_ref, o_ref, xn_ref):
    i = pl.program_id(0)
    H = prev_ref.shape[0]

    @pl.when(pl.program_id(1) == 0)
    def _():
        xn_ref[0:H] = prev_ref[...]
        xn_ref[H:H + tm] = x_ref[...]
        xn_ref[H + tm:H + tm + H] = next_ref[...]

    r = _dot(xn_ref[...], w_ref[...])
    row = lax.broadcasted_iota(jnp.int32, r.shape, 0)
    pos0 = (i * tm) % L
    outside = ((row < H) & (pos0 == 0)) | ((row >= H + tm) & (pos0 + tm == L))
    r = jnp.where(outside, 0.0, r)
    n = r.shape[0]
    y = pltpu.roll(r, 1, 0) * cw_ref[0:1, :] + r * cw_ref[1:2, :] + pltpu.roll(r, n - 1, 0) * cw_ref[2:3, :]
    o_ref[0] = (y[H:H + tm] + cb_ref[...]).astype(o_ref.dtype)


def hyena_inputs(x, w, conv_w, conv_b, L, *, tm=1024):
    M, D = x.shape
    C = w.shape[1] // 3
    H = SLAB
    nh = M // H
    assert L % tm == 0 and tm % H == 0
    return pl.pallas_call(
        partial(_hyena_in_kernel, tm, L),
        grid=(M // tm, 3),
        in_specs=[
            pl.BlockSpec((tm, D), lambda i, j: (i, 0)),
            pl.BlockSpec((H, D), lambda i, j: (jnp.maximum(i * (tm // H) - 1, 0), 0)),
            pl.BlockSpec((H, D), lambda i, j: (jnp.minimum((i + 1) * (tm // H), nh - 1), 0)),
            pl.BlockSpec((D, C), lambda i, j: (0, j)),
            pl.BlockSpec((SHORT_CONV, C), lambda i, j: (0, j)),
            pl.BlockSpec((1, C), lambda i, j: (0, j)),
        ],
        out_specs=pl.BlockSpec((1, tm, C), lambda i, j: (j, i, 0)),
        out_shape=jax.ShapeDtypeStruct((3, M, C), BF16),
        scratch_shapes=[pltpu.VMEM((tm + 2 * H, D), BF16)],
        compiler_params=_params("parallel", "arbitrary"),
        name="hyena_inputs",
    )(x, x, x, w, conv_w, conv_b)


def _filter_kernel(L, tl, w1_ref, b1_ref, w2_ref, b2_ref, w3_ref, b3_ref, w4hi_ref, w4lo_ref,
                   freq_ref, fvec_ref, delta_ref, kf_ref):
    i = pl.program_id(0)
    HALF = LANES // 2
    wide = delta_ref.shape[1]
    C = wide // (2 * HYENA_ORDER)
    row = i * tl + lax.broadcasted_iota(jnp.int32, (tl, LANES), 0)
    lane = lax.broadcasted_iota(jnp.int32, (tl, LANES), 1)
    posf = jnp.where(lane < HALF, row, L - row).astype(F32)
    t = posf / (L - 1)
    a = fvec_ref[...] * ((2.0 * math.pi / L) * posf)
    lh = lane % HALF
    feats = jnp.where(lh < FILTER_BANDS, jnp.cos(a),
                      jnp.where(lh < 2 * FILTER_BANDS, -jnp.sin(a),
                                jnp.where(lh == 2 * FILTER_BANDS, t, 0.0)))
    freq = freq_ref[...]
    dense = lambda v, w_ref, b_ref: jnp.sin(freq * (
        jnp.dot(v, w_ref[...], precision=HIGHEST, preferred_element_type=F32) + b_ref[...]))
    h = dense(dense(dense(feats, w1_ref, b1_ref), w2_ref, b2_ref), w3_ref, b3_ref)
    h_hi = h.astype(BF16)
    h_lo = (h - h_hi.astype(F32)).astype(BF16)
    taps = _dot(h_hi, w4hi_ref[...]) + _dot(h_lo, w4hi_ref[...]) + _dot(h_hi, w4lo_ref[...])
    row_w = i * tl + lax.broadcasted_iota(jnp.int32, (tl, wide), 0)
    col_w = lax.broadcasted_iota(jnp.int32, (tl, wide), 1)
    backward = col_w >= wide // 2
    t_w = jnp.where(backward, L - row_w, row_w).astype(F32) / (L - 1)
    taps = taps * jnp.exp(-t_w * delta_ref[...])
    taps = jnp.where(backward & (row_w == 0), 0.0, taps)
    for n in range(HYENA_ORDER):
        for direction in range(2):
            c0 = (direction * HYENA_ORDER + n) * C
            kf_ref[n, direction] = taps[:, c0:c0 + C].astype(kf_ref.dtype)


def hyena_filter_taps(L, fw1, fb1, fw2, fb2, fw3, fb3, fw4, ffreq, *, tl=512):
    C = D_HYENA
    H = fw2.shape[0]
    HALF = LANES // 2
    assert H <= HALF and 2 * FILTER_BANDS + 1 <= HALF

    def both(a, rows):
        blk = jnp.zeros((HALF if rows else 1, HALF), F32).at[:a.shape[0], :a.shape[1]].set(a)
        if not rows:
            return jnp.concatenate([blk, blk], axis=1)
        zero = jnp.zeros_like(blk)
        return jnp.concatenate([jnp.concatenate([blk, zero], axis=1), jnp.concatenate([zero, blk], axis=1)], axis=0)

    w1 = both(jnp.concatenate([fw1[1:], fw1[:1]], axis=0), True)
    w2, w3 = both(fw2, True), both(fw3, True)
    b1, b2, b3, freq = (both(v[None, :], False) for v in (fb1, fb2, fb3, ffreq))
    w4 = fw4.reshape(H, HYENA_ORDER, 2, C)
    wide = 2 * HYENA_ORDER * C
    w4p = jnp.zeros((LANES, wide), F32)
    w4p = w4p.at[:H, :wide // 2].set(w4[:, :, 0].reshape(H, HYENA_ORDER * C))
    w4p = w4p.at[HALF:HALF + H, wide // 2:].set(w4[:, :, 1].reshape(H, HYENA_ORDER * C))
    w4hi = w4p.astype(BF16)
    w4lo = (w4p - w4hi.astype(F32)).astype(BF16)
    bands = np.linspace(1e-4, FILTER_BANDS - 1, FILTER_BANDS, dtype=np.float32)
    fvec = np.zeros((1, LANES), np.float32)
    for base in (0, HALF):
        fvec[0, base:base + FILTER_BANDS] = bands
        fvec[0, base + FILTER_BANDS:base + 2 * FILTER_BANDS] = bands
    max_decay = math.log(DECAY_TARGET) / FAST_DECAY_PCT
    min_decay = math.log(DECAY_TARGET) / SLOW_DECAY_PCT
    deltas = np.abs(np.linspace(min_decay, max_decay, C, dtype=np.float32))
    delta_w = np.tile(deltas[None, :], (1, 2 * HYENA_ORDER))
    const = lambda i: (0, 0)
    args = (w1, b1, w2, b2, w3, b3, w4hi, w4lo, freq, jnp.asarray(fvec), jnp.asarray(delta_w))
    return pl.pallas_call(
        partial(_filter_kernel, L, tl),
        grid=(L // tl,),
        in_specs=[pl.BlockSpec(a.shape, const) for a in args],
        out_specs=pl.BlockSpec((HYENA_ORDER, 2, tl, C), lambda i: (0, 0, i, 0)),
        out_shape=jax.ShapeDtypeStruct((HYENA_ORDER, 2, L, C), BF16),
        compiler_params=_params("parallel"),
        name="hyena_filter_taps",
    )(*args)


FFT_ROWS = 16
FFT_COLS = 512
FFT_COLS_IN = 1024


def _stage1_kernel(s_ref, x_ref, y_ref):
    _, h, rows, cols = x_ref.shape
    s = s_ref[...]
    xt = pltpu.einshape("abc->bac", x_ref[...].reshape(2 * h, rows, cols))
    yt = jnp.stack([_dot(s, xt[b]).astype(y_ref.dtype) for b in range(rows)], axis=0)
    y_ref[...] = pltpu.einshape("bac->abc", yt).reshape(y_ref.shape)


def fft_stage1(x, which, s):
    _, _, h, N2, C = x.shape
    N1 = 2 * h
    return pl.pallas_call(
        _stage1_kernel,
        grid=(N2 // FFT_ROWS, C // FFT_COLS_IN),
        in_specs=[pl.BlockSpec((2 * N1, N1), lambda j, c: (0, 0)),
                  pl.BlockSpec((None, 2, h, FFT_ROWS, FFT_COLS_IN), lambda j, c: (which, 0, 0, j, c))],
        out_specs=pl.BlockSpec((2, N1, FFT_ROWS, FFT_COLS_IN), lambda j, c: (0, 0, j, c)),
        out_shape=jax.ShapeDtypeStruct((2, N1, N2, C), BF16),
        compiler_params=_params("parallel", "parallel"),
        name="fft_stage1",
    )(s, x)


FFT_K1_PER_STEP = 4


def _mid_kernel(inv_n, f2r_ref, f2i_ref, twr_ref, twi_ref, yf_ref, y_ref, u_ref):
    f2r, f2i = f2r_ref[...], f2i_ref[...]
    for k in range(y_ref.shape[1]):
        twr, twi = twr_ref[k], twi_ref[k]
        gr, gi = f2r * twr - f2i * twi, f2r * twi + f2i * twr
        grb, gib = gr.astype(BF16), gi.astype(BF16)

        def inner(ref):
            re, im = ref[0, k], ref[1, k]
            return _dot(grb, re) - _dot(gib, im), _dot(gib, re) + _dot(grb, im)

        hr, hi = inner(yf_ref)
        zr, zi = inner(y_ref)
        pr = (zr * hr - zi * hi).astype(BF16)
        pi = (zr * hi + zi * hr).astype(BF16)
        irb = (gr.T * inv_n).astype(BF16)
        iib = (gi.T * (-inv_n)).astype(BF16)
        u_ref[0, k] = (_dot(irb, pr) - _dot(iib, pi)).astype(u_ref.dtype)
        u_ref[1, k] = (_dot(iib, pr) + _dot(irb, pi)).astype(u_ref.dtype)


def fft_mid(yf, y, dc):
    _, N1, N2, C = y.shape
    kb = FFT_K1_PER_STEP
    blk = pl.BlockSpec((2, kb, N2, C), lambda k: (0, k, 0, 0))
    const = pl.BlockSpec((N2, N2), lambda k: (0, 0))
    tw = pl.BlockSpec((kb, 1, N2), lambda k: (k, 0, 0))
    return pl.pallas_call(
        partial(_mid_kernel, 1.0 / (N1 * N2)),
        grid=(N1 // kb,),
        in_specs=[const, const, tw, tw, blk, blk],
        out_specs=blk,
        out_shape=jax.ShapeDtypeStruct((2, N1, N2, C), BF16),
        compiler_params=_params("parallel"),
        name="fft_mid",
    )(dc['f2r'], dc['f2i'], dc['twr'], dc['twi'], yf, y)


def _final_kernel(t_ref, u_ref, z_ref, gate_ref, bias_ref, o_ref):
    _, n1, rows, cols = u_ref.shape
    t = t_ref[...]
    ut = pltpu.einshape("abc->bac", u_ref[...].reshape(2 * n1, rows, cols))
    convt = jnp.stack([_dot(t, ut[b]).astype(BF16) for b in range(rows)], axis=0)
    conv = pltpu.einshape("bac->abc", convt).reshape(o_ref.shape).astype(F32)
    z = z_ref[...].astype(F32)
    o_ref[...] = (gate_ref[...].astype(F32) * (conv + bias_ref[...] * z)).astype(o_ref.dtype)


def fft_final(u, t_fin, z, z_which, gate, gate_which, bias, bias_which):
    _, N1, N2, C = u.shape
    h = N1 // 2

    def half(which):
        return pl.BlockSpec((None, 2, h, FFT_ROWS, FFT_COLS), lambda j, c: (which, 0, 0, j, c))

    return pl.pallas_call(
        _final_kernel,
        grid=(N2 // FFT_ROWS, C // FFT_COLS),
        in_specs=[pl.BlockSpec((N1, 2 * N1), lambda j, c: (0, 0)),
                  pl.BlockSpec((2, N1, FFT_ROWS, FFT_COLS), lambda j, c: (0, 0, j, c)),
                  half(z_which), half(gate_which),
                  pl.BlockSpec((None, 1, FFT_COLS), lambda j, c: (bias_which, 0, c))],
        out_specs=half(0),
        out_shape=jax.ShapeDtypeStruct((1, 2, h, N2, C), BF16),
        compiler_params=_params("parallel", "parallel"),
        name="fft_final",
    )(t_fin, u, z, gate, bias.reshape(bias.shape[0], 1, C))


def hyena_branch(zs, L, p):
    _, M, C = zs.shape
    assert M == 2 * L, "the batch pair rides as real/imaginary parts"
    dc = _dft_constants(L)
    split = (2, dc['N1'] // 2, FFT_INNER, C)
    taps = hyena_filter_taps(L, *p['filt']).reshape(HYENA_ORDER, *split)
    zs = zs.reshape(3, *split)
    z, z_which = zs, 0
    for n in range(HYENA_ORDER):
        u = fft_mid(fft_stage1(taps, n, dc['s_filt']), fft_stage1(z, z_which, dc['s_data']), dc)
        z, z_which = fft_final(u, dc['t_fin'], z, z_which, zs, n + 1, p['hy_bias'], n), 0
    return z.reshape(M, C)


def _split_w_in(w):
    c_q = 3 * D_HYENA
    c_v = c_q + 2 * D_ATT
    c_g = c_v + D_ATT
    return tuple(w[:, a:b].astype(BF16) for a, b in ((0, c_q), (c_q, c_v), (c_v, c_g), (c_g, w.shape[1])))


def _layer(x, p):
    Bsz, L, D = x.shape
    M = Bsz * L
    x0 = x.reshape(M, D)
    x1, u = ffn_block(x0, p['ffn1_pre_g'], p['ffn1_post_g'], p['ffn1_w_gate'], p['ffn1_w_up'], p['ffn1_w_down'],
                      next_g=p['mix_pre_g'])
    w_hy, w_qk, w_v, w_gate = p['w_in']
    hy_in = hyena_inputs(u, w_hy, p['hy_conv_w'], p['hy_conv_b'], L)
    qk = projection(u, w_qk, BF16, tm=1024, tn=D_ATT, rope_len=L, rope=p['rope'], name="inproj_qk")
    v = projection(u, w_v, BF16, tm=1024, tn=D_ATT, name="inproj_v")
    gates = projection(u, w_gate, BF16, tm=1024, tn=D_MODEL, name="inproj_gates")
    a2 = hyena_branch(hy_in, L, p)
    att = []
    for g in range(N_GROUPS):
        srcs = ((qk, g), (qk, N_GROUPS + g), (v, g))
        fn = dilated_attention_group if ATT_GROUPS[g][1] == 1 else dilated_attention_slabs
        att.append(fn(srcs, Bsz, L, g))
    x2, xn2 = merge_out(x1, a2, [o for o, _ in att], [l for _, l in att], gates, p['mix_post_g'], p['ffn2_pre_g'],
                        p['w_hy_proj'], p['w_att_proj'], p['w_out'])
    x3 = ffn_block(x2, p['ffn2_pre_g'], p['ffn2_post_g'], p['ffn2_w_gate'], p['ffn2_w_up'], p['ffn2_w_down'],
                   xn=xn2)
    return x3.reshape(Bsz, L, D)


def kernel(x_prompt, x_sample, ffn1_pre_g, ffn1_post_g, ffn1_w_gate, ffn1_w_up, ffn1_w_down, mix_pre_g, mix_post_g, w_in, hy_conv_w, hy_conv_b, filt_w1, filt_b1, filt_w2, filt_b2, filt_w3, filt_b3, filt_w4, filt_freq, hy_bias, w_hy_proj, w_att_proj, w_out, ffn2_pre_g, ffn2_post_g, ffn2_w_gate, ffn2_w_up, ffn2_w_down):
    assert ffn1_w_gate.shape[0] == 1
    p = {
        'ffn1_pre_g': ffn1_pre_g, 'ffn1_post_g': ffn1_post_g,
        'ffn1_w_gate': ffn1_w_gate[0].astype(BF16), 'ffn1_w_up': ffn1_w_up[0].astype(BF16),
        'ffn1_w_down': ffn1_w_down[0].astype(BF16),
        'mix_pre_g': mix_pre_g, 'mix_post_g': mix_post_g,
        'w_in': _split_w_in(w_in[0]),
        'hy_conv_w': hy_conv_w[0], 'hy_conv_b': hy_conv_b,
        'filt': (filt_w1[0], filt_b1[0], filt_w2[0], filt_b2[0], filt_w3[0], filt_b3[0], filt_w4[0], filt_freq[0]),
        'hy_bias': hy_bias[0],
        'w_hy_proj': w_hy_proj[0].astype(BF16), 'w_att_proj': w_att_proj[0].astype(BF16),
        'w_out': w_out[0].astype(BF16),
        'ffn2_pre_g': ffn2_pre_g, 'ffn2_post_g': ffn2_post_g,
        'ffn2_w_gate': ffn2_w_gate[0].astype(BF16), 'ffn2_w_up': ffn2_w_up[0].astype(BF16),
        'ffn2_w_down': ffn2_w_down[0].astype(BF16),
    }
    p['rope'] = rope_tables(max(x_prompt.shape[1], x_sample.shape[1]))
    return (_layer(x_prompt, p), _layer(x_sample, p))
```

```python
import math
from functools import partial

import numpy as np
import jax
import jax.numpy as jnp
from jax import lax
from jax.experimental import pallas as pl
from jax.experimental.pallas import tpu as pltpu

D_MODEL = 2048
D_HYENA = 1024
HYENA_ORDER = 2
SHORT_CONV = 3
FILTER_EMB = 33
FILTER_BANDS = (FILTER_EMB - 1) // 2
FAST_DECAY_PCT = 0.3
SLOW_DECAY_PCT = 1.5
DECAY_TARGET = 1e-2
HEAD_DIM = 128
HEADS_PER_GROUP = 4
ATT_GROUPS = ((128, 1), (512, 4), (2048, 16))
N_GROUPS = len(ATT_GROUPS)
D_ATT = N_GROUPS * HEADS_PER_GROUP * HEAD_DIM
D_ATT_OUT = HEADS_PER_GROUP * HEAD_DIM
ROPE_DIM = HEAD_DIM // 4
ROPE_THETA = 500000.0
N_BRANCH = 2
D_IN_PROJ = 3 * D_HYENA + 3 * D_ATT + N_BRANCH * D_MODEL
D_FF = 5632
EPS = 1e-6
NEG_INF = -1e30

LANES = 128
FFT_INNER = 256
VMEM_BYTES_V7X = 64 * 1024 * 1024
VMEM_LIMIT_BYTES = VMEM_BYTES_V7X - 4 * 1024 * 1024
BF16 = jnp.bfloat16
F32 = jnp.float32
HIGHEST = lax.Precision.HIGHEST


def _rms(x, g):
    return x * lax.rsqrt(jnp.mean(x * x, axis=-1, keepdims=True) + EPS) * g


def _dot(a, b):
    return jnp.dot(a, b, preferred_element_type=F32)


def _params(*sem):
    return pltpu.CompilerParams(dimension_semantics=sem, vmem_limit_bytes=VMEM_LIMIT_BYTES)


FFN_SPLIT = 2


def _ffn_kernel(norm_in, norm_out, x_ref, xn_src_ref, post_g_ref, next_g_ref, wg_ref, wu_ref, wd_ref, *rest):
    o_ref = rest[0]
    u_ref = rest[1] if norm_out else None
    xn_ref = rest[-1] if norm_in else xn_src_ref
    j = pl.program_id(1)
    tf = wg_ref.shape[1]

    @pl.when(j == 0)
    def _():
        if norm_in:
            xn_ref[...] = _rms(x_ref[...], xn_src_ref[...]).astype(BF16)
        o_ref[...] = jnp.zeros_like(o_ref)

    xn = xn_ref[...]
    hs = []
    for c in range(FFN_SPLIT):
        cols = slice(c * (tf // FFN_SPLIT), (c + 1) * (tf // FFN_SPLIT))
        gate = _dot(xn, wg_ref[:, cols])
        up = _dot(xn, wu_ref[:, cols])
        hs.append((gate * jax.nn.sigmoid(gate) * up).astype(BF16))
    o_ref[...] += _dot(jnp.concatenate(hs, axis=1), wd_ref[...])

    @pl.when(j == pl.num_programs(1) - 1)
    def _():
        y = x_ref[...] + 0.5 * _rms(o_ref[...], post_g_ref[...])
        o_ref[...] = y
        if norm_out:
            u_ref[...] = _rms(y, next_g_ref[...]).astype(u_ref.dtype)


def ffn_block(x, pre_g, post_g, wg, wu, wd, *, xn=None, next_g=None, tm=512, tf=512):
    M, D = x.shape
    FF = wg.shape[1]
    norm_in, norm_out = xn is None, next_g is not None
    rows = pl.BlockSpec((tm, D), lambda i, j: (i, 0))
    gain = pl.BlockSpec((1, D), lambda i, j: (0, 0))
    out = pl.pallas_call(
        partial(_ffn_kernel, norm_in, norm_out),
        grid=(M // tm, FF // tf),
        in_specs=[
            rows,
            gain if norm_in else rows,
            gain, gain,
            pl.BlockSpec((D, tf), lambda i, j: (0, j)),
            pl.BlockSpec((D, tf), lambda i, j: (0, j)),
            pl.BlockSpec((tf, D), lambda i, j: (j, 0)),
        ],
        out_specs=[rows, rows] if norm_out else [rows],
        out_shape=[jax.ShapeDtypeStruct((M, D), F32)] + ([jax.ShapeDtypeStruct((M, D), BF16)] if norm_out else []),
        scratch_shapes=[pltpu.VMEM((tm, D), BF16)] if norm_in else [],
        compiler_params=_params("parallel", "arbitrary"),
        name="ffn_block",
    )(x, pre_g if norm_in else xn, post_g, next_g if norm_out else post_g, wg, wu, wd)
    return out if norm_out else out[0]


def rope_tables(L):
    half = ROPE_DIM // 2
    inv_freq = jnp.power(ROPE_THETA, -jnp.arange(half, dtype=F32) / half)
    ang = jnp.arange(L, dtype=F32)[:, None] * inv_freq[None, :]
    cos, sin = jnp.cos(ang), jnp.sin(ang)
    rest = HEAD_DIM - ROPE_DIM
    c = jnp.concatenate([cos, cos, jnp.ones((L, rest), F32)], axis=1)
    s_lo = jnp.concatenate([-sin, jnp.zeros((L, half + rest), F32)], axis=1)
    s_hi = jnp.concatenate([jnp.zeros((L, half), F32), sin, jnp.zeros((L, rest), F32)], axis=1)
    scale = HEAD_DIM ** -0.5
    return jnp.stack([c * scale, c]), jnp.stack([s_lo * scale, s_lo]), jnp.stack([s_hi * scale, s_hi])


def _proj_kernel(rotate, xn_ref, w_ref, *rest):
    o_ref = rest[-1]
    r = _dot(xn_ref[...], w_ref[...])
    if rotate:
        c_ref, slo_ref, shi_ref = rest[:3]
        tn = r.shape[1]
        half = ROPE_DIM // 2
        wide = lambda t_ref: jnp.concatenate([t_ref[...]] * (tn // HEAD_DIM), axis=1)
        r = r * wide(c_ref) + pltpu.roll(r, tn - half, 1) * wide(slo_ref) + pltpu.roll(r, half, 1) * wide(shi_ref)
    o_ref[...] = r.astype(o_ref.dtype)


def projection(xn, w, out_dtype, *, tm, tn, rope_len=None, rope=None, name):
    M, D = xn.shape
    N = w.shape[1]
    in_specs = [
        pl.BlockSpec((tm, D), lambda i, j: (i, 0)),
        pl.BlockSpec((D, tn), lambda i, j: (0, j)),
    ]
    args = [xn, w]
    if rope_len is not None:
        assert tn == D_ATT and N == 2 * D_ATT and rope_len % tm == 0
        spec = pl.BlockSpec((None, tm, HEAD_DIM), lambda i, j: (j, i % (rope_len // tm), 0))
        in_specs += [spec, spec, spec]
        assert all(t.shape[1] >= rope_len for t in rope)
        args += list(rope)
    return pl.pallas_call(
        partial(_proj_kernel, rope_len is not None),
        grid=(M // tm, N // tn),
        in_specs=in_specs,
        out_specs=pl.BlockSpec((tm, tn), lambda i, j: (i, j)),
        out_shape=jax.ShapeDtypeStruct((M, N), out_dtype),
        compiler_params=_params("parallel", "arbitrary"),
        name=name,
    )(*args)


ATT_RADIUS = 64
assert all(w // (2 * d) == ATT_RADIUS for w, d in ATT_GROUPS)


def _attn_kernel(T, Ls, q_ref, kp_ref, kc_ref, kn_ref, vp_ref, vc_ref, vn_ref, o_ref, lse_ref, kbuf, vbuf):
    t = pl.program_id(2)
    R = ATT_RADIUS
    SB = 2 * R
    for buf, prv, cur, nxt in ((kbuf, kp_ref, kc_ref, kn_ref), (vbuf, vp_ref, vc_ref, vn_ref)):
        buf[0:R] = prv[...]
        buf[R:R + T] = cur[...]
        buf[R + T:R + T + R] = nxt[...]
    qi = lax.broadcasted_iota(jnp.int32, (SB, 2 * SB), 0)
    kk = lax.broadcasted_iota(jnp.int32, (SB, 2 * SB), 1)
    band = (kk >= qi) & (kk <= qi + 2 * R)
    lane = lax.broadcasted_iota(jnp.int32, (SB, LANES), 1)
    for sb in range(T // SB):
        kpos = t * T + (sb * SB - R) + kk
        mask = band & (kpos >= 0) & (kpos < Ls)
        lse = jnp.zeros((SB, LANES), F32)
        for h in range(HEADS_PER_GROUP):
            cols = slice(h * HEAD_DIM, (h + 1) * HEAD_DIM)
            q = q_ref[sb * SB:(sb + 1) * SB, cols]
            k = kbuf[sb * SB:(sb + 2) * SB, cols]
            v = vbuf[sb * SB:(sb + 2) * SB, cols]
            s = lax.dot_general(q, k, (((1,), (1,)), ((), ())), preferred_element_type=F32)
            s = jnp.where(mask, s, NEG_INF)
            m = jnp.max(s, axis=1, keepdims=True)
            p = jnp.exp(s - m)
            l = jnp.sum(p, axis=1, keepdims=True)
            o = _dot(p.astype(BF16), v) / l
            o_ref[sb * SB:(sb + 1) * SB, cols] = o.astype(o_ref.dtype)
            lse = jnp.where(lane == h, m + jnp.log(l), lse)
        lse_ref[sb * SB:(sb + 1) * SB, :] = lse


SLAB = 16


def _softmax_heads(q, k, v, mask):
    lane = lax.broadcasted_iota(jnp.int32, (q.shape[0], LANES), 1)
    lse = jnp.zeros((q.shape[0], LANES), F32)
    outs = []
    for h in range(HEADS_PER_GROUP):
        cols = slice(h * HEAD_DIM, (h + 1) * HEAD_DIM)
        s = lax.dot_general(q[:, cols], k[:, cols], (((1,), (1,)), ((), ())), preferred_element_type=F32)
        s = jnp.where(mask, s, NEG_INF)
        m = jnp.max(s, axis=1, keepdims=True)
        p = jnp.exp(s - m)
        l = jnp.sum(p, axis=1, keepdims=True)
        outs.append(_dot(p.astype(BF16), v[:, cols]) / l)
        lse = jnp.where(lane == h, m + jnp.log(l), lse)
    return jnp.concatenate(outs, axis=1), lse


def _attn_slab_kernel(d, NS, Ls, q_ref, kp_ref, kc_ref, kn_ref, vp_ref, vc_ref, vn_ref, o_ref, lse_ref,
                      qs, ks, vs, os_, ls):
    t = pl.program_id(1)
    R = ATT_RADIUS
    SB = 2 * R
    J = SLAB // d
    NH = kp_ref.shape[0]
    QA = SB // J
    KA = QA + 2 * NH
    n_sb = NS // QA
    to_class_major = lambda ref: pltpu.einshape("abc->bac", ref[...])
    qs[...] = to_class_major(q_ref)
    for buf, prv, cur, nxt in ((ks, kp_ref, kc_ref, kn_ref), (vs, vp_ref, vc_ref, vn_ref)):
        buf[:, 0:NH] = to_class_major(prv)
        buf[:, NH:NH + NS] = to_class_major(cur)
        buf[:, NH + NS:NH + NS + NH] = to_class_major(nxt)
    qi = lax.broadcasted_iota(jnp.int32, (SB, 2 * SB), 0)
    kk = lax.broadcasted_iota(jnp.int32, (SB, 2 * SB), 1)
    sq_rel = J * (qi % QA) + qi // QA
    sk_rel = J * (kk % KA - NH) + kk // KA
    band = jnp.abs(sk_rel - sq_rel) <= R

    def body(it, carry):
        r = it // n_sb
        a0 = pl.multiple_of((it % n_sb) * QA, QA)
        sk = J * (t * NS + a0) + sk_rel
        mask = band & (sk >= 0) & (sk < Ls)
        gather = lambda buf, n: jnp.concatenate([buf[j * d + r, pl.ds(a0, n), :] for j in range(J)], axis=0)
        o, lse = _softmax_heads(gather(qs, QA), gather(ks, KA), gather(vs, KA), mask)
        o = o.astype(os_.dtype)
        for j in range(J):
            os_[j * d + r, pl.ds(a0, QA), :] = o[j * QA:(j + 1) * QA]
            ls[j * d + r, pl.ds(a0, QA), :] = lse[j * QA:(j + 1) * QA]
        return carry

    lax.fori_loop(0, d * n_sb, body, 0, unroll=True)
    o_ref[...] = pltpu.einshape("bac->abc", os_[...])
    lse_ref[...] = pltpu.einshape("bac->abc", ls[...])


def dilated_attention_slabs(srcs, Bsz, L, g, *, NS=128):
    d = ATT_GROUPS[g][1]
    M = Bsz * L
    R = ATT_RADIUS
    GW = HEADS_PER_GROUP * HEAD_DIM
    NH = R * d // SLAB
    TB = NS * SLAB
    assert SLAB % d == 0 and L % TB == 0 and NS % NH == 0 and (2 * R) % (SLAB // d) == 0
    nmb, nhb, hpm = L // TB, L // (NH * SLAB), NS // NH
    views = [a.reshape(M // SLAB, SLAB, a.shape[1]) for a, _ in srcs]
    col = lambda which: srcs[which][1]

    def main(which):
        return pl.BlockSpec((NS, SLAB, GW), lambda b, t: (b * nmb + t, 0, col(which)))

    def prev(which):
        return pl.BlockSpec((NH, SLAB, GW), lambda b, t: (jnp.maximum(b * nhb + t * hpm - 1, b * nhb), 0, col(which)))

    def nxt(which):
        return pl.BlockSpec((NH, SLAB, GW),
                            lambda b, t: (jnp.minimum(b * nhb + (t + 1) * hpm, (b + 1) * nhb - 1), 0, col(which)))

    o, lse = pl.pallas_call(
        partial(_attn_slab_kernel, d, NS, L // d),
        grid=(Bsz, nmb),
        in_specs=[main(0), prev(1), main(1), nxt(1), prev(2), main(2), nxt(2)],
        out_specs=[pl.BlockSpec((NS, SLAB, GW), lambda b, t: (b * nmb + t, 0, 0)),
                   pl.BlockSpec((NS, SLAB, LANES), lambda b, t: (b * nmb + t, 0, 0))],
        out_shape=[jax.ShapeDtypeStruct((M // SLAB, SLAB, GW), BF16),
                   jax.ShapeDtypeStruct((M // SLAB, SLAB, LANES), F32)],
        scratch_shapes=[pltpu.VMEM((SLAB, NS, GW), BF16),
                        pltpu.VMEM((SLAB, NS + 2 * NH, GW), BF16), pltpu.VMEM((SLAB, NS + 2 * NH, GW), BF16),
                        pltpu.VMEM((SLAB, NS, GW), BF16), pltpu.VMEM((SLAB, NS, LANES), F32)],
        compiler_params=_params("parallel", "arbitrary"),
        name=f"dilated_attention_g{g}",
    )(views[0], views[1], views[1], views[1], views[2], views[2], views[2])
    return o.reshape(M, GW), lse.reshape(M, LANES)


def dilated_attention_group(srcs, Bsz, L, g, *, T=1024):
    d = ATT_GROUPS[g][1]
    M = Bsz * L
    Ls = L // d
    R = ATT_RADIUS
    GW = HEADS_PER_GROUP * HEAD_DIM
    assert d == 1 and Ls % T == 0 and T % (2 * R) == 0
    nrb, nhb, hpt = Ls // T, Ls // R, T // R

    def main(which):
        return pl.BlockSpec((T, GW), lambda b, r, t: (b * nrb + t, srcs[which][1]))

    def prev(which):
        return pl.BlockSpec((R, GW), lambda b, r, t: (jnp.maximum(b * nhb + t * hpt - 1, b * nhb), srcs[which][1]))

    def nxt(which):
        return pl.BlockSpec((R, GW), lambda b, r, t: (jnp.minimum(b * nhb + (t + 1) * hpt, (b + 1) * nhb - 1),
                                                      srcs[which][1]))

    o, lse = pl.pallas_call(
        partial(_attn_kernel, T, Ls),
        grid=(Bsz, d, nrb),
        in_specs=[main(0), prev(1), main(1), nxt(1), prev(2), main(2), nxt(2)],
        out_specs=[pl.BlockSpec((T, GW), lambda b, r, t: (b * nrb + t, r)),
                   pl.BlockSpec((T, LANES), lambda b, r, t: (b * nrb + t, r))],
        out_shape=[jax.ShapeDtypeStruct((M // d, d * GW), BF16),
                   jax.ShapeDtypeStruct((M // d, d * LANES), F32)],
        scratch_shapes=[pltpu.VMEM((T + 2 * R, GW), BF16), pltpu.VMEM((T + 2 * R, GW), BF16)],
        compiler_params=_params("parallel", "parallel", "arbitrary"),
        name=f"dilated_attention_g{g}",
    )(srcs[0][0], srcs[1][0], srcs[1][0], srcs[1][0], srcs[2][0], srcs[2][0], srcs[2][0])
    return o.reshape(M, GW), lse.reshape(M, LANES)


def _merge_kernel(x_ref, a_ref, o0_ref, o1_ref, o2_ref, l0_ref, l1_ref, l2_ref, ga_ref, gb_ref,
                  post_g_ref, next_g_ref, whp_ref, wap_ref, wo_ref, o_ref, u_ref):
    lses = [l0_ref[...], l1_ref[...], l2_ref[...]]
    mx = jnp.maximum(jnp.maximum(lses[0], lses[1]), lses[2])
    es = [jnp.exp(l - mx) for l in lses]
    den = es[0] + es[1] + es[2]
    wts = [e / den for e in es]
    outs = [o0_ref, o1_ref, o2_ref]
    heads = []
    for h in range(HEADS_PER_GROUP):
        cols = slice(h * HEAD_DIM, (h + 1) * HEAD_DIM)
        heads.append(sum(wts[g][:, h:h + 1] * outs[g][:, cols].astype(F32) for g in range(N_GROUPS)))
    att = jnp.concatenate(heads, axis=1).astype(BF16)
    a = _dot(a_ref[...], whp_ref[...])
    b = _dot(att, wap_ref[...])
    gate_a, gate_b = ga_ref[...].astype(F32), gb_ref[...].astype(F32)
    merged = (jax.nn.sigmoid(gate_a) * a + jax.nn.sigmoid(gate_b) * b).astype(BF16)
    mix = _dot(merged, wo_ref[...])
    y = x_ref[...] + _rms(mix, post_g_ref[...])
    o_ref[...] = y
    u_ref[...] = _rms(y, next_g_ref[...]).astype(u_ref.dtype)


def merge_out(x, a_in, att_outs, att_lses, gates, post_g, next_g, whp, wap, wo, *, tm=512):
    M, D = x.shape
    const = lambda i: (0, 0)
    rows = lambda a: pl.BlockSpec((tm, a.shape[1]), lambda i: (i, 0))
    return pl.pallas_call(
        _merge_kernel,
        grid=(M // tm,),
        in_specs=[rows(x), rows(a_in)] + [rows(o) for o in att_outs] + [rows(l) for l in att_lses] + [
            pl.BlockSpec((tm, D), lambda i: (i, 0)),
            pl.BlockSpec((tm, D), lambda i: (i, 1)),
            pl.BlockSpec((1, D), const),
            pl.BlockSpec((1, D), const),
            pl.BlockSpec(whp.shape, const, pipeline_mode=pl.Buffered(1)),
            pl.BlockSpec(wap.shape, const, pipeline_mode=pl.Buffered(1)),
            pl.BlockSpec(wo.shape, const, pipeline_mode=pl.Buffered(1)),
        ],
        out_specs=[pl.BlockSpec((tm, D), lambda i: (i, 0)), pl.BlockSpec((tm, D), lambda i: (i, 0))],
        out_shape=[jax.ShapeDtypeStruct((M, D), F32), jax.ShapeDtypeStruct((M, D), BF16)],
        compiler_params=_params("parallel"),
        name="merge_out",
    )(x, a_in, *att_outs, *att_lses, gates, gates, post_g, next_g, whp, wap, wo)


def _dft_constants(L):
    N = 2 * L
    N2 = FFT_INNER
    N1 = N // N2
    h = N1 // 2
    idx1 = np.arange(N1)
    ang1 = -2.0 * np.pi * ((idx1[:, None] * idx1[None, :]) % N1) / N1
    f1r, f1i = np.cos(ang1), np.sin(ang1)
    s_data = np.block([[f1r[:, :h], -f1i[:, :h]], [f1i[:, :h], f1r[:, :h]]])
    s_filt = np.concatenate([f1r, f1i], axis=0)
    ar, ai = f1r[:h, :], -f1i[:h, :]
    t_fin = np.block([[ar, -ai], [ai, ar]])
    idx2 = np.arange(N2)
    ang2 = -2.0 * np.pi * ((idx2[:, None] * idx2[None, :]) % N2) / N2
    angt = -2.0 * np.pi * (idx1[:, None] * idx2[None, :]) / N
    return dict(
        N1=N1,
        s_data=jnp.asarray(s_data, BF16), s_filt=jnp.asarray(s_filt, BF16), t_fin=jnp.asarray(t_fin, BF16),
        f2r=jnp.asarray(np.cos(ang2), F32), f2i=jnp.asarray(np.sin(ang2), F32),
        twr=jnp.asarray(np.cos(angt).reshape(N1, 1, N2), F32),
        twi=jnp.asarray(np.sin(angt).reshape(N1, 1, N2), F32),
    )


def _hyena_in_kernel(tm, L, x_ref, prev_ref, next_ref, w_ref, cw_ref, cb_ref, o_ref, xn_ref):
    i = pl.program_id(0)
    H = prev_ref.shape[0]

    @pl.when(pl.program_id(1) == 0)
    def _():
        xn_ref[0:H] = prev_ref[...]
        xn_ref[H:H + tm] = x_ref[...]
        xn_ref[H + tm:H + tm + H] = next_ref[...]

    r = _dot(xn_ref[...], w_ref[...])
    row = lax.broadcasted_iota(jnp.int32, r.shape, 0)
    pos0 = (i * tm) % L
    outside = ((row < H) & (pos0 == 0)) | ((row >= H + tm) & (pos0 + tm == L))
    r = jnp.where(outside, 0.0, r)
    n = r.shape[0]
    y = pltpu.roll(r, 1, 0) * cw_ref[0:1, :] + r * cw_ref[1:2, :] + pltpu.roll(r, n - 1, 0) * cw_ref[2:3, :]
    o_ref[0] = (y[H:H + tm] + cb_ref[...]).astype(o_ref.dtype)


def hyena_inputs(x, w, conv_w, conv_b, L, *, tm=1024):
    M, D = x.shape
    C = w.shape[1] // 3
    H = SLAB
    nh = M // H
    assert L % tm == 0 and tm % H == 0
    return pl.pallas_call(
        partial(_hyena_in_kernel, tm, L),
        grid=(M // tm, 3),
        in_specs=[
            pl.BlockSpec((tm, D), lambda i, j: (i, 0)),
            pl.BlockSpec((H, D), lambda i, j: (jnp.maximum(i * (tm // H) - 1, 0), 0)),
            pl.BlockSpec((H, D), lambda i, j: (jnp.minimum((i + 1) * (tm // H), nh - 1), 0)),
            pl.BlockSpec((D, C), lambda i, j: (0, j)),
            pl.BlockSpec((SHORT_CONV, C), lambda i, j: (0, j)),
            pl.BlockSpec((1, C), lambda i, j: (0, j)),
        ],
        out_specs=pl.BlockSpec((1, tm, C), lambda i, j: (j, i, 0)),
        out_shape=jax.ShapeDtypeStruct((3, M, C), BF16),
        scratch_shapes=[pltpu.VMEM((tm + 2 * H, D), BF16)],
        compiler_params=_params("parallel", "arbitrary"),
        name="hyena_inputs",
    )(x, x, x, w, conv_w, conv_b)


def _filter_kernel(L, h, w1_ref, b1_ref, w2_ref, b2_ref, w3_ref, b3_ref, w4hi_ref, w4lo_ref,
                   freq_ref, fvec_ref, delta_ref, kf_ref):
    i = pl.program_id(0)
    HALF = LANES // 2
    wide = delta_ref.shape[1]
    C = wide // (2 * HYENA_ORDER)
    tl = FFT_ROWS * h

    def lag(shape):
        r = lax.broadcasted_iota(jnp.int32, shape, 0)
        return (r % h) * FFT_INNER + i * FFT_ROWS + r // h

    row = lag((tl, LANES))
    lane = lax.broadcasted_iota(jnp.int32, (tl, LANES), 1)
    posf = jnp.where(lane < HALF, row, L - row).astype(F32)
    t = posf / (L - 1)
    a = fvec_ref[...] * ((2.0 * math.pi / L) * posf)
    lh = lane % HALF
    feats = jnp.where(lh < FILTER_BANDS, jnp.cos(a),
                      jnp.where(lh < 2 * FILTER_BANDS, -jnp.sin(a),
                                jnp.where(lh == 2 * FILTER_BANDS, t, 0.0)))
    freq = freq_ref[...]
    dense = lambda v, w_ref, b_ref: jnp.sin(freq * (
        jnp.dot(v, w_ref[...], precision=HIGHEST, preferred_element_type=F32) + b_ref[...]))
    hid = dense(dense(dense(feats, w1_ref, b1_ref), w2_ref, b2_ref), w3_ref, b3_ref)
    h_hi = hid.astype(BF16)
    h_lo = (hid - h_hi.astype(F32)).astype(BF16)
    taps = _dot(h_hi, w4hi_ref[...]) + _dot(h_lo, w4hi_ref[...]) + _dot(h_hi, w4lo_ref[...])
    row_w = lag((tl, wide))
    col_w = lax.broadcasted_iota(jnp.int32, (tl, wide), 1)
    backward = col_w >= wide // 2
    t_w = jnp.where(backward, L - row_w, row_w).astype(F32) / (L - 1)
    taps = taps * jnp.exp(-t_w * delta_ref[...])
    taps = jnp.where(backward & (row_w == 0), 0.0, taps)
    for n in range(HYENA_ORDER):
        for direction in range(2):
            c0 = (direction * HYENA_ORDER + n) * C
            part = taps[:, c0:c0 + C].astype(kf_ref.dtype).reshape(FFT_ROWS, h, C)
            kf_ref[n, 0, :, direction * h:(direction + 1) * h, :] = part


def hyena_filter_taps(L, fw1, fb1, fw2, fb2, fw3, fb3, fw4, ffreq):
    C = D_HYENA
    h = L // FFT_INNER
    groups = FFT_INNER // FFT_ROWS
    H = fw2.shape[0]
    HALF = LANES // 2
    assert H <= HALF and 2 * FILTER_BANDS + 1 <= HALF

    def both(a, rows):
        blk = jnp.zeros((HALF if rows else 1, HALF), F32).at[:a.shape[0], :a.shape[1]].set(a)
        if not rows:
            return jnp.concatenate([blk, blk], axis=1)
        zero = jnp.zeros_like(blk)
        return jnp.concatenate([jnp.concatenate([blk, zero], axis=1), jnp.concatenate([zero, blk], axis=1)], axis=0)

    w1 = both(jnp.concatenate([fw1[1:], fw1[:1]], axis=0), True)
    w2, w3 = both(fw2, True), both(fw3, True)
    b1, b2, b3, freq = (both(v[None, :], False) for v in (fb1, fb2, fb3, ffreq))
    w4 = fw4.reshape(H, HYENA_ORDER, 2, C)
    wide = 2 * HYENA_ORDER * C
    w4p = jnp.zeros((LANES, wide), F32)
    w4p = w4p.at[:H, :wide // 2].set(w4[:, :, 0].reshape(H, HYENA_ORDER * C))
    w4p = w4p.at[HALF:HALF + H, wide // 2:].set(w4[:, :, 1].reshape(H, HYENA_ORDER * C))
    w4hi = w4p.astype(BF16)
    w4lo = (w4p - w4hi.astype(F32)).astype(BF16)
    bands = np.linspace(1e-4, FILTER_BANDS - 1, FILTER_BANDS, dtype=np.float32)
    fvec = np.zeros((1, LANES), np.float32)
    for base in (0, HALF):
        fvec[0, base:base + FILTER_BANDS] = bands
        fvec[0, base + FILTER_BANDS:base + 2 * FILTER_BANDS] = bands
    max_decay = math.log(DECAY_TARGET) / FAST_DECAY_PCT
    min_decay = math.log(DECAY_TARGET) / SLOW_DECAY_PCT
    deltas = np.abs(np.linspace(min_decay, max_decay, C, dtype=np.float32))
    delta_w = np.tile(deltas[None, :], (1, 2 * HYENA_ORDER))
    const = lambda i: (0, 0)
    args = (w1, b1, w2, b2, w3, b3, w4hi, w4lo, freq, jnp.asarray(fvec), jnp.asarray(delta_w))
    return pl.pallas_call(
        partial(_filter_kernel, L, h),
        grid=(groups,),
        in_specs=[pl.BlockSpec(a.shape, const) for a in args],
        out_specs=pl.BlockSpec((HYENA_ORDER, 1, FFT_ROWS, 2 * h, C), lambda i: (0, i, 0, 0, 0)),
        out_shape=jax.ShapeDtypeStruct((HYENA_ORDER, groups, FFT_ROWS, 2 * h, C), BF16),
        compiler_params=_params("parallel"),
        name="hyena_filter_taps",
    )(*args)


FFT_ROWS = 16
FFT_COLS = 512
FFT_COLS_IN = 1024


def _stage1_kernel(inner_major, s_ref, x_ref, y_ref):
    s = s_ref[...]
    if inner_major:
        xt = x_ref[...]
    else:
        _, h, rows, cols = x_ref.shape
        xt = pltpu.einshape("abc->bac", x_ref[...].reshape(2 * h, rows, cols))
    yt = jnp.stack([_dot(s, xt[b]).astype(y_ref.dtype) for b in range(xt.shape[0])], axis=0)
    y_ref[...] = pltpu.einshape("bac->abc", yt).reshape(y_ref.shape)


def fft_stage1(x, which, s, *, inner_major=False):
    if inner_major:
        _, groups, _, N1, C = x.shape
        N2 = groups * FFT_ROWS
        x_spec = pl.BlockSpec((None, None, FFT_ROWS, N1, FFT_COLS_IN), lambda j, c: (which, j, 0, 0, c))
    else:
        _, _, h, N2, C = x.shape
        N1 = 2 * h
        x_spec = pl.BlockSpec((None, 2, h, FFT_ROWS, FFT_COLS_IN), lambda j, c: (which, 0, 0, j, c))
    return pl.pallas_call(
        partial(_stage1_kernel, inner_major),
        grid=(N2 // FFT_ROWS, C // FFT_COLS_IN),
        in_specs=[pl.BlockSpec((2 * N1, N1), lambda j, c: (0, 0)), x_spec],
        out_specs=pl.BlockSpec((2, N1, FFT_ROWS, FFT_COLS_IN), lambda j, c: (0, 0, j, c)),
        out_shape=jax.ShapeDtypeStruct((2, N1, N2, C), BF16),
        compiler_params=_params("parallel", "parallel"),
        name="fft_stage1",
    )(s, x)


FFT_K1_PER_STEP = 4


def _mid_kernel(inv_n, f2r_ref, f2i_ref, twr_ref, twi_ref, yf_ref, y_ref, u_ref):
    f2r, f2i = f2r_ref[...], f2i_ref[...]
    for k in range(y_ref.shape[1]):
        twr, twi = twr_ref[k], twi_ref[k]
        gr, gi = f2r * twr - f2i * twi, f2r * twi + f2i * twr
        grb, gib = gr.astype(BF16), gi.astype(BF16)

        def inner(ref):
            re, im = ref[0, k], ref[1, k]
            return _dot(grb, re) - _dot(gib, im), _dot(gib, re) + _dot(grb, im)

        hr, hi = inner(yf_ref)
        zr, zi = inner(y_ref)
        pr = (zr * hr - zi * hi).astype(BF16)
        pi = (zr * hi + zi * hr).astype(BF16)
        irb = (gr.T * inv_n).astype(BF16)
        iib = (gi.T * (-inv_n)).astype(BF16)
        u_ref[0, k] = (_dot(irb, pr) - _dot(iib, pi)).astype(u_ref.dtype)
        u_ref[1, k] = (_dot(iib, pr) + _dot(irb, pi)).astype(u_ref.dtype)


def fft_mid(yf, y, dc):
    _, N1, N2, C = y.shape
    kb = FFT_K1_PER_STEP
    blk = pl.BlockSpec((2, kb, N2, C), lambda k: (0, k, 0, 0))
    const = pl.BlockSpec((N2, N2), lambda k: (0, 0))
    tw = pl.BlockSpec((kb, 1, N2), lambda k: (k, 0, 0))
    return pl.pallas_call(
        partial(_mid_kernel, 1.0 / (N1 * N2)),
        grid=(N1 // kb,),
        in_specs=[const, const, tw, tw, blk, blk],
        out_specs=blk,
        out_shape=jax.ShapeDtypeStruct((2, N1, N2, C), BF16),
        compiler_params=_params("parallel"),
        name="fft_mid",
    )(dc['f2r'], dc['f2i'], dc['twr'], dc['twi'], yf, y)


def _final_kernel(t_ref, u_ref, z_ref, gate_ref, bias_ref, o_ref):
    _, n1, rows, cols = u_ref.shape
    t = t_ref[...]
    ut = pltpu.einshape("abc->bac", u_ref[...].reshape(2 * n1, rows, cols))
    convt = jnp.stack([_dot(t, ut[b]).astype(BF16) for b in range(rows)], axis=0)
    conv = pltpu.einshape("bac->abc", convt).reshape(o_ref.shape).astype(F32)
    z = z_ref[...].astype(F32)
    o_ref[...] = (gate_ref[...].astype(F32) * (conv + bias_ref[...] * z)).astype(o_ref.dtype)


def fft_final(u, t_fin, z, z_which, gate, gate_which, bias, bias_which):
    _, N1, N2, C = u.shape
    h = N1 // 2

    def half(which):
        return pl.BlockSpec((None, 2, h, FFT_ROWS, FFT_COLS), lambda j, c: (which, 0, 0, j, c))

    return pl.pallas_call(
        _final_kernel,
        grid=(N2 // FFT_ROWS, C // FFT_COLS),
        in_specs=[pl.BlockSpec((N1, 2 * N1), lambda j, c: (0, 0)),
                  pl.BlockSpec((2, N1, FFT_ROWS, FFT_COLS), lambda j, c: (0, 0, j, c)),
                  half(z_which), half(gate_which),
                  pl.BlockSpec((None, 1, FFT_COLS), lambda j, c: (bias_which, 0, c))],
        out_specs=half(0),
        out_shape=jax.ShapeDtypeStruct((1, 2, h, N2, C), BF16),
        compiler_params=_params("parallel", "parallel"),
        name="fft_final",
    )(t_fin, u, z, gate, bias.reshape(bias.shape[0], 1, C))


def hyena_branch(zs, L, p):
    _, M, C = zs.shape
    assert M == 2 * L, "the batch pair rides as real/imaginary parts"
    dc = _dft_constants(L)
    split = (2, dc['N1'] // 2, FFT_INNER, C)
    taps = hyena_filter_taps(L, *p['filt'])
    zs = zs.reshape(3, *split)
    z, z_which = zs, 0
    for n in range(HYENA_ORDER):
        yf = fft_stage1(taps, n, dc['s_filt'], inner_major=True)
        u = fft_mid(yf, fft_stage1(z, z_which, dc['s_data']), dc)
        z, z_which = fft_final(u, dc['t_fin'], z, z_which, zs, n + 1, p['hy_bias'], n), 0
    return z.reshape(M, C)


def _split_w_in(w):
    c_q = 3 * D_HYENA
    c_v = c_q + 2 * D_ATT
    c_g = c_v + D_ATT
    return tuple(w[:, a:b].astype(BF16) for a, b in ((0, c_q), (c_q, c_v), (c_v, c_g), (c_g, w.shape[1])))


def _layer(x, p):
    Bsz, L, D = x.shape
    M = Bsz * L
    x0 = x.reshape(M, D)
    x1, u = ffn_block(x0, p['ffn1_pre_g'], p['ffn1_post_g'], p['ffn1_w_gate'], p['ffn1_w_up'], p['ffn1_w_down'],
                      next_g=p['mix_pre_g'])
    w_hy, w_qk, w_v, w_gate = p['w_in']
    hy_in = hyena_inputs(u, w_hy, p['hy_conv_w'], p['hy_conv_b'], L)
    qk = projection(u, w_qk, BF16, tm=1024, tn=D_ATT, rope_len=L, rope=p['rope'], name="inproj_qk")
    v = projection(u, w_v, BF16, tm=1024, tn=D_ATT, name="inproj_v")
    gates = projection(u, w_gate, BF16, tm=1024, tn=D_MODEL, name="inproj_gates")
    a2 = hyena_branch(hy_in, L, p)
    att = []
    for g in range(N_GROUPS):
        srcs = ((qk, g), (qk, N_GROUPS + g), (v, g))
        fn = dilated_attention_group if ATT_GROUPS[g][1] == 1 else dilated_attention_slabs
        att.append(fn(srcs, Bsz, L, g))
    x2, xn2 = merge_out(x1, a2, [o for o, _ in att], [l for _, l in att], gates, p['mix_post_g'], p['ffn2_pre_g'],
                        p['w_hy_proj'], p['w_att_proj'], p['w_out'])
    x3 = ffn_block(x2, p['ffn2_pre_g'], p['ffn2_post_g'], p['ffn2_w_gate'], p['ffn2_w_up'], p['ffn2_w_down'],
                   xn=xn2)
    return x3.reshape(Bsz, L, D)


def kernel(x_prompt, x_sample, ffn1_pre_g, ffn1_post_g, ffn1_w_gate, ffn1_w_up, ffn1_w_down, mix_pre_g, mix_post_g, w_in, hy_conv_w, hy_conv_b, filt_w1, filt_b1, filt_w2, filt_b2, filt_w3, filt_b3, filt_w4, filt_freq, hy_bias, w_hy_proj, w_att_proj, w_out, ffn2_pre_g, ffn2_post_g, ffn2_w_gate, ffn2_w_up, ffn2_w_down):
    assert ffn1_w_gate.shape[0] == 1
    p = {
        'ffn1_pre_g': ffn1_pre_g, 'ffn1_post_g': ffn1_post_g,
        'ffn1_w_gate': ffn1_w_gate[0].astype(BF16), 'ffn1_w_up': ffn1_w_up[0].astype(BF16),
        'ffn1_w_down': ffn1_w_down[0].astype(BF16),
        'mix_pre_g': mix_pre_g, 'mix_post_g': mix_post_g,
        'w_in': _split_w_in(w_in[0]),
        'hy_conv_w': hy_conv_w[0], 'hy_conv_b': hy_conv_b,
        'filt': (filt_w1[0], filt_b1[0], filt_w2[0], filt_b2[0], filt_w3[0], filt_b3[0], filt_w4[0], filt_freq[0]),
        'hy_bias': hy_bias[0],
        'w_hy_proj': w_hy_proj[0].astype(BF16), 'w_att_proj': w_att_proj[0].astype(BF16),
        'w_out': w_out[0].astype(BF16),
        'ffn2_pre_g': ffn2_pre_g, 'ffn2_post_g': ffn2_post_g,
        'ffn2_w_gate': ffn2_w_gate[0].astype(BF16), 'ffn2_w_up': ffn2_w_up[0].astype(BF16),
        'ffn2_w_down': ffn2_w_down[0].astype(BF16),
    }
    p['rope'] = rope_tables(max(x_prompt.shape[1], x_sample.shape[1]))
    return (_layer(x_prompt, p), _layer(x_sample, p))
```

```python
import math
from functools import partial

import numpy as np
import jax
import jax.numpy as jnp
from jax import lax
from jax.experimental import pallas as pl
from jax.experimental.pallas import tpu as pltpu

D_MODEL = 2048
D_HYENA = 1024
HYENA_ORDER = 2
SHORT_CONV = 3
FILTER_EMB = 33
FILTER_BANDS = (FILTER_EMB - 1) // 2
FAST_DECAY_PCT = 0.3
SLOW_DECAY_PCT = 1.5
DECAY_TARGET = 1e-2
HEAD_DIM = 128
HEADS_PER_GROUP = 4
ATT_GROUPS = ((128, 1), (512, 4), (2048, 16))
N_GROUPS = len(ATT_GROUPS)
D_ATT = N_GROUPS * HEADS_PER_GROUP * HEAD_DIM
D_ATT_OUT = HEADS_PER_GROUP * HEAD_DIM
ROPE_DIM = HEAD_DIM // 4
ROPE_THETA = 500000.0
N_BRANCH = 2
D_IN_PROJ = 3 * D_HYENA + 3 * D_ATT + N_BRANCH * D_MODEL
D_FF = 5632
EPS = 1e-6
NEG_INF = -1e30

LANES = 128
FFT_INNER = 256
VMEM_BYTES_V7X = 64 * 1024 * 1024
VMEM_LIMIT_BYTES = VMEM_BYTES_V7X - 4 * 1024 * 1024
BF16 = jnp.bfloat16
F32 = jnp.float32
HIGHEST = lax.Precision.HIGHEST


def _rms(x, g):
    return x * lax.rsqrt(jnp.mean(x * x, axis=-1, keepdims=True) + EPS) * g


def _dot(a, b):
    return jnp.dot(a, b, preferred_element_type=F32)


def _params(*sem):
    return pltpu.CompilerParams(dimension_semantics=sem, vmem_limit_bytes=VMEM_LIMIT_BYTES)


FFN_SPLIT = 2


def _ffn_kernel(norm_in, norm_out, x_ref, xn_src_ref, post_g_ref, next_g_ref, wg_ref, wu_ref, wd_ref, *rest):
    o_ref = rest[0]
    u_ref = rest[1] if norm_out else None
    xn_ref = rest[-1] if norm_in else xn_src_ref
    j = pl.program_id(1)
    tf = wg_ref.shape[1]

    @pl.when(j == 0)
    def _():
        if norm_in:
            xn_ref[...] = _rms(x_ref[...], xn_src_ref[...]).astype(BF16)
        o_ref[...] = jnp.zeros_like(o_ref)

    xn = xn_ref[...]
    hs = []
    for c in range(FFN_SPLIT):
        cols = slice(c * (tf // FFN_SPLIT), (c + 1) * (tf // FFN_SPLIT))
        gate = _dot(xn, wg_ref[:, cols])
        up = _dot(xn, wu_ref[:, cols])
        hs.append((gate * jax.nn.sigmoid(gate) * up).astype(BF16))
    o_ref[...] += _dot(jnp.concatenate(hs, axis=1), wd_ref[...])

    @pl.when(j == pl.num_programs(1) - 1)
    def _():
        y = x_ref[...] + 0.5 * _rms(o_ref[...], post_g_ref[...])
        o_ref[...] = y
        if norm_out:
            u_ref[...] = _rms(y, next_g_ref[...]).astype(u_ref.dtype)


def ffn_block(x, pre_g, post_g, wg, wu, wd, *, xn=None, next_g=None, tm=512, tf=512):
    M, D = x.shape
    FF = wg.shape[1]
    norm_in, norm_out = xn is None, next_g is not None
    rows = pl.BlockSpec((tm, D), lambda i, j: (i, 0))
    gain = pl.BlockSpec((1, D), lambda i, j: (0, 0))
    out = pl.pallas_call(
        partial(_ffn_kernel, norm_in, norm_out),
        grid=(M // tm, FF // tf),
        in_specs=[
            rows,
            gain if norm_in else rows,
            gain, gain,
            pl.BlockSpec((D, tf), lambda i, j: (0, j)),
            pl.BlockSpec((D, tf), lambda i, j: (0, j)),
            pl.BlockSpec((tf, D), lambda i, j: (j, 0)),
        ],
        out_specs=[rows, rows] if norm_out else [rows],
        out_shape=[jax.ShapeDtypeStruct((M, D), F32)] + ([jax.ShapeDtypeStruct((M, D), BF16)] if norm_out else []),
        scratch_shapes=[pltpu.VMEM((tm, D), BF16)] if norm_in else [],
        compiler_params=_params("parallel", "arbitrary"),
        name="ffn_block",
    )(x, pre_g if norm_in else xn, post_g, next_g if norm_out else post_g, wg, wu, wd)
    return out if norm_out else out[0]


def rope_tables(L):
    half = ROPE_DIM // 2
    inv_freq = jnp.power(ROPE_THETA, -jnp.arange(half, dtype=F32) / half)
    ang = jnp.arange(L, dtype=F32)[:, None] * inv_freq[None, :]
    cos, sin = jnp.cos(ang), jnp.sin(ang)
    rest = HEAD_DIM - ROPE_DIM
    c = jnp.concatenate([cos, cos, jnp.ones((L, rest), F32)], axis=1)
    s_lo = jnp.concatenate([-sin, jnp.zeros((L, half + rest), F32)], axis=1)
    s_hi = jnp.concatenate([jnp.zeros((L, half), F32), sin, jnp.zeros((L, rest), F32)], axis=1)
    scale = HEAD_DIM ** -0.5
    return jnp.stack([c * scale, c]), jnp.stack([s_lo * scale, s_lo]), jnp.stack([s_hi * scale, s_hi])


def _proj_kernel(rotate, xn_ref, w_ref, *rest):
    o_ref = rest[-1]
    r = _dot(xn_ref[...], w_ref[...])
    if rotate:
        c_ref, slo_ref, shi_ref = rest[:3]
        tn = r.shape[1]
        half = ROPE_DIM // 2
        wide = lambda t_ref: jnp.concatenate([t_ref[...]] * (tn // HEAD_DIM), axis=1)
        r = r * wide(c_ref) + pltpu.roll(r, tn - half, 1) * wide(slo_ref) + pltpu.roll(r, half, 1) * wide(shi_ref)
    o_ref[...] = r.astype(o_ref.dtype)


def projection(xn, w, out_dtype, *, tm, tn, rope_len=None, rope=None, name):
    M, D = xn.shape
    N = w.shape[1]
    in_specs = [
        pl.BlockSpec((tm, D), lambda i, j: (i, 0)),
        pl.BlockSpec((D, tn), lambda i, j: (0, j)),
    ]
    args = [xn, w]
    if rope_len is not None:
        assert tn == D_ATT and N == 2 * D_ATT and rope_len % tm == 0
        spec = pl.BlockSpec((None, tm, HEAD_DIM), lambda i, j: (j, i % (rope_len // tm), 0))
        in_specs += [spec, spec, spec]
        assert all(t.shape[1] >= rope_len for t in rope)
        args += list(rope)
    return pl.pallas_call(
        partial(_proj_kernel, rope_len is not None),
        grid=(M // tm, N // tn),
        in_specs=in_specs,
        out_specs=pl.BlockSpec((tm, tn), lambda i, j: (i, j)),
        out_shape=jax.ShapeDtypeStruct((M, N), out_dtype),
        compiler_params=_params("parallel", "arbitrary"),
        name=name,
    )(*args)


ATT_RADIUS = 64
assert all(w // (2 * d) == ATT_RADIUS for w, d in ATT_GROUPS)


def _attn_kernel(T, Ls, q_ref, kp_ref, kc_ref, kn_ref, vp_ref, vc_ref, vn_ref, o_ref, lse_ref, kbuf, vbuf):
    t = pl.program_id(2)
    R = ATT_RADIUS
    SB = 2 * R
    for buf, prv, cur, nxt in ((kbuf, kp_ref, kc_ref, kn_ref), (vbuf, vp_ref, vc_ref, vn_ref)):
        buf[0:R] = prv[...]
        buf[R:R + T] = cur[...]
        buf[R + T:R + T + R] = nxt[...]
    qi = lax.broadcasted_iota(jnp.int32, (SB, 2 * SB), 0)
    kk = lax.broadcasted_iota(jnp.int32, (SB, 2 * SB), 1)
    band = (kk >= qi) & (kk <= qi + 2 * R)
    lane = lax.broadcasted_iota(jnp.int32, (SB, LANES), 1)
    for sb in range(T // SB):
        kpos = t * T + (sb * SB - R) + kk
        mask = band & (kpos >= 0) & (kpos < Ls)
        lse = jnp.zeros((SB, LANES), F32)
        for h in range(HEADS_PER_GROUP):
            cols = slice(h * HEAD_DIM, (h + 1) * HEAD_DIM)
            q = q_ref[sb * SB:(sb + 1) * SB, cols]
            k = kbuf[sb * SB:(sb + 2) * SB, cols]
            v = vbuf[sb * SB:(sb + 2) * SB, cols]
            s = lax.dot_general(q, k, (((1,), (1,)), ((), ())), preferred_element_type=F32)
            s = jnp.where(mask, s, NEG_INF)
            m = jnp.max(s, axis=1, keepdims=True)
            p = jnp.exp(s - m)
            l = jnp.sum(p, axis=1, keepdims=True)
            o = _dot(p.astype(BF16), v) / l
            o_ref[sb * SB:(sb + 1) * SB, cols] = o.astype(o_ref.dtype)
            lse = jnp.where(lane == h, m + jnp.log(l), lse)
        lse_ref[sb * SB:(sb + 1) * SB, :] = lse


SLAB = 16


def _softmax_heads(q, k, v, mask):
    lane = lax.broadcasted_iota(jnp.int32, (q.shape[0], LANES), 1)
    lse = jnp.zeros((q.shape[0], LANES), F32)
    outs = []
    for h in range(HEADS_PER_GROUP):
        cols = slice(h * HEAD_DIM, (h + 1) * HEAD_DIM)
        s = lax.dot_general(q[:, cols], k[:, cols], (((1,), (1,)), ((), ())), preferred_element_type=F32)
        s = jnp.where(mask, s, NEG_INF)
        m = jnp.max(s, axis=1, keepdims=True)
        p = jnp.exp(s - m)
        l = jnp.sum(p, axis=1, keepdims=True)
        outs.append(_dot(p.astype(BF16), v[:, cols]) / l)
        lse = jnp.where(lane == h, m + jnp.log(l), lse)
    return jnp.concatenate(outs, axis=1), lse


def _attn_slab_kernel(d, NS, Ls, q_ref, kp_ref, kc_ref, kn_ref, vp_ref, vc_ref, vn_ref, o_ref, lse_ref,
                      qs, ks, vs, os_, ls):
    t = pl.program_id(1)
    R = ATT_RADIUS
    SB = 2 * R
    J = SLAB // d
    NH = kp_ref.shape[0]
    QA = SB // J
    KA = QA + 2 * NH
    n_sb = NS // QA
    to_class_major = lambda ref: pltpu.einshape("abc->bac", ref[...])
    qs[...] = to_class_major(q_ref)
    for buf, prv, cur, nxt in ((ks, kp_ref, kc_ref, kn_ref), (vs, vp_ref, vc_ref, vn_ref)):
        buf[:, 0:NH] = to_class_major(prv)
        buf[:, NH:NH + NS] = to_class_major(cur)
        buf[:, NH + NS:NH + NS + NH] = to_class_major(nxt)
    qi = lax.broadcasted_iota(jnp.int32, (SB, 2 * SB), 0)
    kk = lax.broadcasted_iota(jnp.int32, (SB, 2 * SB), 1)
    sq_rel = J * (qi % QA) + qi // QA
    sk_rel = J * (kk % KA - NH) + kk // KA
    band = jnp.abs(sk_rel - sq_rel) <= R

    def body(it, carry):
        r = it // n_sb
        a0 = pl.multiple_of((it % n_sb) * QA, QA)
        sk = J * (t * NS + a0) + sk_rel
        mask = band & (sk >= 0) & (sk < Ls)
        gather = lambda buf, n: jnp.concatenate([buf[j * d + r, pl.ds(a0, n), :] for j in range(J)], axis=0)
        o, lse = _softmax_heads(gather(qs, QA), gather(ks, KA), gather(vs, KA), mask)
        o = o.astype(os_.dtype)
        for j in range(J):
            os_[j * d + r, pl.ds(a0, QA), :] = o[j * QA:(j + 1) * QA]
            ls[j * d + r, pl.ds(a0, QA), :] = lse[j * QA:(j + 1) * QA]
        return carry

    lax.fori_loop(0, d * n_sb, body, 0, unroll=True)
    o_ref[...] = pltpu.einshape("bac->abc", os_[...])
    lse_ref[...] = pltpu.einshape("bac->abc", ls[...])


def dilated_attention_slabs(srcs, Bsz, L, g, *, NS=128):
    d = ATT_GROUPS[g][1]
    M = Bsz * L
    R = ATT_RADIUS
    GW = HEADS_PER_GROUP * HEAD_DIM
    NH = R * d // SLAB
    TB = NS * SLAB
    assert SLAB % d == 0 and L % TB == 0 and NS % NH == 0 and (2 * R) % (SLAB // d) == 0
    nmb, nhb, hpm = L // TB, L // (NH * SLAB), NS // NH
    views = [a.reshape(M // SLAB, SLAB, a.shape[1]) for a, _ in srcs]
    col = lambda which: srcs[which][1]

    def main(which):
        return pl.BlockSpec((NS, SLAB, GW), lambda b, t: (b * nmb + t, 0, col(which)))

    def prev(which):
        return pl.BlockSpec((NH, SLAB, GW), lambda b, t: (jnp.maximum(b * nhb + t * hpm - 1, b * nhb), 0, col(which)))

    def nxt(which):
        return pl.BlockSpec((NH, SLAB, GW),
                            lambda b, t: (jnp.minimum(b * nhb + (t + 1) * hpm, (b + 1) * nhb - 1), 0, col(which)))

    o, lse = pl.pallas_call(
        partial(_attn_slab_kernel, d, NS, L // d),
        grid=(Bsz, nmb),
        in_specs=[main(0), prev(1), main(1), nxt(1), prev(2), main(2), nxt(2)],
        out_specs=[pl.BlockSpec((NS, SLAB, GW), lambda b, t: (b * nmb + t, 0, 0)),
                   pl.BlockSpec((NS, SLAB, LANES), lambda b, t: (b * nmb + t, 0, 0))],
        out_shape=[jax.ShapeDtypeStruct((M // SLAB, SLAB, GW), BF16),
                   jax.ShapeDtypeStruct((M // SLAB, SLAB, LANES), F32)],
        scratch_shapes=[pltpu.VMEM((SLAB, NS, GW), BF16),
                        pltpu.VMEM((SLAB, NS + 2 * NH, GW), BF16), pltpu.VMEM((SLAB, NS + 2 * NH, GW), BF16),
                        pltpu.VMEM((SLAB, NS, GW), BF16), pltpu.VMEM((SLAB, NS, LANES), F32)],
        compiler_params=_params("parallel", "arbitrary"),
        name=f"dilated_attention_g{g}",
    )(views[0], views[1], views[1], views[1], views[2], views[2], views[2])
    return o.reshape(M, GW), lse.reshape(M, LANES)


def dilated_attention_group(srcs, Bsz, L, g, *, T=1024):
    d = ATT_GROUPS[g][1]
    M = Bsz * L
    Ls = L // d
    R = ATT_RADIUS
    GW = HEADS_PER_GROUP * HEAD_DIM
    assert d == 1 and Ls % T == 0 and T % (2 * R) == 0
    nrb, nhb, hpt = Ls // T, Ls // R, T // R

    def main(which):
        return pl.BlockSpec((T, GW), lambda b, r, t: (b * nrb + t, srcs[which][1]))

    def prev(which):
        return pl.BlockSpec((R, GW), lambda b, r, t: (jnp.maximum(b * nhb + t * hpt - 1, b * nhb), srcs[which][1]))

    def nxt(which):
        return pl.BlockSpec((R, GW), lambda b, r, t: (jnp.minimum(b * nhb + (t + 1) * hpt, (b + 1) * nhb - 1),
                                                      srcs[which][1]))

    o, lse = pl.pallas_call(
        partial(_attn_kernel, T, Ls),
        grid=(Bsz, d, nrb),
        in_specs=[main(0), prev(1), main(1), nxt(1), prev(2), main(2), nxt(2)],
        out_specs=[pl.BlockSpec((T, GW), lambda b, r, t: (b * nrb + t, r)),
                   pl.BlockSpec((T, LANES), lambda b, r, t: (b * nrb + t, r))],
        out_shape=[jax.ShapeDtypeStruct((M // d, d * GW), BF16),
                   jax.ShapeDtypeStruct((M // d, d * LANES), F32)],
        scratch_shapes=[pltpu.VMEM((T + 2 * R, GW), BF16), pltpu.VMEM((T + 2 * R, GW), BF16)],
        compiler_params=_params("parallel", "parallel", "arbitrary"),
        name=f"dilated_attention_g{g}",
    )(srcs[0][0], srcs[1][0], srcs[1][0], srcs[1][0], srcs[2][0], srcs[2][0], srcs[2][0])
    return o.reshape(M, GW), lse.reshape(M, LANES)


def _merge_kernel(x_ref, a_ref, o0_ref, o1_ref, o2_ref, l0_ref, l1_ref, l2_ref, ga_ref, gb_ref,
                  post_g_ref, next_g_ref, whp_ref, wap_ref, wo_ref, o_ref, u_ref):
    lses = [l0_ref[...], l1_ref[...], l2_ref[...]]
    mx = jnp.maximum(jnp.maximum(lses[0], lses[1]), lses[2])
    es = [jnp.exp(l - mx) for l in lses]
    den = es[0] + es[1] + es[2]
    wts = [e / den for e in es]
    outs = [o0_ref, o1_ref, o2_ref]
    heads = []
    for h in range(HEADS_PER_GROUP):
        cols = slice(h * HEAD_DIM, (h + 1) * HEAD_DIM)
        heads.append(sum(wts[g][:, h:h + 1] * outs[g][:, cols].astype(F32) for g in range(N_GROUPS)))
    att = jnp.concatenate(heads, axis=1).astype(BF16)
    a = _dot(a_ref[...], whp_ref[...])
    b = _dot(att, wap_ref[...])
    gate_a, gate_b = ga_ref[...].astype(F32), gb_ref[...].astype(F32)
    merged = (jax.nn.sigmoid(gate_a) * a + jax.nn.sigmoid(gate_b) * b).astype(BF16)
    mix = _dot(merged, wo_ref[...])
    y = x_ref[...] + _rms(mix, post_g_ref[...])
    o_ref[...] = y
    u_ref[...] = _rms(y, next_g_ref[...]).astype(u_ref.dtype)


def merge_out(x, a_in, att_outs, att_lses, gates, post_g, next_g, whp, wap, wo, *, tm=512):
    M, D = x.shape
    const = lambda i: (0, 0)
    rows = lambda a: pl.BlockSpec((tm, a.shape[1]), lambda i: (i, 0))
    return pl.pallas_call(
        _merge_kernel,
        grid=(M // tm,),
        in_specs=[rows(x), rows(a_in)] + [rows(o) for o in att_outs] + [rows(l) for l in att_lses] + [
            pl.BlockSpec((tm, D), lambda i: (i, 0)),
            pl.BlockSpec((tm, D), lambda i: (i, 1)),
            pl.BlockSpec((1, D), const),
            pl.BlockSpec((1, D), const),
            pl.BlockSpec(whp.shape, const, pipeline_mode=pl.Buffered(1)),
            pl.BlockSpec(wap.shape, const, pipeline_mode=pl.Buffered(1)),
            pl.BlockSpec(wo.shape, const, pipeline_mode=pl.Buffered(1)),
        ],
        out_specs=[pl.BlockSpec((tm, D), lambda i: (i, 0)), pl.BlockSpec((tm, D), lambda i: (i, 0))],
        out_shape=[jax.ShapeDtypeStruct((M, D), F32), jax.ShapeDtypeStruct((M, D), BF16)],
        compiler_params=_params("parallel"),
        name="merge_out",
    )(x, a_in, *att_outs, *att_lses, gates, gates, post_g, next_g, whp, wap, wo)


def _dft_constants(L):
    N = 2 * L
    N2 = FFT_INNER
    N1 = N // N2
    h = N1 // 2
    idx1 = np.arange(N1)
    ang1 = -2.0 * np.pi * ((idx1[:, None] * idx1[None, :]) % N1) / N1
    f1r, f1i = np.cos(ang1), np.sin(ang1)
    s_data = np.block([[f1r[:, :h], -f1i[:, :h]], [f1i[:, :h], f1r[:, :h]]])
    hk = h + FFT_ROWS
    s_filt = np.concatenate([f1r[:hk], f1i[:hk]], axis=0)
    ar, ai = f1r[:h, :], -f1i[:h, :]
    t_fin = np.block([[ar, -ai], [ai, ar]])
    idx2 = np.arange(N2)
    ang2 = -2.0 * np.pi * ((idx2[:, None] * idx2[None, :]) % N2) / N2
    angt = -2.0 * np.pi * (idx1[:, None] * idx2[None, :]) / N
    return dict(
        N1=N1,
        s_data=jnp.asarray(s_data, BF16), s_filt=jnp.asarray(s_filt, BF16), t_fin=jnp.asarray(t_fin, BF16),
        f2r=jnp.asarray(np.cos(ang2), F32), f2i=jnp.asarray(np.sin(ang2), F32),
        twr=jnp.asarray(np.cos(angt).reshape(N1, 1, N2), F32),
        twi=jnp.asarray(np.sin(angt).reshape(N1, 1, N2), F32),
    )


def _hyena_in_kernel(tm, L, x_ref, prev_ref, next_ref, w_ref, cw_ref, cb_ref, o_ref, xn_ref):
    i = pl.program_id(0)
    H = prev_ref.shape[0]

    @pl.when(pl.program_id(1) == 0)
    def _():
        xn_ref[0:H] = prev_ref[...]
        xn_ref[H:H + tm] = x_ref[...]
        xn_ref[H + tm:H + tm + H] = next_ref[...]

    r = _dot(xn_ref[...], w_ref[...])
    row = lax.broadcasted_iota(jnp.int32, r.shape, 0)
    pos0 = (i * tm) % L
    outside = ((row < H) & (pos0 == 0)) | ((row >= H + tm) & (pos0 + tm == L))
    r = jnp.where(outside, 0.0, r)
    n = r.shape[0]
    y = pltpu.roll(r, 1, 0) * cw_ref[0:1, :] + r * cw_ref[1:2, :] + pltpu.roll(r, n - 1, 0) * cw_ref[2:3, :]
    o_ref[0] = (y[H:H + tm] + cb_ref[...]).astype(o_ref.dtype)


def hyena_inputs(x, w, conv_w, conv_b, L, *, tm=1024):
    M, D = x.shape
    C = w.shape[1] // 3
    H = SLAB
    nh = M // H
    assert L % tm == 0 and tm % H == 0
    return pl.pallas_call(
        partial(_hyena_in_kernel, tm, L),
        grid=(M // tm, 3),
        in_specs=[
            pl.BlockSpec((tm, D), lambda i, j: (i, 0)),
            pl.BlockSpec((H, D), lambda i, j: (jnp.maximum(i * (tm // H) - 1, 0), 0)),
            pl.BlockSpec((H, D), lambda i, j: (jnp.minimum((i + 1) * (tm // H), nh - 1), 0)),
            pl.BlockSpec((D, C), lambda i, j: (0, j)),
            pl.BlockSpec((SHORT_CONV, C), lambda i, j: (0, j)),
            pl.BlockSpec((1, C), lambda i, j: (0, j)),
        ],
        out_specs=pl.BlockSpec((1, tm, C), lambda i, j: (j, i, 0)),
        out_shape=jax.ShapeDtypeStruct((3, M, C), BF16),
        scratch_shapes=[pltpu.VMEM((tm + 2 * H, D), BF16)],
        compiler_params=_params("parallel", "arbitrary"),
        name="hyena_inputs",
    )(x, x, x, w, conv_w, conv_b)


def _filter_kernel(L, h, w1_ref, b1_ref, w2_ref, b2_ref, w3_ref, b3_ref, w4hi_ref, w4lo_ref,
                   freq_ref, fvec_ref, delta_ref, kf_ref):
    i = pl.program_id(0)
    HALF = LANES // 2
    wide = delta_ref.shape[1]
    C = wide // (2 * HYENA_ORDER)
    tl = FFT_ROWS * h

    def lag(shape):
        r = lax.broadcasted_iota(jnp.int32, shape, 0)
        return (r % h) * FFT_INNER + i * FFT_ROWS + r // h

    row = lag((tl, LANES))
    lane = lax.broadcasted_iota(jnp.int32, (tl, LANES), 1)
    posf = jnp.where(lane < HALF, row, L - row).astype(F32)
    t = posf / (L - 1)
    a = fvec_ref[...] * ((2.0 * math.pi / L) * posf)
    lh = lane % HALF
    feats = jnp.where(lh < FILTER_BANDS, jnp.cos(a),
                      jnp.where(lh < 2 * FILTER_BANDS, -jnp.sin(a),
                                jnp.where(lh == 2 * FILTER_BANDS, t, 0.0)))
    freq = freq_ref[...]
    dense = lambda v, w_ref, b_ref: jnp.sin(freq * (
        jnp.dot(v, w_ref[...], precision=HIGHEST, preferred_element_type=F32) + b_ref[...]))
    hid = dense(dense(dense(feats, w1_ref, b1_ref), w2_ref, b2_ref), w3_ref, b3_ref)
    h_hi = hid.astype(BF16)
    h_lo = (hid - h_hi.astype(F32)).astype(BF16)
    taps = _dot(h_hi, w4hi_ref[...]) + _dot(h_lo, w4hi_ref[...]) + _dot(h_hi, w4lo_ref[...])
    row_w = lag((tl, wide))
    col_w = lax.broadcasted_iota(jnp.int32, (tl, wide), 1)
    backward = col_w >= wide // 2
    t_w = jnp.where(backward, L - row_w, row_w).astype(F32) / (L - 1)
    taps = taps * jnp.exp(-t_w * delta_ref[...])
    taps = jnp.where(backward & (row_w == 0), 0.0, taps)
    for n in range(HYENA_ORDER):
        for direction in range(2):
            c0 = (direction * HYENA_ORDER + n) * C
            part = taps[:, c0:c0 + C].astype(kf_ref.dtype).reshape(FFT_ROWS, h, C)
            kf_ref[n, 0, :, direction * h:(direction + 1) * h, :] = part


def hyena_filter_taps(L, fw1, fb1, fw2, fb2, fw3, fb3, fw4, ffreq):
    C = D_HYENA
    h = L // FFT_INNER
    groups = FFT_INNER // FFT_ROWS
    H = fw2.shape[0]
    HALF = LANES // 2
    assert H <= HALF and 2 * FILTER_BANDS + 1 <= HALF

    def both(a, rows):
        blk = jnp.zeros((HALF if rows else 1, HALF), F32).at[:a.shape[0], :a.shape[1]].set(a)
        if not rows:
            return jnp.concatenate([blk, blk], axis=1)
        zero = jnp.zeros_like(blk)
        return jnp.concatenate([jnp.concatenate([blk, zero], axis=1), jnp.concatenate([zero, blk], axis=1)], axis=0)

    w1 = both(jnp.concatenate([fw1[1:], fw1[:1]], axis=0), True)
    w2, w3 = both(fw2, True), both(fw3, True)
    b1, b2, b3, freq = (both(v[None, :], False) for v in (fb1, fb2, fb3, ffreq))
    w4 = fw4.reshape(H, HYENA_ORDER, 2, C)
    wide = 2 * HYENA_ORDER * C
    w4p = jnp.zeros((LANES, wide), F32)
    w4p = w4p.at[:H, :wide // 2].set(w4[:, :, 0].reshape(H, HYENA_ORDER * C))
    w4p = w4p.at[HALF:HALF + H, wide // 2:].set(w4[:, :, 1].reshape(H, HYENA_ORDER * C))
    w4hi = w4p.astype(BF16)
    w4lo = (w4p - w4hi.astype(F32)).astype(BF16)
    bands = np.linspace(1e-4, FILTER_BANDS - 1, FILTER_BANDS, dtype=np.float32)
    fvec = np.zeros((1, LANES), np.float32)
    for base in (0, HALF):
        fvec[0, base:base + FILTER_BANDS] = bands
        fvec[0, base + FILTER_BANDS:base + 2 * FILTER_BANDS] = bands
    max_decay = math.log(DECAY_TARGET) / FAST_DECAY_PCT
    min_decay = math.log(DECAY_TARGET) / SLOW_DECAY_PCT
    deltas = np.abs(np.linspace(min_decay, max_decay, C, dtype=np.float32))
    delta_w = np.tile(deltas[None, :], (1, 2 * HYENA_ORDER))
    const = lambda i: (0, 0)
    args = (w1, b1, w2, b2, w3, b3, w4hi, w4lo, freq, jnp.asarray(fvec), jnp.asarray(delta_w))
    return pl.pallas_call(
        partial(_filter_kernel, L, h),
        grid=(groups,),
        in_specs=[pl.BlockSpec(a.shape, const) for a in args],
        out_specs=pl.BlockSpec((HYENA_ORDER, 1, FFT_ROWS, 2 * h, C), lambda i: (0, i, 0, 0, 0)),
        out_shape=jax.ShapeDtypeStruct((HYENA_ORDER, groups, FFT_ROWS, 2 * h, C), BF16),
        compiler_params=_params("parallel"),
        name="hyena_filter_taps",
    )(*args)


FFT_ROWS = 16
FFT_COLS = 512
FFT_COLS_IN = 1024


def _stage1_kernel(inner_major, s_ref, x_ref, y_ref):
    s = s_ref[...]
    if inner_major:
        xt = x_ref[...]
    else:
        _, h, rows, cols = x_ref.shape
        xt = pltpu.einshape("abc->bac", x_ref[...].reshape(2 * h, rows, cols))
    yt = jnp.stack([_dot(s, xt[b]).astype(y_ref.dtype) for b in range(xt.shape[0])], axis=0)
    y_ref[...] = pltpu.einshape("bac->abc", yt).reshape(y_ref.shape)


def fft_stage1(x, which, s, *, inner_major=False):
    if inner_major:
        _, groups, _, N1, C = x.shape
        N2 = groups * FFT_ROWS
        x_spec = pl.BlockSpec((None, None, FFT_ROWS, N1, FFT_COLS_IN), lambda j, c: (which, j, 0, 0, c))
    else:
        _, _, h, N2, C = x.shape
        N1 = 2 * h
        x_spec = pl.BlockSpec((None, 2, h, FFT_ROWS, FFT_COLS_IN), lambda j, c: (which, 0, 0, j, c))
    return pl.pallas_call(
        partial(_stage1_kernel, inner_major),
        grid=(N2 // FFT_ROWS, C // FFT_COLS_IN),
        in_specs=[pl.BlockSpec(s.shape, lambda j, c: (0, 0)), x_spec],
        out_specs=pl.BlockSpec((2, s.shape[0] // 2, FFT_ROWS, FFT_COLS_IN), lambda j, c: (0, 0, j, c)),
        out_shape=jax.ShapeDtypeStruct((2, s.shape[0] // 2, N2, C), BF16),
        compiler_params=_params("parallel", "parallel"),
        name="fft_stage1",
    )(s, x)


FFT_K1_PER_STEP = 4


def _mid_kernel(inv_n, n1, f2r_ref, f2i_ref, twr_ref, twi_ref, *refs):
    kb = len(refs) - 2
    yf_refs, y_ref, u_ref = refs[:kb], refs[kb], refs[kb + 1]
    f2r, f2i = f2r_ref[...], f2i_ref[...]
    for k in range(kb):
        twr, twi = twr_ref[k], twi_ref[k]
        gr, gi = f2r * twr - f2i * twi, f2r * twi + f2i * twr
        grb, gib = gr.astype(BF16), gi.astype(BF16)

        def inner(re, im, sign=1.0):
            return _dot(grb, re) - sign * _dot(gib, im), _dot(gib, re) + sign * _dot(grb, im)

        conj = jnp.where(pl.program_id(0) * kb + k > n1 // 2, -1.0, 1.0)
        hr, hi = inner(yf_refs[k][0, 0], yf_refs[k][1, 0], conj)
        zr, zi = inner(y_ref[0, k], y_ref[1, k])
        pr = (zr * hr - zi * hi).astype(BF16)
        pi = (zr * hi + zi * hr).astype(BF16)
        irb = (gr.T * inv_n).astype(BF16)
        iib = (gi.T * (-inv_n)).astype(BF16)
        u_ref[0, k] = (_dot(irb, pr) - _dot(iib, pi)).astype(u_ref.dtype)
        u_ref[1, k] = (_dot(iib, pr) + _dot(irb, pi)).astype(u_ref.dtype)


def fft_mid(yf, y, dc):
    _, N1, N2, C = y.shape
    kb = FFT_K1_PER_STEP
    assert yf.shape[1] > N1 // 2
    blk = pl.BlockSpec((2, kb, N2, C), lambda k: (0, k, 0, 0))
    const = pl.BlockSpec((N2, N2), lambda k: (0, 0))
    tw = pl.BlockSpec((kb, 1, N2), lambda k: (k, 0, 0))

    def filt_blk(j):
        def index(k):
            q = k * kb + j
            return (0, jnp.where(q > N1 // 2, N1 - q, q), 0, 0)
        return pl.BlockSpec((2, 1, N2, C), index)

    return pl.pallas_call(
        partial(_mid_kernel, 1.0 / (N1 * N2), N1),
        grid=(N1 // kb,),
        in_specs=[const, const, tw, tw] + [filt_blk(j) for j in range(kb)] + [blk],
        out_specs=blk,
        out_shape=jax.ShapeDtypeStruct((2, N1, N2, C), BF16),
        compiler_params=_params("parallel"),
        name="fft_mid",
    )(dc['f2r'], dc['f2i'], dc['twr'], dc['twi'], *([yf] * kb), y)


def _final_kernel(t_ref, u_ref, z_ref, gate_ref, bias_ref, o_ref):
    _, n1, rows, cols = u_ref.shape
    t = t_ref[...]
    ut = pltpu.einshape("abc->bac", u_ref[...].reshape(2 * n1, rows, cols))
    convt = jnp.stack([_dot(t, ut[b]).astype(BF16) for b in range(rows)], axis=0)
    conv = pltpu.einshape("bac->abc", convt).reshape(o_ref.shape).astype(F32)
    z = z_ref[...].astype(F32)
    o_ref[...] = (gate_ref[...].astype(F32) * (conv + bias_ref[...] * z)).astype(o_ref.dtype)


def fft_final(u, t_fin, z, z_which, gate, gate_which, bias, bias_which):
    _, N1, N2, C = u.shape
    h = N1 // 2

    def half(which):
        return pl.BlockSpec((None, 2, h, FFT_ROWS, FFT_COLS), lambda j, c: (which, 0, 0, j, c))

    return pl.pallas_call(
        _final_kernel,
        grid=(N2 // FFT_ROWS, C // FFT_COLS),
        in_specs=[pl.BlockSpec((N1, 2 * N1), lambda j, c: (0, 0)),
                  pl.BlockSpec((2, N1, FFT_ROWS, FFT_COLS), lambda j, c: (0, 0, j, c)),
                  half(z_which), half(gate_which),
                  pl.BlockSpec((None, 1, FFT_COLS), lambda j, c: (bias_which, 0, c))],
        out_specs=half(0),
        out_shape=jax.ShapeDtypeStruct((1, 2, h, N2, C), BF16),
        compiler_params=_params("parallel", "parallel"),
        name="fft_final",
    )(t_fin, u, z, gate, bias.reshape(bias.shape[0], 1, C))


def hyena_branch(zs, L, p):
    _, M, C = zs.shape
    assert M == 2 * L, "the batch pair rides as real/imaginary parts"
    dc = _dft_constants(L)
    split = (2, dc['N1'] // 2, FFT_INNER, C)
    taps = hyena_filter_taps(L, *p['filt'])
    zs = zs.reshape(3, *split)
    z, z_which = zs, 0
    for n in range(HYENA_ORDER):
        yf = fft_stage1(taps, n, dc['s_filt'], inner_major=True)
        u = fft_mid(yf, fft_stage1(z, z_which, dc['s_data']), dc)
        z, z_which = fft_final(u, dc['t_fin'], z, z_which, zs, n + 1, p['hy_bias'], n), 0
    return z.reshape(M, C)


def _split_w_in(w):
    c_q = 3 * D_HYENA
    c_v = c_q + 2 * D_ATT
    c_g = c_v + D_ATT
    return tuple(w[:, a:b].astype(BF16) for a, b in ((0, c_q), (c_q, c_v), (c_v, c_g), (c_g, w.shape[1])))


def _layer(x, p):
    Bsz, L, D = x.shape
    M = Bsz * L
    x0 = x.reshape(M, D)
    x1, u = ffn_block(x0, p['ffn1_pre_g'], p['ffn1_post_g'], p['ffn1_w_gate'], p['ffn1_w_up'], p['ffn1_w_down'],
                      next_g=p['mix_pre_g'])
    w_hy, w_qk, w_v, w_gate = p['w_in']
    hy_in = hyena_inputs(u, w_hy, p['hy_conv_w'], p['hy_conv_b'], L)
    qk = projection(u, w_qk, BF16, tm=1024, tn=D_ATT, rope_len=L, rope=p['rope'], name="inproj_qk")
    v = projection(u, w_v, BF16, tm=1024, tn=D_ATT, name="inproj_v")
    gates = projection(u, w_gate, BF16, tm=1024, tn=D_MODEL, name="inproj_gates")
    a2 = hyena_branch(hy_in, L, p)
    att = []
    for g in range(N_GROUPS):
        srcs = ((qk, g), (qk, N_GROUPS + g), (v, g))
        fn = dilated_attention_group if ATT_GROUPS[g][1] == 1 else dilated_attention_slabs
        att.append(fn(srcs, Bsz, L, g))
    x2, xn2 = merge_out(x1, a2, [o for o, _ in att], [l for _, l in att], gates, p['mix_post_g'], p['ffn2_pre_g'],
                        p['w_hy_proj'], p['w_att_proj'], p['w_out'])
    x3 = ffn_block(x2, p['ffn2_pre_g'], p['ffn2_post_g'], p['ffn2_w_gate'], p['ffn2_w_up'], p['ffn2_w_down'],
                   xn=xn2)
    return x3.reshape(Bsz, L, D)


def kernel(x_prompt, x_sample, ffn1_pre_g, ffn1_post_g, ffn1_w_gate, ffn1_w_up, ffn1_w_down, mix_pre_g, mix_post_g, w_in, hy_conv_w, hy_conv_b, filt_w1, filt_b1, filt_w2, filt_b2, filt_w3, filt_b3, filt_w4, filt_freq, hy_bias, w_hy_proj, w_att_proj, w_out, ffn2_pre_g, ffn2_post_g, ffn2_w_gate, ffn2_w_up, ffn2_w_down):
    assert ffn1_w_gate.shape[0] == 1
    p = {
        'ffn1_pre_g': ffn1_pre_g, 'ffn1_post_g': ffn1_post_g,
        'ffn1_w_gate': ffn1_w_gate[0].astype(BF16), 'ffn1_w_up': ffn1_w_up[0].astype(BF16),
        'ffn1_w_down': ffn1_w_down[0].astype(BF16),
        'mix_pre_g': mix_pre_g, 'mix_post_g': mix_post_g,
        'w_in': _split_w_in(w_in[0]),
        'hy_conv_w': hy_conv_w[0], 'hy_conv_b': hy_conv_b,
        'filt': (filt_w1[0], filt_b1[0], filt_w2[0], filt_b2[0], filt_w3[0], filt_b3[0], filt_w4[0], filt_freq[0]),
        'hy_bias': hy_bias[0],
        'w_hy_proj': w_hy_proj[0].astype(BF16), 'w_att_proj': w_att_proj[0].astype(BF16),
        'w_out': w_out[0].astype(BF16),
        'ffn2_pre_g': ffn2_pre_g, 'ffn2_post_g': ffn2_post_g,
        'ffn2_w_gate': ffn2_w_gate[0].astype(BF16), 'ffn2_w_up': ffn2_w_up[0].astype(BF16),
        'ffn2_w_down': ffn2_w_down[0].astype(BF16),
    }
    p['rope'] = rope_tables(max(x_prompt.shape[1], x_sample.shape[1]))
    return (_layer(x_prompt, p), _layer(x_sample, p))
```

```python
import math
from functools import partial

import numpy as np
import jax
import jax.numpy as jnp
from jax import lax
from jax.experimental import pallas as pl
from jax.experimental.pallas import tpu as pltpu

D_MODEL = 2048
D_HYENA = 1024
HYENA_ORDER = 2
SHORT_CONV = 3
FILTER_EMB = 33
FILTER_BANDS = (FILTER_EMB - 1) // 2
FAST_DECAY_PCT = 0.3
SLOW_DECAY_PCT = 1.5
DECAY_TARGET = 1e-2
HEAD_DIM = 128
HEADS_PER_GROUP = 4
ATT_GROUPS = ((128, 1), (512, 4), (2048, 16))
N_GROUPS = len(ATT_GROUPS)
D_ATT = N_GROUPS * HEADS_PER_GROUP * HEAD_DIM
D_ATT_OUT = HEADS_PER_GROUP * HEAD_DIM
ROPE_DIM = HEAD_DIM // 4
ROPE_THETA = 500000.0
N_BRANCH = 2
D_IN_PROJ = 3 * D_HYENA + 3 * D_ATT + N_BRANCH * D_MODEL
D_FF = 5632
EPS = 1e-6
NEG_INF = -1e30

LANES = 128
FFT_INNER = 256
VMEM_BYTES_V7X = 64 * 1024 * 1024
VMEM_LIMIT_BYTES = VMEM_BYTES_V7X - 4 * 1024 * 1024
BF16 = jnp.bfloat16
F32 = jnp.float32
HIGHEST = lax.Precision.HIGHEST


def _rms(x, g):
    return x * lax.rsqrt(jnp.mean(x * x, axis=-1, keepdims=True) + EPS) * g


def _dot(a, b):
    return jnp.dot(a, b, preferred_element_type=F32)


def _params(*sem):
    return pltpu.CompilerParams(dimension_semantics=sem, vmem_limit_bytes=VMEM_LIMIT_BYTES)


FFN_SPLIT = 2


def _ffn_kernel(norm_in, norm_out, x_ref, xn_src_ref, post_g_ref, next_g_ref, wg_ref, wu_ref, wd_ref, *rest):
    o_ref = rest[0]
    u_ref = rest[1] if norm_out else None
    xn_ref = rest[-1] if norm_in else xn_src_ref
    j = pl.program_id(1)
    tf = wg_ref.shape[1]

    @pl.when(j == 0)
    def _():
        if norm_in:
            xn_ref[...] = _rms(x_ref[...], xn_src_ref[...]).astype(BF16)
        o_ref[...] = jnp.zeros_like(o_ref)

    xn = xn_ref[...]
    hs = []
    for c in range(FFN_SPLIT):
        cols = slice(c * (tf // FFN_SPLIT), (c + 1) * (tf // FFN_SPLIT))
        gate = _dot(xn, wg_ref[:, cols])
        up = _dot(xn, wu_ref[:, cols])
        hs.append((gate * jax.nn.sigmoid(gate) * up).astype(BF16))
    o_ref[...] += _dot(jnp.concatenate(hs, axis=1), wd_ref[...])

    @pl.when(j == pl.num_programs(1) - 1)
    def _():
        y = x_ref[...] + 0.5 * _rms(o_ref[...], post_g_ref[...])
        o_ref[...] = y
        if norm_out:
            u_ref[...] = _rms(y, next_g_ref[...]).astype(u_ref.dtype)


def ffn_block(x, pre_g, post_g, wg, wu, wd, *, xn=None, next_g=None, tm=512, tf=512):
    M, D = x.shape
    FF = wg.shape[1]
    norm_in, norm_out = xn is None, next_g is not None
    rows = pl.BlockSpec((tm, D), lambda i, j: (i, 0))
    gain = pl.BlockSpec((1, D), lambda i, j: (0, 0))
    out = pl.pallas_call(
        partial(_ffn_kernel, norm_in, norm_out),
        grid=(M // tm, FF // tf),
        in_specs=[
            rows,
            gain if norm_in else rows,
            gain, gain,
            pl.BlockSpec((D, tf), lambda i, j: (0, j)),
            pl.BlockSpec((D, tf), lambda i, j: (0, j)),
            pl.BlockSpec((tf, D), lambda i, j: (j, 0)),
        ],
        out_specs=[rows, rows] if norm_out else [rows],
        out_shape=[jax.ShapeDtypeStruct((M, D), F32)] + ([jax.ShapeDtypeStruct((M, D), BF16)] if norm_out else []),
        scratch_shapes=[pltpu.VMEM((tm, D), BF16)] if norm_in else [],
        compiler_params=_params("parallel", "arbitrary"),
        name="ffn_block",
    )(x, pre_g if norm_in else xn, post_g, next_g if norm_out else post_g, wg, wu, wd)
    return out if norm_out else out[0]


def rope_tables(L):
    half = ROPE_DIM // 2
    inv_freq = jnp.power(ROPE_THETA, -jnp.arange(half, dtype=F32) / half)
    ang = jnp.arange(L, dtype=F32)[:, None] * inv_freq[None, :]
    cos, sin = jnp.cos(ang), jnp.sin(ang)
    rest = HEAD_DIM - ROPE_DIM
    c = jnp.concatenate([cos, cos, jnp.ones((L, rest), F32)], axis=1)
    s_lo = jnp.concatenate([-sin, jnp.zeros((L, half + rest), F32)], axis=1)
    s_hi = jnp.concatenate([jnp.zeros((L, half), F32), sin, jnp.zeros((L, rest), F32)], axis=1)
    scale = HEAD_DIM ** -0.5
    return jnp.stack([c * scale, c]), jnp.stack([s_lo * scale, s_lo]), jnp.stack([s_hi * scale, s_hi])


def _proj_kernel(rotate, xn_ref, w_ref, *rest):
    o_ref = rest[-1]
    r = _dot(xn_ref[...], w_ref[...])
    if rotate:
        c_ref, slo_ref, shi_ref = rest[:3]
        tn = r.shape[1]
        half = ROPE_DIM // 2
        wide = lambda t_ref: jnp.concatenate([t_ref[...]] * (tn // HEAD_DIM), axis=1)
        r = r * wide(c_ref) + pltpu.roll(r, tn - half, 1) * wide(slo_ref) + pltpu.roll(r, half, 1) * wide(shi_ref)
    o_ref[...] = r.astype(o_ref.dtype)


def projection(xn, w, out_dtype, *, tm, tn, rope_len=None, rope=None, name):
    M, D = xn.shape
    N = w.shape[1]
    in_specs = [
        pl.BlockSpec((tm, D), lambda i, j: (i, 0)),
        pl.BlockSpec((D, tn), lambda i, j: (0, j)),
    ]
    args = [xn, w]
    if rope_len is not None:
        assert tn == D_ATT and N == 2 * D_ATT and rope_len % tm == 0
        spec = pl.BlockSpec((None, tm, HEAD_DIM), lambda i, j: (j, i % (rope_len // tm), 0))
        in_specs += [spec, spec, spec]
        assert all(t.shape[1] >= rope_len for t in rope)
        args += list(rope)
    return pl.pallas_call(
        partial(_proj_kernel, rope_len is not None),
        grid=(M // tm, N // tn),
        in_specs=in_specs,
        out_specs=pl.BlockSpec((tm, tn), lambda i, j: (i, j)),
        out_shape=jax.ShapeDtypeStruct((M, N), out_dtype),
        compiler_params=_params("parallel", "arbitrary"),
        name=name,
    )(*args)


ATT_RADIUS = 64
assert all(w // (2 * d) == ATT_RADIUS for w, d in ATT_GROUPS)


def _attn_kernel(T, Ls, q_ref, kp_ref, kc_ref, kn_ref, vp_ref, vc_ref, vn_ref, o_ref, lse_ref, kbuf, vbuf):
    t = pl.program_id(2)
    R = ATT_RADIUS
    SB = 2 * R
    for buf, prv, cur, nxt in ((kbuf, kp_ref, kc_ref, kn_ref), (vbuf, vp_ref, vc_ref, vn_ref)):
        buf[0:R] = prv[...]
        buf[R:R + T] = cur[...]
        buf[R + T:R + T + R] = nxt[...]
    qi = lax.broadcasted_iota(jnp.int32, (SB, 2 * SB), 0)
    kk = lax.broadcasted_iota(jnp.int32, (SB, 2 * SB), 1)
    band = (kk >= qi) & (kk <= qi + 2 * R)
    lane = lax.broadcasted_iota(jnp.int32, (SB, LANES), 1)
    for sb in range(T // SB):
        kpos = t * T + (sb * SB - R) + kk
        mask = band & (kpos >= 0) & (kpos < Ls)
        lse = jnp.zeros((SB, LANES), F32)
        for h in range(HEADS_PER_GROUP):
            cols = slice(h * HEAD_DIM, (h + 1) * HEAD_DIM)
            q = q_ref[sb * SB:(sb + 1) * SB, cols]
            k = kbuf[sb * SB:(sb + 2) * SB, cols]
            v = vbuf[sb * SB:(sb + 2) * SB, cols]
            s = lax.dot_general(q, k, (((1,), (1,)), ((), ())), preferred_element_type=F32)
            s = jnp.where(mask, s, NEG_INF)
            m = jnp.max(s, axis=1, keepdims=True)
            p = jnp.exp(s - m)
            l = jnp.sum(p, axis=1, keepdims=True)
            o = _dot(p.astype(BF16), v) / l
            o_ref[sb * SB:(sb + 1) * SB, cols] = o.astype(o_ref.dtype)
            lse = jnp.where(lane == h, m + jnp.log(l), lse)
        lse_ref[sb * SB:(sb + 1) * SB, :] = lse


SLAB = 16


def _softmax_heads(q, k, v, mask):
    lane = lax.broadcasted_iota(jnp.int32, (q.shape[0], LANES), 1)
    lse = jnp.zeros((q.shape[0], LANES), F32)
    outs = []
    for h in range(HEADS_PER_GROUP):
        cols = slice(h * HEAD_DIM, (h + 1) * HEAD_DIM)
        s = lax.dot_general(q[:, cols], k[:, cols], (((1,), (1,)), ((), ())), preferred_element_type=F32)
        s = jnp.where(mask, s, NEG_INF)
        m = jnp.max(s, axis=1, keepdims=True)
        p = jnp.exp(s - m)
        l = jnp.sum(p, axis=1, keepdims=True)
        outs.append(_dot(p.astype(BF16), v[:, cols]) / l)
        lse = jnp.where(lane == h, m + jnp.log(l), lse)
    return jnp.concatenate(outs, axis=1), lse


def _attn_slab_kernel(d, NS, Ls, q_ref, kp_ref, kc_ref, kn_ref, vp_ref, vc_ref, vn_ref, o_ref, lse_ref,
                      qs, ks, vs, os_, ls):
    t = pl.program_id(1)
    R = ATT_RADIUS
    SB = 2 * R
    J = SLAB // d
    NH = kp_ref.shape[0]
    QA = SB // J
    KA = QA + 2 * NH
    n_sb = NS // QA
    to_class_major = lambda ref: pltpu.einshape("abc->bac", ref[...])
    qs[...] = to_class_major(q_ref)
    for buf, prv, cur, nxt in ((ks, kp_ref, kc_ref, kn_ref), (vs, vp_ref, vc_ref, vn_ref)):
        buf[:, 0:NH] = to_class_major(prv)
        buf[:, NH:NH + NS] = to_class_major(cur)
        buf[:, NH + NS:NH + NS + NH] = to_class_major(nxt)
    qi = lax.broadcasted_iota(jnp.int32, (SB, 2 * SB), 0)
    kk = lax.broadcasted_iota(jnp.int32, (SB, 2 * SB), 1)
    sq_rel = J * (qi % QA) + qi // QA
    sk_rel = J * (kk % KA - NH) + kk // KA
    band = jnp.abs(sk_rel - sq_rel) <= R

    def body(it, carry):
        r = it // n_sb
        a0 = pl.multiple_of((it % n_sb) * QA, QA)
        sk = J * (t * NS + a0) + sk_rel
        mask = band & (sk >= 0) & (sk < Ls)
        gather = lambda buf, n: jnp.concatenate([buf[j * d + r, pl.ds(a0, n), :] for j in range(J)], axis=0)
        o, lse = _softmax_heads(gather(qs, QA), gather(ks, KA), gather(vs, KA), mask)
        o = o.astype(os_.dtype)
        for j in range(J):
            os_[j * d + r, pl.ds(a0, QA), :] = o[j * QA:(j + 1) * QA]
            ls[j * d + r, pl.ds(a0, QA), :] = lse[j * QA:(j + 1) * QA]
        return carry

    lax.fori_loop(0, d * n_sb, body, 0, unroll=True)
    o_ref[...] = pltpu.einshape("bac->abc", os_[...])
    lse_ref[...] = pltpu.einshape("bac->abc", ls[...])


def dilated_attention_slabs(srcs, Bsz, L, g, *, NS=128):
    d = ATT_GROUPS[g][1]
    M = Bsz * L
    R = ATT_RADIUS
    GW = HEADS_PER_GROUP * HEAD_DIM
    NH = R * d // SLAB
    TB = NS * SLAB
    assert SLAB % d == 0 and L % TB == 0 and NS % NH == 0 and (2 * R) % (SLAB // d) == 0
    nmb, nhb, hpm = L // TB, L // (NH * SLAB), NS // NH
    views = [a.reshape(M // SLAB, SLAB, a.shape[1]) for a, _ in srcs]
    col = lambda which: srcs[which][1]

    def main(which):
        return pl.BlockSpec((NS, SLAB, GW), lambda b, t: (b * nmb + t, 0, col(which)))

    def prev(which):
        return pl.BlockSpec((NH, SLAB, GW), lambda b, t: (jnp.maximum(b * nhb + t * hpm - 1, b * nhb), 0, col(which)))

    def nxt(which):
        return pl.BlockSpec((NH, SLAB, GW),
                            lambda b, t: (jnp.minimum(b * nhb + (t + 1) * hpm, (b + 1) * nhb - 1), 0, col(which)))

    o, lse = pl.pallas_call(
        partial(_attn_slab_kernel, d, NS, L // d),
        grid=(Bsz, nmb),
        in_specs=[main(0), prev(1), main(1), nxt(1), prev(2), main(2), nxt(2)],
        out_specs=[pl.BlockSpec((NS, SLAB, GW), lambda b, t: (b * nmb + t, 0, 0)),
                   pl.BlockSpec((NS, SLAB, LANES), lambda b, t: (b * nmb + t, 0, 0))],
        out_shape=[jax.ShapeDtypeStruct((M // SLAB, SLAB, GW), BF16),
                   jax.ShapeDtypeStruct((M // SLAB, SLAB, LANES), F32)],
        scratch_shapes=[pltpu.VMEM((SLAB, NS, GW), BF16),
                        pltpu.VMEM((SLAB, NS + 2 * NH, GW), BF16), pltpu.VMEM((SLAB, NS + 2 * NH, GW), BF16),
                        pltpu.VMEM((SLAB, NS, GW), BF16), pltpu.VMEM((SLAB, NS, LANES), F32)],
        compiler_params=_params("parallel", "arbitrary"),
        name=f"dilated_attention_g{g}",
    )(views[0], views[1], views[1], views[1], views[2], views[2], views[2])
    return o.reshape(M, GW), lse.reshape(M, LANES)


def dilated_attention_group(srcs, Bsz, L, g, *, T=1024):
    d = ATT_GROUPS[g][1]
    M = Bsz * L
    Ls = L // d
    R = ATT_RADIUS
    GW = HEADS_PER_GROUP * HEAD_DIM
    assert d == 1 and Ls % T == 0 and T % (2 * R) == 0
    nrb, nhb, hpt = Ls // T, Ls // R, T // R

    def main(which):
        return pl.BlockSpec((T, GW), lambda b, r, t: (b * nrb + t, srcs[which][1]))

    def prev(which):
        return pl.BlockSpec((R, GW), lambda b, r, t: (jnp.maximum(b * nhb + t * hpt - 1, b * nhb), srcs[which][1]))

    def nxt(which):
        return pl.BlockSpec((R, GW), lambda b, r, t: (jnp.minimum(b * nhb + (t + 1) * hpt, (b + 1) * nhb - 1),
                                                      srcs[which][1]))

    o, lse = pl.pallas_call(
        partial(_attn_kernel, T, Ls),
        grid=(Bsz, d, nrb),
        in_specs=[main(0), prev(1), main(1), nxt(1), prev(2), main(2), nxt(2)],
        out_specs=[pl.BlockSpec((T, GW), lambda b, r, t: (b * nrb + t, r)),
                   pl.BlockSpec((T, LANES), lambda b, r, t: (b * nrb + t, r))],
        out_shape=[jax.ShapeDtypeStruct((M // d, d * GW), BF16),
                   jax.ShapeDtypeStruct((M // d, d * LANES), F32)],
        scratch_shapes=[pltpu.VMEM((T + 2 * R, GW), BF16), pltpu.VMEM((T + 2 * R, GW), BF16)],
        compiler_params=_params("parallel", "parallel", "arbitrary"),
        name=f"dilated_attention_g{g}",
    )(srcs[0][0], srcs[1][0], srcs[1][0], srcs[1][0], srcs[2][0], srcs[2][0], srcs[2][0])
    return o.reshape(M, GW), lse.reshape(M, LANES)


def _merge_kernel(x_ref, a_ref, o0_ref, o1_ref, o2_ref, l0_ref, l1_ref, l2_ref, ga_ref, gb_ref,
                  post_g_ref, next_g_ref, whp_ref, wap_ref, wo_ref, o_ref, u_ref):
    lses = [l0_ref[...], l1_ref[...], l2_ref[...]]
    mx = jnp.maximum(jnp.maximum(lses[0], lses[1]), lses[2])
    es = [jnp.exp(l - mx) for l in lses]
    den = es[0] + es[1] + es[2]
    wts = [e / den for e in es]
    outs = [o0_ref, o1_ref, o2_ref]
    heads = []
    for h in range(HEADS_PER_GROUP):
        cols = slice(h * HEAD_DIM, (h + 1) * HEAD_DIM)
        heads.append(sum(wts[g][:, h:h + 1] * outs[g][:, cols].astype(F32) for g in range(N_GROUPS)))
    att = jnp.concatenate(heads, axis=1).astype(BF16)
    a = _dot(a_ref[...], whp_ref[...])
    b = _dot(att, wap_ref[...])
    gate_a, gate_b = ga_ref[...].astype(F32), gb_ref[...].astype(F32)
    merged = (jax.nn.sigmoid(gate_a) * a + jax.nn.sigmoid(gate_b) * b).astype(BF16)
    mix = _dot(merged, wo_ref[...])
    y = x_ref[...] + _rms(mix, post_g_ref[...])
    o_ref[...] = y
    u_ref[...] = _rms(y, next_g_ref[...]).astype(u_ref.dtype)


def merge_out(x, a_in, att_outs, att_lses, gates, post_g, next_g, whp, wap, wo, *, tm=512):
    M, D = x.shape
    const = lambda i: (0, 0)
    rows = lambda a: pl.BlockSpec((tm, a.shape[1]), lambda i: (i, 0))
    return pl.pallas_call(
        _merge_kernel,
        grid=(M // tm,),
        in_specs=[rows(x), rows(a_in)] + [rows(o) for o in att_outs] + [rows(l) for l in att_lses] + [
            pl.BlockSpec((tm, D), lambda i: (i, 0)),
            pl.BlockSpec((tm, D), lambda i: (i, 1)),
            pl.BlockSpec((1, D), const),
            pl.BlockSpec((1, D), const),
            pl.BlockSpec(whp.shape, const, pipeline_mode=pl.Buffered(1)),
            pl.BlockSpec(wap.shape, const, pipeline_mode=pl.Buffered(1)),
            pl.BlockSpec(wo.shape, const, pipeline_mode=pl.Buffered(1)),
        ],
        out_specs=[pl.BlockSpec((tm, D), lambda i: (i, 0)), pl.BlockSpec((tm, D), lambda i: (i, 0))],
        out_shape=[jax.ShapeDtypeStruct((M, D), F32), jax.ShapeDtypeStruct((M, D), BF16)],
        compiler_params=_params("parallel"),
        name="merge_out",
    )(x, a_in, *att_outs, *att_lses, gates, gates, post_g, next_g, whp, wap, wo)


def _dft_constants(L):
    N = 2 * L
    N2 = FFT_INNER
    N1 = N // N2
    h = N1 // 2
    idx1 = np.arange(N1)
    ang1 = -2.0 * np.pi * ((idx1[:, None] * idx1[None, :]) % N1) / N1
    f1r, f1i = np.cos(ang1), np.sin(ang1)
    s_data = np.block([[f1r[:, :h], -f1i[:, :h]], [f1i[:, :h], f1r[:, :h]]])
    hk = h + FFT_ROWS
    s_filt = np.concatenate([f1r[:hk], f1i[:hk]], axis=0)
    ar, ai = f1r[:h, :], -f1i[:h, :]
    t_fin = np.block([[ar, -ai], [ai, ar]])
    idx2 = np.arange(N2)
    ang2 = -2.0 * np.pi * ((idx2[:, None] * idx2[None, :]) % N2) / N2
    angt = -2.0 * np.pi * (idx1[:, None] * idx2[None, :]) / N
    return dict(
        N1=N1,
        s_data=jnp.asarray(s_data, BF16), s_filt=jnp.asarray(s_filt, BF16), t_fin=jnp.asarray(t_fin, BF16),
        f2r=jnp.asarray(np.cos(ang2), F32), f2i=jnp.asarray(np.sin(ang2), F32),
        twr=jnp.asarray(np.cos(angt).reshape(N1, 1, N2), F32),
        twi=jnp.asarray(np.sin(angt).reshape(N1, 1, N2), F32),
    )


def _hyena_in_kernel(tm, L, x_ref, prev_ref, next_ref, w_ref, cw_ref, cb_ref, o_ref, xn_ref):
    i = pl.program_id(0)
    H = prev_ref.shape[0]

    @pl.when(pl.program_id(1) == 0)
    def _():
        xn_ref[0:H] = prev_ref[...]
        xn_ref[H:H + tm] = x_ref[...]
        xn_ref[H + tm:H + tm + H] = next_ref[...]

    r = _dot(xn_ref[...], w_ref[...])
    row = lax.broadcasted_iota(jnp.int32, r.shape, 0)
    pos0 = (i * tm) % L
    outside = ((row < H) & (pos0 == 0)) | ((row >= H + tm) & (pos0 + tm == L))
    r = jnp.where(outside, 0.0, r)
    n = r.shape[0]
    y = pltpu.roll(r, 1, 0) * cw_ref[0:1, :] + r * cw_ref[1:2, :] + pltpu.roll(r, n - 1, 0) * cw_ref[2:3, :]
    o_ref[0] = (y[H:H + tm] + cb_ref[...]).astype(o_ref.dtype)


def hyena_inputs(x, w, conv_w, conv_b, L, *, tm=1024):
    M, D = x.shape
    C = w.shape[1] // 3
    H = SLAB
    nh = M // H
    assert L % tm == 0 and tm % H == 0
    return pl.pallas_call(
        partial(_hyena_in_kernel, tm, L),
        grid=(M // tm, 3),
        in_specs=[
            pl.BlockSpec((tm, D), lambda i, j: (i, 0)),
            pl.BlockSpec((H, D), lambda i, j: (jnp.maximum(i * (tm // H) - 1, 0), 0)),
            pl.BlockSpec((H, D), lambda i, j: (jnp.minimum((i + 1) * (tm // H), nh - 1), 0)),
            pl.BlockSpec((D, C), lambda i, j: (0, j)),
            pl.BlockSpec((SHORT_CONV, C), lambda i, j: (0, j)),
            pl.BlockSpec((1, C), lambda i, j: (0, j)),
        ],
        out_specs=pl.BlockSpec((1, tm, C), lambda i, j: (j, i, 0)),
        out_shape=jax.ShapeDtypeStruct((3, M, C), BF16),
        scratch_shapes=[pltpu.VMEM((tm + 2 * H, D), BF16)],
        compiler_params=_params("parallel", "arbitrary"),
        name="hyena_inputs",
    )(x, x, x, w, conv_w, conv_b)


def _filter_kernel(L, h, w1_ref, b1_ref, w2_ref, b2_ref, w3_ref, b3_ref, w4hi_ref, w4lo_ref,
                   freq_ref, fvec_ref, delta_ref, kf_ref):
    i = pl.program_id(0)
    HALF = LANES // 2
    wide = delta_ref.shape[1]
    C = wide // (2 * HYENA_ORDER)
    tl = FFT_ROWS * h

    def lag(shape):
        r = lax.broadcasted_iota(jnp.int32, shape, 0)
        return (r % h) * FFT_INNER + i * FFT_ROWS + r // h

    row = lag((tl, LANES))
    lane = lax.broadcasted_iota(jnp.int32, (tl, LANES), 1)
    posf = jnp.where(lane < HALF, row, L - row).astype(F32)
    t = posf / (L - 1)
    a = fvec_ref[...] * ((2.0 * math.pi / L) * posf)
    lh = lane % HALF
    feats = jnp.where(lh < FILTER_BANDS, jnp.cos(a),
                      jnp.where(lh < 2 * FILTER_BANDS, -jnp.sin(a),
                                jnp.where(lh == 2 * FILTER_BANDS, t, 0.0)))
    freq = freq_ref[...]
    dense = lambda v, w_ref, b_ref: jnp.sin(freq * (
        jnp.dot(v, w_ref[...], precision=HIGHEST, preferred_element_type=F32) + b_ref[...]))
    hid = dense(dense(dense(feats, w1_ref, b1_ref), w2_ref, b2_ref), w3_ref, b3_ref)
    h_hi = hid.astype(BF16)
    h_lo = (hid - h_hi.astype(F32)).astype(BF16)
    taps = _dot(h_hi, w4hi_ref[...]) + _dot(h_lo, w4hi_ref[...]) + _dot(h_hi, w4lo_ref[...])
    row_w = lag((tl, wide))
    col_w = lax.broadcasted_iota(jnp.int32, (tl, wide), 1)
    backward = col_w >= wide // 2
    t_w = jnp.where(backward, L - row_w, row_w).astype(F32) / (L - 1)
    taps = taps * jnp.exp(-t_w * delta_ref[...])
    taps = jnp.where(backward & (row_w == 0), 0.0, taps)
    for n in range(HYENA_ORDER):
        for direction in range(2):
            c0 = (direction * HYENA_ORDER + n) * C
            part = taps[:, c0:c0 + C].astype(kf_ref.dtype).reshape(FFT_ROWS, h, C)
            kf_ref[n, 0, :, direction * h:(direction + 1) * h, :] = part


def hyena_filter_taps(L, fw1, fb1, fw2, fb2, fw3, fb3, fw4, ffreq):
    C = D_HYENA
    h = L // FFT_INNER
    groups = FFT_INNER // FFT_ROWS
    H = fw2.shape[0]
    HALF = LANES // 2
    assert H <= HALF and 2 * FILTER_BANDS + 1 <= HALF

    def both(a, rows):
        blk = jnp.zeros((HALF if rows else 1, HALF), F32).at[:a.shape[0], :a.shape[1]].set(a)
        if not rows:
            return jnp.concatenate([blk, blk], axis=1)
        zero = jnp.zeros_like(blk)
        return jnp.concatenate([jnp.concatenate([blk, zero], axis=1), jnp.concatenate([zero, blk], axis=1)], axis=0)

    w1 = both(jnp.concatenate([fw1[1:], fw1[:1]], axis=0), True)
    w2, w3 = both(fw2, True), both(fw3, True)
    b1, b2, b3, freq = (both(v[None, :], False) for v in (fb1, fb2, fb3, ffreq))
    w4 = fw4.reshape(H, HYENA_ORDER, 2, C)
    wide = 2 * HYENA_ORDER * C
    w4p = jnp.zeros((LANES, wide), F32)
    w4p = w4p.at[:H, :wide // 2].set(w4[:, :, 0].reshape(H, HYENA_ORDER * C))
    w4p = w4p.at[HALF:HALF + H, wide // 2:].set(w4[:, :, 1].reshape(H, HYENA_ORDER * C))
    w4hi = w4p.astype(BF16)
    w4lo = (w4p - w4hi.astype(F32)).astype(BF16)
    bands = np.linspace(1e-4, FILTER_BANDS - 1, FILTER_BANDS, dtype=np.float32)
    fvec = np.zeros((1, LANES), np.float32)
    for base in (0, HALF):
        fvec[0, base:base + FILTER_BANDS] = bands
        fvec[0, base + FILTER_BANDS:base + 2 * FILTER_BANDS] = bands
    max_decay = math.log(DECAY_TARGET) / FAST_DECAY_PCT
    min_decay = math.log(DECAY_TARGET) / SLOW_DECAY_PCT
    deltas = np.abs(np.linspace(min_decay, max_decay, C, dtype=np.float32))
    delta_w = np.tile(deltas[None, :], (1, 2 * HYENA_ORDER))
    const = lambda i: (0, 0)
    args = (w1, b1, w2, b2, w3, b3, w4hi, w4lo, freq, jnp.asarray(fvec), jnp.asarray(delta_w))
    return pl.pallas_call(
        partial(_filter_kernel, L, h),
        grid=(groups,),
        in_specs=[pl.BlockSpec(a.shape, const) for a in args],
        out_specs=pl.BlockSpec((HYENA_ORDER, 1, FFT_ROWS, 2 * h, C), lambda i: (0, i, 0, 0, 0)),
        out_shape=jax.ShapeDtypeStruct((HYENA_ORDER, groups, FFT_ROWS, 2 * h, C), BF16),
        compiler_params=_params("parallel"),
        name="hyena_filter_taps",
    )(*args)


FFT_ROWS = 16
FFT_COLS = 512
FFT_COLS_IN = 1024


def _stage1_kernel(inner_major, s_ref, x_ref, y_ref):
    s = s_ref[...]
    if inner_major:
        xt = x_ref[...]
    else:
        _, h, rows, cols = x_ref.shape
        xt = pltpu.einshape("abc->bac", x_ref[...].reshape(2 * h, rows, cols))
    yt = jnp.stack([_dot(s, xt[b]).astype(y_ref.dtype) for b in range(xt.shape[0])], axis=0)
    y_ref[...] = pltpu.einshape("bac->abc", yt).reshape(y_ref.shape)


def fft_stage1(x, which, s, *, inner_major=False):
    if inner_major:
        _, groups, _, N1, C = x.shape
        N2 = groups * FFT_ROWS
        x_spec = pl.BlockSpec((None, None, FFT_ROWS, N1, FFT_COLS_IN), lambda j, c: (which, j, 0, 0, c))
    else:
        _, _, h, N2, C = x.shape
        N1 = 2 * h
        x_spec = pl.BlockSpec((None, 2, h, FFT_ROWS, FFT_COLS_IN), lambda j, c: (which, 0, 0, j, c))
    return pl.pallas_call(
        partial(_stage1_kernel, inner_major),
        grid=(N2 // FFT_ROWS, C // FFT_COLS_IN),
        in_specs=[pl.BlockSpec(s.shape, lambda j, c: (0, 0)), x_spec],
        out_specs=pl.BlockSpec((2, s.shape[0] // 2, FFT_ROWS, FFT_COLS_IN), lambda j, c: (0, 0, j, c)),
        out_shape=jax.ShapeDtypeStruct((2, s.shape[0] // 2, N2, C), BF16),
        compiler_params=_params("parallel", "parallel"),
        name="fft_stage1",
    )(s, x)


FFT_K1_PER_STEP = 4


def _mid_kernel(inv_n, n1, f2r_ref, f2i_ref, twr_ref, twi_ref, *refs):
    kb = len(refs) - 2
    yf_refs, y_ref, u_ref = refs[:kb], refs[kb], refs[kb + 1]
    f2r, f2i = f2r_ref[...], f2i_ref[...]
    for k in range(kb):
        twr, twi = twr_ref[k], twi_ref[k]
        gr, gi = f2r * twr - f2i * twi, f2r * twi + f2i * twr
        grb, gib = gr.astype(BF16), gi.astype(BF16)

        def inner(re, im):
            return _dot(grb, re) - _dot(gib, im), _dot(gib, re) + _dot(grb, im)

        mirrored = pl.program_id(0) * kb + k > n1 // 2
        yf_im = yf_refs[k][1, 0]
        hr, hi = inner(yf_refs[k][0, 0], jnp.where(mirrored, -yf_im, yf_im))
        zr, zi = inner(y_ref[0, k], y_ref[1, k])
        pr = (zr * hr - zi * hi).astype(BF16)
        pi = (zr * hi + zi * hr).astype(BF16)
        irb = (gr.T * inv_n).astype(BF16)
        iib = (gi.T * (-inv_n)).astype(BF16)
        u_ref[0, k] = (_dot(irb, pr) - _dot(iib, pi)).astype(u_ref.dtype)
        u_ref[1, k] = (_dot(iib, pr) + _dot(irb, pi)).astype(u_ref.dtype)


def fft_mid(yf, y, dc):
    _, N1, N2, C = y.shape
    kb = FFT_K1_PER_STEP
    assert yf.shape[1] > N1 // 2
    blk = pl.BlockSpec((2, kb, N2, C), lambda k: (0, k, 0, 0))
    const = pl.BlockSpec((N2, N2), lambda k: (0, 0))
    tw = pl.BlockSpec((kb, 1, N2), lambda k: (k, 0, 0))

    def filt_blk(j):
        def index(k):
            q = k * kb + j
            return (0, jnp.where(q > N1 // 2, N1 - q, q), 0, 0)
        return pl.BlockSpec((2, 1, N2, C), index)

    return pl.pallas_call(
        partial(_mid_kernel, 1.0 / (N1 * N2), N1),
        grid=(N1 // kb,),
        in_specs=[const, const, tw, tw] + [filt_blk(j) for j in range(kb)] + [blk],
        out_specs=blk,
        out_shape=jax.ShapeDtypeStruct((2, N1, N2, C), BF16),
        compiler_params=_params("parallel"),
        name="fft_mid",
    )(dc['f2r'], dc['f2i'], dc['twr'], dc['twi'], *([yf] * kb), y)


def _final_kernel(t_ref, u_ref, z_ref, gate_ref, bias_ref, o_ref):
    _, n1, rows, cols = u_ref.shape
    t = t_ref[...]
    ut = pltpu.einshape("abc->bac", u_ref[...].reshape(2 * n1, rows, cols))
    convt = jnp.stack([_dot(t, ut[b]).astype(BF16) for b in range(rows)], axis=0)
    conv = pltpu.einshape("bac->abc", convt).reshape(o_ref.shape).astype(F32)
    z = z_ref[...].astype(F32)
    o_ref[...] = (gate_ref[...].astype(F32) * (conv + bias_ref[...] * z)).astype(o_ref.dtype)


def fft_final(u, t_fin, z, z_which, gate, gate_which, bias, bias_which):
    _, N1, N2, C = u.shape
    h = N1 // 2

    def half(which):
        return pl.BlockSpec((None, 2, h, FFT_ROWS, FFT_COLS), lambda j, c: (which, 0, 0, j, c))

    return pl.pallas_call(
        _final_kernel,
        grid=(N2 // FFT_ROWS, C // FFT_COLS),
        in_specs=[pl.BlockSpec((N1, 2 * N1), lambda j, c: (0, 0)),
                  pl.BlockSpec((2, N1, FFT_ROWS, FFT_COLS), lambda j, c: (0, 0, j, c)),
                  half(z_which), half(gate_which),
                  pl.BlockSpec((None, 1, FFT_COLS), lambda j, c: (bias_which, 0, c))],
        out_specs=half(0),
        out_shape=jax.ShapeDtypeStruct((1, 2, h, N2, C), BF16),
        compiler_params=_params("parallel", "parallel"),
        name="fft_final",
    )(t_fin, u, z, gate, bias.reshape(bias.shape[0], 1, C))


def hyena_branch(zs, L, p):
    _, M, C = zs.shape
    assert M == 2 * L, "the batch pair rides as real/imaginary parts"
    dc = _dft_constants(L)
    split = (2, dc['N1'] // 2, FFT_INNER, C)
    taps = hyena_filter_taps(L, *p['filt'])
    zs = zs.reshape(3, *split)
    z, z_which = zs, 0
    for n in range(HYENA_ORDER):
        yf = fft_stage1(taps, n, dc['s_filt'], inner_major=True)
        u = fft_mid(yf, fft_stage1(z, z_which, dc['s_data']), dc)
        z, z_which = fft_final(u, dc['t_fin'], z, z_which, zs, n + 1, p['hy_bias'], n), 0
    return z.reshape(M, C)


def _split_w_in(w):
    c_q = 3 * D_HYENA
    c_v = c_q + 2 * D_ATT
    c_g = c_v + D_ATT
    return tuple(w[:, a:b].astype(BF16) for a, b in ((0, c_q), (c_q, c_v), (c_v, c_g), (c_g, w.shape[1])))


def _layer(x, p):
    Bsz, L, D = x.shape
    M = Bsz * L
    x0 = x.reshape(M, D)
    x1, u = ffn_block(x0, p['ffn1_pre_g'], p['ffn1_post_g'], p['ffn1_w_gate'], p['ffn1_w_up'], p['ffn1_w_down'],
                      next_g=p['mix_pre_g'])
    w_hy, w_qk, w_v, w_gate = p['w_in']
    hy_in = hyena_inputs(u, w_hy, p['hy_conv_w'], p['hy_conv_b'], L)
    qk = projection(u, w_qk, BF16, tm=1024, tn=D_ATT, rope_len=L, rope=p['rope'], name="inproj_qk")
    v = projection(u, w_v, BF16, tm=1024, tn=D_ATT, name="inproj_v")
    gates = projection(u, w_gate, BF16, tm=1024, tn=D_MODEL, name="inproj_gates")
    a2 = hyena_branch(hy_in, L, p)
    att = []
    for g in range(N_GROUPS):
        srcs = ((qk, g), (qk, N_GROUPS + g), (v, g))
        fn = dilated_attention_group if ATT_GROUPS[g][1] == 1 else dilated_attention_slabs
        att.append(fn(srcs, Bsz, L, g))
    x2, xn2 = merge_out(x1, a2, [o for o, _ in att], [l for _, l in att], gates, p['mix_post_g'], p['ffn2_pre_g'],
                        p['w_hy_proj'], p['w_att_proj'], p['w_out'])
    x3 = ffn_block(x2, p['ffn2_pre_g'], p['ffn2_post_g'], p['ffn2_w_gate'], p['ffn2_w_up'], p['ffn2_w_down'],
                   xn=xn2)
    return x3.reshape(Bsz, L, D)


def kernel(x_prompt, x_sample, ffn1_pre_g, ffn1_post_g, ffn1_w_gate, ffn1_w_up, ffn1_w_down, mix_pre_g, mix_post_g, w_in, hy_conv_w, hy_conv_b, filt_w1, filt_b1, filt_w2, filt_b2, filt_w3, filt_b3, filt_w4, filt_freq, hy_bias, w_hy_proj, w_att_proj, w_out, ffn2_pre_g, ffn2_post_g, ffn2_w_gate, ffn2_w_up, ffn2_w_down):
    assert ffn1_w_gate.shape[0] == 1
    p = {
        'ffn1_pre_g': ffn1_pre_g, 'ffn1_post_g': ffn1_post_g,
        'ffn1_w_gate': ffn1_w_gate[0].astype(BF16), 'ffn1_w_up': ffn1_w_up[0].astype(BF16),
        'ffn1_w_down': ffn1_w_down[0].astype(BF16),
        'mix_pre_g': mix_pre_g, 'mix_post_g': mix_post_g,
        'w_in': _split_w_in(w_in[0]),
        'hy_conv_w': hy_conv_w[0], 'hy_conv_b': hy_conv_b,
        'filt': (filt_w1[0], filt_b1[0], filt_w2[0], filt_b2[0], filt_w3[0], filt_b3[0], filt_w4[0], filt_freq[0]),
        'hy_bias': hy_bias[0],
        'w_hy_proj': w_hy_proj[0].astype(BF16), 'w_att_proj': w_att_proj[0].astype(BF16),
        'w_out': w_out[0].astype(BF16),
        'ffn2_pre_g': ffn2_pre_g, 'ffn2_post_g': ffn2_post_g,
        'ffn2_w_gate': ffn2_w_gate[0].astype(BF16), 'ffn2_w_up': ffn2_w_up[0].astype(BF16),
        'ffn2_w_down': ffn2_w_down[0].astype(BF16),
    }
    p['rope'] = rope_tables(max(x_prompt.shape[1], x_sample.shape[1]))
    return (_layer(x_prompt, p), _layer(x_sample, p))
```

```python
import math
from functools import partial

import numpy as np
import jax
import jax.numpy as jnp
from jax import lax
from jax.experimental import pallas as pl
from jax.experimental.pallas import tpu as pltpu

D_MODEL = 2048
D_HYENA = 1024
HYENA_ORDER = 2
SHORT_CONV = 3
FILTER_EMB = 33
FILTER_BANDS = (FILTER_EMB - 1) // 2
FAST_DECAY_PCT = 0.3
SLOW_DECAY_PCT = 1.5
DECAY_TARGET = 1e-2
HEAD_DIM = 128
HEADS_PER_GROUP = 4
ATT_GROUPS = ((128, 1), (512, 4), (2048, 16))
N_GROUPS = len(ATT_GROUPS)
D_ATT = N_GROUPS * HEADS_PER_GROUP * HEAD_DIM
D_ATT_OUT = HEADS_PER_GROUP * HEAD_DIM
ROPE_DIM = HEAD_DIM // 4
ROPE_THETA = 500000.0
N_BRANCH = 2
D_IN_PROJ = 3 * D_HYENA + 3 * D_ATT + N_BRANCH * D_MODEL
D_FF = 5632
EPS = 1e-6
NEG_INF = -1e30

LANES = 128
FFT_INNER = 256
VMEM_BYTES_V7X = 64 * 1024 * 1024
VMEM_LIMIT_BYTES = VMEM_BYTES_V7X - 4 * 1024 * 1024
BF16 = jnp.bfloat16
F32 = jnp.float32
HIGHEST = lax.Precision.HIGHEST


def _rms(x, g):
    return x * lax.rsqrt(jnp.mean(x * x, axis=-1, keepdims=True) + EPS) * g


def _dot(a, b):
    return jnp.dot(a, b, preferred_element_type=F32)


def _params(*sem):
    return pltpu.CompilerParams(dimension_semantics=sem, vmem_limit_bytes=VMEM_LIMIT_BYTES)


FFN_SPLIT = 2


def _ffn_kernel(norm_in, norm_out, x_ref, xn_src_ref, post_g_ref, next_g_ref, wg_ref, wu_ref, wd_ref, *rest):
    o_ref = rest[0]
    u_ref = rest[1] if norm_out else None
    xn_ref = rest[-1] if norm_in else xn_src_ref
    j = pl.program_id(1)
    tf = wg_ref.shape[1]

    @pl.when(j == 0)
    def _():
        if norm_in:
            xn_ref[...] = _rms(x_ref[...], xn_src_ref[...]).astype(BF16)
        o_ref[...] = jnp.zeros_like(o_ref)

    xn = xn_ref[...]
    hs = []
    for c in range(FFN_SPLIT):
        cols = slice(c * (tf // FFN_SPLIT), (c + 1) * (tf // FFN_SPLIT))
        gate = _dot(xn, wg_ref[:, cols])
        up = _dot(xn, wu_ref[:, cols])
        hs.append((gate * jax.nn.sigmoid(gate) * up).astype(BF16))
    o_ref[...] += _dot(jnp.concatenate(hs, axis=1), wd_ref[...])

    @pl.when(j == pl.num_programs(1) - 1)
    def _():
        y = x_ref[...] + 0.5 * _rms(o_ref[...], post_g_ref[...])
        o_ref[...] = y
        if norm_out:
            u_ref[...] = _rms(y, next_g_ref[...]).astype(u_ref.dtype)


def ffn_block(x, pre_g, post_g, wg, wu, wd, *, xn=None, next_g=None, tm=512, tf=512):
    M, D = x.shape
    FF = wg.shape[1]
    norm_in, norm_out = xn is None, next_g is not None
    rows = pl.BlockSpec((tm, D), lambda i, j: (i, 0))
    gain = pl.BlockSpec((1, D), lambda i, j: (0, 0))
    out = pl.pallas_call(
        partial(_ffn_kernel, norm_in, norm_out),
        grid=(M // tm, FF // tf),
        in_specs=[
            rows,
            gain if norm_in else rows,
            gain, gain,
            pl.BlockSpec((D, tf), lambda i, j: (0, j)),
            pl.BlockSpec((D, tf), lambda i, j: (0, j)),
            pl.BlockSpec((tf, D), lambda i, j: (j, 0)),
        ],
        out_specs=[rows, rows] if norm_out else [rows],
        out_shape=[jax.ShapeDtypeStruct((M, D), F32)] + ([jax.ShapeDtypeStruct((M, D), BF16)] if norm_out else []),
        scratch_shapes=[pltpu.VMEM((tm, D), BF16)] if norm_in else [],
        compiler_params=_params("parallel", "arbitrary"),
        name="ffn_block",
    )(x, pre_g if norm_in else xn, post_g, next_g if norm_out else post_g, wg, wu, wd)
    return out if norm_out else out[0]


def rope_tables(L):
    half = ROPE_DIM // 2
    inv_freq = jnp.power(ROPE_THETA, -jnp.arange(half, dtype=F32) / half)
    ang = jnp.arange(L, dtype=F32)[:, None] * inv_freq[None, :]
    cos, sin = jnp.cos(ang), jnp.sin(ang)
    rest = HEAD_DIM - ROPE_DIM
    c = jnp.concatenate([cos, cos, jnp.ones((L, rest), F32)], axis=1)
    s_lo = jnp.concatenate([-sin, jnp.zeros((L, half + rest), F32)], axis=1)
    s_hi = jnp.concatenate([jnp.zeros((L, half), F32), sin, jnp.zeros((L, rest), F32)], axis=1)
    scale = HEAD_DIM ** -0.5
    return jnp.stack([c * scale, c]), jnp.stack([s_lo * scale, s_lo]), jnp.stack([s_hi * scale, s_hi])


def _proj_kernel(rotate, xn_ref, w_ref, *rest):
    o_ref = rest[-1]
    r = _dot(xn_ref[...], w_ref[...])
    if rotate:
        c_ref, slo_ref, shi_ref = rest[:3]
        tn = r.shape[1]
        half = ROPE_DIM // 2
        wide = lambda t_ref: jnp.concatenate([t_ref[...]] * (tn // HEAD_DIM), axis=1)
        r = r * wide(c_ref) + pltpu.roll(r, tn - half, 1) * wide(slo_ref) + pltpu.roll(r, half, 1) * wide(shi_ref)
    o_ref[...] = r.astype(o_ref.dtype)


def projection(xn, w, out_dtype, *, tm, tn, rope_len=None, rope=None, name):
    M, D = xn.shape
    N = w.shape[1]
    in_specs = [
        pl.BlockSpec((tm, D), lambda i, j: (i, 0)),
        pl.BlockSpec((D, tn), lambda i, j: (0, j)),
    ]
    args = [xn, w]
    if rope_len is not None:
        assert tn == D_ATT and N == 2 * D_ATT and rope_len % tm == 0
        spec = pl.BlockSpec((None, tm, HEAD_DIM), lambda i, j: (j, i % (rope_len // tm), 0))
        in_specs += [spec, spec, spec]
        assert all(t.shape[1] >= rope_len for t in rope)
        args += list(rope)
    return pl.pallas_call(
        partial(_proj_kernel, rope_len is not None),
        grid=(M // tm, N // tn),
        in_specs=in_specs,
        out_specs=pl.BlockSpec((tm, tn), lambda i, j: (i, j)),
        out_shape=jax.ShapeDtypeStruct((M, N), out_dtype),
        compiler_params=_params("parallel", "arbitrary"),
        name=name,
    )(*args)


ATT_RADIUS = 64
assert all(w // (2 * d) == ATT_RADIUS for w, d in ATT_GROUPS)


def _attn_kernel(T, Ls, q_ref, kp_ref, kc_ref, kn_ref, vp_ref, vc_ref, vn_ref, o_ref, lse_ref, kbuf, vbuf):
    t = pl.program_id(2)
    R = ATT_RADIUS
    SB = 2 * R
    for buf, prv, cur, nxt in ((kbuf, kp_ref, kc_ref, kn_ref), (vbuf, vp_ref, vc_ref, vn_ref)):
        buf[0:R] = prv[...]
        buf[R:R + T] = cur[...]
        buf[R + T:R + T + R] = nxt[...]
    qi = lax.broadcasted_iota(jnp.int32, (SB, 2 * SB), 0)
    kk = lax.broadcasted_iota(jnp.int32, (SB, 2 * SB), 1)
    band = (kk >= qi) & (kk <= qi + 2 * R)
    lane = lax.broadcasted_iota(jnp.int32, (SB, LANES), 1)
    for sb in range(T // SB):
        kpos = t * T + (sb * SB - R) + kk
        mask = band & (kpos >= 0) & (kpos < Ls)
        lse = jnp.zeros((SB, LANES), F32)
        for h in range(HEADS_PER_GROUP):
            cols = slice(h * HEAD_DIM, (h + 1) * HEAD_DIM)
            q = q_ref[sb * SB:(sb + 1) * SB, cols]
            k = kbuf[sb * SB:(sb + 2) * SB, cols]
            v = vbuf[sb * SB:(sb + 2) * SB, cols]
            s = lax.dot_general(q, k, (((1,), (1,)), ((), ())), preferred_element_type=F32)
            s = jnp.where(mask, s, NEG_INF)
            m = jnp.max(s, axis=1, keepdims=True)
            p = jnp.exp(s - m)
            l = jnp.sum(p, axis=1, keepdims=True)
            o = _dot(p.astype(BF16), v) / l
            o_ref[sb * SB:(sb + 1) * SB, cols] = o.astype(o_ref.dtype)
            lse = jnp.where(lane == h, m + jnp.log(l), lse)
        lse_ref[sb * SB:(sb + 1) * SB, :] = lse


SLAB = 16


def _softmax_heads(q, k, v, mask):
    lane = lax.broadcasted_iota(jnp.int32, (q.shape[0], LANES), 1)
    lse = jnp.zeros((q.shape[0], LANES), F32)
    outs = []
    for h in range(HEADS_PER_GROUP):
        cols = slice(h * HEAD_DIM, (h + 1) * HEAD_DIM)
        s = lax.dot_general(q[:, cols], k[:, cols], (((1,), (1,)), ((), ())), preferred_element_type=F32)
        s = jnp.where(mask, s, NEG_INF)
        m = jnp.max(s, axis=1, keepdims=True)
        p = jnp.exp(s - m)
        l = jnp.sum(p, axis=1, keepdims=True)
        outs.append(_dot(p.astype(BF16), v[:, cols]) / l)
        lse = jnp.where(lane == h, m + jnp.log(l), lse)
    return jnp.concatenate(outs, axis=1), lse


def _attn_slab_kernel(d, NS, Ls, q_ref, kp_ref, kc_ref, kn_ref, vp_ref, vc_ref, vn_ref, o_ref, lse_ref,
                      qs, ks, vs, os_, ls):
    t = pl.program_id(1)
    R = ATT_RADIUS
    SB = 2 * R
    J = SLAB // d
    NH = kp_ref.shape[0]
    QA = SB // J
    KA = QA + 2 * NH
    n_sb = NS // QA
    to_class_major = lambda ref: pltpu.einshape("abc->bac", ref[...])
    qs[...] = to_class_major(q_ref)
    for buf, prv, cur, nxt in ((ks, kp_ref, kc_ref, kn_ref), (vs, vp_ref, vc_ref, vn_ref)):
        buf[:, 0:NH] = to_class_major(prv)
        buf[:, NH:NH + NS] = to_class_major(cur)
        buf[:, NH + NS:NH + NS + NH] = to_class_major(nxt)
    qi = lax.broadcasted_iota(jnp.int32, (SB, 2 * SB), 0)
    kk = lax.broadcasted_iota(jnp.int32, (SB, 2 * SB), 1)
    sq_rel = J * (qi % QA) + qi // QA
    sk_rel = J * (kk % KA - NH) + kk // KA
    band = jnp.abs(sk_rel - sq_rel) <= R

    def body(it, carry):
        r = it // n_sb
        a0 = pl.multiple_of((it % n_sb) * QA, QA)
        sk = J * (t * NS + a0) + sk_rel
        mask = band & (sk >= 0) & (sk < Ls)
        gather = lambda buf, n: jnp.concatenate([buf[j * d + r, pl.ds(a0, n), :] for j in range(J)], axis=0)
        o, lse = _softmax_heads(gather(qs, QA), gather(ks, KA), gather(vs, KA), mask)
        o = o.astype(os_.dtype)
        for j in range(J):
            os_[j * d + r, pl.ds(a0, QA), :] = o[j * QA:(j + 1) * QA]
            ls[j * d + r, pl.ds(a0, QA), :] = lse[j * QA:(j + 1) * QA]
        return carry

    lax.fori_loop(0, d * n_sb, body, 0, unroll=True)
    o_ref[...] = pltpu.einshape("bac->abc", os_[...])
    lse_ref[...] = pltpu.einshape("bac->abc", ls[...])


def dilated_attention_slabs(srcs, Bsz, L, g, *, NS=128):
    d = ATT_GROUPS[g][1]
    M = Bsz * L
    R = ATT_RADIUS
    GW = HEADS_PER_GROUP * HEAD_DIM
    NH = R * d // SLAB
    TB = NS * SLAB
    assert SLAB % d == 0 and L % TB == 0 and NS % NH == 0 and (2 * R) % (SLAB // d) == 0
    nmb, nhb, hpm = L // TB, L // (NH * SLAB), NS // NH
    views = [a.reshape(M // SLAB, SLAB, a.shape[1]) for a, _ in srcs]
    col = lambda which: srcs[which][1]

    def main(which):
        return pl.BlockSpec((NS, SLAB, GW), lambda b, t: (b * nmb + t, 0, col(which)))

    def prev(which):
        return pl.BlockSpec((NH, SLAB, GW), lambda b, t: (jnp.maximum(b * nhb + t * hpm - 1, b * nhb), 0, col(which)))

    def nxt(which):
        return pl.BlockSpec((NH, SLAB, GW),
                            lambda b, t: (jnp.minimum(b * nhb + (t + 1) * hpm, (b + 1) * nhb - 1), 0, col(which)))

    o, lse = pl.pallas_call(
        partial(_attn_slab_kernel, d, NS, L // d),
        grid=(Bsz, nmb),
        in_specs=[main(0), prev(1), main(1), nxt(1), prev(2), main(2), nxt(2)],
        out_specs=[pl.BlockSpec((NS, SLAB, GW), lambda b, t: (b * nmb + t, 0, 0)),
                   pl.BlockSpec((NS, SLAB, LANES), lambda b, t: (b * nmb + t, 0, 0))],
        out_shape=[jax.ShapeDtypeStruct((M // SLAB, SLAB, GW), BF16),
                   jax.ShapeDtypeStruct((M // SLAB, SLAB, LANES), F32)],
        scratch_shapes=[pltpu.VMEM((SLAB, NS, GW), BF16),
                        pltpu.VMEM((SLAB, NS + 2 * NH, GW), BF16), pltpu.VMEM((SLAB, NS + 2 * NH, GW), BF16),
                        pltpu.VMEM((SLAB, NS, GW), BF16), pltpu.VMEM((SLAB, NS, LANES), F32)],
        compiler_params=_params("parallel", "arbitrary"),
        name=f"dilated_attention_g{g}",
    )(views[0], views[1], views[1], views[1], views[2], views[2], views[2])
    return o.reshape(M, GW), lse.reshape(M, LANES)


def dilated_attention_group(srcs, Bsz, L, g, *, T=1024):
    d = ATT_GROUPS[g][1]
    M = Bsz * L
    Ls = L // d
    R = ATT_RADIUS
    GW = HEADS_PER_GROUP * HEAD_DIM
    assert d == 1 and Ls % T == 0 and T % (2 * R) == 0
    nrb, nhb, hpt = Ls // T, Ls // R, T // R

    def main(which):
        return pl.BlockSpec((T, GW), lambda b, r, t: (b * nrb + t, srcs[which][1]))

    def prev(which):
        return pl.BlockSpec((R, GW), lambda b, r, t: (jnp.maximum(b * nhb + t * hpt - 1, b * nhb), srcs[which][1]))

    def nxt(which):
        return pl.BlockSpec((R, GW), lambda b, r, t: (jnp.minimum(b * nhb + (t + 1) * hpt, (b + 1) * nhb - 1),
                                                      srcs[which][1]))

    o, lse = pl.pallas_call(
        partial(_attn_kernel, T, Ls),
        grid=(Bsz, d, nrb),
        in_specs=[main(0), prev(1), main(1), nxt(1), prev(2), main(2), nxt(2)],
        out_specs=[pl.BlockSpec((T, GW), lambda b, r, t: (b * nrb + t, r)),
                   pl.BlockSpec((T, LANES), lambda b, r, t: (b * nrb + t, r))],
        out_shape=[jax.ShapeDtypeStruct((M // d, d * GW), BF16),
                   jax.ShapeDtypeStruct((M // d, d * LANES), F32)],
        scratch_shapes=[pltpu.VMEM((T + 2 * R, GW), BF16), pltpu.VMEM((T + 2 * R, GW), BF16)],
        compiler_params=_params("parallel", "parallel", "arbitrary"),
        name=f"dilated_attention_g{g}",
    )(srcs[0][0], srcs[1][0], srcs[1][0], srcs[1][0], srcs[2][0], srcs[2][0], srcs[2][0])
    return o.reshape(M, GW), lse.reshape(M, LANES)


def _merge_kernel(x_ref, a_ref, o0_ref, o1_ref, o2_ref, l0_ref, l1_ref, l2_ref, ga_ref, gb_ref,
                  post_g_ref, next_g_ref, whp_ref, wap_ref, wo_ref, o_ref, u_ref):
    lses = [l0_ref[...], l1_ref[...], l2_ref[...]]
    mx = jnp.maximum(jnp.maximum(lses[0], lses[1]), lses[2])
    es = [jnp.exp(l - mx) for l in lses]
    den = es[0] + es[1] + es[2]
    wts = [e / den for e in es]
    outs = [o0_ref, o1_ref, o2_ref]
    heads = []
    for h in range(HEADS_PER_GROUP):
        cols = slice(h * HEAD_DIM, (h + 1) * HEAD_DIM)
        heads.append(sum(wts[g][:, h:h + 1] * outs[g][:, cols].astype(F32) for g in range(N_GROUPS)))
    att = jnp.concatenate(heads, axis=1).astype(BF16)
    a = _dot(a_ref[...], whp_ref[...])
    b = _dot(att, wap_ref[...])
    gate_a, gate_b = ga_ref[...].astype(F32), gb_ref[...].astype(F32)
    merged = (jax.nn.sigmoid(gate_a) * a + jax.nn.sigmoid(gate_b) * b).astype(BF16)
    mix = _dot(merged, wo_ref[...])
    y = x_ref[...] + _rms(mix, post_g_ref[...])
    o_ref[...] = y
    u_ref[...] = _rms(y, next_g_ref[...]).astype(u_ref.dtype)


def merge_out(x, a_in, att_outs, att_lses, gates, post_g, next_g, whp, wap, wo, *, tm=512):
    M, D = x.shape
    const = lambda i: (0, 0)
    rows = lambda a: pl.BlockSpec((tm, a.shape[1]), lambda i: (i, 0))
    return pl.pallas_call(
        _merge_kernel,
        grid=(M // tm,),
        in_specs=[rows(x), rows(a_in)] + [rows(o) for o in att_outs] + [rows(l) for l in att_lses] + [
            pl.BlockSpec((tm, D), lambda i: (i, 0)),
            pl.BlockSpec((tm, D), lambda i: (i, 1)),
            pl.BlockSpec((1, D), const),
            pl.BlockSpec((1, D), const),
            pl.BlockSpec(whp.shape, const, pipeline_mode=pl.Buffered(1)),
            pl.BlockSpec(wap.shape, const, pipeline_mode=pl.Buffered(1)),
            pl.BlockSpec(wo.shape, const, pipeline_mode=pl.Buffered(1)),
        ],
        out_specs=[pl.BlockSpec((tm, D), lambda i: (i, 0)), pl.BlockSpec((tm, D), lambda i: (i, 0))],
        out_shape=[jax.ShapeDtypeStruct((M, D), F32), jax.ShapeDtypeStruct((M, D), BF16)],
        compiler_params=_params("parallel"),
        name="merge_out",
    )(x, a_in, *att_outs, *att_lses, gates, gates, post_g, next_g, whp, wap, wo)


def _dft_constants(L):
    N = 2 * L
    N2 = FFT_INNER
    N1 = N // N2
    h = N1 // 2
    idx1 = np.arange(N1)
    ang1 = -2.0 * np.pi * ((idx1[:, None] * idx1[None, :]) % N1) / N1
    f1r, f1i = np.cos(ang1), np.sin(ang1)
    s_data = np.block([[f1r[:, :h], -f1i[:, :h]], [f1i[:, :h], f1r[:, :h]]])
    hk = h + FFT_ROWS
    s_filt = np.concatenate([f1r[:hk], f1i[:hk]], axis=0)
    ar, ai = f1r[:h, :], -f1i[:h, :]
    t_fin = np.block([[ar, -ai], [ai, ar]])
    idx2 = np.arange(N2)
    ang2 = -2.0 * np.pi * ((idx2[:, None] * idx2[None, :]) % N2) / N2
    angt = -2.0 * np.pi * (idx1[:, None] * idx2[None, :]) / N
    return dict(
        N1=N1,
        s_data=jnp.asarray(s_data, BF16), s_filt=jnp.asarray(s_filt, BF16), t_fin=jnp.asarray(t_fin, BF16),
        f2r=jnp.asarray(np.cos(ang2), F32), f2i=jnp.asarray(np.sin(ang2), F32),
        twr=jnp.asarray(np.cos(angt).reshape(N1, 1, N2), F32),
        twi=jnp.asarray(np.sin(angt).reshape(N1, 1, N2), F32),
    )


def _hyena_in_kernel(tm, L, x_ref, prev_ref, next_ref, w_ref, cw_ref, cb_ref, o_ref, xn_ref):
    i = pl.program_id(0)
    H = prev_ref.shape[0]

    @pl.when(pl.program_id(1) == 0)
    def _():
        xn_ref[0:H] = prev_ref[...]
        xn_ref[H:H + tm] = x_ref[...]
        xn_ref[H + tm:H + tm + H] = next_ref[...]

    r = _dot(xn_ref[...], w_ref[...])
    row = lax.broadcasted_iota(jnp.int32, r.shape, 0)
    pos0 = (i * tm) % L
    outside = ((row < H) & (pos0 == 0)) | ((row >= H + tm) & (pos0 + tm == L))
    r = jnp.where(outside, 0.0, r)
    n = r.shape[0]
    y = pltpu.roll(r, 1, 0) * cw_ref[0:1, :] + r * cw_ref[1:2, :] + pltpu.roll(r, n - 1, 0) * cw_ref[2:3, :]
    o_ref[0] = (y[H:H + tm] + cb_ref[...]).astype(o_ref.dtype)


def hyena_inputs(x, w, conv_w, conv_b, L, *, tm=1024):
    M, D = x.shape
    C = w.shape[1] // 3
    H = SLAB
    nh = M // H
    assert L % tm == 0 and tm % H == 0
    return pl.pallas_call(
        partial(_hyena_in_kernel, tm, L),
        grid=(M // tm, 3),
        in_specs=[
            pl.BlockSpec((tm, D), lambda i, j: (i, 0)),
            pl.BlockSpec((H, D), lambda i, j: (jnp.maximum(i * (tm // H) - 1, 0), 0)),
            pl.BlockSpec((H, D), lambda i, j: (jnp.minimum((i + 1) * (tm // H), nh - 1), 0)),
            pl.BlockSpec((D, C), lambda i, j: (0, j)),
            pl.BlockSpec((SHORT_CONV, C), lambda i, j: (0, j)),
            pl.BlockSpec((1, C), lambda i, j: (0, j)),
        ],
        out_specs=pl.BlockSpec((1, tm, C), lambda i, j: (j, i, 0)),
        out_shape=jax.ShapeDtypeStruct((3, M, C), BF16),
        scratch_shapes=[pltpu.VMEM((tm + 2 * H, D), BF16)],
        compiler_params=_params("parallel", "arbitrary"),
        name="hyena_inputs",
    )(x, x, x, w, conv_w, conv_b)


def _filter_kernel(L, h, w1_ref, b1_ref, w2_ref, b2_ref, w3_ref, b3_ref, w4_ref,
                   freq_ref, fvec_ref, delta_ref, kf_ref):
    i = pl.program_id(0)
    HALF = LANES // 2
    wide = delta_ref.shape[1]
    C = wide // (2 * HYENA_ORDER)
    tl = FFT_ROWS * h

    def lag(shape):
        r = lax.broadcasted_iota(jnp.int32, shape, 0)
        return (r % h) * FFT_INNER + i * FFT_ROWS + r // h

    row = lag((tl, LANES))
    lane = lax.broadcasted_iota(jnp.int32, (tl, LANES), 1)
    posf = jnp.where(lane < HALF, row, L - row).astype(F32)
    t = posf / (L - 1)
    a = fvec_ref[...] * ((2.0 * math.pi / L) * posf)
    lh = lane % HALF
    feats = jnp.where(lh < FILTER_BANDS, jnp.cos(a),
                      jnp.where(lh < 2 * FILTER_BANDS, -jnp.sin(a),
                                jnp.where(lh == 2 * FILTER_BANDS, t, 0.0)))
    freq = freq_ref[...]
    dense = lambda v, w_ref, b_ref: jnp.sin(freq * (
        jnp.dot(v, w_ref[...], precision=HIGHEST, preferred_element_type=F32) + b_ref[...]))
    hid = dense(dense(dense(feats, w1_ref, b1_ref), w2_ref, b2_ref), w3_ref, b3_ref)
    h_hi = hid.astype(BF16)
    h_lo = (hid - h_hi.astype(F32)).astype(BF16)
    taps = _dot(jnp.concatenate([h_hi, h_lo, h_hi], axis=1), w4_ref[...])
    row_w = lag((tl, wide))
    col_w = lax.broadcasted_iota(jnp.int32, (tl, wide), 1)
    backward = col_w >= wide // 2
    t_w = jnp.where(backward, L - row_w, row_w).astype(F32) / (L - 1)
    taps = taps * jnp.exp(-t_w * delta_ref[...])
    taps = jnp.where(backward & (row_w == 0), 0.0, taps)
    for n in range(HYENA_ORDER):
        for direction in range(2):
            c0 = (direction * HYENA_ORDER + n) * C
            part = taps[:, c0:c0 + C].astype(kf_ref.dtype).reshape(FFT_ROWS, h, C)
            kf_ref[n, 0, :, direction * h:(direction + 1) * h, :] = part


def hyena_filter_taps(L, fw1, fb1, fw2, fb2, fw3, fb3, fw4, ffreq):
    C = D_HYENA
    h = L // FFT_INNER
    groups = FFT_INNER // FFT_ROWS
    H = fw2.shape[0]
    HALF = LANES // 2
    assert H <= HALF and 2 * FILTER_BANDS + 1 <= HALF

    def both(a, rows):
        blk = jnp.zeros((HALF if rows else 1, HALF), F32).at[:a.shape[0], :a.shape[1]].set(a)
        if not rows:
            return jnp.concatenate([blk, blk], axis=1)
        zero = jnp.zeros_like(blk)
        return jnp.concatenate([jnp.concatenate([blk, zero], axis=1), jnp.concatenate([zero, blk], axis=1)], axis=0)

    w1 = both(jnp.concatenate([fw1[1:], fw1[:1]], axis=0), True)
    w2, w3 = both(fw2, True), both(fw3, True)
    b1, b2, b3, freq = (both(v[None, :], False) for v in (fb1, fb2, fb3, ffreq))
    w4 = fw4.reshape(H, HYENA_ORDER, 2, C)
    wide = 2 * HYENA_ORDER * C
    w4p = jnp.zeros((LANES, wide), F32)
    w4p = w4p.at[:H, :wide // 2].set(w4[:, :, 0].reshape(H, HYENA_ORDER * C))
    w4p = w4p.at[HALF:HALF + H, wide // 2:].set(w4[:, :, 1].reshape(H, HYENA_ORDER * C))
    w4hi = w4p.astype(BF16)
    w4lo = (w4p - w4hi.astype(F32)).astype(BF16)
    w4_stack = jnp.concatenate([w4hi, w4hi, w4lo], axis=0)
    bands = np.linspace(1e-4, FILTER_BANDS - 1, FILTER_BANDS, dtype=np.float32)
    fvec = np.zeros((1, LANES), np.float32)
    for base in (0, HALF):
        fvec[0, base:base + FILTER_BANDS] = bands
        fvec[0, base + FILTER_BANDS:base + 2 * FILTER_BANDS] = bands
    max_decay = math.log(DECAY_TARGET) / FAST_DECAY_PCT
    min_decay = math.log(DECAY_TARGET) / SLOW_DECAY_PCT
    deltas = np.abs(np.linspace(min_decay, max_decay, C, dtype=np.float32))
    delta_w = np.tile(deltas[None, :], (1, 2 * HYENA_ORDER))
    const = lambda i: (0, 0)
    args = (w1, b1, w2, b2, w3, b3, w4_stack, freq, jnp.asarray(fvec), jnp.asarray(delta_w))
    return pl.pallas_call(
        partial(_filter_kernel, L, h),
        grid=(groups,),
        in_specs=[pl.BlockSpec(a.shape, const) for a in args],
        out_specs=pl.BlockSpec((HYENA_ORDER, 1, FFT_ROWS, 2 * h, C), lambda i: (0, i, 0, 0, 0)),
        out_shape=jax.ShapeDtypeStruct((HYENA_ORDER, groups, FFT_ROWS, 2 * h, C), BF16),
        compiler_params=_params("parallel"),
        name="hyena_filter_taps",
    )(*args)


FFT_ROWS = 16
FFT_COLS = 512
FFT_COLS_IN = 1024


def _stage1_kernel(inner_major, s_ref, x_ref, y_ref):
    s = s_ref[...]
    if inner_major:
        xt = x_ref[...]
    else:
        _, h, rows, cols = x_ref.shape
        xt = pltpu.einshape("abc->bac", x_ref[...].reshape(2 * h, rows, cols))
    yt = jnp.stack([_dot(s, xt[b]).astype(y_ref.dtype) for b in range(xt.shape[0])], axis=0)
    y_ref[...] = pltpu.einshape("bac->abc", yt).reshape(y_ref.shape)


def fft_stage1(x, which, s, *, inner_major=False):
    if inner_major:
        _, groups, _, N1, C = x.shape
        N2 = groups * FFT_ROWS
        x_spec = pl.BlockSpec((None, None, FFT_ROWS, N1, FFT_COLS_IN), lambda j, c: (which, j, 0, 0, c))
    else:
        _, _, h, N2, C = x.shape
        N1 = 2 * h
        x_spec = pl.BlockSpec((None, 2, h, FFT_ROWS, FFT_COLS_IN), lambda j, c: (which, 0, 0, j, c))
    return pl.pallas_call(
        partial(_stage1_kernel, inner_major),
        grid=(N2 // FFT_ROWS, C // FFT_COLS_IN),
        in_specs=[pl.BlockSpec(s.shape, lambda j, c: (0, 0)), x_spec],
        out_specs=pl.BlockSpec((2, s.shape[0] // 2, FFT_ROWS, FFT_COLS_IN), lambda j, c: (0, 0, j, c)),
        out_shape=jax.ShapeDtypeStruct((2, s.shape[0] // 2, N2, C), BF16),
        compiler_params=_params("parallel", "parallel"),
        name="fft_stage1",
    )(s, x)


FFT_K1_PER_STEP = 4


def _mid_kernel(inv_n, n1, f2r_ref, f2i_ref, twr_ref, twi_ref, *refs):
    kb = len(refs) - 2
    yf_refs, y_ref, u_ref = refs[:kb], refs[kb], refs[kb + 1]
    f2r, f2i = f2r_ref[...], f2i_ref[...]
    for k in range(kb):
        twr, twi = twr_ref[k], twi_ref[k]
        gr, gi = f2r * twr - f2i * twi, f2r * twi + f2i * twr
        grb, gib = gr.astype(BF16), gi.astype(BF16)

        def inner(re, im):
            return _dot(grb, re) - _dot(gib, im), _dot(gib, re) + _dot(grb, im)

        mirrored = pl.program_id(0) * kb + k > n1 // 2
        yf_im = yf_refs[k][1, 0]
        hr, hi = inner(yf_refs[k][0, 0], jnp.where(mirrored, -yf_im, yf_im))
        zr, zi = inner(y_ref[0, k], y_ref[1, k])
        pr = (zr * hr - zi * hi).astype(BF16)
        pi = (zr * hi + zi * hr).astype(BF16)
        irb = (gr.T * inv_n).astype(BF16)
        iib = (gi.T * (-inv_n)).astype(BF16)
        u_ref[0, k] = (_dot(irb, pr) - _dot(iib, pi)).astype(u_ref.dtype)
        u_ref[1, k] = (_dot(iib, pr) + _dot(irb, pi)).astype(u_ref.dtype)


def fft_mid(yf, y, dc):
    _, N1, N2, C = y.shape
    kb = FFT_K1_PER_STEP
    assert yf.shape[1] > N1 // 2
    blk = pl.BlockSpec((2, kb, N2, C), lambda k: (0, k, 0, 0))
    const = pl.BlockSpec((N2, N2), lambda k: (0, 0))
    tw = pl.BlockSpec((kb, 1, N2), lambda k: (k, 0, 0))

    def filt_blk(j):
        def index(k):
            q = k * kb + j
            return (0, jnp.where(q > N1 // 2, N1 - q, q), 0, 0)
        return pl.BlockSpec((2, 1, N2, C), index)

    return pl.pallas_call(
        partial(_mid_kernel, 1.0 / (N1 * N2), N1),
        grid=(N1 // kb,),
        in_specs=[const, const, tw, tw] + [filt_blk(j) for j in range(kb)] + [blk],
        out_specs=blk,
        out_shape=jax.ShapeDtypeStruct((2, N1, N2, C), BF16),
        compiler_params=_params("parallel"),
        name="fft_mid",
    )(dc['f2r'], dc['f2i'], dc['twr'], dc['twi'], *([yf] * kb), y)


def _final_kernel(t_ref, u_ref, z_ref, gate_ref, bias_ref, o_ref):
    _, n1, rows, cols = u_ref.shape
    t = t_ref[...]
    ut = pltpu.einshape("abc->bac", u_ref[...].reshape(2 * n1, rows, cols))
    convt = jnp.stack([_dot(t, ut[b]).astype(BF16) for b in range(rows)], axis=0)
    conv = pltpu.einshape("bac->abc", convt).reshape(o_ref.shape).astype(F32)
    z = z_ref[...].astype(F32)
    o_ref[...] = (gate_ref[...].astype(F32) * (conv + bias_ref[...] * z)).astype(o_ref.dtype)


def fft_final(u, t_fin, z, z_which, gate, gate_which, bias, bias_which):
    _, N1, N2, C = u.shape
    h = N1 // 2

    def half(which):
        return pl.BlockSpec((None, 2, h, FFT_ROWS, FFT_COLS), lambda j, c: (which, 0, 0, j, c))

    return pl.pallas_call(
        _final_kernel,
        grid=(N2 // FFT_ROWS, C // FFT_COLS),
        in_specs=[pl.BlockSpec((N1, 2 * N1), lambda j, c: (0, 0)),
                  pl.BlockSpec((2, N1, FFT_ROWS, FFT_COLS), lambda j, c: (0, 0, j, c)),
                  half(z_which), half(gate_which),
                  pl.BlockSpec((None, 1, FFT_COLS), lambda j, c: (bias_which, 0, c))],
        out_specs=half(0),
        out_shape=jax.ShapeDtypeStruct((1, 2, h, N2, C), BF16),
        compiler_params=_params("parallel", "parallel"),
        name="fft_final",
    )(t_fin, u, z, gate, bias.reshape(bias.shape[0], 1, C))


def hyena_branch(zs, L, p):
    _, M, C = zs.shape
    assert M == 2 * L, "the batch pair rides as real/imaginary parts"
    dc = _dft_constants(L)
    split = (2, dc['N1'] // 2, FFT_INNER, C)
    taps = hyena_filter_taps(L, *p['filt'])
    zs = zs.reshape(3, *split)
    z, z_which = zs, 0
    for n in range(HYENA_ORDER):
        yf = fft_stage1(taps, n, dc['s_filt'], inner_major=True)
        u = fft_mid(yf, fft_stage1(z, z_which, dc['s_data']), dc)
        z, z_which = fft_final(u, dc['t_fin'], z, z_which, zs, n + 1, p['hy_bias'], n), 0
    return z.reshape(M, C)


def _split_w_in(w):
    c_q = 3 * D_HYENA
    c_v = c_q + 2 * D_ATT
    c_g = c_v + D_ATT
    return tuple(w[:, a:b].astype(BF16) for a, b in ((0, c_q), (c_q, c_v), (c_v, c_g), (c_g, w.shape[1])))


def _layer(x, p):
    Bsz, L, D = x.shape
    M = Bsz * L
    x0 = x.reshape(M, D)
    x1, u = ffn_block(x0, p['ffn1_pre_g'], p['ffn1_post_g'], p['ffn1_w_gate'], p['ffn1_w_up'], p['ffn1_w_down'],
                      next_g=p['mix_pre_g'])
    w_hy, w_qk, w_v, w_gate = p['w_in']
    hy_in = hyena_inputs(u, w_hy, p['hy_conv_w'], p['hy_conv_b'], L)
    qk = projection(u, w_qk, BF16, tm=1024, tn=D_ATT, rope_len=L, rope=p['rope'], name="inproj_qk")
    v = projection(u, w_v, BF16, tm=1024, tn=D_ATT, name="inproj_v")
    gates = projection(u, w_gate, BF16, tm=1024, tn=D_MODEL, name="inproj_gates")
    a2 = hyena_branch(hy_in, L, p)
    att = []
    for g in range(N_GROUPS):
        srcs = ((qk, g), (qk, N_GROUPS + g), (v, g))
        fn = dilated_attention_group if ATT_GROUPS[g][1] == 1 else dilated_attention_slabs
        att.append(fn(srcs, Bsz, L, g))
    x2, xn2 = merge_out(x1, a2, [o for o, _ in att], [l for _, l in att], gates, p['mix_post_g'], p['ffn2_pre_g'],
                        p['w_hy_proj'], p['w_att_proj'], p['w_out'])
    x3 = ffn_block(x2, p['ffn2_pre_g'], p['ffn2_post_g'], p['ffn2_w_gate'], p['ffn2_w_up'], p['ffn2_w_down'],
                   xn=xn2)
    return x3.reshape(Bsz, L, D)


def kernel(x_prompt, x_sample, ffn1_pre_g, ffn1_post_g, ffn1_w_gate, ffn1_w_up, ffn1_w_down, mix_pre_g, mix_post_g, w_in, hy_conv_w, hy_conv_b, filt_w1, filt_b1, filt_w2, filt_b2, filt_w3, filt_b3, filt_w4, filt_freq, hy_bias, w_hy_proj, w_att_proj, w_out, ffn2_pre_g, ffn2_post_g, ffn2_w_gate, ffn2_w_up, ffn2_w_down):
    assert ffn1_w_gate.shape[0] == 1
    p = {
        'ffn1_pre_g': ffn1_pre_g, 'ffn1_post_g': ffn1_post_g,
        'ffn1_w_gate': ffn1_w_gate[0].astype(BF16), 'ffn1_w_up': ffn1_w_up[0].astype(BF16),
        'ffn1_w_down': ffn1_w_down[0].astype(BF16),
        'mix_pre_g': mix_pre_g, 'mix_post_g': mix_post_g,
        'w_in': _split_w_in(w_in[0]),
        'hy_conv_w': hy_conv_w[0], 'hy_conv_b': hy_conv_b,
        'filt': (filt_w1[0], filt_b1[0], filt_w2[0], filt_b2[0], filt_w3[0], filt_b3[0], filt_w4[0], filt_freq[0]),
        'hy_bias': hy_bias[0],
        'w_hy_proj': w_hy_proj[0].astype(BF16), 'w_att_proj': w_att_proj[0].astype(BF16),
        'w_out': w_out[0].astype(BF16),
        'ffn2_pre_g': ffn2_pre_g, 'ffn2_post_g': ffn2_post_g,
        'ffn2_w_gate': ffn2_w_gate[0].astype(BF16), 'ffn2_w_up': ffn2_w_up[0].astype(BF16),
        'ffn2_w_down': ffn2_w_down[0].astype(BF16),
    }
    p['rope'] = rope_tables(max(x_prompt.shape[1], x_sample.shape[1]))
    return (_layer(x_prompt, p), _layer(x_sample, p))
```

```python
import math
from functools import partial

import numpy as np
import jax
import jax.numpy as jnp
from jax import lax
from jax.experimental import pallas as pl
from jax.experimental.pallas import tpu as pltpu

D_MODEL = 2048
D_HYENA = 1024
HYENA_ORDER = 2
SHORT_CONV = 3
FILTER_EMB = 33
FILTER_BANDS = (FILTER_EMB - 1) // 2
FAST_DECAY_PCT = 0.3
SLOW_DECAY_PCT = 1.5
DECAY_TARGET = 1e-2
HEAD_DIM = 128
HEADS_PER_GROUP = 4
ATT_GROUPS = ((128, 1), (512, 4), (2048, 16))
N_GROUPS = len(ATT_GROUPS)
D_ATT = N_GROUPS * HEADS_PER_GROUP * HEAD_DIM
D_ATT_OUT = HEADS_PER_GROUP * HEAD_DIM
ROPE_DIM = HEAD_DIM // 4
ROPE_THETA = 500000.0
N_BRANCH = 2
D_IN_PROJ = 3 * D_HYENA + 3 * D_ATT + N_BRANCH * D_MODEL
D_FF = 5632
EPS = 1e-6
NEG_INF = -1e30

LANES = 128
FFT_INNER = 256
VMEM_BYTES_V7X = 64 * 1024 * 1024
VMEM_LIMIT_BYTES = VMEM_BYTES_V7X - 4 * 1024 * 1024
BF16 = jnp.bfloat16
F32 = jnp.float32
HIGHEST = lax.Precision.HIGHEST


def _rms(x, g):
    return x * lax.rsqrt(jnp.mean(x * x, axis=-1, keepdims=True) + EPS) * g


def _dot(a, b):
    return jnp.dot(a, b, preferred_element_type=F32)


def _params(*sem):
    return pltpu.CompilerParams(dimension_semantics=sem, vmem_limit_bytes=VMEM_LIMIT_BYTES)


FFN_SPLIT = 2


def _ffn_kernel(norm_in, norm_out, x_ref, xn_src_ref, post_g_ref, next_g_ref, wg_ref, wu_ref, wd_ref, *rest):
    o_ref = rest[0]
    u_ref = rest[1] if norm_out else None
    xn_ref = rest[-1] if norm_in else xn_src_ref
    j = pl.program_id(1)
    tf = wg_ref.shape[1]

    @pl.when(j == 0)
    def _():
        if norm_in:
            xn_ref[...] = _rms(x_ref[...], xn_src_ref[...]).astype(BF16)
        o_ref[...] = jnp.zeros_like(o_ref)

    xn = xn_ref[...]
    hs = []
    for c in range(FFN_SPLIT):
        cols = slice(c * (tf // FFN_SPLIT), (c + 1) * (tf // FFN_SPLIT))
        gate = _dot(xn, wg_ref[:, cols])
        up = _dot(xn, wu_ref[:, cols])
        hs.append((gate * jax.nn.sigmoid(gate) * up).astype(BF16))
    o_ref[...] += _dot(jnp.concatenate(hs, axis=1), wd_ref[...])

    @pl.when(j == pl.num_programs(1) - 1)
    def _():
        y = x_ref[...] + 0.5 * _rms(o_ref[...], post_g_ref[...])
        o_ref[...] = y
        if norm_out:
            u_ref[...] = _rms(y, next_g_ref[...]).astype(u_ref.dtype)


def ffn_block(x, pre_g, post_g, wg, wu, wd, *, xn=None, next_g=None, tm=512, tf=512):
    M, D = x.shape
    FF = wg.shape[1]
    norm_in, norm_out = xn is None, next_g is not None
    rows = pl.BlockSpec((tm, D), lambda i, j: (i, 0))
    gain = pl.BlockSpec((1, D), lambda i, j: (0, 0))
    out = pl.pallas_call(
        partial(_ffn_kernel, norm_in, norm_out),
        grid=(M // tm, FF // tf),
        in_specs=[
            rows,
            gain if norm_in else rows,
            gain, gain,
            pl.BlockSpec((D, tf), lambda i, j: (0, j)),
            pl.BlockSpec((D, tf), lambda i, j: (0, j)),
            pl.BlockSpec((tf, D), lambda i, j: (j, 0)),
        ],
        out_specs=[rows, rows] if norm_out else [rows],
        out_shape=[jax.ShapeDtypeStruct((M, D), F32)] + ([jax.ShapeDtypeStruct((M, D), BF16)] if norm_out else []),
        scratch_shapes=[pltpu.VMEM((tm, D), BF16)] if norm_in else [],
        compiler_params=_params("parallel", "arbitrary"),
        name="ffn_block",
    )(x, pre_g if norm_in else xn, post_g, next_g if norm_out else post_g, wg, wu, wd)
    return out if norm_out else out[0]


def rope_tables(L):
    half = ROPE_DIM // 2
    inv_freq = jnp.power(ROPE_THETA, -jnp.arange(half, dtype=F32) / half)
    ang = jnp.arange(L, dtype=F32)[:, None] * inv_freq[None, :]
    cos, sin = jnp.cos(ang), jnp.sin(ang)
    rest = HEAD_DIM - ROPE_DIM
    c = jnp.concatenate([cos, cos, jnp.ones((L, rest), F32)], axis=1)
    s_lo = jnp.concatenate([-sin, jnp.zeros((L, half + rest), F32)], axis=1)
    s_hi = jnp.concatenate([jnp.zeros((L, half), F32), sin, jnp.zeros((L, rest), F32)], axis=1)
    scale = HEAD_DIM ** -0.5
    return jnp.stack([c * scale, c]), jnp.stack([s_lo * scale, s_lo]), jnp.stack([s_hi * scale, s_hi])


def _proj_kernel(rotate, xn_ref, w_ref, *rest):
    o_ref = rest[-1]
    r = _dot(xn_ref[...], w_ref[...])
    if rotate:
        c_ref, slo_ref, shi_ref = rest[:3]
        tn = r.shape[1]
        half = ROPE_DIM // 2
        wide = lambda t_ref: jnp.concatenate([t_ref[...]] * (tn // HEAD_DIM), axis=1)
        r = r * wide(c_ref) + pltpu.roll(r, tn - half, 1) * wide(slo_ref) + pltpu.roll(r, half, 1) * wide(shi_ref)
    o_ref[...] = r.astype(o_ref.dtype)


def projection(xn, w, out_dtype, *, tm, tn, rope_len=None, rope=None, name):
    M, D = xn.shape
    N = w.shape[1]
    in_specs = [
        pl.BlockSpec((tm, D), lambda i, j: (i, 0)),
        pl.BlockSpec((D, tn), lambda i, j: (0, j)),
    ]
    args = [xn, w]
    if rope_len is not None:
        assert tn == D_ATT and N == 2 * D_ATT and rope_len % tm == 0
        spec = pl.BlockSpec((None, tm, HEAD_DIM), lambda i, j: (j, i % (rope_len // tm), 0))
        in_specs += [spec, spec, spec]
        assert all(t.shape[1] >= rope_len for t in rope)
        args += list(rope)
    return pl.pallas_call(
        partial(_proj_kernel, rope_len is not None),
        grid=(M // tm, N // tn),
        in_specs=in_specs,
        out_specs=pl.BlockSpec((tm, tn), lambda i, j: (i, j)),
        out_shape=jax.ShapeDtypeStruct((M, N), out_dtype),
        compiler_params=_params("parallel", "arbitrary"),
        name=name,
    )(*args)


ATT_RADIUS = 64
assert all(w // (2 * d) == ATT_RADIUS for w, d in ATT_GROUPS)


def _attn_kernel(T, Ls, q_ref, kp_ref, kc_ref, kn_ref, vp_ref, vc_ref, vn_ref, o_ref, lse_ref, kbuf, vbuf):
    t = pl.program_id(2)
    R = ATT_RADIUS
    SB = 2 * R
    for buf, prv, cur, nxt in ((kbuf, kp_ref, kc_ref, kn_ref), (vbuf, vp_ref, vc_ref, vn_ref)):
        buf[0:R] = prv[...]
        buf[R:R + T] = cur[...]
        buf[R + T:R + T + R] = nxt[...]
    qi = lax.broadcasted_iota(jnp.int32, (SB, 2 * SB), 0)
    kk = lax.broadcasted_iota(jnp.int32, (SB, 2 * SB), 1)
    band = (kk >= qi) & (kk <= qi + 2 * R)
    lane = lax.broadcasted_iota(jnp.int32, (SB, LANES), 1)
    for sb in range(T // SB):
        kpos = t * T + (sb * SB - R) + kk
        mask = band & (kpos >= 0) & (kpos < Ls)
        lse = jnp.zeros((SB, LANES), F32)
        for h in range(HEADS_PER_GROUP):
            cols = slice(h * HEAD_DIM, (h + 1) * HEAD_DIM)
            q = q_ref[sb * SB:(sb + 1) * SB, cols]
            k = kbuf[sb * SB:(sb + 2) * SB, cols]
            v = vbuf[sb * SB:(sb + 2) * SB, cols]
            s = lax.dot_general(q, k, (((1,), (1,)), ((), ())), preferred_element_type=F32)
            s = jnp.where(mask, s, NEG_INF)
            m = jnp.max(s, axis=1, keepdims=True)
            p = jnp.exp(s - m)
            l = jnp.sum(p, axis=1, keepdims=True)
            o = _dot(p.astype(BF16), v) / l
            o_ref[sb * SB:(sb + 1) * SB, cols] = o.astype(o_ref.dtype)
            lse = jnp.where(lane == h, m + jnp.log(l), lse)
        lse_ref[sb * SB:(sb + 1) * SB, :] = lse


SLAB = 16


def _softmax_heads(q, k, v, mask):
    lane = lax.broadcasted_iota(jnp.int32, (q.shape[0], LANES), 1)
    lse = jnp.zeros((q.shape[0], LANES), F32)
    outs = []
    for h in range(HEADS_PER_GROUP):
        cols = slice(h * HEAD_DIM, (h + 1) * HEAD_DIM)
        s = lax.dot_general(q[:, cols], k[:, cols], (((1,), (1,)), ((), ())), preferred_element_type=F32)
        s = jnp.where(mask, s, NEG_INF)
        m = jnp.max(s, axis=1, keepdims=True)
        p = jnp.exp(s - m)
        l = jnp.sum(p, axis=1, keepdims=True)
        outs.append(_dot(p.astype(BF16), v[:, cols]) / l)
        lse = jnp.where(lane == h, m + jnp.log(l), lse)
    return jnp.concatenate(outs, axis=1), lse


def _attn_slab_kernel(d, NS, Ls, q_ref, kp_ref, kc_ref, kn_ref, vp_ref, vc_ref, vn_ref, o_ref, lse_ref,
                      qs, ks, vs, os_, ls):
    t = pl.program_id(1)
    R = ATT_RADIUS
    SB = 2 * R
    J = SLAB // d
    NH = kp_ref.shape[0]
    QA = SB // J
    KA = QA + 2 * NH
    n_sb = NS // QA
    to_class_major = lambda ref: pltpu.einshape("abc->bac", ref[...])
    qs[...] = to_class_major(q_ref)
    for buf, prv, cur, nxt in ((ks, kp_ref, kc_ref, kn_ref), (vs, vp_ref, vc_ref, vn_ref)):
        buf[:, 0:NH] = to_class_major(prv)
        buf[:, NH:NH + NS] = to_class_major(cur)
        buf[:, NH + NS:NH + NS + NH] = to_class_major(nxt)
    qi = lax.broadcasted_iota(jnp.int32, (SB, 2 * SB), 0)
    kk = lax.broadcasted_iota(jnp.int32, (SB, 2 * SB), 1)
    sq_rel = J * (qi % QA) + qi // QA
    sk_rel = J * (kk % KA - NH) + kk // KA
    band = jnp.abs(sk_rel - sq_rel) <= R

    def body(it, carry):
        r = it // n_sb
        a0 = pl.multiple_of((it % n_sb) * QA, QA)
        sk = J * (t * NS + a0) + sk_rel
        mask = band & (sk >= 0) & (sk < Ls)
        gather = lambda buf, n: jnp.concatenate([buf[j * d + r, pl.ds(a0, n), :] for j in range(J)], axis=0)
        o, lse = _softmax_heads(gather(qs, QA), gather(ks, KA), gather(vs, KA), mask)
        o = o.astype(os_.dtype)
        for j in range(J):
            os_[j * d + r, pl.ds(a0, QA), :] = o[j * QA:(j + 1) * QA]
            ls[j * d + r, pl.ds(a0, QA), :] = lse[j * QA:(j + 1) * QA]
        return carry

    lax.fori_loop(0, d * n_sb, body, 0, unroll=True)
    o_ref[...] = pltpu.einshape("bac->abc", os_[...])
    lse_ref[...] = pltpu.einshape("bac->abc", ls[...])


def dilated_attention_slabs(srcs, Bsz, L, g, *, NS=128):
    d = ATT_GROUPS[g][1]
    M = Bsz * L
    R = ATT_RADIUS
    GW = HEADS_PER_GROUP * HEAD_DIM
    NH = R * d // SLAB
    TB = NS * SLAB
    assert SLAB % d == 0 and L % TB == 0 and NS % NH == 0 and (2 * R) % (SLAB // d) == 0
    nmb, nhb, hpm = L // TB, L // (NH * SLAB), NS // NH
    views = [a.reshape(M // SLAB, SLAB, a.shape[1]) for a, _ in srcs]
    col = lambda which: srcs[which][1]

    def main(which):
        return pl.BlockSpec((NS, SLAB, GW), lambda b, t: (b * nmb + t, 0, col(which)))

    def prev(which):
        return pl.BlockSpec((NH, SLAB, GW), lambda b, t: (jnp.maximum(b * nhb + t * hpm - 1, b * nhb), 0, col(which)))

    def nxt(which):
        return pl.BlockSpec((NH, SLAB, GW),
                            lambda b, t: (jnp.minimum(b * nhb + (t + 1) * hpm, (b + 1) * nhb - 1), 0, col(which)))

    o, lse = pl.pallas_call(
        partial(_attn_slab_kernel, d, NS, L // d),
        grid=(Bsz, nmb),
        in_specs=[main(0), prev(1), main(1), nxt(1), prev(2), main(2), nxt(2)],
        out_specs=[pl.BlockSpec((NS, SLAB, GW), lambda b, t: (b * nmb + t, 0, 0)),
                   pl.BlockSpec((NS, SLAB, LANES), lambda b, t: (b * nmb + t, 0, 0))],
        out_shape=[jax.ShapeDtypeStruct((M // SLAB, SLAB, GW), BF16),
                   jax.ShapeDtypeStruct((M // SLAB, SLAB, LANES), F32)],
        scratch_shapes=[pltpu.VMEM((SLAB, NS, GW), BF16),
                        pltpu.VMEM((SLAB, NS + 2 * NH, GW), BF16), pltpu.VMEM((SLAB, NS + 2 * NH, GW), BF16),
                        pltpu.VMEM((SLAB, NS, GW), BF16), pltpu.VMEM((SLAB, NS, LANES), F32)],
        compiler_params=_params("parallel", "arbitrary"),
        name=f"dilated_attention_g{g}",
    )(views[0], views[1], views[1], views[1], views[2], views[2], views[2])
    return o.reshape(M, GW), lse.reshape(M, LANES)


def dilated_attention_group(srcs, Bsz, L, g, *, T=1024):
    d = ATT_GROUPS[g][1]
    M = Bsz * L
    Ls = L // d
    R = ATT_RADIUS
    GW = HEADS_PER_GROUP * HEAD_DIM
    assert d == 1 and Ls % T == 0 and T % (2 * R) == 0
    nrb, nhb, hpt = Ls // T, Ls // R, T // R

    def main(which):
        return pl.BlockSpec((T, GW), lambda b, r, t: (b * nrb + t, srcs[which][1]))

    def prev(which):
        return pl.BlockSpec((R, GW), lambda b, r, t: (jnp.maximum(b * nhb + t * hpt - 1, b * nhb), srcs[which][1]))

    def nxt(which):
        return pl.BlockSpec((R, GW), lambda b, r, t: (jnp.minimum(b * nhb + (t + 1) * hpt, (b + 1) * nhb - 1),
                                                      srcs[which][1]))

    o, lse = pl.pallas_call(
        partial(_attn_kernel, T, Ls),
        grid=(Bsz, d, nrb),
        in_specs=[main(0), prev(1), main(1), nxt(1), prev(2), main(2), nxt(2)],
        out_specs=[pl.BlockSpec((T, GW), lambda b, r, t: (b * nrb + t, r)),
                   pl.BlockSpec((T, LANES), lambda b, r, t: (b * nrb + t, r))],
        out_shape=[jax.ShapeDtypeStruct((M // d, d * GW), BF16),
                   jax.ShapeDtypeStruct((M // d, d * LANES), F32)],
        scratch_shapes=[pltpu.VMEM((T + 2 * R, GW), BF16), pltpu.VMEM((T + 2 * R, GW), BF16)],
        compiler_params=_params("parallel", "parallel", "arbitrary"),
        name=f"dilated_attention_g{g}",
    )(srcs[0][0], srcs[1][0], srcs[1][0], srcs[1][0], srcs[2][0], srcs[2][0], srcs[2][0])
    return o.reshape(M, GW), lse.reshape(M, LANES)


def _merge_kernel(x_ref, a_ref, o0_ref, o1_ref, o2_ref, l0_ref, l1_ref, l2_ref, ga_ref, gb_ref,
                  post_g_ref, next_g_ref, whp_ref, wap_ref, wo_ref, o_ref, u_ref):
    lses = [l0_ref[...], l1_ref[...], l2_ref[...]]
    mx = jnp.maximum(jnp.maximum(lses[0], lses[1]), lses[2])
    es = [jnp.exp(l - mx) for l in lses]
    den = es[0] + es[1] + es[2]
    wts = [e / den for e in es]
    outs = [o0_ref, o1_ref, o2_ref]
    heads = []
    for h in range(HEADS_PER_GROUP):
        cols = slice(h * HEAD_DIM, (h + 1) * HEAD_DIM)
        heads.append(sum(wts[g][:, h:h + 1] * outs[g][:, cols].astype(F32) for g in range(N_GROUPS)))
    att = jnp.concatenate(heads, axis=1).astype(BF16)
    a = _dot(a_ref[...], whp_ref[...])
    b = _dot(att, wap_ref[...])
    gate_a, gate_b = ga_ref[...].astype(F32), gb_ref[...].astype(F32)
    merged = (jax.nn.sigmoid(gate_a) * a + jax.nn.sigmoid(gate_b) * b).astype(BF16)
    mix = _dot(merged, wo_ref[...])
    y = x_ref[...] + _rms(mix, post_g_ref[...])
    o_ref[...] = y
    u_ref[...] = _rms(y, next_g_ref[...]).astype(u_ref.dtype)


def merge_out(x, a_in, att_outs, att_lses, gates, post_g, next_g, whp, wap, wo, *, tm=512):
    M, D = x.shape
    const = lambda i: (0, 0)
    rows = lambda a: pl.BlockSpec((tm, a.shape[1]), lambda i: (i, 0))
    return pl.pallas_call(
        _merge_kernel,
        grid=(M // tm,),
        in_specs=[rows(x), rows(a_in)] + [rows(o) for o in att_outs] + [rows(l) for l in att_lses] + [
            pl.BlockSpec((tm, D), lambda i: (i, 0)),
            pl.BlockSpec((tm, D), lambda i: (i, 1)),
            pl.BlockSpec((1, D), const),
            pl.BlockSpec((1, D), const),
            pl.BlockSpec(whp.shape, const, pipeline_mode=pl.Buffered(1)),
            pl.BlockSpec(wap.shape, const, pipeline_mode=pl.Buffered(1)),
            pl.BlockSpec(wo.shape, const, pipeline_mode=pl.Buffered(1)),
        ],
        out_specs=[pl.BlockSpec((tm, D), lambda i: (i, 0)), pl.BlockSpec((tm, D), lambda i: (i, 0))],
        out_shape=[jax.ShapeDtypeStruct((M, D), F32), jax.ShapeDtypeStruct((M, D), BF16)],
        compiler_params=_params("parallel"),
        name="merge_out",
    )(x, a_in, *att_outs, *att_lses, gates, gates, post_g, next_g, whp, wap, wo)


def _dft_constants(L):
    N = 2 * L
    N2 = FFT_INNER
    N1 = N // N2
    h = N1 // 2
    idx1 = np.arange(N1)
    ang1 = -2.0 * np.pi * ((idx1[:, None] * idx1[None, :]) % N1) / N1
    f1r, f1i = np.cos(ang1), np.sin(ang1)
    s_data = np.block([[f1r[:, :h], -f1i[:, :h]], [f1i[:, :h], f1r[:, :h]]])
    hk = h + FFT_ROWS
    s_filt = np.concatenate([f1r[:hk], f1i[:hk]], axis=0)
    ar, ai = f1r[:h, :], -f1i[:h, :]
    t_fin = np.block([[ar, -ai], [ai, ar]])
    idx2 = np.arange(N2)
    ang2 = -2.0 * np.pi * ((idx2[:, None] * idx2[None, :]) % N2) / N2
    angt = -2.0 * np.pi * (idx1[:, None] * idx2[None, :]) / N
    return dict(
        N1=N1,
        s_data=jnp.asarray(s_data, BF16), s_filt=jnp.asarray(s_filt, BF16), t_fin=jnp.asarray(t_fin, BF16),
        f2r=jnp.asarray(np.cos(ang2), F32), f2i=jnp.asarray(np.sin(ang2), F32),
        twr=jnp.asarray(np.cos(angt).reshape(N1, 1, N2), F32),
        twi=jnp.asarray(np.sin(angt).reshape(N1, 1, N2), F32),
    )


def _hyena_in_kernel(tm, L, x_ref, prev_ref, next_ref, w_ref, cw_ref, cb_ref, o_ref, xn_ref):
    i = pl.program_id(0)
    H = prev_ref.shape[0]

    @pl.when(pl.program_id(1) == 0)
    def _():
        xn_ref[0:H] = prev_ref[...]
        xn_ref[H:H + tm] = x_ref[...]
        xn_ref[H + tm:H + tm + H] = next_ref[...]

    r = _dot(xn_ref[...], w_ref[...])
    row = lax.broadcasted_iota(jnp.int32, r.shape, 0)
    pos0 = (i * tm) % L
    outside = ((row < H) & (pos0 == 0)) | ((row >= H + tm) & (pos0 + tm == L))
    r = jnp.where(outside, 0.0, r)
    n = r.shape[0]
    y = pltpu.roll(r, 1, 0) * cw_ref[0:1, :] + r * cw_ref[1:2, :] + pltpu.roll(r, n - 1, 0) * cw_ref[2:3, :]
    o_ref[0] = (y[H:H + tm] + cb_ref[...]).astype(o_ref.dtype)


def hyena_inputs(x, w, conv_w, conv_b, L, *, tm=1024):
    M, D = x.shape
    C = w.shape[1] // 3
    H = SLAB
    nh = M // H
    assert L % tm == 0 and tm % H == 0
    return pl.pallas_call(
        partial(_hyena_in_kernel, tm, L),
        grid=(M // tm, 3),
        in_specs=[
            pl.BlockSpec((tm, D), lambda i, j: (i, 0)),
            pl.BlockSpec((H, D), lambda i, j: (jnp.maximum(i * (tm // H) - 1, 0), 0)),
            pl.BlockSpec((H, D), lambda i, j: (jnp.minimum((i + 1) * (tm // H), nh - 1), 0)),
            pl.BlockSpec((D, C), lambda i, j: (0, j)),
            pl.BlockSpec((SHORT_CONV, C), lambda i, j: (0, j)),
            pl.BlockSpec((1, C), lambda i, j: (0, j)),
        ],
        out_specs=pl.BlockSpec((1, tm, C), lambda i, j: (j, i, 0)),
        out_shape=jax.ShapeDtypeStruct((3, M, C), BF16),
        scratch_shapes=[pltpu.VMEM((tm + 2 * H, D), BF16)],
        compiler_params=_params("parallel", "arbitrary"),
        name="hyena_inputs",
    )(x, x, x, w, conv_w, conv_b)


def _filter_kernel(L, h, w1_ref, b1_ref, w2_ref, b2_ref, w3_ref, b3_ref, w4_ref,
                   freq_ref, fvec_ref, delta_ref, kf_ref):
    i = pl.program_id(0)
    HALF = LANES // 2
    wide = delta_ref.shape[1]
    C = wide // (2 * HYENA_ORDER)
    tl = FFT_ROWS * h

    def lag(shape):
        r = lax.broadcasted_iota(jnp.int32, shape, 0)
        return (r % h) * FFT_INNER + i * FFT_ROWS + r // h

    row = lag((tl, LANES))
    lane = lax.broadcasted_iota(jnp.int32, (tl, LANES), 1)
    posf = jnp.where(lane < HALF, row, L - row).astype(F32)
    t = posf / (L - 1)
    a = fvec_ref[...] * ((2.0 * math.pi / L) * posf)
    lh = lane % HALF
    feats = jnp.where(lh < FILTER_BANDS, jnp.cos(a),
                      jnp.where(lh < 2 * FILTER_BANDS, -jnp.sin(a),
                                jnp.where(lh == 2 * FILTER_BANDS, t, 0.0)))
    freq = freq_ref[...]
    dense = lambda v, w_ref, b_ref: jnp.sin(freq * (
        jnp.dot(v, w_ref[...], precision=HIGHEST, preferred_element_type=F32) + b_ref[...]))
    hid = dense(dense(dense(feats, w1_ref, b1_ref), w2_ref, b2_ref), w3_ref, b3_ref)
    h_hi = hid.astype(BF16)
    h_lo = (hid - h_hi.astype(F32)).astype(BF16)
    taps = _dot(jnp.concatenate([h_hi, h_lo, h_hi], axis=1), w4_ref[...])
    row_w = lag((tl, wide))
    col_w = lax.broadcasted_iota(jnp.int32, (tl, wide), 1)
    backward = col_w >= wide // 2
    t_w = jnp.where(backward, L - row_w, row_w).astype(F32) / (L - 1)
    taps = taps * jnp.exp(-t_w * delta_ref[...])
    taps = jnp.where(backward & (row_w == 0), 0.0, taps)
    for n in range(HYENA_ORDER):
        for direction in range(2):
            c0 = (direction * HYENA_ORDER + n) * C
            part = taps[:, c0:c0 + C].astype(kf_ref.dtype).reshape(FFT_ROWS, h, C)
            kf_ref[n, 0, :, direction * h:(direction + 1) * h, :] = part


def hyena_filter_taps(L, fw1, fb1, fw2, fb2, fw3, fb3, fw4, ffreq):
    C = D_HYENA
    h = L // FFT_INNER
    groups = FFT_INNER // FFT_ROWS
    H = fw2.shape[0]
    HALF = LANES // 2
    assert H <= HALF and 2 * FILTER_BANDS + 1 <= HALF

    def both(a, rows):
        blk = jnp.zeros((HALF if rows else 1, HALF), F32).at[:a.shape[0], :a.shape[1]].set(a)
        if not rows:
            return jnp.concatenate([blk, blk], axis=1)
        zero = jnp.zeros_like(blk)
        return jnp.concatenate([jnp.concatenate([blk, zero], axis=1), jnp.concatenate([zero, blk], axis=1)], axis=0)

    w1 = both(jnp.concatenate([fw1[1:], fw1[:1]], axis=0), True)
    w2, w3 = both(fw2, True), both(fw3, True)
    b1, b2, b3, freq = (both(v[None, :], False) for v in (fb1, fb2, fb3, ffreq))
    w4 = fw4.reshape(H, HYENA_ORDER, 2, C)
    wide = 2 * HYENA_ORDER * C
    w4p = jnp.zeros((LANES, wide), F32)
    w4p = w4p.at[:H, :wide // 2].set(w4[:, :, 0].reshape(H, HYENA_ORDER * C))
    w4p = w4p.at[HALF:HALF + H, wide // 2:].set(w4[:, :, 1].reshape(H, HYENA_ORDER * C))
    w4hi = w4p.astype(BF16)
    w4lo = (w4p - w4hi.astype(F32)).astype(BF16)
    w4_stack = jnp.concatenate([w4hi, w4hi, w4lo], axis=0)
    bands = np.linspace(1e-4, FILTER_BANDS - 1, FILTER_BANDS, dtype=np.float32)
    fvec = np.zeros((1, LANES), np.float32)
    for base in (0, HALF):
        fvec[0, base:base + FILTER_BANDS] = bands
        fvec[0, base + FILTER_BANDS:base + 2 * FILTER_BANDS] = bands
    max_decay = math.log(DECAY_TARGET) / FAST_DECAY_PCT
    min_decay = math.log(DECAY_TARGET) / SLOW_DECAY_PCT
    deltas = np.abs(np.linspace(min_decay, max_decay, C, dtype=np.float32))
    delta_w = np.tile(deltas[None, :], (1, 2 * HYENA_ORDER))
    const = lambda i: (0, 0)
    args = (w1, b1, w2, b2, w3, b3, w4_stack, freq, jnp.asarray(fvec), jnp.asarray(delta_w))
    return pl.pallas_call(
        partial(_filter_kernel, L, h),
        grid=(groups,),
        in_specs=[pl.BlockSpec(a.shape, const) for a in args],
        out_specs=pl.BlockSpec((HYENA_ORDER, 1, FFT_ROWS, 2 * h, C), lambda i: (0, i, 0, 0, 0)),
        out_shape=jax.ShapeDtypeStruct((HYENA_ORDER, groups, FFT_ROWS, 2 * h, C), BF16),
        compiler_params=_params("parallel"),
        name="hyena_filter_taps",
    )(*args)


FFT_ROWS = 16
FFT_COLS = 512
FFT_COLS_IN = 1024


def _stage1_kernel(inner_major, s_ref, x_ref, y_ref):
    s = s_ref[...]
    if inner_major:
        xt = x_ref[...]
    else:
        _, h, rows, cols = x_ref.shape
        xt = pltpu.einshape("abc->bac", x_ref[...].reshape(2 * h, rows, cols))
    yt = jnp.stack([_dot(s, xt[b]).astype(y_ref.dtype) for b in range(xt.shape[0])], axis=0)
    y_ref[...] = pltpu.einshape("bac->abc", yt).reshape(y_ref.shape)


def fft_stage1(x, which, s, *, inner_major=False):
    if inner_major:
        _, groups, _, N1, C = x.shape
        N2 = groups * FFT_ROWS
        x_spec = pl.BlockSpec((None, None, FFT_ROWS, N1, FFT_COLS_IN), lambda j, c: (which, j, 0, 0, c))
    else:
        _, _, h, N2, C = x.shape
        N1 = 2 * h
        x_spec = pl.BlockSpec((None, 2, h, FFT_ROWS, FFT_COLS_IN), lambda j, c: (which, 0, 0, j, c))
    return pl.pallas_call(
        partial(_stage1_kernel, inner_major),
        grid=(N2 // FFT_ROWS, C // FFT_COLS_IN),
        in_specs=[pl.BlockSpec(s.shape, lambda j, c: (0, 0)), x_spec],
        out_specs=pl.BlockSpec((2, s.shape[0] // 2, FFT_ROWS, FFT_COLS_IN), lambda j, c: (0, 0, j, c)),
        out_shape=jax.ShapeDtypeStruct((2, s.shape[0] // 2, N2, C), BF16),
        compiler_params=_params("parallel", "parallel"),
        name="fft_stage1",
    )(s, x)


FFT_K1_PER_STEP = 4


def _mid_kernel(inv_n, n1, f2r_ref, f2i_ref, twr_ref, twi_ref, *refs):
    kb = len(refs) - 2
    yf_refs, y_ref, u_ref = refs[:kb], refs[kb], refs[kb + 1]
    f2r, f2i = f2r_ref[...], f2i_ref[...]
    for k in range(kb):
        twr, twi = twr_ref[k], twi_ref[k]
        gr, gi = f2r * twr - f2i * twi, f2r * twi + f2i * twr

        def times(ar, ai, x):
            ar, ai = ar.astype(BF16), ai.astype(BF16)
            return (_dot(jnp.concatenate([ar, -ai], axis=1), x), _dot(jnp.concatenate([ai, ar], axis=1), x))

        mirrored = pl.program_id(0) * kb + k > n1 // 2
        yf_im = yf_refs[k][1, 0]
        taps = jnp.concatenate([yf_refs[k][0, 0], jnp.where(mirrored, -yf_im, yf_im)], axis=0)
        hr, hi = times(gr, gi, taps)
        zr, zi = times(gr, gi, y_ref[:, k].reshape(2 * gr.shape[0], -1))
        prod = jnp.concatenate([(zr * hr - zi * hi).astype(BF16), (zr * hi + zi * hr).astype(BF16)], axis=0)
        ur, ui = times(gr.T * inv_n, gi.T * (-inv_n), prod)
        u_ref[0, k] = ur.astype(u_ref.dtype)
        u_ref[1, k] = ui.astype(u_ref.dtype)


def fft_mid(yf, y, dc):
    _, N1, N2, C = y.shape
    kb = FFT_K1_PER_STEP
    assert yf.shape[1] > N1 // 2
    blk = pl.BlockSpec((2, kb, N2, C), lambda k: (0, k, 0, 0))
    const = pl.BlockSpec((N2, N2), lambda k: (0, 0))
    tw = pl.BlockSpec((kb, 1, N2), lambda k: (k, 0, 0))

    def filt_blk(j):
        def index(k):
            q = k * kb + j
            return (0, jnp.where(q > N1 // 2, N1 - q, q), 0, 0)
        return pl.BlockSpec((2, 1, N2, C), index)

    return pl.pallas_call(
        partial(_mid_kernel, 1.0 / (N1 * N2), N1),
        grid=(N1 // kb,),
        in_specs=[const, const, tw, tw] + [filt_blk(j) for j in range(kb)] + [blk],
        out_specs=blk,
        out_shape=jax.ShapeDtypeStruct((2, N1, N2, C), BF16),
        compiler_params=_params("parallel"),
        name="fft_mid",
    )(dc['f2r'], dc['f2i'], dc['twr'], dc['twi'], *([yf] * kb), y)


def _final_kernel(t_ref, u_ref, z_ref, gate_ref, bias_ref, o_ref):
    _, n1, rows, cols = u_ref.shape
    t = t_ref[...]
    ut = pltpu.einshape("abc->bac", u_ref[...].reshape(2 * n1, rows, cols))
    convt = jnp.stack([_dot(t, ut[b]).astype(BF16) for b in range(rows)], axis=0)
    conv = pltpu.einshape("bac->abc", convt).reshape(o_ref.shape).astype(F32)
    z = z_ref[...].astype(F32)
    o_ref[...] = (gate_ref[...].astype(F32) * (conv + bias_ref[...] * z)).astype(o_ref.dtype)


def fft_final(u, t_fin, z, z_which, gate, gate_which, bias, bias_which):
    _, N1, N2, C = u.shape
    h = N1 // 2

    def half(which):
        return pl.BlockSpec((None, 2, h, FFT_ROWS, FFT_COLS), lambda j, c: (which, 0, 0, j, c))

    return pl.pallas_call(
        _final_kernel,
        grid=(N2 // FFT_ROWS, C // FFT_COLS),
        in_specs=[pl.BlockSpec((N1, 2 * N1), lambda j, c: (0, 0)),
                  pl.BlockSpec((2, N1, FFT_ROWS, FFT_COLS), lambda j, c: (0, 0, j, c)),
                  half(z_which), half(gate_which),
                  pl.BlockSpec((None, 1, FFT_COLS), lambda j, c: (bias_which, 0, c))],
        out_specs=half(0),
        out_shape=jax.ShapeDtypeStruct((1, 2, h, N2, C), BF16),
        compiler_params=_params("parallel", "parallel"),
        name="fft_final",
    )(t_fin, u, z, gate, bias.reshape(bias.shape[0], 1, C))


def hyena_branch(zs, L, p):
    _, M, C = zs.shape
    assert M == 2 * L, "the batch pair rides as real/imaginary parts"
    dc = _dft_constants(L)
    split = (2, dc['N1'] // 2, FFT_INNER, C)
    taps = hyena_filter_taps(L, *p['filt'])
    zs = zs.reshape(3, *split)
    z, z_which = zs, 0
    for n in range(HYENA_ORDER):
        yf = fft_stage1(taps, n, dc['s_filt'], inner_major=True)
        u = fft_mid(yf, fft_stage1(z, z_which, dc['s_data']), dc)
        z, z_which = fft_final(u, dc['t_fin'], z, z_which, zs, n + 1, p['hy_bias'], n), 0
    return z.reshape(M, C)


def _split_w_in(w):
    c_q = 3 * D_HYENA
    c_v = c_q + 2 * D_ATT
    c_g = c_v + D_ATT
    return tuple(w[:, a:b].astype(BF16) for a, b in ((0, c_q), (c_q, c_v), (c_v, c_g), (c_g, w.shape[1])))


def _layer(x, p):
    Bsz, L, D = x.shape
    M = Bsz * L
    x0 = x.reshape(M, D)
    x1, u = ffn_block(x0, p['ffn1_pre_g'], p['ffn1_post_g'], p['ffn1_w_gate'], p['ffn1_w_up'], p['ffn1_w_down'],
                      next_g=p['mix_pre_g'])
    w_hy, w_qk, w_v, w_gate = p['w_in']
    hy_in = hyena_inputs(u, w_hy, p['hy_conv_w'], p['hy_conv_b'], L)
    qk = projection(u, w_qk, BF16, tm=1024, tn=D_ATT, rope_len=L, rope=p['rope'], name="inproj_qk")
    v = projection(u, w_v, BF16, tm=1024, tn=D_ATT, name="inproj_v")
    gates = projection(u, w_gate, BF16, tm=1024, tn=D_MODEL, name="inproj_gates")
    a2 = hyena_branch(hy_in, L, p)
    att = []
    for g in range(N_GROUPS):
        srcs = ((qk, g), (qk, N_GROUPS + g), (v, g))
        fn = dilated_attention_group if ATT_GROUPS[g][1] == 1 else dilated_attention_slabs
        att.append(fn(srcs, Bsz, L, g))
    x2, xn2 = merge_out(x1, a2, [o for o, _ in att], [l for _, l in att], gates, p['mix_post_g'], p['ffn2_pre_g'],
                        p['w_hy_proj'], p['w_att_proj'], p['w_out'])
    x3 = ffn_block(x2, p['ffn2_pre_g'], p['ffn2_post_g'], p['ffn2_w_gate'], p['ffn2_w_up'], p['ffn2_w_down'],
                   xn=xn2)
    return x3.reshape(Bsz, L, D)


def kernel(x_prompt, x_sample, ffn1_pre_g, ffn1_post_g, ffn1_w_gate, ffn1_w_up, ffn1_w_down, mix_pre_g, mix_post_g, w_in, hy_conv_w, hy_conv_b, filt_w1, filt_b1, filt_w2, filt_b2, filt_w3, filt_b3, filt_w4, filt_freq, hy_bias, w_hy_proj, w_att_proj, w_out, ffn2_pre_g, ffn2_post_g, ffn2_w_gate, ffn2_w_up, ffn2_w_down):
    assert ffn1_w_gate.shape[0] == 1
    p = {
        'ffn1_pre_g': ffn1_pre_g, 'ffn1_post_g': ffn1_post_g,
        'ffn1_w_gate': ffn1_w_gate[0].astype(BF16), 'ffn1_w_up': ffn1_w_up[0].astype(BF16),
        'ffn1_w_down': ffn1_w_down[0].astype(BF16),
        'mix_pre_g': mix_pre_g, 'mix_post_g': mix_post_g,
        'w_in': _split_w_in(w_in[0]),
        'hy_conv_w': hy_conv_w[0], 'hy_conv_b': hy_conv_b,
        'filt': (filt_w1[0], filt_b1[0], filt_w2[0], filt_b2[0], filt_w3[0], filt_b3[0], filt_w4[0], filt_freq[0]),
        'hy_bias': hy_bias[0],
        'w_hy_proj': w_hy_proj[0].astype(BF16), 'w_att_proj': w_att_proj[0].astype(BF16),
        'w_out': w_out[0].astype(BF16),
        'ffn2_pre_g': ffn2_pre_g, 'ffn2_post_g': ffn2_post_g,
        'ffn2_w_gate': ffn2_w_gate[0].astype(BF16), 'ffn2_w_up': ffn2_w_up[0].astype(BF16),
        'ffn2_w_down': ffn2_w_down[0].astype(BF16),
    }
    p['rope'] = rope_tables(max(x_prompt.shape[1], x_sample.shape[1]))
    return (_layer(x_prompt, p), _layer(x_sample, p))
```

```python
import math
from functools import partial

import numpy as np
import jax
import jax.numpy as jnp
from jax import lax
from jax.experimental import pallas as pl
from jax.experimental.pallas import tpu as pltpu

D_MODEL = 2048
D_HYENA = 1024
HYENA_ORDER = 2
SHORT_CONV = 3
FILTER_EMB = 33
FILTER_BANDS = (FILTER_EMB - 1) // 2
FAST_DECAY_PCT = 0.3
SLOW_DECAY_PCT = 1.5
DECAY_TARGET = 1e-2
HEAD_DIM = 128
HEADS_PER_GROUP = 4
ATT_GROUPS = ((128, 1), (512, 4), (2048, 16))
N_GROUPS = len(ATT_GROUPS)
D_ATT = N_GROUPS * HEADS_PER_GROUP * HEAD_DIM
D_ATT_OUT = HEADS_PER_GROUP * HEAD_DIM
ROPE_DIM = HEAD_DIM // 4
ROPE_THETA = 500000.0
N_BRANCH = 2
D_IN_PROJ = 3 * D_HYENA + 3 * D_ATT + N_BRANCH * D_MODEL
D_FF = 5632
EPS = 1e-6
NEG_INF = -1e30

LANES = 128
FFT_INNER = 256
VMEM_BYTES_V7X = 64 * 1024 * 1024
VMEM_LIMIT_BYTES = VMEM_BYTES_V7X - 4 * 1024 * 1024
BF16 = jnp.bfloat16
F32 = jnp.float32
HIGHEST = lax.Precision.HIGHEST


def _rms(x, g):
    return x * lax.rsqrt(jnp.mean(x * x, axis=-1, keepdims=True) + EPS) * g


def _dot(a, b):
    return jnp.dot(a, b, preferred_element_type=F32)


def _params(*sem):
    return pltpu.CompilerParams(dimension_semantics=sem, vmem_limit_bytes=VMEM_LIMIT_BYTES)


FFN_SPLIT = 2


def _ffn_kernel(norm_in, norm_out, x_ref, xn_src_ref, post_g_ref, next_g_ref, wg_ref, wu_ref, wd_ref, *rest):
    o_ref = rest[0]
    u_ref = rest[1] if norm_out else None
    xn_ref = rest[-1] if norm_in else xn_src_ref
    j = pl.program_id(1)
    tf = wg_ref.shape[1]

    @pl.when(j == 0)
    def _():
        if norm_in:
            xn_ref[...] = _rms(x_ref[...], xn_src_ref[...]).astype(BF16)
        o_ref[...] = jnp.zeros_like(o_ref)

    xn = xn_ref[...]
    hs = []
    for c in range(FFN_SPLIT):
        cols = slice(c * (tf // FFN_SPLIT), (c + 1) * (tf // FFN_SPLIT))
        gate = _dot(xn, wg_ref[:, cols])
        up = _dot(xn, wu_ref[:, cols])
        hs.append((gate * jax.nn.sigmoid(gate) * up).astype(BF16))
    o_ref[...] += _dot(jnp.concatenate(hs, axis=1), wd_ref[...])

    @pl.when(j == pl.num_programs(1) - 1)
    def _():
        y = x_ref[...] + 0.5 * _rms(o_ref[...], post_g_ref[...])
        o_ref[...] = y
        if norm_out:
            u_ref[...] = _rms(y, next_g_ref[...]).astype(u_ref.dtype)


def ffn_block(x, pre_g, post_g, wg, wu, wd, *, xn=None, next_g=None, tm=512, tf=512):
    M, D = x.shape
    FF = wg.shape[1]
    norm_in, norm_out = xn is None, next_g is not None
    rows = pl.BlockSpec((tm, D), lambda i, j: (i, 0))
    gain = pl.BlockSpec((1, D), lambda i, j: (0, 0))
    out = pl.pallas_call(
        partial(_ffn_kernel, norm_in, norm_out),
        grid=(M // tm, FF // tf),
        in_specs=[
            rows,
            gain if norm_in else rows,
            gain, gain,
            pl.BlockSpec((D, tf), lambda i, j: (0, j)),
            pl.BlockSpec((D, tf), lambda i, j: (0, j)),
            pl.BlockSpec((tf, D), lambda i, j: (j, 0)),
        ],
        out_specs=[rows, rows] if norm_out else [rows],
        out_shape=[jax.ShapeDtypeStruct((M, D), F32)] + ([jax.ShapeDtypeStruct((M, D), BF16)] if norm_out else []),
        scratch_shapes=[pltpu.VMEM((tm, D), BF16)] if norm_in else [],
        compiler_params=_params("parallel", "arbitrary"),
        name="ffn_block",
    )(x, pre_g if norm_in else xn, post_g, next_g if norm_out else post_g, wg, wu, wd)
    return out if norm_out else out[0]


def rope_tables(L):
    half = ROPE_DIM // 2
    inv_freq = jnp.power(ROPE_THETA, -jnp.arange(half, dtype=F32) / half)
    ang = jnp.arange(L, dtype=F32)[:, None] * inv_freq[None, :]
    cos, sin = jnp.cos(ang), jnp.sin(ang)
    rest = HEAD_DIM - ROPE_DIM
    c = jnp.concatenate([cos, cos, jnp.ones((L, rest), F32)], axis=1)
    s_lo = jnp.concatenate([-sin, jnp.zeros((L, half + rest), F32)], axis=1)
    s_hi = jnp.concatenate([jnp.zeros((L, half), F32), sin, jnp.zeros((L, rest), F32)], axis=1)
    scale = HEAD_DIM ** -0.5
    return jnp.stack([c * scale, c]), jnp.stack([s_lo * scale, s_lo]), jnp.stack([s_hi * scale, s_hi])


def _proj_kernel(rotate, xn_ref, w_ref, *rest):
    o_ref = rest[-1]
    r = _dot(xn_ref[...], w_ref[...])
    if rotate:
        c_ref, slo_ref, shi_ref = rest[:3]
        tn = r.shape[1]
        half = ROPE_DIM // 2
        wide = lambda t_ref: jnp.concatenate([t_ref[...]] * (tn // HEAD_DIM), axis=1)
        r = r * wide(c_ref) + pltpu.roll(r, tn - half, 1) * wide(slo_ref) + pltpu.roll(r, half, 1) * wide(shi_ref)
    o_ref[...] = r.astype(o_ref.dtype)


def projection(xn, w, out_dtype, *, tm, tn, rope_len=None, rope=None, name):
    M, D = xn.shape
    N = w.shape[1]
    in_specs = [
        pl.BlockSpec((tm, D), lambda i, j: (i, 0)),
        pl.BlockSpec((D, tn), lambda i, j: (0, j)),
    ]
    args = [xn, w]
    if rope_len is not None:
        assert tn == D_ATT and N == 2 * D_ATT and rope_len % tm == 0
        spec = pl.BlockSpec((None, tm, HEAD_DIM), lambda i, j: (j, i % (rope_len // tm), 0))
        in_specs += [spec, spec, spec]
        assert all(t.shape[1] >= rope_len for t in rope)
        args += list(rope)
    return pl.pallas_call(
        partial(_proj_kernel, rope_len is not None),
        grid=(M // tm, N // tn),
        in_specs=in_specs,
        out_specs=pl.BlockSpec((tm, tn), lambda i, j: (i, j)),
        out_shape=jax.ShapeDtypeStruct((M, N), out_dtype),
        compiler_params=_params("parallel", "arbitrary"),
        name=name,
    )(*args)


ATT_RADIUS = 64
assert all(w // (2 * d) == ATT_RADIUS for w, d in ATT_GROUPS)


def _attn_kernel(T, Ls, q_ref, kp_ref, kc_ref, kn_ref, vp_ref, vc_ref, vn_ref, o_ref, lse_ref, kbuf, vbuf):
    t = pl.program_id(2)
    R = ATT_RADIUS
    SB = 2 * R
    for buf, prv, cur, nxt in ((kbuf, kp_ref, kc_ref, kn_ref), (vbuf, vp_ref, vc_ref, vn_ref)):
        buf[0:R] = prv[...]
        buf[R:R + T] = cur[...]
        buf[R + T:R + T + R] = nxt[...]
    qi = lax.broadcasted_iota(jnp.int32, (SB, 2 * SB), 0)
    kk = lax.broadcasted_iota(jnp.int32, (SB, 2 * SB), 1)
    band = (kk >= qi) & (kk <= qi + 2 * R)
    lane = lax.broadcasted_iota(jnp.int32, (SB, LANES), 1)
    for sb in range(T // SB):
        kpos = t * T + (sb * SB - R) + kk
        mask = band & (kpos >= 0) & (kpos < Ls)
        lse = jnp.zeros((SB, LANES), F32)
        for h in range(HEADS_PER_GROUP):
            cols = slice(h * HEAD_DIM, (h + 1) * HEAD_DIM)
            q = q_ref[sb * SB:(sb + 1) * SB, cols]
            k = kbuf[sb * SB:(sb + 2) * SB, cols]
            v = vbuf[sb * SB:(sb + 2) * SB, cols]
            s = lax.dot_general(q, k, (((1,), (1,)), ((), ())), preferred_element_type=F32)
            s = jnp.where(mask, s, NEG_INF)
            m = jnp.max(s, axis=1, keepdims=True)
            p = jnp.exp(s - m)
            l = jnp.sum(p, axis=1, keepdims=True)
            o = _dot(p.astype(BF16), v) / l
            o_ref[sb * SB:(sb + 1) * SB, cols] = o.astype(o_ref.dtype)
            lse = jnp.where(lane == h, m + jnp.log(l), lse)
        lse_ref[sb * SB:(sb + 1) * SB, :] = lse


SLAB = 16


def _softmax_heads(q, k, v, mask):
    lane = lax.broadcasted_iota(jnp.int32, (q.shape[0], LANES), 1)
    lse = jnp.zeros((q.shape[0], LANES), F32)
    outs = []
    for h in range(HEADS_PER_GROUP):
        cols = slice(h * HEAD_DIM, (h + 1) * HEAD_DIM)
        s = lax.dot_general(q[:, cols], k[:, cols], (((1,), (1,)), ((), ())), preferred_element_type=F32)
        s = jnp.where(mask, s, NEG_INF)
        m = jnp.max(s, axis=1, keepdims=True)
        p = jnp.exp(s - m)
        l = jnp.sum(p, axis=1, keepdims=True)
        outs.append(_dot(p.astype(BF16), v[:, cols]) / l)
        lse = jnp.where(lane == h, m + jnp.log(l), lse)
    return jnp.concatenate(outs, axis=1), lse


def _attn_slab_kernel(d, NS, Ls, q_ref, kp_ref, kc_ref, kn_ref, vp_ref, vc_ref, vn_ref, o_ref, lse_ref,
                      qs, ks, vs, os_, ls):
    t = pl.program_id(1)
    R = ATT_RADIUS
    SB = 2 * R
    J = SLAB // d
    NH = kp_ref.shape[0]
    QA = SB // J
    KA = QA + 2 * NH
    n_sb = NS // QA
    to_class_major = lambda ref: pltpu.einshape("abc->bac", ref[...])
    qs[...] = to_class_major(q_ref)
    for buf, prv, cur, nxt in ((ks, kp_ref, kc_ref, kn_ref), (vs, vp_ref, vc_ref, vn_ref)):
        buf[:, 0:NH] = to_class_major(prv)
        buf[:, NH:NH + NS] = to_class_major(cur)
        buf[:, NH + NS:NH + NS + NH] = to_class_major(nxt)
    qi = lax.broadcasted_iota(jnp.int32, (SB, 2 * SB), 0)
    kk = lax.broadcasted_iota(jnp.int32, (SB, 2 * SB), 1)
    sq_rel = J * (qi % QA) + qi // QA
    sk_rel = J * (kk % KA - NH) + kk // KA
    band = jnp.abs(sk_rel - sq_rel) <= R

    def body(it, carry):
        r = it // n_sb
        a0 = pl.multiple_of((it % n_sb) * QA, QA)
        sk = J * (t * NS + a0) + sk_rel
        mask = band & (sk >= 0) & (sk < Ls)
        gather = lambda buf, n: jnp.concatenate([buf[j * d + r, pl.ds(a0, n), :] for j in range(J)], axis=0)
        o, lse = _softmax_heads(gather(qs, QA), gather(ks, KA), gather(vs, KA), mask)
        o = o.astype(os_.dtype)
        for j in range(J):
            os_[j * d + r, pl.ds(a0, QA), :] = o[j * QA:(j + 1) * QA]
            ls[j * d + r, pl.ds(a0, QA), :] = lse[j * QA:(j + 1) * QA]
        return carry

    lax.fori_loop(0, d * n_sb, body, 0, unroll=True)
    o_ref[...] = pltpu.einshape("bac->abc", os_[...])
    lse_ref[...] = pltpu.einshape("bac->abc", ls[...])


def dilated_attention_slabs(srcs, Bsz, L, g, *, NS=128):
    d = ATT_GROUPS[g][1]
    M = Bsz * L
    R = ATT_RADIUS
    GW = HEADS_PER_GROUP * HEAD_DIM
    NH = R * d // SLAB
    TB = NS * SLAB
    assert SLAB % d == 0 and L % TB == 0 and NS % NH == 0 and (2 * R) % (SLAB // d) == 0
    nmb, nhb, hpm = L // TB, L // (NH * SLAB), NS // NH
    views = [a.reshape(M // SLAB, SLAB, a.shape[1]) for a, _ in srcs]
    col = lambda which: srcs[which][1]

    def main(which):
        return pl.BlockSpec((NS, SLAB, GW), lambda b, t: (b * nmb + t, 0, col(which)))

    def prev(which):
        return pl.BlockSpec((NH, SLAB, GW), lambda b, t: (jnp.maximum(b * nhb + t * hpm - 1, b * nhb), 0, col(which)))

    def nxt(which):
        return pl.BlockSpec((NH, SLAB, GW),
                            lambda b, t: (jnp.minimum(b * nhb + (t + 1) * hpm, (b + 1) * nhb - 1), 0, col(which)))

    o, lse = pl.pallas_call(
        partial(_attn_slab_kernel, d, NS, L // d),
        grid=(Bsz, nmb),
        in_specs=[main(0), prev(1), main(1), nxt(1), prev(2), main(2), nxt(2)],
        out_specs=[pl.BlockSpec((NS, SLAB, GW), lambda b, t: (b * nmb + t, 0, 0)),
                   pl.BlockSpec((NS, SLAB, LANES), lambda b, t: (b * nmb + t, 0, 0))],
        out_shape=[jax.ShapeDtypeStruct((M // SLAB, SLAB, GW), BF16),
                   jax.ShapeDtypeStruct((M // SLAB, SLAB, LANES), F32)],
        scratch_shapes=[pltpu.VMEM((SLAB, NS, GW), BF16),
                        pltpu.VMEM((SLAB, NS + 2 * NH, GW), BF16), pltpu.VMEM((SLAB, NS + 2 * NH, GW), BF16),
                        pltpu.VMEM((SLAB, NS, GW), BF16), pltpu.VMEM((SLAB, NS, LANES), F32)],
        compiler_params=_params("parallel", "arbitrary"),
        name=f"dilated_attention_g{g}",
    )(views[0], views[1], views[1], views[1], views[2], views[2], views[2])
    return o.reshape(M, GW), lse.reshape(M, LANES)


def dilated_attention_group(srcs, Bsz, L, g, *, T=1024):
    d = ATT_GROUPS[g][1]
    M = Bsz * L
    Ls = L // d
    R = ATT_RADIUS
    GW = HEADS_PER_GROUP * HEAD_DIM
    assert d == 1 and Ls % T == 0 and T % (2 * R) == 0
    nrb, nhb, hpt = Ls // T, Ls // R, T // R

    def main(which):
        return pl.BlockSpec((T, GW), lambda b, r, t: (b * nrb + t, srcs[which][1]))

    def prev(which):
        return pl.BlockSpec((R, GW), lambda b, r, t: (jnp.maximum(b * nhb + t * hpt - 1, b * nhb), srcs[which][1]))

    def nxt(which):
        return pl.BlockSpec((R, GW), lambda b, r, t: (jnp.minimum(b * nhb + (t + 1) * hpt, (b + 1) * nhb - 1),
                                                      srcs[which][1]))

    o, lse = pl.pallas_call(
        partial(_attn_kernel, T, Ls),
        grid=(Bsz, d, nrb),
        in_specs=[main(0), prev(1), main(1), nxt(1), prev(2), main(2), nxt(2)],
        out_specs=[pl.BlockSpec((T, GW), lambda b, r, t: (b * nrb + t, r)),
                   pl.BlockSpec((T, LANES), lambda b, r, t: (b * nrb + t, r))],
        out_shape=[jax.ShapeDtypeStruct((M // d, d * GW), BF16),
                   jax.ShapeDtypeStruct((M // d, d * LANES), F32)],
        scratch_shapes=[pltpu.VMEM((T + 2 * R, GW), BF16), pltpu.VMEM((T + 2 * R, GW), BF16)],
        compiler_params=_params("parallel", "parallel", "arbitrary"),
        name=f"dilated_attention_g{g}",
    )(srcs[0][0], srcs[1][0], srcs[1][0], srcs[1][0], srcs[2][0], srcs[2][0], srcs[2][0])
    return o.reshape(M, GW), lse.reshape(M, LANES)


def _merge_kernel(x_ref, a_ref, o0_ref, o1_ref, o2_ref, l0_ref, l1_ref, l2_ref, ga_ref, gb_ref,
                  post_g_ref, next_g_ref, whp_ref, wap_ref, wo_ref, o_ref, u_ref):
    lses = [l0_ref[...], l1_ref[...], l2_ref[...]]
    mx = jnp.maximum(jnp.maximum(lses[0], lses[1]), lses[2])
    es = [jnp.exp(l - mx) for l in lses]
    den = es[0] + es[1] + es[2]
    wts = [e / den for e in es]
    outs = [o0_ref, o1_ref, o2_ref]
    heads = []
    for h in range(HEADS_PER_GROUP):
        cols = slice(h * HEAD_DIM, (h + 1) * HEAD_DIM)
        heads.append(sum(wts[g][:, h:h + 1] * outs[g][:, cols].astype(F32) for g in range(N_GROUPS)))
    att = jnp.concatenate(heads, axis=1).astype(BF16)
    a = _dot(a_ref[...], whp_ref[...])
    b = _dot(att, wap_ref[...])
    gate_a, gate_b = ga_ref[...].astype(F32), gb_ref[...].astype(F32)
    merged = (jax.nn.sigmoid(gate_a) * a + jax.nn.sigmoid(gate_b) * b).astype(BF16)
    mix = _dot(merged, wo_ref[...])
    y = x_ref[...] + _rms(mix, post_g_ref[...])
    o_ref[...] = y
    u_ref[...] = _rms(y, next_g_ref[...]).astype(u_ref.dtype)


def merge_out(x, a_in, att_outs, att_lses, gates, post_g, next_g, whp, wap, wo, *, tm=512):
    M, D = x.shape
    const = lambda i: (0, 0)
    rows = lambda a: pl.BlockSpec((tm, a.shape[1]), lambda i: (i, 0))
    return pl.pallas_call(
        _merge_kernel,
        grid=(M // tm,),
        in_specs=[rows(x), rows(a_in)] + [rows(o) for o in att_outs] + [rows(l) for l in att_lses] + [
            pl.BlockSpec((tm, D), lambda i: (i, 0)),
            pl.BlockSpec((tm, D), lambda i: (i, 1)),
            pl.BlockSpec((1, D), const),
            pl.BlockSpec((1, D), const),
            pl.BlockSpec(whp.shape, const, pipeline_mode=pl.Buffered(1)),
            pl.BlockSpec(wap.shape, const, pipeline_mode=pl.Buffered(1)),
            pl.BlockSpec(wo.shape, const, pipeline_mode=pl.Buffered(1)),
        ],
        out_specs=[pl.BlockSpec((tm, D), lambda i: (i, 0)), pl.BlockSpec((tm, D), lambda i: (i, 0))],
        out_shape=[jax.ShapeDtypeStruct((M, D), F32), jax.ShapeDtypeStruct((M, D), BF16)],
        compiler_params=_params("parallel"),
        name="merge_out",
    )(x, a_in, *att_outs, *att_lses, gates, gates, post_g, next_g, whp, wap, wo)


def _dft_constants(L):
    N = 2 * L
    N2 = FFT_INNER
    N1 = N // N2
    h = N1 // 2
    idx1 = np.arange(N1)
    ang1 = -2.0 * np.pi * ((idx1[:, None] * idx1[None, :]) % N1) / N1
    f1r, f1i = np.cos(ang1), np.sin(ang1)
    s_data = np.block([[f1r[:, :h], -f1i[:, :h]], [f1i[:, :h], f1r[:, :h]]])
    hk = h + FFT_ROWS
    s_filt = np.concatenate([f1r[:hk], f1i[:hk]], axis=0)
    ar, ai = f1r[:h, :], -f1i[:h, :]
    t_fin = np.block([[ar, -ai], [ai, ar]])
    idx2 = np.arange(N2)
    ang2 = -2.0 * np.pi * ((idx2[:, None] * idx2[None, :]) % N2) / N2
    angt = -2.0 * np.pi * (idx1[:, None] * idx2[None, :]) / N
    return dict(
        N1=N1,
        s_data=jnp.asarray(s_data, BF16), s_filt=jnp.asarray(s_filt, BF16), t_fin=jnp.asarray(t_fin, BF16),
        f2r=jnp.asarray(np.cos(ang2), F32), f2i=jnp.asarray(np.sin(ang2), F32),
        twr=jnp.asarray(np.cos(angt).reshape(N1, 1, N2), F32),
        twi=jnp.asarray(np.sin(angt).reshape(N1, 1, N2), F32),
    )


def _hyena_in_kernel(tm, L, x_ref, prev_ref, next_ref, w_ref, cw_ref, cb_ref, o_ref, xn_ref):
    i = pl.program_id(0)
    H = prev_ref.shape[0]

    @pl.when(pl.program_id(1) == 0)
    def _():
        xn_ref[0:H] = prev_ref[...]
        xn_ref[H:H + tm] = x_ref[...]
        xn_ref[H + tm:H + tm + H] = next_ref[...]

    r = _dot(xn_ref[...], w_ref[...])
    row = lax.broadcasted_iota(jnp.int32, r.shape, 0)
    pos0 = (i * tm) % L
    outside = ((row < H) & (pos0 == 0)) | ((row >= H + tm) & (pos0 + tm == L))
    r = jnp.where(outside, 0.0, r)
    n = r.shape[0]
    y = pltpu.roll(r, 1, 0) * cw_ref[0:1, :] + r * cw_ref[1:2, :] + pltpu.roll(r, n - 1, 0) * cw_ref[2:3, :]
    o_ref[0] = (y[H:H + tm] + cb_ref[...]).astype(o_ref.dtype)


def hyena_inputs(x, w, conv_w, conv_b, L, *, tm=1024):
    M, D = x.shape
    C = w.shape[1] // 3
    H = SLAB
    nh = M // H
    assert L % tm == 0 and tm % H == 0
    return pl.pallas_call(
        partial(_hyena_in_kernel, tm, L),
        grid=(M // tm, 3),
        in_specs=[
            pl.BlockSpec((tm, D), lambda i, j: (i, 0)),
            pl.BlockSpec((H, D), lambda i, j: (jnp.maximum(i * (tm // H) - 1, 0), 0)),
            pl.BlockSpec((H, D), lambda i, j: (jnp.minimum((i + 1) * (tm // H), nh - 1), 0)),
            pl.BlockSpec((D, C), lambda i, j: (0, j)),
            pl.BlockSpec((SHORT_CONV, C), lambda i, j: (0, j)),
            pl.BlockSpec((1, C), lambda i, j: (0, j)),
        ],
        out_specs=pl.BlockSpec((1, tm, C), lambda i, j: (j, i, 0)),
        out_shape=jax.ShapeDtypeStruct((3, M, C), BF16),
        scratch_shapes=[pltpu.VMEM((tm + 2 * H, D), BF16)],
        compiler_params=_params("parallel", "arbitrary"),
        name="hyena_inputs",
    )(x, x, x, w, conv_w, conv_b)


def _filter_kernel(L, h, w1_ref, b1_ref, w2_ref, b2_ref, w3_ref, b3_ref, w4_ref,
                   freq_ref, fvec_ref, delta_ref, kf_ref):
    i = pl.program_id(0)
    HALF = LANES // 2
    wide = delta_ref.shape[1]
    C = wide // (2 * HYENA_ORDER)
    tl = FFT_ROWS * h

    def lag(shape):
        r = lax.broadcasted_iota(jnp.int32, shape, 0)
        return (r % h) * FFT_INNER + i * FFT_ROWS + r // h

    row = lag((tl, LANES))
    lane = lax.broadcasted_iota(jnp.int32, (tl, LANES), 1)
    posf = jnp.where(lane < HALF, row, L - row).astype(F32)
    t = posf / (L - 1)
    a = fvec_ref[...] * ((2.0 * math.pi / L) * posf)
    lh = lane % HALF
    feats = jnp.where(lh < FILTER_BANDS, jnp.cos(a),
                      jnp.where(lh < 2 * FILTER_BANDS, -jnp.sin(a),
                                jnp.where(lh == 2 * FILTER_BANDS, t, 0.0)))
    freq = freq_ref[...]
    def dot3(v, w_ref):
        v_hi = v.astype(BF16)
        v_lo = (v - v_hi.astype(F32)).astype(BF16)
        return _dot(jnp.concatenate([v_hi, v_lo, v_hi], axis=1), w_ref[...])

    dense = lambda v, w_ref, b_ref: jnp.sin(freq * (dot3(v, w_ref) + b_ref[...]))
    hid = dense(dense(dense(feats, w1_ref, b1_ref), w2_ref, b2_ref), w3_ref, b3_ref)
    taps = dot3(hid, w4_ref)
    row_w = lag((tl, wide))
    col_w = lax.broadcasted_iota(jnp.int32, (tl, wide), 1)
    backward = col_w >= wide // 2
    t_w = jnp.where(backward, L - row_w, row_w).astype(F32) / (L - 1)
    taps = taps * jnp.exp(-t_w * delta_ref[...])
    taps = jnp.where(backward & (row_w == 0), 0.0, taps)
    for n in range(HYENA_ORDER):
        for direction in range(2):
            c0 = (direction * HYENA_ORDER + n) * C
            part = taps[:, c0:c0 + C].astype(kf_ref.dtype).reshape(FFT_ROWS, h, C)
            kf_ref[n, 0, :, direction * h:(direction + 1) * h, :] = part


def hyena_filter_taps(L, fw1, fb1, fw2, fb2, fw3, fb3, fw4, ffreq):
    C = D_HYENA
    h = L // FFT_INNER
    groups = FFT_INNER // FFT_ROWS
    H = fw2.shape[0]
    HALF = LANES // 2
    assert H <= HALF and 2 * FILTER_BANDS + 1 <= HALF

    def both(a, rows):
        blk = jnp.zeros((HALF if rows else 1, HALF), F32).at[:a.shape[0], :a.shape[1]].set(a)
        if not rows:
            return jnp.concatenate([blk, blk], axis=1)
        zero = jnp.zeros_like(blk)
        return jnp.concatenate([jnp.concatenate([blk, zero], axis=1), jnp.concatenate([zero, blk], axis=1)], axis=0)

    w1 = both(jnp.concatenate([fw1[1:], fw1[:1]], axis=0), True)
    w2, w3 = both(fw2, True), both(fw3, True)
    b1, b2, b3, freq = (both(v[None, :], False) for v in (fb1, fb2, fb3, ffreq))
    w4 = fw4.reshape(H, HYENA_ORDER, 2, C)
    wide = 2 * HYENA_ORDER * C
    w4p = jnp.zeros((LANES, wide), F32)
    w4p = w4p.at[:H, :wide // 2].set(w4[:, :, 0].reshape(H, HYENA_ORDER * C))
    w4p = w4p.at[HALF:HALF + H, wide // 2:].set(w4[:, :, 1].reshape(H, HYENA_ORDER * C))
    def stack3(w):
        hi = w.astype(BF16)
        lo = (w - hi.astype(F32)).astype(BF16)
        return jnp.concatenate([hi, hi, lo], axis=0)

    w1, w2, w3, w4_stack = stack3(w1), stack3(w2), stack3(w3), stack3(w4p)
    bands = np.linspace(1e-4, FILTER_BANDS - 1, FILTER_BANDS, dtype=np.float32)
    fvec = np.zeros((1, LANES), np.float32)
    for base in (0, HALF):
        fvec[0, base:base + FILTER_BANDS] = bands
        fvec[0, base + FILTER_BANDS:base + 2 * FILTER_BANDS] = bands
    max_decay = math.log(DECAY_TARGET) / FAST_DECAY_PCT
    min_decay = math.log(DECAY_TARGET) / SLOW_DECAY_PCT
    deltas = np.abs(np.linspace(min_decay, max_decay, C, dtype=np.float32))
    delta_w = np.tile(deltas[None, :], (1, 2 * HYENA_ORDER))
    const = lambda i: (0, 0)
    args = (w1, b1, w2, b2, w3, b3, w4_stack, freq, jnp.asarray(fvec), jnp.asarray(delta_w))
    return pl.pallas_call(
        partial(_filter_kernel, L, h),
        grid=(groups,),
        in_specs=[pl.BlockSpec(a.shape, const) for a in args],
        out_specs=pl.BlockSpec((HYENA_ORDER, 1, FFT_ROWS, 2 * h, C), lambda i: (0, i, 0, 0, 0)),
        out_shape=jax.ShapeDtypeStruct((HYENA_ORDER, groups, FFT_ROWS, 2 * h, C), BF16),
        compiler_params=_params("parallel"),
        name="hyena_filter_taps",
    )(*args)


FFT_ROWS = 16
FFT_COLS = 512
FFT_COLS_IN = 1024


def _stage1_kernel(inner_major, s_ref, x_ref, y_ref):
    s = s_ref[...]
    if inner_major:
        xt = x_ref[...]
    else:
        _, h, rows, cols = x_ref.shape
        xt = pltpu.einshape("abc->bac", x_ref[...].reshape(2 * h, rows, cols))
    yt = jnp.stack([_dot(s, xt[b]).astype(y_ref.dtype) for b in range(xt.shape[0])], axis=0)
    y_ref[...] = pltpu.einshape("bac->abc", yt).reshape(y_ref.shape)


def fft_stage1(x, which, s, *, inner_major=False):
    if inner_major:
        _, groups, _, N1, C = x.shape
        N2 = groups * FFT_ROWS
        x_spec = pl.BlockSpec((None, None, FFT_ROWS, N1, FFT_COLS_IN), lambda j, c: (which, j, 0, 0, c))
    else:
        _, _, h, N2, C = x.shape
        N1 = 2 * h
        x_spec = pl.BlockSpec((None, 2, h, FFT_ROWS, FFT_COLS_IN), lambda j, c: (which, 0, 0, j, c))
    return pl.pallas_call(
        partial(_stage1_kernel, inner_major),
        grid=(N2 // FFT_ROWS, C // FFT_COLS_IN),
        in_specs=[pl.BlockSpec(s.shape, lambda j, c: (0, 0)), x_spec],
        out_specs=pl.BlockSpec((2, s.shape[0] // 2, FFT_ROWS, FFT_COLS_IN), lambda j, c: (0, 0, j, c)),
        out_shape=jax.ShapeDtypeStruct((2, s.shape[0] // 2, N2, C), BF16),
        compiler_params=_params("parallel", "parallel"),
        name="fft_stage1",
    )(s, x)


FFT_K1_PER_STEP = 4


def _mid_kernel(inv_n, n1, f2r_ref, f2i_ref, twr_ref, twi_ref, *refs):
    kb = len(refs) - 2
    yf_refs, y_ref, u_ref = refs[:kb], refs[kb], refs[kb + 1]
    f2r, f2i = f2r_ref[...], f2i_ref[...]
    for k in range(kb):
        twr, twi = twr_ref[k], twi_ref[k]
        gr, gi = f2r * twr - f2i * twi, f2r * twi + f2i * twr

        def times(ar, ai, x):
            ar, ai = ar.astype(BF16), ai.astype(BF16)
            return (_dot(jnp.concatenate([ar, -ai], axis=1), x), _dot(jnp.concatenate([ai, ar], axis=1), x))

        mirrored = pl.program_id(0) * kb + k > n1 // 2
        yf_im = yf_refs[k][1, 0]
        taps = jnp.concatenate([yf_refs[k][0, 0], jnp.where(mirrored, -yf_im, yf_im)], axis=0)
        hr, hi = times(gr, gi, taps)
        zr, zi = times(gr, gi, y_ref[:, k].reshape(2 * gr.shape[0], -1))
        prod = jnp.concatenate([(zr * hr - zi * hi).astype(BF16), (zr * hi + zi * hr).astype(BF16)], axis=0)
        ur, ui = times(gr.T * inv_n, gi.T * (-inv_n), prod)
        u_ref[0, k] = ur.astype(u_ref.dtype)
        u_ref[1, k] = ui.astype(u_ref.dtype)


def fft_mid(yf, y, dc):
    _, N1, N2, C = y.shape
    kb = FFT_K1_PER_STEP
    assert yf.shape[1] > N1 // 2
    blk = pl.BlockSpec((2, kb, N2, C), lambda k: (0, k, 0, 0))
    const = pl.BlockSpec((N2, N2), lambda k: (0, 0))
    tw = pl.BlockSpec((kb, 1, N2), lambda k: (k, 0, 0))

    def filt_blk(j):
        def index(k):
            q = k * kb + j
            return (0, jnp.where(q > N1 // 2, N1 - q, q), 0, 0)
        return pl.BlockSpec((2, 1, N2, C), index)

    return pl.pallas_call(
        partial(_mid_kernel, 1.0 / (N1 * N2), N1),
        grid=(N1 // kb,),
        in_specs=[const, const, tw, tw] + [filt_blk(j) for j in range(kb)] + [blk],
        out_specs=blk,
        out_shape=jax.ShapeDtypeStruct((2, N1, N2, C), BF16),
        compiler_params=_params("parallel"),
        name="fft_mid",
    )(dc['f2r'], dc['f2i'], dc['twr'], dc['twi'], *([yf] * kb), y)


def _final_kernel(t_ref, u_ref, z_ref, gate_ref, bias_ref, o_ref):
    _, n1, rows, cols = u_ref.shape
    t = t_ref[...]
    ut = pltpu.einshape("abc->bac", u_ref[...].reshape(2 * n1, rows, cols))
    convt = jnp.stack([_dot(t, ut[b]).astype(BF16) for b in range(rows)], axis=0)
    conv = pltpu.einshape("bac->abc", convt).reshape(o_ref.shape).astype(F32)
    z = z_ref[...].astype(F32)
    o_ref[...] = (gate_ref[...].astype(F32) * (conv + bias_ref[...] * z)).astype(o_ref.dtype)


def fft_final(u, t_fin, z, z_which, gate, gate_which, bias, bias_which):
    _, N1, N2, C = u.shape
    h = N1 // 2

    def half(which):
        return pl.BlockSpec((None, 2, h, FFT_ROWS, FFT_COLS), lambda j, c: (which, 0, 0, j, c))

    return pl.pallas_call(
        _final_kernel,
        grid=(N2 // FFT_ROWS, C // FFT_COLS),
        in_specs=[pl.BlockSpec((N1, 2 * N1), lambda j, c: (0, 0)),
                  pl.BlockSpec((2, N1, FFT_ROWS, FFT_COLS), lambda j, c: (0, 0, j, c)),
                  half(z_which), half(gate_which),
                  pl.BlockSpec((None, 1, FFT_COLS), lambda j, c: (bias_which, 0, c))],
        out_specs=half(0),
        out_shape=jax.ShapeDtypeStruct((1, 2, h, N2, C), BF16),
        compiler_params=_params("parallel", "parallel"),
        name="fft_final",
    )(t_fin, u, z, gate, bias.reshape(bias.shape[0], 1, C))


def hyena_branch(zs, L, p):
    _, M, C = zs.shape
    assert M == 2 * L, "the batch pair rides as real/imaginary parts"
    dc = _dft_constants(L)
    split = (2, dc['N1'] // 2, FFT_INNER, C)
    taps = hyena_filter_taps(L, *p['filt'])
    zs = zs.reshape(3, *split)
    z, z_which = zs, 0
    for n in range(HYENA_ORDER):
        yf = fft_stage1(taps, n, dc['s_filt'], inner_major=True)
        u = fft_mid(yf, fft_stage1(z, z_which, dc['s_data']), dc)
        z, z_which = fft_final(u, dc['t_fin'], z, z_which, zs, n + 1, p['hy_bias'], n), 0
    return z.reshape(M, C)


def _split_w_in(w):
    c_q = 3 * D_HYENA
    c_v = c_q + 2 * D_ATT
    c_g = c_v + D_ATT
    return tuple(w[:, a:b].astype(BF16) for a, b in ((0, c_q), (c_q, c_v), (c_v, c_g), (c_g, w.shape[1])))


def _layer(x, p):
    Bsz, L, D = x.shape
    M = Bsz * L
    x0 = x.reshape(M, D)
    x1, u = ffn_block(x0, p['ffn1_pre_g'], p['ffn1_post_g'], p['ffn1_w_gate'], p['ffn1_w_up'], p['ffn1_w_down'],
                      next_g=p['mix_pre_g'])
    w_hy, w_qk, w_v, w_gate = p['w_in']
    hy_in = hyena_inputs(u, w_hy, p['hy_conv_w'], p['hy_conv_b'], L)
    qk = projection(u, w_qk, BF16, tm=1024, tn=D_ATT, rope_len=L, rope=p['rope'], name="inproj_qk")
    v = projection(u, w_v, BF16, tm=1024, tn=D_ATT, name="inproj_v")
    gates = projection(u, w_gate, BF16, tm=1024, tn=D_MODEL, name="inproj_gates")
    a2 = hyena_branch(hy_in, L, p)
    att = []
    for g in range(N_GROUPS):
        srcs = ((qk, g), (qk, N_GROUPS + g), (v, g))
        fn = dilated_attention_group if ATT_GROUPS[g][1] == 1 else dilated_attention_slabs
        att.append(fn(srcs, Bsz, L, g))
    x2, xn2 = merge_out(x1, a2, [o for o, _ in att], [l for _, l in att], gates, p['mix_post_g'], p['ffn2_pre_g'],
                        p['w_hy_proj'], p['w_att_proj'], p['w_out'])
    x3 = ffn_block(x2, p['ffn2_pre_g'], p['ffn2_post_g'], p['ffn2_w_gate'], p['ffn2_w_up'], p['ffn2_w_down'],
                   xn=xn2)
    return x3.reshape(Bsz, L, D)


def kernel(x_prompt, x_sample, ffn1_pre_g, ffn1_post_g, ffn1_w_gate, ffn1_w_up, ffn1_w_down, mix_pre_g, mix_post_g, w_in, hy_conv_w, hy_conv_b, filt_w1, filt_b1, filt_w2, filt_b2, filt_w3, filt_b3, filt_w4, filt_freq, hy_bias, w_hy_proj, w_att_proj, w_out, ffn2_pre_g, ffn2_post_g, ffn2_w_gate, ffn2_w_up, ffn2_w_down):
    assert ffn1_w_gate.shape[0] == 1
    p = {
        'ffn1_pre_g': ffn1_pre_g, 'ffn1_post_g': ffn1_post_g,
        'ffn1_w_gate': ffn1_w_gate[0].astype(BF16), 'ffn1_w_up': ffn1_w_up[0].astype(BF16),
        'ffn1_w_down': ffn1_w_down[0].astype(BF16),
        'mix_pre_g': mix_pre_g, 'mix_post_g': mix_post_g,
        'w_in': _split_w_in(w_in[0]),
        'hy_conv_w': hy_conv_w[0], 'hy_conv_b': hy_conv_b,
        'filt': (filt_w1[0], filt_b1[0], filt_w2[0], filt_b2[0], filt_w3[0], filt_b3[0], filt_w4[0], filt_freq[0]),
        'hy_bias': hy_bias[0],
        'w_hy_proj': w_hy_proj[0].astype(BF16), 'w_att_proj': w_att_proj[0].astype(BF16),
        'w_out': w_out[0].astype(BF16),
        'ffn2_pre_g': ffn2_pre_g, 'ffn2_post_g': ffn2_post_g,
        'ffn2_w_gate': ffn2_w_gate[0].astype(BF16), 'ffn2_w_up': ffn2_w_up[0].astype(BF16),
        'ffn2_w_down': ffn2_w_down[0].astype(BF16),
    }
    p['rope'] = rope_tables(max(x_prompt.shape[1], x_sample.shape[1]))
    return (_layer(x_prompt, p), _layer(x_sample, p))
```

```python
import math
from functools import partial

import numpy as np
import jax
import jax.numpy as jnp
from jax import lax
from jax.experimental import pallas as pl
from jax.experimental.pallas import tpu as pltpu

D_MODEL = 2048
D_HYENA = 1024
HYENA_ORDER = 2
SHORT_CONV = 3
FILTER_EMB = 33
FILTER_BANDS = (FILTER_EMB - 1) // 2
FAST_DECAY_PCT = 0.3
SLOW_DECAY_PCT = 1.5
DECAY_TARGET = 1e-2
HEAD_DIM = 128
HEADS_PER_GROUP = 4
ATT_GROUPS = ((128, 1), (512, 4), (2048, 16))
N_GROUPS = len(ATT_GROUPS)
D_ATT = N_GROUPS * HEADS_PER_GROUP * HEAD_DIM
D_ATT_OUT = HEADS_PER_GROUP * HEAD_DIM
ROPE_DIM = HEAD_DIM // 4
ROPE_THETA = 500000.0
N_BRANCH = 2
D_IN_PROJ = 3 * D_HYENA + 3 * D_ATT + N_BRANCH * D_MODEL
D_FF = 5632
EPS = 1e-6
NEG_INF = -1e30

LANES = 128
FFT_INNER = 256
VMEM_BYTES_V7X = 64 * 1024 * 1024
VMEM_LIMIT_BYTES = VMEM_BYTES_V7X - 4 * 1024 * 1024
BF16 = jnp.bfloat16
F32 = jnp.float32
HIGHEST = lax.Precision.HIGHEST


def _rms(x, g):
    return x * lax.rsqrt(jnp.mean(x * x, axis=-1, keepdims=True) + EPS) * g


def _dot(a, b):
    return jnp.dot(a, b, preferred_element_type=F32)


def _params(*sem):
    return pltpu.CompilerParams(dimension_semantics=sem, vmem_limit_bytes=VMEM_LIMIT_BYTES)


FFN_SPLIT = 2


def _ffn_kernel(norm_in, norm_out, x_ref, xn_src_ref, post_g_ref, next_g_ref, wg_ref, wu_ref, wd_ref, *rest):
    o_ref = rest[0]
    u_ref = rest[1] if norm_out else None
    xn_ref = rest[-1] if norm_in else xn_src_ref
    j = pl.program_id(1)
    tf = wg_ref.shape[1]

    @pl.when(j == 0)
    def _():
        if norm_in:
            xn_ref[...] = _rms(x_ref[...], xn_src_ref[...]).astype(BF16)
        o_ref[...] = jnp.zeros_like(o_ref)

    xn = xn_ref[...]
    hs = []
    for c in range(FFN_SPLIT):
        cols = slice(c * (tf // FFN_SPLIT), (c + 1) * (tf // FFN_SPLIT))
        gate = _dot(xn, wg_ref[:, cols])
        up = _dot(xn, wu_ref[:, cols])
        hs.append((gate * jax.nn.sigmoid(gate) * up).astype(BF16))
    o_ref[...] += _dot(jnp.concatenate(hs, axis=1), wd_ref[...])

    @pl.when(j == pl.num_programs(1) - 1)
    def _():
        y = x_ref[...] + 0.5 * _rms(o_ref[...], post_g_ref[...])
        o_ref[...] = y
        if norm_out:
            u_ref[...] = _rms(y, next_g_ref[...]).astype(u_ref.dtype)


def ffn_block(x, pre_g, post_g, wg, wu, wd, *, xn=None, next_g=None, tm=512, tf=512):
    M, D = x.shape
    FF = wg.shape[1]
    norm_in, norm_out = xn is None, next_g is not None
    rows = pl.BlockSpec((tm, D), lambda i, j: (i, 0))
    gain = pl.BlockSpec((1, D), lambda i, j: (0, 0))
    out = pl.pallas_call(
        partial(_ffn_kernel, norm_in, norm_out),
        grid=(M // tm, FF // tf),
        in_specs=[
            rows,
            gain if norm_in else rows,
            gain, gain,
            pl.BlockSpec((D, tf), lambda i, j: (0, j)),
            pl.BlockSpec((D, tf), lambda i, j: (0, j)),
            pl.BlockSpec((tf, D), lambda i, j: (j, 0)),
        ],
        out_specs=[rows, rows] if norm_out else [rows],
        out_shape=[jax.ShapeDtypeStruct((M, D), F32)] + ([jax.ShapeDtypeStruct((M, D), BF16)] if norm_out else []),
        scratch_shapes=[pltpu.VMEM((tm, D), BF16)] if norm_in else [],
        compiler_params=_params("parallel", "arbitrary"),
        name="ffn_block",
    )(x, pre_g if norm_in else xn, post_g, next_g if norm_out else post_g, wg, wu, wd)
    return out if norm_out else out[0]


def rope_tables(L):
    half = ROPE_DIM // 2
    inv_freq = jnp.power(ROPE_THETA, -jnp.arange(half, dtype=F32) / half)
    ang = jnp.arange(L, dtype=F32)[:, None] * inv_freq[None, :]
    cos, sin = jnp.cos(ang), jnp.sin(ang)
    rest = HEAD_DIM - ROPE_DIM
    c = jnp.concatenate([cos, cos, jnp.ones((L, rest), F32)], axis=1)
    s_lo = jnp.concatenate([-sin, jnp.zeros((L, half + rest), F32)], axis=1)
    s_hi = jnp.concatenate([jnp.zeros((L, half), F32), sin, jnp.zeros((L, rest), F32)], axis=1)
    scale = HEAD_DIM ** -0.5
    return jnp.stack([c * scale, c]), jnp.stack([s_lo * scale, s_lo]), jnp.stack([s_hi * scale, s_hi])


def _proj_kernel(rotate, xn_ref, w_ref, *rest):
    o_ref = rest[-1]
    r = _dot(xn_ref[...], w_ref[...])
    if rotate:
        c_ref, slo_ref, shi_ref = rest[:3]
        tn = r.shape[1]
        half = ROPE_DIM // 2
        wide = lambda t_ref: jnp.concatenate([t_ref[...]] * (tn // HEAD_DIM), axis=1)
        r = r * wide(c_ref) + pltpu.roll(r, tn - half, 1) * wide(slo_ref) + pltpu.roll(r, half, 1) * wide(shi_ref)
    o_ref[...] = r.astype(o_ref.dtype)


def projection(xn, w, out_dtype, *, tm, tn, rope_len=None, rope=None, name):
    M, D = xn.shape
    N = w.shape[1]
    in_specs = [
        pl.BlockSpec((tm, D), lambda i, j: (i, 0)),
        pl.BlockSpec((D, tn), lambda i, j: (0, j)),
    ]
    args = [xn, w]
    if rope_len is not None:
        assert tn == D_ATT and N == 2 * D_ATT and rope_len % tm == 0
        spec = pl.BlockSpec((None, tm, HEAD_DIM), lambda i, j: (j, i % (rope_len // tm), 0))
        in_specs += [spec, spec, spec]
        assert all(t.shape[1] >= rope_len for t in rope)
        args += list(rope)
    return pl.pallas_call(
        partial(_proj_kernel, rope_len is not None),
        grid=(M // tm, N // tn),
        in_specs=in_specs,
        out_specs=pl.BlockSpec((tm, tn), lambda i, j: (i, j)),
        out_shape=jax.ShapeDtypeStruct((M, N), out_dtype),
        compiler_params=_params("parallel", "arbitrary"),
        name=name,
    )(*args)


ATT_RADIUS = 64
assert all(w // (2 * d) == ATT_RADIUS for w, d in ATT_GROUPS)


def _attn_kernel(T, Ls, q_ref, kp_ref, kc_ref, kn_ref, vp_ref, vc_ref, vn_ref, o_ref, lse_ref, kbuf, vbuf):
    t = pl.program_id(2)
    R = ATT_RADIUS
    SB = 2 * R
    for buf, prv, cur, nxt in ((kbuf, kp_ref, kc_ref, kn_ref), (vbuf, vp_ref, vc_ref, vn_ref)):
        buf[0:R] = prv[...]
        buf[R:R + T] = cur[...]
        buf[R + T:R + T + R] = nxt[...]
    qi = lax.broadcasted_iota(jnp.int32, (SB, 2 * SB), 0)
    kk = lax.broadcasted_iota(jnp.int32, (SB, 2 * SB), 1)
    band = (kk >= qi) & (kk <= qi + 2 * R)
    lane = lax.broadcasted_iota(jnp.int32, (SB, LANES), 1)
    for sb in range(T // SB):
        kpos = t * T + (sb * SB - R) + kk
        mask = band & (kpos >= 0) & (kpos < Ls)
        lse = jnp.zeros((SB, LANES), F32)
        for h in range(HEADS_PER_GROUP):
            cols = slice(h * HEAD_DIM, (h + 1) * HEAD_DIM)
            q = q_ref[sb * SB:(sb + 1) * SB, cols]
            k = kbuf[sb * SB:(sb + 2) * SB, cols]
            v = vbuf[sb * SB:(sb + 2) * SB, cols]
            s = lax.dot_general(q, k, (((1,), (1,)), ((), ())), preferred_element_type=F32)
            s = jnp.where(mask, s, NEG_INF)
            m = jnp.max(s, axis=1, keepdims=True)
            p = jnp.exp(s - m)
            l = jnp.sum(p, axis=1, keepdims=True)
            o = _dot(p.astype(BF16), v) / l
            o_ref[sb * SB:(sb + 1) * SB, cols] = o.astype(o_ref.dtype)
            lse = jnp.where(lane == h, m + jnp.log(l), lse)
        lse_ref[sb * SB:(sb + 1) * SB, :] = lse


SLAB = 16


def _softmax_heads(q, k, v, mask):
    lane = lax.broadcasted_iota(jnp.int32, (q.shape[0], LANES), 1)
    lse = jnp.zeros((q.shape[0], LANES), F32)
    outs = []
    for h in range(HEADS_PER_GROUP):
        cols = slice(h * HEAD_DIM, (h + 1) * HEAD_DIM)
        s = lax.dot_general(q[:, cols], k[:, cols], (((1,), (1,)), ((), ())), preferred_element_type=F32)
        s = jnp.where(mask, s, NEG_INF)
        m = jnp.max(s, axis=1, keepdims=True)
        p = jnp.exp(s - m)
        l = jnp.sum(p, axis=1, keepdims=True)
        outs.append(_dot(p.astype(BF16), v[:, cols]) / l)
        lse = jnp.where(lane == h, m + jnp.log(l), lse)
    return jnp.concatenate(outs, axis=1), lse


def _attn_slab_kernel(d, NS, Ls, q_ref, kp_ref, kc_ref, kn_ref, vp_ref, vc_ref, vn_ref, o_ref, lse_ref,
                      qs, ks, vs, os_, ls):
    t = pl.program_id(1)
    R = ATT_RADIUS
    SB = 2 * R
    J = SLAB // d
    NH = kp_ref.shape[0]
    QA = SB // J
    KA = QA + 2 * NH
    n_sb = NS // QA
    to_class_major = lambda ref: pltpu.einshape("abc->bac", ref[...])
    qs[...] = to_class_major(q_ref)
    for buf, prv, cur, nxt in ((ks, kp_ref, kc_ref, kn_ref), (vs, vp_ref, vc_ref, vn_ref)):
        buf[:, 0:NH] = to_class_major(prv)
        buf[:, NH:NH + NS] = to_class_major(cur)
        buf[:, NH + NS:NH + NS + NH] = to_class_major(nxt)
    qi = lax.broadcasted_iota(jnp.int32, (SB, 2 * SB), 0)
    kk = lax.broadcasted_iota(jnp.int32, (SB, 2 * SB), 1)
    sq_rel = J * (qi % QA) + qi // QA
    sk_rel = J * (kk % KA - NH) + kk // KA
    band = jnp.abs(sk_rel - sq_rel) <= R

    def body(it, carry):
        r = it // n_sb
        a0 = pl.multiple_of((it % n_sb) * QA, QA)
        sk = J * (t * NS + a0) + sk_rel
        mask = band & (sk >= 0) & (sk < Ls)
        gather = lambda buf, n: jnp.concatenate([buf[j * d + r, pl.ds(a0, n), :] for j in range(J)], axis=0)
        o, lse = _softmax_heads(gather(qs, QA), gather(ks, KA), gather(vs, KA), mask)
        o = o.astype(os_.dtype)
        for j in range(J):
            os_[j * d + r, pl.ds(a0, QA), :] = o[j * QA:(j + 1) * QA]
            ls[j * d + r, pl.ds(a0, QA), :] = lse[j * QA:(j + 1) * QA]
        return carry

    lax.fori_loop(0, d * n_sb, body, 0, unroll=True)
    o_ref[...] = pltpu.einshape("bac->abc", os_[...])
    lse_ref[...] = pltpu.einshape("bac->abc", ls[...])


def dilated_attention_slabs(srcs, Bsz, L, g, *, NS=128):
    d = ATT_GROUPS[g][1]
    M = Bsz * L
    R = ATT_RADIUS
    GW = HEADS_PER_GROUP * HEAD_DIM
    NH = R * d // SLAB
    TB = NS * SLAB
    assert SLAB % d == 0 and L % TB == 0 and NS % NH == 0 and (2 * R) % (SLAB // d) == 0
    nmb, nhb, hpm = L // TB, L // (NH * SLAB), NS // NH
    views = [a.reshape(M // SLAB, SLAB, a.shape[1]) for a, _ in srcs]
    col = lambda which: srcs[which][1]

    def main(which):
        return pl.BlockSpec((NS, SLAB, GW), lambda b, t: (b * nmb + t, 0, col(which)))

    def prev(which):
        return pl.BlockSpec((NH, SLAB, GW), lambda b, t: (jnp.maximum(b * nhb + t * hpm - 1, b * nhb), 0, col(which)))

    def nxt(which):
        return pl.BlockSpec((NH, SLAB, GW),
                            lambda b, t: (jnp.minimum(b * nhb + (t + 1) * hpm, (b + 1) * nhb - 1), 0, col(which)))

    o, lse = pl.pallas_call(
        partial(_attn_slab_kernel, d, NS, L // d),
        grid=(Bsz, nmb),
        in_specs=[main(0), prev(1), main(1), nxt(1), prev(2), main(2), nxt(2)],
        out_specs=[pl.BlockSpec((NS, SLAB, GW), lambda b, t: (b * nmb + t, 0, 0)),
                   pl.BlockSpec((NS, SLAB, LANES), lambda b, t: (b * nmb + t, 0, 0))],
        out_shape=[jax.ShapeDtypeStruct((M // SLAB, SLAB, GW), BF16),
                   jax.ShapeDtypeStruct((M // SLAB, SLAB, LANES), F32)],
        scratch_shapes=[pltpu.VMEM((SLAB, NS, GW), BF16),
                        pltpu.VMEM((SLAB, NS + 2 * NH, GW), BF16), pltpu.VMEM((SLAB, NS + 2 * NH, GW), BF16),
                        pltpu.VMEM((SLAB, NS, GW), BF16), pltpu.VMEM((SLAB, NS, LANES), F32)],
        compiler_params=_params("parallel", "arbitrary"),
        name=f"dilated_attention_g{g}",
    )(views[0], views[1], views[1], views[1], views[2], views[2], views[2])
    return o.reshape(M, GW), lse.reshape(M, LANES)


def dilated_attention_group(srcs, Bsz, L, g, *, T=1024):
    d = ATT_GROUPS[g][1]
    M = Bsz * L
    Ls = L // d
    R = ATT_RADIUS
    GW = HEADS_PER_GROUP * HEAD_DIM
    assert d == 1 and Ls % T == 0 and T % (2 * R) == 0
    nrb, nhb, hpt = Ls // T, Ls // R, T // R

    def main(which):
        return pl.BlockSpec((T, GW), lambda b, r, t: (b * nrb + t, srcs[which][1]))

    def prev(which):
        return pl.BlockSpec((R, GW), lambda b, r, t: (jnp.maximum(b * nhb + t * hpt - 1, b * nhb), srcs[which][1]))

    def nxt(which):
        return pl.BlockSpec((R, GW), lambda b, r, t: (jnp.minimum(b * nhb + (t + 1) * hpt, (b + 1) * nhb - 1),
                                                      srcs[which][1]))

    o, lse = pl.pallas_call(
        partial(_attn_kernel, T, Ls),
        grid=(Bsz, d, nrb),
        in_specs=[main(0), prev(1), main(1), nxt(1), prev(2), main(2), nxt(2)],
        out_specs=[pl.BlockSpec((T, GW), lambda b, r, t: (b * nrb + t, r)),
                   pl.BlockSpec((T, LANES), lambda b, r, t: (b * nrb + t, r))],
        out_shape=[jax.ShapeDtypeStruct((M // d, d * GW), BF16),
                   jax.ShapeDtypeStruct((M // d, d * LANES), F32)],
        scratch_shapes=[pltpu.VMEM((T + 2 * R, GW), BF16), pltpu.VMEM((T + 2 * R, GW), BF16)],
        compiler_params=_params("parallel", "parallel", "arbitrary"),
        name=f"dilated_attention_g{g}",
    )(srcs[0][0], srcs[1][0], srcs[1][0], srcs[1][0], srcs[2][0], srcs[2][0], srcs[2][0])
    return o.reshape(M, GW), lse.reshape(M, LANES)


def _merge_kernel(x_ref, a_ref, o0_ref, o1_ref, o2_ref, l0_ref, l1_ref, l2_ref, ga_ref, gb_ref,
                  post_g_ref, next_g_ref, whp_ref, wap_ref, wo_ref, o_ref, u_ref):
    lses = [l0_ref[...], l1_ref[...], l2_ref[...]]
    mx = jnp.maximum(jnp.maximum(lses[0], lses[1]), lses[2])
    es = [jnp.exp(l - mx) for l in lses]
    den = es[0] + es[1] + es[2]
    wts = [e / den for e in es]
    outs = [o0_ref, o1_ref, o2_ref]
    heads = []
    for h in range(HEADS_PER_GROUP):
        cols = slice(h * HEAD_DIM, (h + 1) * HEAD_DIM)
        heads.append(sum(wts[g][:, h:h + 1] * outs[g][:, cols].astype(F32) for g in range(N_GROUPS)))
    att = jnp.concatenate(heads, axis=1).astype(BF16)
    a = _dot(a_ref[...], whp_ref[...])
    b = _dot(att, wap_ref[...])
    gate_a, gate_b = ga_ref[...].astype(F32), gb_ref[...].astype(F32)
    merged = (jax.nn.sigmoid(gate_a) * a + jax.nn.sigmoid(gate_b) * b).astype(BF16)
    mix = _dot(merged, wo_ref[...])
    y = x_ref[...] + _rms(mix, post_g_ref[...])
    o_ref[...] = y
    u_ref[...] = _rms(y, next_g_ref[...]).astype(u_ref.dtype)


def merge_out(x, a_in, att_outs, att_lses, gates, post_g, next_g, whp, wap, wo, *, tm=512):
    M, D = x.shape
    const = lambda i: (0, 0)
    rows = lambda a: pl.BlockSpec((tm, a.shape[1]), lambda i: (i, 0))
    return pl.pallas_call(
        _merge_kernel,
        grid=(M // tm,),
        in_specs=[rows(x), rows(a_in)] + [rows(o) for o in att_outs] + [rows(l) for l in att_lses] + [
            pl.BlockSpec((tm, D), lambda i: (i, 0)),
            pl.BlockSpec((tm, D), lambda i: (i, 1)),
            pl.BlockSpec((1, D), const),
            pl.BlockSpec((1, D), const),
            pl.BlockSpec(whp.shape, const, pipeline_mode=pl.Buffered(1)),
            pl.BlockSpec(wap.shape, const, pipeline_mode=pl.Buffered(1)),
            pl.BlockSpec(wo.shape, const, pipeline_mode=pl.Buffered(1)),
        ],
        out_specs=[pl.BlockSpec((tm, D), lambda i: (i, 0)), pl.BlockSpec((tm, D), lambda i: (i, 0))],
        out_shape=[jax.ShapeDtypeStruct((M, D), F32), jax.ShapeDtypeStruct((M, D), BF16)],
        compiler_params=_params("parallel"),
        name="merge_out",
    )(x, a_in, *att_outs, *att_lses, gates, gates, post_g, next_g, whp, wap, wo)


def _dft_constants(L):
    N = 2 * L
    N2 = FFT_INNER
    N1 = N // N2
    h = N1 // 2
    idx1 = np.arange(N1)
    ang1 = -2.0 * np.pi * ((idx1[:, None] * idx1[None, :]) % N1) / N1
    f1r, f1i = np.cos(ang1), np.sin(ang1)
    s_data = np.block([[f1r[:, :h], -f1i[:, :h]], [f1i[:, :h], f1r[:, :h]]])
    hk = h + FFT_ROWS
    s_filt = np.concatenate([f1r[:hk], f1i[:hk]], axis=0)
    ar, ai = f1r[:h, :], -f1i[:h, :]
    t_fin = np.block([[ar, -ai], [ai, ar]])
    idx2 = np.arange(N2)
    ang2 = -2.0 * np.pi * ((idx2[:, None] * idx2[None, :]) % N2) / N2
    angt = -2.0 * np.pi * (idx1[:, None] * idx2[None, :]) / N
    return dict(
        N1=N1,
        s_data=jnp.asarray(s_data, BF16), s_filt=jnp.asarray(s_filt, BF16), t_fin=jnp.asarray(t_fin, BF16),
        f2r=jnp.asarray(np.cos(ang2), F32), f2i=jnp.asarray(np.sin(ang2), F32),
        twr=jnp.asarray(np.cos(angt).reshape(N1, 1, N2), F32),
        twi=jnp.asarray(np.sin(angt).reshape(N1, 1, N2), F32),
    )


def _hyena_in_kernel(tm, L, x_ref, prev_ref, next_ref, w_ref, cw_ref, cb_ref, o_ref, xn_ref):
    i = pl.program_id(0)
    H = prev_ref.shape[0]

    @pl.when(pl.program_id(1) == 0)
    def _():
        xn_ref[0:H] = prev_ref[...]
        xn_ref[H:H + tm] = x_ref[...]
        xn_ref[H + tm:H + tm + H] = next_ref[...]

    r = _dot(xn_ref[...], w_ref[...])
    row = lax.broadcasted_iota(jnp.int32, r.shape, 0)
    pos0 = (i * tm) % L
    outside = ((row < H) & (pos0 == 0)) | ((row >= H + tm) & (pos0 + tm == L))
    r = jnp.where(outside, 0.0, r)
    n = r.shape[0]
    y = pltpu.roll(r, 1, 0) * cw_ref[0:1, :] + r * cw_ref[1:2, :] + pltpu.roll(r, n - 1, 0) * cw_ref[2:3, :]
    o_ref[0] = (y[H:H + tm] + cb_ref[...]).astype(o_ref.dtype)


def hyena_inputs(x, w, conv_w, conv_b, L, *, tm=1024):
    M, D = x.shape
    C = w.shape[1] // 3
    H = SLAB
    nh = M // H
    assert L % tm == 0 and tm % H == 0
    return pl.pallas_call(
        partial(_hyena_in_kernel, tm, L),
        grid=(M // tm, 3),
        in_specs=[
            pl.BlockSpec((tm, D), lambda i, j: (i, 0)),
            pl.BlockSpec((H, D), lambda i, j: (jnp.maximum(i * (tm // H) - 1, 0), 0)),
            pl.BlockSpec((H, D), lambda i, j: (jnp.minimum((i + 1) * (tm // H), nh - 1), 0)),
            pl.BlockSpec((D, C), lambda i, j: (0, j)),
            pl.BlockSpec((SHORT_CONV, C), lambda i, j: (0, j)),
            pl.BlockSpec((1, C), lambda i, j: (0, j)),
        ],
        out_specs=pl.BlockSpec((1, tm, C), lambda i, j: (j, i, 0)),
        out_shape=jax.ShapeDtypeStruct((3, M, C), BF16),
        scratch_shapes=[pltpu.VMEM((tm + 2 * H, D), BF16)],
        compiler_params=_params("parallel", "arbitrary"),
        name="hyena_inputs",
    )(x, x, x, w, conv_w, conv_b)


def _filter_kernel(L, h, w1_ref, b1_ref, w2_ref, b2_ref, w3_ref, b3_ref, w4_ref,
                   freq_ref, fvec_ref, delta_ref, kf_ref):
    i = pl.program_id(0)
    HALF = LANES // 2
    wide = delta_ref.shape[1]
    C = wide // (2 * HYENA_ORDER)
    tl = FFT_ROWS * h

    def lag(shape):
        r = lax.broadcasted_iota(jnp.int32, shape, 0)
        return (r % h) * FFT_INNER + i * FFT_ROWS + r // h

    row = lag((tl, LANES))
    lane = lax.broadcasted_iota(jnp.int32, (tl, LANES), 1)
    posf = jnp.where(lane < HALF, row, L - row).astype(F32)
    t = posf / (L - 1)
    a = fvec_ref[...] * ((2.0 * math.pi / L) * posf)
    lh = lane % HALF
    feats = jnp.where(lh < FILTER_BANDS, jnp.cos(a),
                      jnp.where(lh < 2 * FILTER_BANDS, -jnp.sin(a),
                                jnp.where(lh == 2 * FILTER_BANDS, t, 0.0)))
    freq = freq_ref[...]
    def dot3(v, w_ref):
        v_hi = v.astype(BF16)
        v_lo = (v - v_hi.astype(F32)).astype(BF16)
        return _dot(jnp.concatenate([v_hi, v_lo, v_hi], axis=1), w_ref[...])

    dense = lambda v, w_ref, b_ref: jnp.sin(freq * (dot3(v, w_ref) + b_ref[...]))
    hid = dense(dense(dense(feats, w1_ref, b1_ref), w2_ref, b2_ref), w3_ref, b3_ref)
    h_hi = hid.astype(BF16)
    lhs = jnp.concatenate([h_hi, (hid - h_hi.astype(F32)).astype(BF16), h_hi], axis=1)
    row_c = lag((tl, C))
    t_dir = (row_c.astype(F32) / (L - 1), (L - row_c).astype(F32) / (L - 1))
    for n in range(HYENA_ORDER):
        for direction in range(2):
            c0 = (direction * HYENA_ORDER + n) * C
            part = _dot(lhs, w4_ref[:, c0:c0 + C]) * jnp.exp(-t_dir[direction] * delta_ref[:, c0:c0 + C])
            if direction == 1:
                part = jnp.where(row_c == 0, 0.0, part)
            kf_ref[n, 0, :, direction * h:(direction + 1) * h, :] = (
                part.astype(kf_ref.dtype).reshape(FFT_ROWS, h, C))


def hyena_filter_taps(L, fw1, fb1, fw2, fb2, fw3, fb3, fw4, ffreq):
    C = D_HYENA
    h = L // FFT_INNER
    groups = FFT_INNER // FFT_ROWS
    H = fw2.shape[0]
    HALF = LANES // 2
    assert H <= HALF and 2 * FILTER_BANDS + 1 <= HALF

    def both(a, rows):
        blk = jnp.zeros((HALF if rows else 1, HALF), F32).at[:a.shape[0], :a.shape[1]].set(a)
        if not rows:
            return jnp.concatenate([blk, blk], axis=1)
        zero = jnp.zeros_like(blk)
        return jnp.concatenate([jnp.concatenate([blk, zero], axis=1), jnp.concatenate([zero, blk], axis=1)], axis=0)

    w1 = both(jnp.concatenate([fw1[1:], fw1[:1]], axis=0), True)
    w2, w3 = both(fw2, True), both(fw3, True)
    b1, b2, b3, freq = (both(v[None, :], False) for v in (fb1, fb2, fb3, ffreq))
    w4 = fw4.reshape(H, HYENA_ORDER, 2, C)
    wide = 2 * HYENA_ORDER * C
    w4p = jnp.zeros((LANES, wide), F32)
    w4p = w4p.at[:H, :wide // 2].set(w4[:, :, 0].reshape(H, HYENA_ORDER * C))
    w4p = w4p.at[HALF:HALF + H, wide // 2:].set(w4[:, :, 1].reshape(H, HYENA_ORDER * C))
    def stack3(w):
        hi = w.astype(BF16)
        lo = (w - hi.astype(F32)).astype(BF16)
        return jnp.concatenate([hi, hi, lo], axis=0)

    w1, w2, w3, w4_stack = stack3(w1), stack3(w2), stack3(w3), stack3(w4p)
    bands = np.linspace(1e-4, FILTER_BANDS - 1, FILTER_BANDS, dtype=np.float32)
    fvec = np.zeros((1, LANES), np.float32)
    for base in (0, HALF):
        fvec[0, base:base + FILTER_BANDS] = bands
        fvec[0, base + FILTER_BANDS:base + 2 * FILTER_BANDS] = bands
    max_decay = math.log(DECAY_TARGET) / FAST_DECAY_PCT
    min_decay = math.log(DECAY_TARGET) / SLOW_DECAY_PCT
    deltas = np.abs(np.linspace(min_decay, max_decay, C, dtype=np.float32))
    delta_w = np.tile(deltas[None, :], (1, 2 * HYENA_ORDER))
    const = lambda i: (0, 0)
    args = (w1, b1, w2, b2, w3, b3, w4_stack, freq, jnp.asarray(fvec), jnp.asarray(delta_w))
    return pl.pallas_call(
        partial(_filter_kernel, L, h),
        grid=(groups,),
        in_specs=[pl.BlockSpec(a.shape, const) for a in args],
        out_specs=pl.BlockSpec((HYENA_ORDER, 1, FFT_ROWS, 2 * h, C), lambda i: (0, i, 0, 0, 0)),
        out_shape=jax.ShapeDtypeStruct((HYENA_ORDER, groups, FFT_ROWS, 2 * h, C), BF16),
        compiler_params=_params("parallel"),
        name="hyena_filter_taps",
    )(*args)


FFT_ROWS = 16
FFT_COLS = 512
FFT_COLS_IN = 1024


def _stage1_kernel(inner_major, s_ref, x_ref, y_ref):
    s = s_ref[...]
    if inner_major:
        xt = x_ref[...]
    else:
        _, h, rows, cols = x_ref.shape
        xt = pltpu.einshape("abc->bac", x_ref[...].reshape(2 * h, rows, cols))
    yt = jnp.stack([_dot(s, xt[b]).astype(y_ref.dtype) for b in range(xt.shape[0])], axis=0)
    y_ref[...] = pltpu.einshape("bac->abc", yt).reshape(y_ref.shape)


def fft_stage1(x, which, s, *, inner_major=False):
    if inner_major:
        _, groups, _, N1, C = x.shape
        N2 = groups * FFT_ROWS
        x_spec = pl.BlockSpec((None, None, FFT_ROWS, N1, FFT_COLS_IN), lambda j, c: (which, j, 0, 0, c))
    else:
        _, _, h, N2, C = x.shape
        N1 = 2 * h
        x_spec = pl.BlockSpec((None, 2, h, FFT_ROWS, FFT_COLS_IN), lambda j, c: (which, 0, 0, j, c))
    return pl.pallas_call(
        partial(_stage1_kernel, inner_major),
        grid=(N2 // FFT_ROWS, C // FFT_COLS_IN),
        in_specs=[pl.BlockSpec(s.shape, lambda j, c: (0, 0)), x_spec],
        out_specs=pl.BlockSpec((2, s.shape[0] // 2, FFT_ROWS, FFT_COLS_IN), lambda j, c: (0, 0, j, c)),
        out_shape=jax.ShapeDtypeStruct((2, s.shape[0] // 2, N2, C), BF16),
        compiler_params=_params("parallel", "parallel"),
        name="fft_stage1",
    )(s, x)


FFT_K1_PER_STEP = 4


def _mid_kernel(inv_n, n1, f2r_ref, f2i_ref, twr_ref, twi_ref, *refs):
    kb = len(refs) - 2
    yf_refs, y_ref, u_ref = refs[:kb], refs[kb], refs[kb + 1]
    f2r, f2i = f2r_ref[...], f2i_ref[...]
    for k in range(kb):
        twr, twi = twr_ref[k], twi_ref[k]
        gr, gi = f2r * twr - f2i * twi, f2r * twi + f2i * twr

        def times(ar, ai, x):
            ar, ai = ar.astype(BF16), ai.astype(BF16)
            return (_dot(jnp.concatenate([ar, -ai], axis=1), x), _dot(jnp.concatenate([ai, ar], axis=1), x))

        mirrored = pl.program_id(0) * kb + k > n1 // 2
        yf_im = yf_refs[k][1, 0]
        taps = jnp.concatenate([yf_refs[k][0, 0], jnp.where(mirrored, -yf_im, yf_im)], axis=0)
        hr, hi = times(gr, gi, taps)
        zr, zi = times(gr, gi, y_ref[:, k].reshape(2 * gr.shape[0], -1))
        prod = jnp.concatenate([(zr * hr - zi * hi).astype(BF16), (zr * hi + zi * hr).astype(BF16)], axis=0)
        ur, ui = times(gr.T * inv_n, gi.T * (-inv_n), prod)
        u_ref[0, k] = ur.astype(u_ref.dtype)
        u_ref[1, k] = ui.astype(u_ref.dtype)


def fft_mid(yf, y, dc):
    _, N1, N2, C = y.shape
    kb = FFT_K1_PER_STEP
    assert yf.shape[1] > N1 // 2
    blk = pl.BlockSpec((2, kb, N2, C), lambda k: (0, k, 0, 0))
    const = pl.BlockSpec((N2, N2), lambda k: (0, 0))
    tw = pl.BlockSpec((kb, 1, N2), lambda k: (k, 0, 0))

    def filt_blk(j):
        def index(k):
            q = k * kb + j
            return (0, jnp.where(q > N1 // 2, N1 - q, q), 0, 0)
        return pl.BlockSpec((2, 1, N2, C), index)

    return pl.pallas_call(
        partial(_mid_kernel, 1.0 / (N1 * N2), N1),
        grid=(N1 // kb,),
        in_specs=[const, const, tw, tw] + [filt_blk(j) for j in range(kb)] + [blk],
        out_specs=blk,
        out_shape=jax.ShapeDtypeStruct((2, N1, N2, C), BF16),
        compiler_params=_params("parallel"),
        name="fft_mid",
    )(dc['f2r'], dc['f2i'], dc['twr'], dc['twi'], *([yf] * kb), y)


def _final_kernel(t_ref, u_ref, z_ref, gate_ref, bias_ref, o_ref):
    _, n1, rows, cols = u_ref.shape
    t = t_ref[...]
    ut = pltpu.einshape("abc->bac", u_ref[...].reshape(2 * n1, rows, cols))
    convt = jnp.stack([_dot(t, ut[b]).astype(BF16) for b in range(rows)], axis=0)
    conv = pltpu.einshape("bac->abc", convt).reshape(o_ref.shape).astype(F32)
    z = z_ref[...].astype(F32)
    o_ref[...] = (gate_ref[...].astype(F32) * (conv + bias_ref[...] * z)).astype(o_ref.dtype)


def fft_final(u, t_fin, z, z_which, gate, gate_which, bias, bias_which):
    _, N1, N2, C = u.shape
    h = N1 // 2

    def half(which):
        return pl.BlockSpec((None, 2, h, FFT_ROWS, FFT_COLS), lambda j, c: (which, 0, 0, j, c))

    return pl.pallas_call(
        _final_kernel,
        grid=(N2 // FFT_ROWS, C // FFT_COLS),
        in_specs=[pl.BlockSpec((N1, 2 * N1), lambda j, c: (0, 0)),
                  pl.BlockSpec((2, N1, FFT_ROWS, FFT_COLS), lambda j, c: (0, 0, j, c)),
                  half(z_which), half(gate_which),
                  pl.BlockSpec((None, 1, FFT_COLS), lambda j, c: (bias_which, 0, c))],
        out_specs=half(0),
        out_shape=jax.ShapeDtypeStruct((1, 2, h, N2, C), BF16),
        compiler_params=_params("parallel", "parallel"),
        name="fft_final",
    )(t_fin, u, z, gate, bias.reshape(bias.shape[0], 1, C))


def hyena_branch(zs, L, p):
    _, M, C = zs.shape
    assert M == 2 * L, "the batch pair rides as real/imaginary parts"
    dc = _dft_constants(L)
    split = (2, dc['N1'] // 2, FFT_INNER, C)
    taps = hyena_filter_taps(L, *p['filt'])
    zs = zs.reshape(3, *split)
    z, z_which = zs, 0
    for n in range(HYENA_ORDER):
        yf = fft_stage1(taps, n, dc['s_filt'], inner_major=True)
        u = fft_mid(yf, fft_stage1(z, z_which, dc['s_data']), dc)
        z, z_which = fft_final(u, dc['t_fin'], z, z_which, zs, n + 1, p['hy_bias'], n), 0
    return z.reshape(M, C)


def _split_w_in(w):
    c_q = 3 * D_HYENA
    c_v = c_q + 2 * D_ATT
    c_g = c_v + D_ATT
    return tuple(w[:, a:b].astype(BF16) for a, b in ((0, c_q), (c_q, c_v), (c_v, c_g), (c_g, w.shape[1])))


def _layer(x, p):
    Bsz, L, D = x.shape
    M = Bsz * L
    x0 = x.reshape(M, D)
    x1, u = ffn_block(x0, p['ffn1_pre_g'], p['ffn1_post_g'], p['ffn1_w_gate'], p['ffn1_w_up'], p['ffn1_w_down'],
                      next_g=p['mix_pre_g'])
    w_hy, w_qk, w_v, w_gate = p['w_in']
    hy_in = hyena_inputs(u, w_hy, p['hy_conv_w'], p['hy_conv_b'], L)
    qk = projection(u, w_qk, BF16, tm=1024, tn=D_ATT, rope_len=L, rope=p['rope'], name="inproj_qk")
    v = projection(u, w_v, BF16, tm=1024, tn=D_ATT, name="inproj_v")
    gates = projection(u, w_gate, BF16, tm=1024, tn=D_MODEL, name="inproj_gates")
    a2 = hyena_branch(hy_in, L, p)
    att = []
    for g in range(N_GROUPS):
        srcs = ((qk, g), (qk, N_GROUPS + g), (v, g))
        fn = dilated_attention_group if ATT_GROUPS[g][1] == 1 else dilated_attention_slabs
        att.append(fn(srcs, Bsz, L, g))
    x2, xn2 = merge_out(x1, a2, [o for o, _ in att], [l for _, l in att], gates, p['mix_post_g'], p['ffn2_pre_g'],
                        p['w_hy_proj'], p['w_att_proj'], p['w_out'])
    x3 = ffn_block(x2, p['ffn2_pre_g'], p['ffn2_post_g'], p['ffn2_w_gate'], p['ffn2_w_up'], p['ffn2_w_down'],
                   xn=xn2)
    return x3.reshape(Bsz, L, D)


def kernel(x_prompt, x_sample, ffn1_pre_g, ffn1_post_g, ffn1_w_gate, ffn1_w_up, ffn1_w_down, mix_pre_g, mix_post_g, w_in, hy_conv_w, hy_conv_b, filt_w1, filt_b1, filt_w2, filt_b2, filt_w3, filt_b3, filt_w4, filt_freq, hy_bias, w_hy_proj, w_att_proj, w_out, ffn2_pre_g, ffn2_post_g, ffn2_w_gate, ffn2_w_up, ffn2_w_down):
    assert ffn1_w_gate.shape[0] == 1
    p = {
        'ffn1_pre_g': ffn1_pre_g, 'ffn1_post_g': ffn1_post_g,
        'ffn1_w_gate': ffn1_w_gate[0].astype(BF16), 'ffn1_w_up': ffn1_w_up[0].astype(BF16),
        'ffn1_w_down': ffn1_w_down[0].astype(BF16),
        'mix_pre_g': mix_pre_g, 'mix_post_g': mix_post_g,
        'w_in': _split_w_in(w_in[0]),
        'hy_conv_w': hy_conv_w[0], 'hy_conv_b': hy_conv_b,
        'filt': (filt_w1[0], filt_b1[0], filt_w2[0], filt_b2[0], filt_w3[0], filt_b3[0], filt_w4[0], filt_freq[0]),
        'hy_bias': hy_bias[0],
        'w_hy_proj': w_hy_proj[0].astype(BF16), 'w_att_proj': w_att_proj[0].astype(BF16),
        'w_out': w_out[0].astype(BF16),
        'ffn2_pre_g': ffn2_pre_g, 'ffn2_post_g': ffn2_post_g,
        'ffn2_w_gate': ffn2_w_gate[0].astype(BF16), 'ffn2_w_up': ffn2_w_up[0].astype(BF16),
        'ffn2_w_down': ffn2_w_down[0].astype(BF16),
    }
    p['rope'] = rope_tables(max(x_prompt.shape[1], x_sample.shape[1]))
    return (_layer(x_prompt, p), _layer(x_sample, p))
```
